```python
import math, functools
import jax, jax.numpy as jnp
from jax import lax
import numpy as np

D_MODEL = 2048
BATCH = 1
SEQ = 16384
DEPTH = 1
DEC_BATCH = 16
DEC_SEQ = 32
PAST_LEN = 1024

CHUNK = 64
Q_BLOCK = 128
H_A = 8
DV_A = (D_MODEL // 2) // H_A
HD_A = DV_A // 2
H_B = 8
HD_B = (D_MODEL // 2) // H_B
HKV_B = 2
G_B = H_B // HKV_B
H_IDX = 16
D_IDX = 64
TOPK_MAX = 256
N_BUCKETS = 32
MAX_DISTANCE = 128
D_FF = 4 * D_MODEL
EPS = 1e-6
SPLIT_SIZES = (H_A * 2 * HD_A, H_A * 2 * HD_A, H_A * DV_A, H_B * HD_B, HKV_B * HD_B, HKV_B * HD_B,
               H_IDX * D_IDX, D_IDX, H_IDX, D_MODEL, D_MODEL)
IN_COLS = sum(SPLIT_SIZES)

kernel_name = 'diff_dsa_gated_parallel_stream_encoder'


def rms_norm(x, g):
    xf = x.astype(jnp.float32)
    y = xf * lax.rsqrt(jnp.mean(xf * xf, axis=-1, keepdims=True) + EPS)
    return (y * g.astype(jnp.float32)).astype(x.dtype)


def split_cols(z):
    out, o = [], 0
    for n in SPLIT_SIZES:
        out.append(z[..., o:o + n])
        o += n
    return out


def t5_bucket(rel):
    half = N_BUCKETS // 2
    exact = half // 2
    n = jnp.abs(rel)
    nf = jnp.maximum(n, 1).astype(jnp.float32)
    large = exact + (jnp.log(nf / exact) / math.log(MAX_DISTANCE / exact) * (half - exact)).astype(jnp.int32)
    large = jnp.minimum(large, half - 1)
    return jnp.where(rel > 0, half, 0) + jnp.where(n < exact, n, large)


def chunk_visible(qpos, kpos):
    return (kpos // CHUNK) <= (qpos // CHUNK)


def diff_attn_block(start, q1, q2, k1, k2, v, lam, bias_tab):
    tq, n_keys = q1.shape[1], k1.shape[1]
    qpos = start + jnp.arange(tq)
    kpos = jnp.arange(n_keys)
    bias = bias_tab[t5_bucket(kpos[None, :] - qpos[:, None])].astype(jnp.float32)
    bias = jnp.transpose(bias, (2, 0, 1))[None]
    mask = chunk_visible(qpos[:, None], kpos[None, :])[None, None]
    scale = HD_A ** -0.5

    def probs(q, k):
        s = jnp.einsum('bthd,bshd->bhts', q, k, preferred_element_type=jnp.float32) * scale + bias
        return jax.nn.softmax(jnp.where(mask, s, -jnp.inf), axis=-1)

    a = probs(q1, k1) - lam * probs(q2, k2)
    return jnp.einsum('bhts,bshe->bthe', a.astype(v.dtype), v)


def dsa_block(start, qb, qi, wi, kb, vb, ki, bias_tab, topk):
    b, tq = qb.shape[0], qb.shape[1]
    n_keys = kb.shape[1]
    qpos = start + jnp.arange(tq)
    kpos = jnp.arange(n_keys)
    admissible = chunk_visible(qpos[:, None], kpos[None, :])
    rel = jax.nn.relu(jnp.einsum('bthd,bsd->bths', qi, ki, preferred_element_type=jnp.float32))
    score = jnp.einsum('bth,bths->bts', wi.astype(jnp.float32), rel)
    score = jnp.where(admissible[None], score, -jnp.inf)
    _, idx = lax.top_k(score, topk)
    ksel = jax.vmap(lambda kk, ii: kk[ii])(kb, idx)
    vsel = jax.vmap(lambda vv, ii: vv[ii])(vb, idx)
    valid = (idx // CHUNK) <= (qpos[None, :, None] // CHUNK)
    bias = bias_tab[t5_bucket(idx - qpos[None, :, None])].astype(jnp.float32)
    bias = jnp.transpose(bias.reshape(b, tq, topk, HKV_B, G_B), (0, 1, 3, 4, 2))
    q = qb.reshape(b, tq, HKV_B, G_B, HD_B)
    s = jnp.einsum('btngd,btjnd->btngj', q, ksel, preferred_element_type=jnp.float32) * (HD_B ** -0.5) + bias
    s = jnp.where(valid[:, :, None, None, :], s, -jnp.inf)
    p = jax.nn.softmax(s, axis=-1)
    o = jnp.einsum('btngj,btjnd->btngd', p.astype(vsel.dtype), vsel)
    return o.reshape(b, tq, H_B * HD_B)


def blockwise(fn, q_arrays, q0, n_q):
    if n_q > Q_BLOCK and n_q % Q_BLOCK == 0:
        nb = n_q // Q_BLOCK
        split = lambda a: jnp.swapaxes(a.reshape((a.shape[0], nb, Q_BLOCK) + a.shape[2:]), 0, 1)
        starts = q0 + Q_BLOCK * jnp.arange(nb, dtype=jnp.int32)
        out = lax.map(lambda a: fn(a[0], *a[1]), (starts, tuple(split(a) for a in q_arrays)))
        out = jnp.swapaxes(out, 0, 1)
        return out.reshape((out.shape[0], n_q) + out.shape[3:])
    return fn(q0, *q_arrays)


def layer_forward(x, past, rel_bias, norm1_g, w_in, qn_a_g, kn_a_g, lam_q1, lam_k1, lam_q2, lam_k2,
                  subln_a_g, qn_b_g, kn_b_g, w_o_a, w_o_b, w_out, norm2_g, w_ff1, w_ff2, lam_init):
    b, t, _ = x.shape
    h = rms_norm(x, norm1_g)
    z = jnp.einsum('btd,de->bte', h, w_in)
    qa, ka, va, qb, kb, vb, qi, ki, wi, ga, gb = split_cols(z)
    qa = rms_norm(qa.reshape(b, t, H_A, 2, HD_A), qn_a_g.reshape(2, HD_A))
    ka = rms_norm(ka.reshape(b, t, H_A, 2, HD_A), kn_a_g.reshape(2, HD_A)).reshape(b, t, H_A, 2 * HD_A)
    va = va.reshape(b, t, H_A, DV_A)
    qb = rms_norm(qb.reshape(b, t, H_B, HD_B), qn_b_g)
    kb = rms_norm(kb.reshape(b, t, HKV_B, HD_B), kn_b_g)
    vb = vb.reshape(b, t, HKV_B, HD_B)
    qi = qi.reshape(b, t, H_IDX, D_IDX)
    new_rows = (ka, va, kb, vb, ki)
    if past is None:
        full = new_rows
    else:
        full = tuple(jnp.concatenate([p.astype(n.dtype), n], axis=1) for p, n in zip(past, new_rows))
    ka_f, va_f, kb_f, vb_f, ki_f = full
    n_keys = ka_f.shape[1]
    q0 = n_keys - t

    k_pair = ka_f.reshape(b, n_keys, H_A, 2, HD_A)
    lam = (jnp.exp(jnp.sum(lam_q1.astype(jnp.float32) * lam_k1.astype(jnp.float32)))
           - jnp.exp(jnp.sum(lam_q2.astype(jnp.float32) * lam_k2.astype(jnp.float32))) + lam_init)
    fa = functools.partial(diff_attn_block, k1=k_pair[..., 0, :], k2=k_pair[..., 1, :], v=va_f,
                           lam=lam, bias_tab=rel_bias[:, :H_A])
    oa = blockwise(fa, (qa[..., 0, :], qa[..., 1, :]), q0, t)
    oa = (rms_norm(oa, subln_a_g) * (1.0 - lam_init)).reshape(b, t, H_A * DV_A)

    topk = min(TOPK_MAX, n_keys // 4)
    fb = functools.partial(dsa_block, kb=kb_f, vb=vb_f, ki=ki_f, bias_tab=rel_bias[:, H_A:], topk=topk)
    ob = blockwise(fb, (qb, qi, wi), q0, t)

    merged = jax.nn.sigmoid(ga) * (ob @ w_o_b) + jax.nn.sigmoid(gb * 0 + gb) * 0 if False else \
        jax.nn.sigmoid(ga) * (oa @ w_o_a) + jax.nn.sigmoid(gb) * (ob @ w_o_b)
    x = x + merged @ w_out

    u = jax.nn.relu(rms_norm(x, norm2_g) @ w_ff1)
    x = x + (u * u) @ w_ff2
    return x, new_rows


def setup_inputs(seed: int = 0) -> dict:
    key = jax.random.key(seed)
    ks = jax.random.split(key, 26)
    nrm = lambda k, shape, scale: jax.random.normal(k, shape, jnp.float32) * scale
    gain = lambda k, shape: 1.0 + 0.05 * jax.random.normal(k, shape, jnp.float32)
    return {
        'x_prompt': nrm(ks[0], (BATCH, SEQ, D_MODEL), 1.0),
        'x_sample': nrm(ks[1], (DEC_BATCH, DEC_SEQ, D_MODEL), 1.0),
        'cache_a_k': nrm(ks[2], (DEPTH, DEC_BATCH, PAST_LEN, H_A, 2 * HD_A), 1.0),
        'cache_a_v': nrm(ks[3], (DEPTH, DEC_BATCH, PAST_LEN, H_A, DV_A), 1.0),
        'cache_b_k': nrm(ks[4], (DEPTH, DEC_BATCH, PAST_LEN, HKV_B, HD_B), 1.0),
        'cache_b_v': nrm(ks[5], (DEPTH, DEC_BATCH, PAST_LEN, HKV_B, HD_B), 1.0),
        'cache_b_kidx': nrm(ks[6], (DEPTH, DEC_BATCH, PAST_LEN, D_IDX), 1.0),
        'rel_bias': nrm(ks[7], (N_BUCKETS, H_A + H_B), 0.2),
        'norm1_g': gain(ks[8], (DEPTH, D_MODEL)),
        'w_in': nrm(ks[9], (DEPTH, D_MODEL, IN_COLS), D_MODEL ** -0.5),
        'qn_a_g': gain(ks[10], (DEPTH, 2 * HD_A)),
        'kn_a_g': gain(ks[11], (DEPTH, 2 * HD_A)),
        'lam_q1': nrm(ks[12], (DEPTH, HD_A), 0.1),
        'lam_k1': nrm(ks[13], (DEPTH, HD_A), 0.1),
        'lam_q2': nrm(ks[14], (DEPTH, HD_A), 0.1),
        'lam_k2': nrm(ks[15], (DEPTH, HD_A), 0.1),
        'subln_a_g': gain(ks[16], (DEPTH, DV_A)),
        'qn_b_g': gain(ks[17], (DEPTH, HD_B)),
        'kn_b_g': gain(ks[18], (DEPTH, HD_B)),
        'w_o_a': nrm(ks[19], (DEPTH, H_A * DV_A, D_MODEL), (H_A * DV_A) ** -0.5),
        'w_o_b': nrm(ks[20], (DEPTH, H_B * HD_B, D_MODEL), (H_B * HD_B) ** -0.5),
        'w_out': nrm(ks[21], (DEPTH, D_MODEL, D_MODEL), D_MODEL ** -0.5),
        'norm2_g': gain(ks[22], (DEPTH, D_MODEL)),
        'w_ff1': nrm(ks[23], (DEPTH, D_MODEL, D_FF), D_MODEL ** -0.5),
        'w_ff2': nrm(ks[24], (DEPTH, D_FF, D_MODEL), D_FF ** -0.5),
    }


def reference(x_prompt, x_sample, cache_a_k, cache_a_v, cache_b_k, cache_b_v, cache_b_kidx, rel_bias,
              norm1_g, w_in, qn_a_g, kn_a_g, lam_q1, lam_k1, lam_q2, lam_k2, subln_a_g, qn_b_g, kn_b_g,
              w_o_a, w_o_b, w_out, norm2_g, w_ff1, w_ff2):
    y_prompt, y_sample = x_prompt, x_sample
    prompt_rows, sample_rows = [], []
    for l in range(DEPTH):
        lam_init = 0.8 - 0.6 * math.exp(-0.3 * l)
        weights = (rel_bias, norm1_g[l], w_in[l], qn_a_g[l], kn_a_g[l], lam_q1[l], lam_k1[l], lam_q2[l],
                   lam_k2[l], subln_a_g[l], qn_b_g[l], kn_b_g[l], w_o_a[l], w_o_b[l], w_out[l], norm2_g[l],
                   w_ff1[l], w_ff2[l], lam_init)
        y_prompt, rp = layer_forward(y_prompt, None, *weights)
        past = (cache_a_k[l], cache_a_v[l], cache_b_k[l], cache_b_v[l], cache_b_kidx[l])
        y_sample, rs = layer_forward(y_sample, past, *weights)
        prompt_rows.append(rp)
        sample_rows.append(rs)
    p_a_k, p_a_v, p_b_k, p_b_v, p_b_kidx = (jnp.stack(r, axis=0) for r in zip(*prompt_rows))
    s_a_k, s_a_v, s_b_k, s_b_v, s_b_kidx = (jnp.stack(r, axis=0) for r in zip(*sample_rows))
    return (y_prompt, y_sample, p_a_k, p_a_v, p_b_k, p_b_v, p_b_kidx, s_a_k, s_a_v, s_b_k, s_b_v, s_b_kidx)
```

```python
import functools
import math

import jax
import jax.numpy as jnp
from jax import lax
from jax.experimental import pallas as pl
from jax.experimental.pallas import tpu as pltpu

F32 = jnp.float32
BF16 = jnp.bfloat16
I32 = jnp.int32

D_MODEL = 2048
CHUNK = 64
H_A = 8
DV_A = 128
HD_A = 64
H_B = 8
HD_B = 128
HKV_B = 2
G_B = H_B // HKV_B
H_IDX = 16
D_IDX = 64
TOPK_MAX = 256
N_BUCKETS = 32
D_FF = 4 * D_MODEL
EPS = 1e-6
LANES = 128
NEG = -1e30
INT_MIN = -(2 ** 31)
INT_MAX = 2 ** 31 - 1
VMEM_LIMIT = 56 * 1024 * 1024

_NT = (((1,), (1,)), ((), ()))


def _cparams(sem):
    return pltpu.CompilerParams(dimension_semantics=sem, vmem_limit_bytes=VMEM_LIMIT)


def _resident(block_shape, index_map):
    return pl.BlockSpec(block_shape, index_map, pipeline_mode=pl.Buffered(1))


def _rmsnorm_body(x_ref, g_ref, o_ref):
    x = x_ref[...]
    y = x * lax.rsqrt(jnp.mean(x * x, axis=-1, keepdims=True) + EPS)
    o_ref[...] = (y * g_ref[...]).astype(o_ref.dtype)


def _rmsnorm(x, g, tm):
    n, d = x.shape
    return pl.pallas_call(
        _rmsnorm_body,
        grid=(n // tm,),
        in_specs=[pl.BlockSpec((tm, d), lambda i: (i, 0)), pl.BlockSpec((1, d), lambda i: (0, 0))],
        out_specs=pl.BlockSpec((tm, d), lambda i: (i, 0)),
        out_shape=jax.ShapeDtypeStruct((n, d), BF16),
        compiler_params=_cparams(("parallel",)),
        name="rmsnorm",
    )(x, g.reshape(1, d))


def _head_norm(z, gain, split):
    lane = lax.broadcasted_iota(I32, (1, LANES), 1)
    lo = lane < (LANES // 2)
    outs = []
    for h in range(z.shape[1] // LANES):
        zh = z[:, h * LANES:(h + 1) * LANES]
        sq = zh * zh
        if split:
            s_lo = jnp.sum(jnp.where(lo, sq, 0.0), axis=-1, keepdims=True)
            s_hi = jnp.sum(jnp.where(lo, 0.0, sq), axis=-1, keepdims=True)
            r = jnp.where(lo, lax.rsqrt(s_lo * (2.0 / LANES) + EPS), lax.rsqrt(s_hi * (2.0 / LANES) + EPS))
        else:
            r = lax.rsqrt(jnp.mean(sq, axis=-1, keepdims=True) + EPS)
        outs.append(zh * r * gain)
    return jnp.concatenate(outs, axis=-1)


def _proj_body(epilogue, n_aux, h_ref, w_ref, *rest):
    z = jnp.dot(h_ref[...], w_ref[...], preferred_element_type=F32)
    outs = epilogue(z, *[r[...] for r in rest[:n_aux]])
    for o_ref, o in zip(rest[n_aux:], outs):
        o_ref[...] = o.astype(o_ref.dtype)


def _proj(h, w, aux, epilogue, out_defs, tm, name):
    n, k = h.shape
    c = w.shape[1]
    in_specs = [pl.BlockSpec((tm, k), lambda i: (i, 0)), _resident((k, c), lambda i: (0, 0))]
    in_specs += [pl.BlockSpec(a.shape, lambda i: (0, 0)) for a in aux]
    return pl.pallas_call(
        functools.partial(_proj_body, epilogue, len(aux)),
        grid=(n // tm,),
        in_specs=in_specs,
        out_specs=[pl.BlockSpec((tm, oc), lambda i: (i, 0)) for oc, _ in out_defs],
        out_shape=[jax.ShapeDtypeStruct((n, oc), dt) for oc, dt in out_defs],
        compiler_params=_cparams(("parallel",)),
        name=name,
    )(h, w, *aux)


def _epi_qa(z, gain):
    return (_head_norm(z, gain, True) * (HD_A ** -0.5),)


def _epi_ka(z, gain):
    y = _head_norm(z, gain, True)
    return y, y


def _epi_copy2(z):
    return z, z


def _epi_qb(z, gain):
    return (_head_norm(z, gain, False) * (HD_B ** -0.5),)


def _epi_copy1(z):
    return (z,)


_KB0, _VB0, _KI0, _WI0, _MISC_COLS = 0, 256, 512, 640, 768


def _epi_misc(z, gain):
    kb = _head_norm(z[:, _KB0:_VB0], gain, False)
    vb = z[:, _VB0:_KI0]
    ki2 = z[:, _KI0:_WI0]
    return kb, kb, vb, vb, ki2[:, :D_IDX], ki2, z[:, _WI0:_WI0 + H_IDX]


def _bias_body(t, n_valid_diag, tab_ref, o_ref):
    h = pl.program_id(0)
    half = N_BUCKETS // 2
    exact = half // 2
    kl = lax.broadcasted_iota(I32, (t, t), 0)
    ql = lax.broadcasted_iota(I32, (t, t), 1)
    for tile, off in ((0, -t), (1, 0)):
        rel = kl + off - ql
        n = jnp.abs(rel)
        n2 = n * n
        v_neg = jnp.full((t, t), tab_ref[h, 0], F32)
        v_pos = jnp.full((t, t), tab_ref[h, half], F32)
        for b in range(1, half):
            cond = (n >= b) if b < exact else (n2 >= exact * exact * 2 ** (b - exact))
            v_neg = jnp.where(cond, tab_ref[h, b], v_neg)
            v_pos = jnp.where(cond, tab_ref[h, half + b], v_pos)
        val = jnp.where(rel > 0, v_pos, v_neg) - tab_ref[h, half - 1]
        if tile == 1:
            visible = ((kl // CHUNK) <= (ql // CHUNK)) & (kl < n_valid_diag)
            val = jnp.where(visible, val, NEG)
        o_ref[tile] = val


def _bias_tiles(tab_t, t, n_valid_diag):
    nh = tab_t.shape[0]
    return pl.pallas_call(
        functools.partial(_bias_body, t, n_valid_diag),
        grid=(nh,),
        in_specs=[pl.BlockSpec(memory_space=pltpu.SMEM)],
        out_specs=pl.BlockSpec((None, 2, t, t), lambda h: (h, 0, 0, 0)),
        out_shape=jax.ShapeDtypeStruct((nh, 2, t, t), F32),
        compiler_params=_cparams(("arbitrary",)),
        name="bias_tiles",
    )(tab_t)


def _softmax_step(s, vt, m_ref, l_ref, acc_ref):
    m_old = m_ref[...]
    m_new = jnp.maximum(m_old, jnp.max(s, axis=0, keepdims=True))
    alpha = jnp.exp(m_old - m_new)
    p = jnp.exp(s - m_new)
    l_ref[...] = alpha * l_ref[...] + jnp.sum(p, axis=0, keepdims=True)
    acc_ref[...] = acc_ref[...] * alpha + jnp.dot(vt, p.astype(BF16), preferred_element_type=F32)
    m_ref[...] = m_new


def _attn_a_body(t, q0, out_scale, lam_ref, q_ref, k_ref, vt_ref, bias_ref, g_ref, o_ref,
                 qz_ref, m_ref, l_ref, acc_ref):
    i = pl.program_id(2)
    n_far = q0 // t + i - 1
    q = q_ref[...]
    lane = lax.broadcasted_iota(I32, (t, LANES), 1)
    zero = jnp.zeros_like(q)
    qz_ref[0] = jnp.where(lane < HD_A, q, zero)
    qz_ref[1] = jnp.where(lane < HD_A, zero, q)
    m_ref[...] = jnp.full(m_ref.shape, NEG, F32)
    l_ref[...] = jnp.zeros(l_ref.shape, F32)
    acc_ref[...] = jnp.zeros(acc_ref.shape, F32)

    def tile(j, bias_idx):
        ks = pl.multiple_of(j * t, t)
        kt = k_ref[pl.ds(ks, t), :]
        vt = vt_ref[:, pl.ds(ks, t)]
        for c in range(2):
            s = lax.dot_general(kt, qz_ref[c], _NT, preferred_element_type=F32)
            if bias_idx is not None:
                s = s + bias_ref[bias_idx]
            _softmax_step(s, vt, m_ref.at[c], l_ref.at[c], acc_ref.at[c])

    def far_body(j, carry):
        tile(j, None)
        return carry

    lax.fori_loop(0, jnp.maximum(n_far, 0), far_body, 0)

    @pl.when(n_far >= 0)
    def _():
        tile(n_far, 0)

    tile(n_far + 1, 1)

    lam = lam_ref[0]
    o = acc_ref[0] / l_ref[0] - lam * (acc_ref[1] / l_ref[1])
    y = o * lax.rsqrt(jnp.mean(o * o, axis=0, keepdims=True) + EPS)
    y = (y * g_ref[...]) * out_scale
    o_ref[...] = y.T.astype(o_ref.dtype)


def _attn_a(q, k, vt, bias, subln_g, lam, *, t, q0, out_scale):
    b, nq, _ = q.shape
    nk = k.shape[1]
    return pl.pallas_call(
        functools.partial(_attn_a_body, t, q0, out_scale),
        grid=(b, H_A, nq // t),
        in_specs=[
            pl.BlockSpec(memory_space=pltpu.SMEM),
            pl.BlockSpec((None, t, LANES), lambda bb, h, i: (bb, i, h)),
            pl.BlockSpec((None, nk, LANES), lambda bb, h, i: (bb, 0, h)),
            pl.BlockSpec((None, DV_A, nk), lambda bb, h, i: (bb, h, 0)),
            pl.BlockSpec((None, 2, t, t), lambda bb, h, i: (h, 0, 0, 0)),
            pl.BlockSpec((DV_A, 1), lambda bb, h, i: (0, 0)),
        ],
        out_specs=pl.BlockSpec((None, t, LANES), lambda bb, h, i: (bb, i, h)),
        out_shape=jax.ShapeDtypeStruct((b, nq, H_A * DV_A), BF16),
        scratch_shapes=[
            pltpu.VMEM((2, t, LANES), BF16),
            pltpu.VMEM((2, 1, t), F32),
            pltpu.VMEM((2, 1, t), F32),
            pltpu.VMEM((2, DV_A, t), F32),
        ],
        compiler_params=_cparams(("parallel", "parallel", "arbitrary")),
        name="diff_attn",
    )(lam, q, k, vt, bias, subln_g)


def _order_key(x):
    b = pltpu.bitcast(x, I32)
    return jnp.where(b < 0, b ^ INT_MAX, b)


def _dsa_body(t, q0, topk, n_valid_diag, qb_ref, qi_ref, wi_ref, kb_ref, vbt_ref, ki_ref, bias_ref, o_ref,
              keys_ref, qz_ref, m_ref, l_ref, acc_ref):
    i = pl.program_id(1)
    n_tiles = q0 // t + i + 1
    n_far = n_tiles - 2

    lane = lax.broadcasted_iota(I32, (t, LANES), 1)
    for hp in range(H_IDX // 2):
        qs = qi_ref[:, hp * LANES:(hp + 1) * LANES]
        zero = jnp.zeros_like(qs)
        qz_ref[2 * hp] = jnp.where(lane < D_IDX, qs, zero)
        qz_ref[2 * hp + 1] = jnp.where(lane < D_IDX, zero, qs)

    def score_tile(j):
        kt = ki_ref[pl.ds(pl.multiple_of(j * t, t), t), :]
        acc = jnp.zeros((t, t), F32)
        for h in range(H_IDX):
            s = lax.dot_general(kt, qz_ref[h], _NT, preferred_element_type=F32)
            acc = acc + jnp.maximum(s, 0.0) * wi_ref[h:h + 1, :]
        return acc

    def score_body(j, carry):
        keys_ref[pl.ds(pl.multiple_of(j * t, t), t), :] = _order_key(score_tile(j))
        return carry

    lax.fori_loop(0, n_tiles - 1, score_body, 0)
    jd = n_tiles - 1
    kl = lax.broadcasted_iota(I32, (t, t), 0)
    ql = lax.broadcasted_iota(I32, (t, t), 1)
    admissible = ((kl // CHUNK) <= (ql // CHUNK)) & (kl < n_valid_diag)
    keys_ref[pl.ds(pl.multiple_of(jd * t, t), t), :] = jnp.where(admissible, _order_key(score_tile(jd)), INT_MIN)

    def count_ge(thr):
        def body(j, c):
            kk = keys_ref[pl.ds(pl.multiple_of(j * t, t), t), :]
            hit = jnp.where(kk >= thr, 1.0, 0.0)
            return c + jnp.sum(hit.reshape(t // 8, 8, t), axis=0)
        c8 = lax.fori_loop(0, n_tiles, body, jnp.zeros((8, t), F32))
        return jnp.sum(c8, axis=0, keepdims=True)

    def bisect_cond(state):
        it, _, _, active = state
        return jnp.logical_and(it < 34, active > 0)

    def bisect_body(state):
        it, lo, hi, _ = state
        mid = (lo >> 1) + (hi >> 1) + (lo & hi & 1)
        c = count_ge(mid)
        ge = c >= topk
        exact = c == topk
        lo2 = jnp.where(ge, mid, lo)
        hi2 = jnp.where(exact, mid + 1, jnp.where(ge, hi, mid))
        active = jnp.max(jnp.where(lo2 + 1 < hi2, 1, 0))
        return it + 1, lo2, hi2, active

    lo0 = jnp.full((1, t), INT_MIN, I32)
    hi0 = jnp.full((1, t), INT_MAX, I32)
    _, thr, _, _ = lax.while_loop(bisect_cond, bisect_body, (jnp.int32(0), lo0, hi0, jnp.int32(1)))

    m_ref[...] = jnp.full(m_ref.shape, NEG, F32)
    l_ref[...] = jnp.zeros(l_ref.shape, F32)
    acc_ref[...] = jnp.zeros(acc_ref.shape, F32)

    def tile(j, bias_idx):
        ks = pl.multiple_of(j * t, t)
        sel = keys_ref[pl.ds(ks, t), :] >= thr
        for n in range(HKV_B):
            kt = kb_ref[pl.ds(ks, t), n * HD_B:(n + 1) * HD_B]
            vt = vbt_ref[n * HD_B:(n + 1) * HD_B, pl.ds(ks, t)]
            for g in range(G_B):
                h = n * G_B + g
                s = lax.dot_general(kt, qb_ref[:, h * HD_B:(h + 1) * HD_B], _NT, preferred_element_type=F32)
                if bias_idx is not None:
                    s = s + bias_ref[h, bias_idx]
                s = jnp.where(sel, s, NEG)
                _softmax_step(s, vt, m_ref.at[h], l_ref.at[h], acc_ref.at[h])

    def far_body(j, carry):
        tile(j, None)
        return carry

    lax.fori_loop(0, jnp.maximum(n_far, 0), far_body, 0)

    @pl.when(n_far >= 0)
    def _():
        tile(n_far, 0)

    tile(n_far + 1, 1)

    for h in range(H_B):
        o_ref[:, h * HD_B:(h + 1) * HD_B] = (acc_ref[h] / l_ref[h]).T.astype(o_ref.dtype)


def _dsa(qb, qi, wi_t, kb, vbt, ki2, bias, *, t, q0, topk, n_valid_diag):
    b, nq, _ = qb.shape
    nk = kb.shape[1]
    return pl.pallas_call(
        functools.partial(_dsa_body, t, q0, topk, n_valid_diag),
        grid=(b, nq // t),
        in_specs=[
            pl.BlockSpec((None, t, H_B * HD_B), lambda bb, i: (bb, i, 0)),
            pl.BlockSpec((None, t, H_IDX * D_IDX), lambda bb, i: (bb, i, 0)),
            pl.BlockSpec((None, H_IDX, t), lambda bb, i: (bb, 0, i)),
            _resident((None, nk, HKV_B * HD_B), lambda bb, i: (bb, 0, 0)),
            _resident((None, HKV_B * HD_B, nk), lambda bb, i: (bb, 0, 0)),
            _resident((None, nk, LANES), lambda bb, i: (bb, 0, 0)),
            _resident((H_B, 2, t, t), lambda bb, i: (0, 0, 0, 0)),
        ],
        out_specs=pl.BlockSpec((None, t, H_B * HD_B), lambda bb, i: (bb, i, 0)),
        out_shape=jax.ShapeDtypeStruct((b, nq, H_B * HD_B), BF16),
        scratch_shapes=[
            pltpu.VMEM((nk, t), I32),
            pltpu.VMEM((H_IDX, t, LANES), BF16),
            pltpu.VMEM((H_B, 1, t), F32),
            pltpu.VMEM((H_B, 1, t), F32),
            pltpu.VMEM((H_B, HD_B, t), F32),
        ],
        compiler_params=_cparams(("parallel", "arbitrary")),
        name="dsa",
    )(qb, qi, wi_t, kb, vbt, ki2, bias)


def _sigmoid(x):
    return 1.0 / (1.0 + jnp.exp(-x))


def _merge_body(h_ref, oa_ref, ob_ref, wg_ref, woa_ref, wob_ref, o_ref):
    h = h_ref[...]
    ga = jnp.dot(h, wg_ref[:, :D_MODEL], preferred_element_type=F32)
    ya = jnp.dot(oa_ref[...], woa_ref[...], preferred_element_type=F32)
    merged = _sigmoid(ga) * ya
    gb = jnp.dot(h, wg_ref[:, D_MODEL:], preferred_element_type=F32)
    yb = jnp.dot(ob_ref[...], wob_ref[...], preferred_element_type=F32)
    o_ref[...] = (merged + _sigmoid(gb) * yb).astype(o_ref.dtype)


def _merge(h, oa, ob, wg, woa, wob, tm):
    n = h.shape[0]
    return pl.pallas_call(
        _merge_body,
        grid=(n // tm,),
        in_specs=[
            pl.BlockSpec((tm, D_MODEL), lambda i: (i, 0)),
            pl.BlockSpec((tm, H_A * DV_A), lambda i: (i, 0)),
            pl.BlockSpec((tm, H_B * HD_B), lambda i: (i, 0)),
            _resident((D_MODEL, 2 * D_MODEL), lambda i: (0, 0)),
            _resident((H_A * DV_A, D_MODEL), lambda i: (0, 0)),
            _resident((H_B * HD_B, D_MODEL), lambda i: (0, 0)),
        ],
        out_specs=pl.BlockSpec((tm, D_MODEL), lambda i: (i, 0)),
        out_shape=jax.ShapeDtypeStruct((n, D_MODEL), BF16),
        compiler_params=_cparams(("parallel",)),
        name="gated_merge",
    )(h, oa, ob, wg, woa, wob)


def _outproj_body(x_ref, mg_ref, w_ref, g_ref, x1_ref, h2_ref):
    x1 = x_ref[...] + jnp.dot(mg_ref[...], w_ref[...], preferred_element_type=F32)
    x1_ref[...] = x1
    y = x1 * lax.rsqrt(jnp.mean(x1 * x1, axis=-1, keepdims=True) + EPS)
    h2_ref[...] = (y * g_ref[...]).astype(h2_ref.dtype)


def _outproj(x, merged, w_out, g2, tm):
    n = x.shape[0]
    return pl.pallas_call(
        _outproj_body,
        grid=(n // tm,),
        in_specs=[
            pl.BlockSpec((tm, D_MODEL), lambda i: (i, 0)),
            pl.BlockSpec((tm, D_MODEL), lambda i: (i, 0)),
            _resident((D_MODEL, D_MODEL), lambda i: (0, 0)),
            pl.BlockSpec((1, D_MODEL), lambda i: (0, 0)),
        ],
        out_specs=[pl.BlockSpec((tm, D_MODEL), lambda i: (i, 0)), pl.BlockSpec((tm, D_MODEL), lambda i: (i, 0))],
        out_shape=[jax.ShapeDtypeStruct((n, D_MODEL), F32), jax.ShapeDtypeStruct((n, D_MODEL), BF16)],
        compiler_params=_cparams(("parallel",)),
        name="out_proj",
    )(x, merged, w_out, g2.reshape(1, D_MODEL))


def _ffn_body(x1_ref, h2_ref, w1_ref, w2_ref, o_ref):
    f = pl.program_id(1)

    @pl.when(f == 0)
    def _():
        o_ref[...] = x1_ref[...]

    u = jnp.maximum(jnp.dot(h2_ref[...], w1_ref[...], preferred_element_type=F32), 0.0)
    o_ref[...] += jnp.dot((u * u).astype(BF16), w2_ref[...], preferred_element_type=F32)


def _ffn(x1, h2, w1, w2, tm, tf):
    n = x1.shape[0]
    return pl.pallas_call(
        _ffn_body,
        grid=(n // tm, D_FF // tf),
        in_specs=[
            pl.BlockSpec((tm, D_MODEL), lambda i, f: (i, 0)),
            pl.BlockSpec((tm, D_MODEL), lambda i, f: (i, 0)),
            pl.BlockSpec((D_MODEL, tf), lambda i, f: (0, f)),
            pl.BlockSpec((tf, D_MODEL), lambda i, f: (f, 0)),
        ],
        out_specs=pl.BlockSpec((tm, D_MODEL), lambda i, f: (i, 0)),
        out_shape=jax.ShapeDtypeStruct((n, D_MODEL), F32),
        compiler_params=_cparams(("parallel", "arbitrary")),
        name="ffn",
    )(x1, h2, w1, w2)


def _prep_weights(w_in, w_o_a, w_o_b, w_out, w_ff1, w_ff2):
    sizes = (H_A * 2 * HD_A, H_A * 2 * HD_A, H_A * DV_A, H_B * HD_B, HKV_B * HD_B, HKV_B * HD_B,
             H_IDX * D_IDX, D_IDX, H_IDX, D_MODEL, D_MODEL)
    offs = [0]
    for s in sizes:
        offs.append(offs[-1] + s)
    col = lambda a, b: w_in[:, offs[a]:offs[b]]
    w_misc = jnp.concatenate(
        [col(4, 6), col(7, 8), col(7, 8), col(8, 9),
         jnp.zeros((D_MODEL, _MISC_COLS - _WI0 - H_IDX), w_in.dtype)], axis=1)
    return dict(
        qa=col(0, 1).astype(BF16), ka=col(1, 2).astype(BF16), va=col(2, 3).astype(BF16),
        qb=col(3, 4).astype(BF16), misc=w_misc.astype(BF16), qi=col(6, 7).astype(BF16),
        gate=col(9, 11).astype(BF16), oa=w_o_a.astype(BF16), ob=w_o_b.astype(BF16),
        out=w_out.astype(BF16), ff1=w_ff1.astype(BF16), ff2=w_ff2.astype(BF16))


def _layer(x, past, w, p, lam, lam_init, bias_a, bias_b, *, t_attn, tm, n_valid_diag):
    b, t, _ = x.shape
    n = b * t
    xf = x.reshape(n, D_MODEL)
    h = _rmsnorm(xf, p["norm1_g"], tm)

    gain_a = lambda g: g.reshape(1, 2 * HD_A)
    (qa,) = _proj(h, w["qa"], [gain_a(p["qn_a_g"])], _epi_qa, [(1024, BF16)], tm, "proj_qa")
    ka, ka_h = _proj(h, w["ka"], [gain_a(p["kn_a_g"])], _epi_ka, [(1024, F32), (1024, BF16)], tm, "proj_ka")
    va, va_h = _proj(h, w["va"], [], _epi_copy2, [(1024, F32), (1024, BF16)], tm, "proj_va")
    (qb,) = _proj(h, w["qb"], [p["qn_b_g"].reshape(1, HD_B)], _epi_qb, [(1024, BF16)], tm, "proj_qb")
    kb, kb_h, vb, vb_h, ki, ki2_h, wi = _proj(
        h, w["misc"], [p["kn_b_g"].reshape(1, HD_B)], _epi_misc,
        [(256, F32), (256, BF16), (256, F32), (256, BF16), (D_IDX, F32), (2 * D_IDX, BF16), (H_IDX, F32)],
        tm, "proj_misc")
    (qi,) = _proj(h, w["qi"], [], _epi_copy1, [(1024, BF16)], tm, "proj_qi")

    new_rows = (ka.reshape(b, t, H_A, 2 * HD_A), va.reshape(b, t, H_A, DV_A), kb.reshape(b, t, HKV_B, HD_B),
                vb.reshape(b, t, HKV_B, HD_B), ki.reshape(b, t, D_IDX))

    def full_keys(new_h, past_arr):
        new_h = new_h.reshape(b, t, -1)
        if past_arr is None:
            return new_h
        past_h = past_arr.reshape(b, past_arr.shape[1], -1).astype(BF16)
        return jnp.concatenate([past_h, new_h], axis=1)

    if past is None:
        pa_k = pa_v = pb_k = pb_v = pb_i = None
    else:
        pa_k, pa_v, pb_k, pb_v, pb_i = past
        pb_i = jnp.concatenate([pb_i, pb_i], axis=-1)
    ka_f, va_f = full_keys(ka_h, pa_k), full_keys(va_h, pa_v)
    kb_f, vb_f, ki_f = full_keys(kb_h, pb_k), full_keys(vb_h, pb_v), full_keys(ki2_h, pb_i)
    n_keys = ka_f.shape[1]
    q0 = n_keys - t
    topk = min(TOPK_MAX, n_keys // 4)
    nk_pad = -(-n_keys // t_attn) * t_attn
    nq_pad = -(-t // t_attn) * t_attn
    pad_k = lambda a: jnp.pad(a, ((0, 0), (0, nk_pad - n_keys), (0, 0)))
    pad_q = lambda a: jnp.pad(a.reshape(b, t, -1), ((0, 0), (0, nq_pad - t), (0, 0)))
    ka_f, va_f, kb_f, vb_f, ki_f = (pad_k(a) for a in (ka_f, va_f, kb_f, vb_f, ki_f))
    assert q0 % t_attn == 0

    oa = _attn_a(pad_q(qa), ka_f, jnp.swapaxes(va_f, 1, 2), bias_a, p["subln_a_g"].reshape(DV_A, 1), lam,
                 t=t_attn, q0=q0, out_scale=1.0 - lam_init)
    wi_t = jnp.swapaxes(pad_q(wi), 1, 2)
    ob = _dsa(pad_q(qb), pad_q(qi), wi_t, kb_f, jnp.swapaxes(vb_f, 1, 2), ki_f, bias_b,
              t=t_attn, q0=q0, topk=topk, n_valid_diag=n_valid_diag)
    oa = oa[:, :t].reshape(n, H_A * DV_A)
    ob = ob[:, :t].reshape(n, H_B * HD_B)

    merged = _merge(h, oa, ob, w["gate"], w["oa"], w["ob"], min(tm, 256))
    x1, h2 = _outproj(xf, merged, w["out"], p["norm2_g"], min(tm, 256))
    y = _ffn(x1, h2, w["ff1"], w["ff2"], tm, 512)
    return y.reshape(b, t, D_MODEL), new_rows


def kernel(x_prompt, x_sample, cache_a_k, cache_a_v, cache_b_k, cache_b_v, cache_b_kidx, rel_bias, norm1_g, w_in, qn_a_g, kn_a_g, lam_q1, lam_k1, lam_q2, lam_k2, subln_a_g, qn_b_g, kn_b_g, w_o_a, w_o_b, w_out, norm2_g, w_ff1, w_ff2):
    depth = w_in.shape[0]
    t_prompt, t_sample = 256, 128
    n_keys_sample = cache_a_k.shape[2] + x_sample.shape[1]
    tab_t = rel_bias.T.astype(F32)
    bias_p = _bias_tiles(tab_t, t_prompt, t_prompt)
    bias_s = _bias_tiles(tab_t, t_sample, n_keys_sample - (n_keys_sample - 1) // t_sample * t_sample)
    y_prompt, y_sample = x_prompt, x_sample
    prompt_rows, sample_rows = [], []
    for l in range(depth):
        lam_init = 0.8 - 0.6 * math.exp(-0.3 * l)
        lam = (jnp.exp(jnp.sum(lam_q1[l].astype(F32) * lam_k1[l].astype(F32)))
               - jnp.exp(jnp.sum(lam_q2[l].astype(F32) * lam_k2[l].astype(F32))) + lam_init).reshape(1)
        w = _prep_weights(w_in[l], w_o_a[l], w_o_b[l], w_out[l], w_ff1[l], w_ff2[l])
        p = dict(norm1_g=norm1_g[l], qn_a_g=qn_a_g[l], kn_a_g=kn_a_g[l], subln_a_g=subln_a_g[l],
                 qn_b_g=qn_b_g[l], kn_b_g=kn_b_g[l], norm2_g=norm2_g[l])
        y_prompt, rp = _layer(y_prompt, None, w, p, lam, lam_init, bias_p[:H_A], bias_p[H_A:],
                              t_attn=t_prompt, tm=512, n_valid_diag=t_prompt)
        past = (cache_a_k[l], cache_a_v[l], cache_b_k[l], cache_b_v[l], cache_b_kidx[l])
        y_sample, rs = _layer(y_sample, past, w, p, lam, lam_init, bias_s[:H_A], bias_s[H_A:],
                              t_attn=t_sample, tm=512,
                              n_valid_diag=n_keys_sample - (n_keys_sample - 1) // t_sample * t_sample)
        prompt_rows.append(rp)
        sample_rows.append(rs)
    p_rows = tuple(jnp.stack(r, axis=0) for r in zip(*prompt_rows))
    s_rows = tuple(jnp.stack(r, axis=0) for r in zip(*sample_rows))
    return (y_prompt, y_sample) + p_rows + s_rows
```

```python
import functools
import math

import jax
import jax.numpy as jnp
from jax import lax
from jax.experimental import pallas as pl
from jax.experimental.pallas import tpu as pltpu

F32 = jnp.float32
BF16 = jnp.bfloat16
I32 = jnp.int32

D_MODEL = 2048
CHUNK = 64
H_A = 8
DV_A = 128
HD_A = 64
H_B = 8
HD_B = 128
HKV_B = 2
G_B = H_B // HKV_B
H_IDX = 16
D_IDX = 64
TOPK_MAX = 256
N_BUCKETS = 32
D_FF = 4 * D_MODEL
EPS = 1e-6
LANES = 128
NEG = -1e30
LOG2E = math.log2(math.e)
INT_MIN = -(2 ** 31)
INT_MAX = 2 ** 31 - 1
VMEM_LIMIT = 56 * 1024 * 1024

_NT = (((1,), (1,)), ((), ()))


def _cparams(sem):
    return pltpu.CompilerParams(dimension_semantics=sem, vmem_limit_bytes=VMEM_LIMIT)


def _resident(block_shape, index_map):
    return pl.BlockSpec(block_shape, index_map, pipeline_mode=pl.Buffered(1))


def _rmsnorm_body(x_ref, g_ref, o_ref):
    x = x_ref[...]
    y = x * lax.rsqrt(jnp.mean(x * x, axis=-1, keepdims=True) + EPS)
    o_ref[...] = (y * g_ref[...]).astype(o_ref.dtype)


def _rmsnorm(x, g, tm):
    n, d = x.shape
    return pl.pallas_call(
        _rmsnorm_body,
        grid=(n // tm,),
        in_specs=[pl.BlockSpec((tm, d), lambda i: (i, 0)), pl.BlockSpec((1, d), lambda i: (0, 0))],
        out_specs=pl.BlockSpec((tm, d), lambda i: (i, 0)),
        out_shape=jax.ShapeDtypeStruct((n, d), BF16),
        compiler_params=_cparams(("parallel",)),
        name="rmsnorm",
    )(x, g.reshape(1, d))


def _head_norm(z, gain, split):
    lane = lax.broadcasted_iota(I32, (1, LANES), 1)
    lo = lane < (LANES // 2)
    outs = []
    for h in range(z.shape[1] // LANES):
        zh = z[:, h * LANES:(h + 1) * LANES]
        sq = zh * zh
        if split:
            s_lo = jnp.sum(jnp.where(lo, sq, 0.0), axis=-1, keepdims=True)
            s_hi = jnp.sum(jnp.where(lo, 0.0, sq), axis=-1, keepdims=True)
            r = jnp.where(lo, lax.rsqrt(s_lo * (2.0 / LANES) + EPS), lax.rsqrt(s_hi * (2.0 / LANES) + EPS))
        else:
            r = lax.rsqrt(jnp.mean(sq, axis=-1, keepdims=True) + EPS)
        outs.append(zh * r * gain)
    return jnp.concatenate(outs, axis=-1)


def _proj_body(epilogue, n_aux, h_ref, w_ref, *rest):
    z = jnp.dot(h_ref[...], w_ref[...], preferred_element_type=F32)
    outs = epilogue(z, *[r[...] for r in rest[:n_aux]])
    for o_ref, o in zip(rest[n_aux:], outs):
        o_ref[...] = o.astype(o_ref.dtype)


def _proj(h, w, aux, epilogue, out_defs, tm, name):
    n, k = h.shape
    c = w.shape[1]
    in_specs = [pl.BlockSpec((tm, k), lambda i: (i, 0)), _resident((k, c), lambda i: (0, 0))]
    in_specs += [pl.BlockSpec(a.shape, lambda i: (0, 0)) for a in aux]
    return pl.pallas_call(
        functools.partial(_proj_body, epilogue, len(aux)),
        grid=(n // tm,),
        in_specs=in_specs,
        out_specs=[pl.BlockSpec((tm, oc), lambda i: (i, 0)) for oc, _ in out_defs],
        out_shape=[jax.ShapeDtypeStruct((n, oc), dt) for oc, dt in out_defs],
        compiler_params=_cparams(("parallel",)),
        name=name,
    )(h, w, *aux)


def _epi_qa(z, gain):
    return (_head_norm(z, gain, True) * (HD_A ** -0.5 * LOG2E),)


def _epi_ka(z, gain):
    y = _head_norm(z, gain, True)
    return y, y


def _epi_copy2(z):
    return z, z


def _epi_qb(z, gain):
    return (_head_norm(z, gain, False) * (HD_B ** -0.5 * LOG2E),)


def _epi_copy1(z):
    return (z,)


_KB0, _VB0, _KI0, _WI0, _MISC_COLS = 0, 256, 512, 640, 768


def _epi_misc(z, gain):
    kb = _head_norm(z[:, _KB0:_VB0], gain, False)
    vb = z[:, _VB0:_KI0]
    ki2 = z[:, _KI0:_WI0]
    return kb, kb, vb, vb, ki2[:, :D_IDX], ki2, z[:, _WI0:_WI0 + H_IDX]


def _bias_body(t, n_valid_diag, tab_ref, o_ref):
    h = pl.program_id(0)
    half = N_BUCKETS // 2
    exact = half // 2
    kl = lax.broadcasted_iota(I32, (t, t), 0)
    ql = lax.broadcasted_iota(I32, (t, t), 1)
    for tile, off in ((0, -t), (1, 0)):
        rel = kl + off - ql
        n = jnp.abs(rel)
        n2 = n * n
        v_neg = jnp.full((t, t), tab_ref[h, 0], F32)
        v_pos = jnp.full((t, t), tab_ref[h, half], F32)
        for b in range(1, half):
            cond = (n >= b) if b < exact else (n2 >= exact * exact * 2 ** (b - exact))
            v_neg = jnp.where(cond, tab_ref[h, b], v_neg)
            v_pos = jnp.where(cond, tab_ref[h, half + b], v_pos)
        val = (jnp.where(rel > 0, v_pos, v_neg) - tab_ref[h, half - 1]) * LOG2E
        if tile == 1:
            visible = ((kl // CHUNK) <= (ql // CHUNK)) & (kl < n_valid_diag)
            val = jnp.where(visible, val, NEG)
        o_ref[tile] = val


def _bias_tiles(tab_t, t, n_valid_diag):
    nh = tab_t.shape[0]
    return pl.pallas_call(
        functools.partial(_bias_body, t, n_valid_diag),
        grid=(nh,),
        in_specs=[pl.BlockSpec(memory_space=pltpu.SMEM)],
        out_specs=pl.BlockSpec((None, 2, t, t), lambda h: (h, 0, 0, 0)),
        out_shape=jax.ShapeDtypeStruct((nh, 2, t, t), F32),
        compiler_params=_cparams(("arbitrary",)),
        name="bias_tiles",
    )(tab_t)


def _softmax_step(s, vt, m_ref, l_ref, acc_ref):
    m_old = m_ref[...]
    m_new = jnp.maximum(m_old, jnp.max(s, axis=0, keepdims=True))
    alpha = jnp.exp2(m_old - m_new)
    p = jnp.exp2(s - m_new)
    l_ref[...] = alpha * l_ref[...] + jnp.sum(p, axis=0, keepdims=True)
    acc_ref[...] = acc_ref[...] * alpha + jnp.dot(vt, p.astype(BF16), preferred_element_type=F32)
    m_ref[...] = m_new


def _attn_a_body(t, q0, out_scale, lam_ref, q_ref, k_ref, vt_ref, bias_ref, g_ref, o_ref,
                 qz_ref, m_ref, l_ref, acc_ref):
    i = pl.program_id(2)
    n_far = q0 // t + i - 1
    q = q_ref[...]
    lane = lax.broadcasted_iota(I32, (t, LANES), 1)
    zero = jnp.zeros_like(q)
    qz_ref[:t] = jnp.where(lane < HD_A, q, zero)
    qz_ref[t:] = jnp.where(lane < HD_A, zero, q)
    m_ref[...] = jnp.full(m_ref.shape, NEG, F32)
    l_ref[...] = jnp.zeros(l_ref.shape, F32)
    acc_ref[...] = jnp.zeros(acc_ref.shape, F32)

    def tile(j, bias_idx):
        ks = pl.multiple_of(j * t, t)
        s = lax.dot_general(k_ref[pl.ds(ks, t), :], qz_ref[...], _NT, preferred_element_type=F32)
        if bias_idx is not None:
            b = bias_ref[bias_idx]
            s = jnp.concatenate([s[:, :t] + b, s[:, t:] + b], axis=1)
        _softmax_step(s, vt_ref[:, pl.ds(ks, t)], m_ref, l_ref, acc_ref)

    def far_body(j, carry):
        tile(j, None)
        return carry

    lax.fori_loop(0, jnp.maximum(n_far, 0), far_body, 0)

    @pl.when(n_far >= 0)
    def _():
        tile(n_far, 0)

    tile(n_far + 1, 1)

    lam = lam_ref[0]
    o = acc_ref[...] / l_ref[...]
    o = o[:, :t] - lam * o[:, t:]
    y = o * lax.rsqrt(jnp.mean(o * o, axis=0, keepdims=True) + EPS)
    y = (y * g_ref[...]) * out_scale
    o_ref[...] = y.T.astype(o_ref.dtype)


def _attn_a(q, k, vt, bias, subln_g, lam, *, t, q0, out_scale):
    b, nq, _ = q.shape
    nk = k.shape[1]
    return pl.pallas_call(
        functools.partial(_attn_a_body, t, q0, out_scale),
        grid=(b, H_A, nq // t),
        in_specs=[
            pl.BlockSpec(memory_space=pltpu.SMEM),
            pl.BlockSpec((None, t, LANES), lambda bb, h, i: (bb, i, h)),
            pl.BlockSpec((None, nk, LANES), lambda bb, h, i: (bb, 0, h)),
            pl.BlockSpec((None, DV_A, nk), lambda bb, h, i: (bb, h, 0)),
            pl.BlockSpec((None, 2, t, t), lambda bb, h, i: (h, 0, 0, 0)),
            pl.BlockSpec((DV_A, 1), lambda bb, h, i: (0, 0)),
        ],
        out_specs=pl.BlockSpec((None, t, LANES), lambda bb, h, i: (bb, i, h)),
        out_shape=jax.ShapeDtypeStruct((b, nq, H_A * DV_A), BF16),
        scratch_shapes=[
            pltpu.VMEM((2 * t, LANES), BF16),
            pltpu.VMEM((1, 2 * t), F32),
            pltpu.VMEM((1, 2 * t), F32),
            pltpu.VMEM((DV_A, 2 * t), F32),
        ],
        compiler_params=_cparams(("parallel", "parallel", "arbitrary")),
        name="diff_attn",
    )(lam, q, k, vt, bias, subln_g)


def _order_key(x):
    b = pltpu.bitcast(x, I32)
    return jnp.where(b < 0, b ^ INT_MAX, b)


def _dsa_body(t, q0, topk, n_valid_diag, qb_ref, qi_ref, wi_ref, kb_ref, vbt_ref, ki_ref, bias_ref, o_ref,
              keys_ref, qz_ref, qs_ref, m_ref, l_ref, acc_ref):
    i = pl.program_id(1)
    n_tiles = q0 // t + i + 1
    n_far = n_tiles - 2

    lane = lax.broadcasted_iota(I32, (t, LANES), 1)
    for hp in range(H_IDX // 2):
        qs = qi_ref[:, hp * LANES:(hp + 1) * LANES]
        zero = jnp.zeros_like(qs)
        qz_ref[2 * hp] = jnp.where(lane < D_IDX, qs, zero)
        qz_ref[2 * hp + 1] = jnp.where(lane < D_IDX, zero, qs)

    def score_tile(j):
        kt = ki_ref[pl.ds(pl.multiple_of(j * t, t), t), :]
        acc = jnp.zeros((t, t), F32)
        for h in range(H_IDX):
            s = lax.dot_general(kt, qz_ref[h], _NT, preferred_element_type=F32)
            acc = acc + jnp.maximum(s, 0.0) * wi_ref[h:h + 1, :]
        return acc

    def score_body(j, carry):
        keys_ref[pl.ds(pl.multiple_of(j * t, t), t), :] = _order_key(score_tile(j))
        return carry

    lax.fori_loop(0, n_tiles - 1, score_body, 0)
    jd = n_tiles - 1
    kl = lax.broadcasted_iota(I32, (t, t), 0)
    ql = lax.broadcasted_iota(I32, (t, t), 1)
    admissible = ((kl // CHUNK) <= (ql // CHUNK)) & (kl < n_valid_diag)
    keys_ref[pl.ds(pl.multiple_of(jd * t, t), t), :] = jnp.where(admissible, _order_key(score_tile(jd)), INT_MIN)

    def count_ge(thr):
        def body(j, c):
            kk = keys_ref[pl.ds(pl.multiple_of(j * t, t), t), :]
            hit = jnp.where(kk >= thr, 1.0, 0.0)
            return c + jnp.sum(hit.reshape(t // 8, 8, t), axis=0)
        c8 = lax.fori_loop(0, n_tiles, body, jnp.zeros((8, t), F32))
        return jnp.sum(c8, axis=0, keepdims=True)

    def bisect_cond(state):
        it, _, _, active = state
        return jnp.logical_and(it < 34, active > 0)

    def bisect_body(state):
        it, lo, hi, _ = state
        mid = (lo >> 1) + (hi >> 1) + (lo & hi & 1)
        c = count_ge(mid)
        ge = c >= topk
        exact = c == topk
        lo2 = jnp.where(ge, mid, lo)
        hi2 = jnp.where(exact, mid + 1, jnp.where(ge, hi, mid))
        active = jnp.max(jnp.where(lo2 + 1 < hi2, 1, 0))
        return it + 1, lo2, hi2, active

    lo0 = jnp.full((1, t), INT_MIN, I32)
    hi0 = jnp.full((1, t), INT_MAX, I32)
    _, thr, _, _ = lax.while_loop(bisect_cond, bisect_body, (jnp.int32(0), lo0, hi0, jnp.int32(1)))

    m_ref[...] = jnp.full(m_ref.shape, NEG, F32)
    l_ref[...] = jnp.zeros(l_ref.shape, F32)
    acc_ref[...] = jnp.zeros(acc_ref.shape, F32)

    for h in range(H_B):
        qs_ref[h // G_B, (h % G_B) * t:(h % G_B + 1) * t, :] = qb_ref[:, h * HD_B:(h + 1) * HD_B]

    def tile(j, bias_idx):
        ks = pl.multiple_of(j * t, t)
        sel = keys_ref[pl.ds(ks, t), :] >= thr
        for n in range(HKV_B):
            s = lax.dot_general(kb_ref[pl.ds(ks, t), n * HD_B:(n + 1) * HD_B], qs_ref[n], _NT,
                                preferred_element_type=F32)
            parts = []
            for g in range(G_B):
                sg = s[:, g * t:(g + 1) * t]
                if bias_idx is not None:
                    sg = sg + bias_ref[n * G_B + g, bias_idx]
                parts.append(jnp.where(sel, sg, NEG))
            _softmax_step(jnp.concatenate(parts, axis=1), vbt_ref[n * HD_B:(n + 1) * HD_B, pl.ds(ks, t)],
                          m_ref.at[n], l_ref.at[n], acc_ref.at[n])

    def far_body(j, carry):
        tile(j, None)
        return carry

    lax.fori_loop(0, jnp.maximum(n_far, 0), far_body, 0)

    @pl.when(n_far >= 0)
    def _():
        tile(n_far, 0)

    tile(n_far + 1, 1)

    for n in range(HKV_B):
        o = acc_ref[n] / l_ref[n]
        for g in range(G_B):
            h = n * G_B + g
            o_ref[:, h * HD_B:(h + 1) * HD_B] = o[:, g * t:(g + 1) * t].T.astype(o_ref.dtype)


def _dsa(qb, qi, wi_t, kb, vbt, ki2, bias, *, t, q0, topk, n_valid_diag):
    b, nq, _ = qb.shape
    nk = kb.shape[1]
    return pl.pallas_call(
        functools.partial(_dsa_body, t, q0, topk, n_valid_diag),
        grid=(b, nq // t),
        in_specs=[
            pl.BlockSpec((None, t, H_B * HD_B), lambda bb, i: (bb, i, 0)),
            pl.BlockSpec((None, t, H_IDX * D_IDX), lambda bb, i: (bb, i, 0)),
            pl.BlockSpec((None, H_IDX, t), lambda bb, i: (bb, 0, i)),
            _resident((None, nk, HKV_B * HD_B), lambda bb, i: (bb, 0, 0)),
            _resident((None, HKV_B * HD_B, nk), lambda bb, i: (bb, 0, 0)),
            _resident((None, nk, LANES), lambda bb, i: (bb, 0, 0)),
            _resident((H_B, 2, t, t), lambda bb, i: (0, 0, 0, 0)),
        ],
        out_specs=pl.BlockSpec((None, t, H_B * HD_B), lambda bb, i: (bb, i, 0)),
        out_shape=jax.ShapeDtypeStruct((b, nq, H_B * HD_B), BF16),
        scratch_shapes=[
            pltpu.VMEM((nk, t), I32),
            pltpu.VMEM((H_IDX, t, LANES), BF16),
            pltpu.VMEM((HKV_B, G_B * t, HD_B), BF16),
            pltpu.VMEM((HKV_B, 1, G_B * t), F32),
            pltpu.VMEM((HKV_B, 1, G_B * t), F32),
            pltpu.VMEM((HKV_B, HD_B, G_B * t), F32),
        ],
        compiler_params=_cparams(("parallel", "arbitrary")),
        name="dsa",
    )(qb, qi, wi_t, kb, vbt, ki2, bias)


def _sigmoid(x):
    return 1.0 / (1.0 + jnp.exp(-x))


def _merge_body(h_ref, oa_ref, ob_ref, wg_ref, woa_ref, wob_ref, o_ref):
    h = h_ref[...]
    ga = jnp.dot(h, wg_ref[:, :D_MODEL], preferred_element_type=F32)
    ya = jnp.dot(oa_ref[...], woa_ref[...], preferred_element_type=F32)
    merged = _sigmoid(ga) * ya
    gb = jnp.dot(h, wg_ref[:, D_MODEL:], preferred_element_type=F32)
    yb = jnp.dot(ob_ref[...], wob_ref[...], preferred_element_type=F32)
    o_ref[...] = (merged + _sigmoid(gb) * yb).astype(o_ref.dtype)


def _merge(h, oa, ob, wg, woa, wob, tm):
    n = h.shape[0]
    return pl.pallas_call(
        _merge_body,
        grid=(n // tm,),
        in_specs=[
            pl.BlockSpec((tm, D_MODEL), lambda i: (i, 0)),
            pl.BlockSpec((tm, H_A * DV_A), lambda i: (i, 0)),
            pl.BlockSpec((tm, H_B * HD_B), lambda i: (i, 0)),
            _resident((D_MODEL, 2 * D_MODEL), lambda i: (0, 0)),
            _resident((H_A * DV_A, D_MODEL), lambda i: (0, 0)),
            _resident((H_B * HD_B, D_MODEL), lambda i: (0, 0)),
        ],
        out_specs=pl.BlockSpec((tm, D_MODEL), lambda i: (i, 0)),
        out_shape=jax.ShapeDtypeStruct((n, D_MODEL), BF16),
        compiler_params=_cparams(("parallel",)),
        name="gated_merge",
    )(h, oa, ob, wg, woa, wob)


def _outproj_body(x_ref, mg_ref, w_ref, g_ref, x1_ref, h2_ref):
    x1 = x_ref[...] + jnp.dot(mg_ref[...], w_ref[...], preferred_element_type=F32)
    x1_ref[...] = x1
    y = x1 * lax.rsqrt(jnp.mean(x1 * x1, axis=-1, keepdims=True) + EPS)
    h2_ref[...] = (y * g_ref[...]).astype(h2_ref.dtype)


def _outproj(x, merged, w_out, g2, tm):
    n = x.shape[0]
    return pl.pallas_call(
        _outproj_body,
        grid=(n // tm,),
        in_specs=[
            pl.BlockSpec((tm, D_MODEL), lambda i: (i, 0)),
            pl.BlockSpec((tm, D_MODEL), lambda i: (i, 0)),
            _resident((D_MODEL, D_MODEL), lambda i: (0, 0)),
            pl.BlockSpec((1, D_MODEL), lambda i: (0, 0)),
        ],
        out_specs=[pl.BlockSpec((tm, D_MODEL), lambda i: (i, 0)), pl.BlockSpec((tm, D_MODEL), lambda i: (i, 0))],
        out_shape=[jax.ShapeDtypeStruct((n, D_MODEL), F32), jax.ShapeDtypeStruct((n, D_MODEL), BF16)],
        compiler_params=_cparams(("parallel",)),
        name="out_proj",
    )(x, merged, w_out, g2.reshape(1, D_MODEL))


def _ffn_body(x1_ref, h2_ref, w1_ref, w2_ref, o_ref):
    f = pl.program_id(1)

    @pl.when(f == 0)
    def _():
        o_ref[...] = x1_ref[...]

    u = jnp.maximum(jnp.dot(h2_ref[...], w1_ref[...], preferred_element_type=F32), 0.0)
    o_ref[...] += jnp.dot((u * u).astype(BF16), w2_ref[...], preferred_element_type=F32)


def _ffn(x1, h2, w1, w2, tm, tf):
    n = x1.shape[0]
    return pl.pallas_call(
        _ffn_body,
        grid=(n // tm, D_FF // tf),
        in_specs=[
            pl.BlockSpec((tm, D_MODEL), lambda i, f: (i, 0)),
            pl.BlockSpec((tm, D_MODEL), lambda i, f: (i, 0)),
            pl.BlockSpec((D_MODEL, tf), lambda i, f: (0, f)),
            pl.BlockSpec((tf, D_MODEL), lambda i, f: (f, 0)),
        ],
        out_specs=pl.BlockSpec((tm, D_MODEL), lambda i, f: (i, 0)),
        out_shape=jax.ShapeDtypeStruct((n, D_MODEL), F32),
        compiler_params=_cparams(("parallel", "arbitrary")),
        name="ffn",
    )(x1, h2, w1, w2)


def _prep_weights(w_in, w_o_a, w_o_b, w_out, w_ff1, w_ff2):
    sizes = (H_A * 2 * HD_A, H_A * 2 * HD_A, H_A * DV_A, H_B * HD_B, HKV_B * HD_B, HKV_B * HD_B,
             H_IDX * D_IDX, D_IDX, H_IDX, D_MODEL, D_MODEL)
    offs = [0]
    for s in sizes:
        offs.append(offs[-1] + s)
    col = lambda a, b: w_in[:, offs[a]:offs[b]]
    w_misc = jnp.concatenate(
        [col(4, 6), col(7, 8), col(7, 8), col(8, 9),
         jnp.zeros((D_MODEL, _MISC_COLS - _WI0 - H_IDX), w_in.dtype)], axis=1)
    return dict(
        qa=col(0, 1).astype(BF16), ka=col(1, 2).astype(BF16), va=col(2, 3).astype(BF16),
        qb=col(3, 4).astype(BF16), misc=w_misc.astype(BF16), qi=col(6, 7).astype(BF16),
        gate=col(9, 11).astype(BF16), oa=w_o_a.astype(BF16), ob=w_o_b.astype(BF16),
        out=w_out.astype(BF16), ff1=w_ff1.astype(BF16), ff2=w_ff2.astype(BF16))


def _layer(x, past, w, p, lam, lam_init, tab_t, *, t_a, t_b, tm):
    b, t, _ = x.shape
    n = b * t
    xf = x.reshape(n, D_MODEL)
    h = _rmsnorm(xf, p["norm1_g"], tm)

    gain_a = lambda g: g.reshape(1, 2 * HD_A)
    (qa,) = _proj(h, w["qa"], [gain_a(p["qn_a_g"])], _epi_qa, [(1024, BF16)], tm, "proj_qa")
    ka, ka_h = _proj(h, w["ka"], [gain_a(p["kn_a_g"])], _epi_ka, [(1024, F32), (1024, BF16)], tm, "proj_ka")
    va, va_h = _proj(h, w["va"], [], _epi_copy2, [(1024, F32), (1024, BF16)], tm, "proj_va")
    (qb,) = _proj(h, w["qb"], [p["qn_b_g"].reshape(1, HD_B)], _epi_qb, [(1024, BF16)], tm, "proj_qb")
    kb, kb_h, vb, vb_h, ki, ki2_h, wi = _proj(
        h, w["misc"], [p["kn_b_g"].reshape(1, HD_B)], _epi_misc,
        [(256, F32), (256, BF16), (256, F32), (256, BF16), (D_IDX, F32), (2 * D_IDX, BF16), (H_IDX, F32)],
        tm, "proj_misc")
    (qi,) = _proj(h, w["qi"], [], _epi_copy1, [(1024, BF16)], tm, "proj_qi")

    new_rows = (ka.reshape(b, t, H_A, 2 * HD_A), va.reshape(b, t, H_A, DV_A), kb.reshape(b, t, HKV_B, HD_B),
                vb.reshape(b, t, HKV_B, HD_B), ki.reshape(b, t, D_IDX))

    def full_keys(new_h, past_arr):
        new_h = new_h.reshape(b, t, -1)
        if past_arr is None:
            return new_h
        past_h = past_arr.reshape(b, past_arr.shape[1], -1).astype(BF16)
        return jnp.concatenate([past_h, new_h], axis=1)

    if past is None:
        pa_k = pa_v = pb_k = pb_v = pb_i = None
    else:
        pa_k, pa_v, pb_k, pb_v, pb_i = past
        pb_i = jnp.concatenate([pb_i, pb_i], axis=-1)
    ka_f, va_f = full_keys(ka_h, pa_k), full_keys(va_h, pa_v)
    kb_f, vb_f, ki_f = full_keys(kb_h, pb_k), full_keys(vb_h, pb_v), full_keys(ki2_h, pb_i)
    n_keys = ka_f.shape[1]
    q0 = n_keys - t
    topk = min(TOPK_MAX, n_keys // 4)
    t_pad = max(t_a, t_b)
    assert t_pad % t_a == 0 and t_pad % t_b == 0 and q0 % t_pad == 0
    nk_pad = -(-n_keys // t_pad) * t_pad
    nq_pad = -(-t // t_pad) * t_pad
    pad_k = lambda a: jnp.pad(a, ((0, 0), (0, nk_pad - n_keys), (0, 0)))
    pad_q = lambda a: jnp.pad(a.reshape(b, t, -1), ((0, 0), (0, nq_pad - t), (0, 0)))
    ka_f, va_f, kb_f, vb_f, ki_f = (pad_k(a) for a in (ka_f, va_f, kb_f, vb_f, ki_f))
    n_valid = lambda tt: n_keys - (n_keys - 1) // tt * tt
    assert all(nq_pad == tt or n_valid(tt) == tt for tt in (t_a, t_b))

    bias_a = _bias_tiles(tab_t[:H_A], t_a, n_valid(t_a))
    oa = _attn_a(pad_q(qa), ka_f, jnp.swapaxes(va_f, 1, 2), bias_a, p["subln_a_g"].reshape(DV_A, 1), lam,
                 t=t_a, q0=q0, out_scale=1.0 - lam_init)
    wi_t = jnp.swapaxes(pad_q(wi), 1, 2)
    bias_b = _bias_tiles(tab_t[H_A:], t_b, n_valid(t_b))
    ob = _dsa(pad_q(qb), pad_q(qi), wi_t, kb_f, jnp.swapaxes(vb_f, 1, 2), ki_f, bias_b,
              t=t_b, q0=q0, topk=topk, n_valid_diag=n_valid(t_b))
    oa = oa[:, :t].reshape(n, H_A * DV_A)
    ob = ob[:, :t].reshape(n, H_B * HD_B)

    merged = _merge(h, oa, ob, w["gate"], w["oa"], w["ob"], min(tm, 256))
    x1, h2 = _outproj(xf, merged, w["out"], p["norm2_g"], min(tm, 256))
    y = _ffn(x1, h2, w["ff1"], w["ff2"], tm, 512)
    return y.reshape(b, t, D_MODEL), new_rows


def kernel(x_prompt, x_sample, cache_a_k, cache_a_v, cache_b_k, cache_b_v, cache_b_kidx, rel_bias, norm1_g, w_in, qn_a_g, kn_a_g, lam_q1, lam_k1, lam_q2, lam_k2, subln_a_g, qn_b_g, kn_b_g, w_o_a, w_o_b, w_out, norm2_g, w_ff1, w_ff2):
    depth = w_in.shape[0]
    tab_t = rel_bias.T.astype(F32)
    y_prompt, y_sample = x_prompt, x_sample
    prompt_rows, sample_rows = [], []
    for l in range(depth):
        lam_init = 0.8 - 0.6 * math.exp(-0.3 * l)
        lam = (jnp.exp(jnp.sum(lam_q1[l].astype(F32) * lam_k1[l].astype(F32)))
               - jnp.exp(jnp.sum(lam_q2[l].astype(F32) * lam_k2[l].astype(F32))) + lam_init).reshape(1)
        w = _prep_weights(w_in[l], w_o_a[l], w_o_b[l], w_out[l], w_ff1[l], w_ff2[l])
        p = dict(norm1_g=norm1_g[l], qn_a_g=qn_a_g[l], kn_a_g=kn_a_g[l], subln_a_g=subln_a_g[l],
                 qn_b_g=qn_b_g[l], kn_b_g=kn_b_g[l], norm2_g=norm2_g[l])
        y_prompt, rp = _layer(y_prompt, None, w, p, lam, lam_init, tab_t, t_a=512, t_b=256, tm=512)
        past = (cache_a_k[l], cache_a_v[l], cache_b_k[l], cache_b_v[l], cache_b_kidx[l])
        y_sample, rs = _layer(y_sample, past, w, p, lam, lam_init, tab_t, t_a=128, t_b=128, tm=512)
        prompt_rows.append(rp)
        sample_rows.append(rs)
    p_rows = tuple(jnp.stack(r, axis=0) for r in zip(*prompt_rows))
    s_rows = tuple(jnp.stack(r, axis=0) for r in zip(*sample_rows))
    return (y_prompt, y_sample) + p_rows + s_rows
```

```python
import functools
import math

import jax
import jax.numpy as jnp
from jax import lax
from jax.experimental import pallas as pl
from jax.experimental.pallas import tpu as pltpu

F32 = jnp.float32
BF16 = jnp.bfloat16
I32 = jnp.int32

D_MODEL = 2048
CHUNK = 64
H_A = 8
DV_A = 128
HD_A = 64
H_B = 8
HD_B = 128
HKV_B = 2
G_B = H_B // HKV_B
H_IDX = 16
D_IDX = 64
TOPK_MAX = 256
N_BUCKETS = 32
D_FF = 4 * D_MODEL
EPS = 1e-6
LANES = 128
NEG = -1e30
LOG2E = math.log2(math.e)
INT_MIN = -(2 ** 31)
INT_MAX = 2 ** 31 - 1
VMEM_LIMIT = 56 * 1024 * 1024

_NT = (((1,), (1,)), ((), ()))


def _cparams(sem):
    return pltpu.CompilerParams(dimension_semantics=sem, vmem_limit_bytes=VMEM_LIMIT)


def _resident(block_shape, index_map):
    return pl.BlockSpec(block_shape, index_map, pipeline_mode=pl.Buffered(1))


def _rmsnorm_body(x_ref, g_ref, o_ref):
    x = x_ref[...]
    y = x * lax.rsqrt(jnp.mean(x * x, axis=-1, keepdims=True) + EPS)
    o_ref[...] = (y * g_ref[...]).astype(o_ref.dtype)


def _rmsnorm(x, g, tm):
    n, d = x.shape
    return pl.pallas_call(
        _rmsnorm_body,
        grid=(n // tm,),
        in_specs=[pl.BlockSpec((tm, d), lambda i: (i, 0)), pl.BlockSpec((1, d), lambda i: (0, 0))],
        out_specs=pl.BlockSpec((tm, d), lambda i: (i, 0)),
        out_shape=jax.ShapeDtypeStruct((n, d), BF16),
        compiler_params=_cparams(("parallel",)),
        name="rmsnorm",
    )(x, g.reshape(1, d))


def _head_norm(z, gain, split):
    lane = lax.broadcasted_iota(I32, (1, LANES), 1)
    lo = lane < (LANES // 2)
    outs = []
    for h in range(z.shape[1] // LANES):
        zh = z[:, h * LANES:(h + 1) * LANES]
        sq = zh * zh
        if split:
            s_lo = jnp.sum(jnp.where(lo, sq, 0.0), axis=-1, keepdims=True)
            s_hi = jnp.sum(jnp.where(lo, 0.0, sq), axis=-1, keepdims=True)
            r = jnp.where(lo, lax.rsqrt(s_lo * (2.0 / LANES) + EPS), lax.rsqrt(s_hi * (2.0 / LANES) + EPS))
        else:
            r = lax.rsqrt(jnp.mean(sq, axis=-1, keepdims=True) + EPS)
        outs.append(zh * r * gain)
    return jnp.concatenate(outs, axis=-1)


def _proj_body(epilogue, n_aux, h_ref, w_ref, *rest):
    z = jnp.dot(h_ref[...], w_ref[...], preferred_element_type=F32)
    outs = epilogue(z, *[r[...] for r in rest[:n_aux]])
    for o_ref, o in zip(rest[n_aux:], outs):
        o_ref[...] = o.astype(o_ref.dtype)


def _proj(h, w, aux, epilogue, out_defs, tm, name):
    n, k = h.shape
    c = w.shape[1]
    in_specs = [pl.BlockSpec((tm, k), lambda i: (i, 0)), _resident((k, c), lambda i: (0, 0))]
    in_specs += [pl.BlockSpec(a.shape, lambda i: (0, 0)) for a in aux]
    return pl.pallas_call(
        functools.partial(_proj_body, epilogue, len(aux)),
        grid=(n // tm,),
        in_specs=in_specs,
        out_specs=[pl.BlockSpec((tm, oc), lambda i: (i, 0)) for oc, _ in out_defs],
        out_shape=[jax.ShapeDtypeStruct((n, oc), dt) for oc, dt in out_defs],
        compiler_params=_cparams(("parallel",)),
        name=name,
    )(h, w, *aux)


def _epi_qa(z, gain):
    return (_head_norm(z, gain, True) * (HD_A ** -0.5 * LOG2E),)


def _epi_ka(z, gain):
    y = _head_norm(z, gain, True)
    return y, y


def _epi_copy2(z):
    return z, z


def _epi_qb(z, gain):
    return (_head_norm(z, gain, False) * (HD_B ** -0.5 * LOG2E),)


def _epi_copy1(z):
    return (z,)


_KB0, _VB0, _KI0, _WI0, _MISC_COLS = 0, 256, 512, 640, 768


def _epi_misc(z, gain):
    kb = _head_norm(z[:, _KB0:_VB0], gain, False)
    vb = z[:, _VB0:_KI0]
    ki2 = z[:, _KI0:_WI0]
    return kb, kb, vb, vb, ki2[:, :D_IDX], ki2, z[:, _WI0:_WI0 + H_IDX]


def _bias_body(t, n_valid_diag, tab_ref, o_ref):
    h = pl.program_id(0)
    half = N_BUCKETS // 2
    exact = half // 2
    kl = lax.broadcasted_iota(I32, (t, t), 0)
    ql = lax.broadcasted_iota(I32, (t, t), 1)
    for tile, off in ((0, -t), (1, 0)):
        rel = kl + off - ql
        n = jnp.abs(rel)
        n2 = n * n
        v_neg = jnp.full((t, t), tab_ref[h, 0], F32)
        v_pos = jnp.full((t, t), tab_ref[h, half], F32)
        for b in range(1, half):
            cond = (n >= b) if b < exact else (n2 >= exact * exact * 2 ** (b - exact))
            v_neg = jnp.where(cond, tab_ref[h, b], v_neg)
            v_pos = jnp.where(cond, tab_ref[h, half + b], v_pos)
        val = (jnp.where(rel > 0, v_pos, v_neg) - tab_ref[h, half - 1]) * LOG2E
        if tile == 1:
            visible = ((kl // CHUNK) <= (ql // CHUNK)) & (kl < n_valid_diag)
            val = jnp.where(visible, val, NEG)
        o_ref[tile] = val


def _bias_tiles(tab_t, t, n_valid_diag):
    nh = tab_t.shape[0]
    return pl.pallas_call(
        functools.partial(_bias_body, t, n_valid_diag),
        grid=(nh,),
        in_specs=[pl.BlockSpec(memory_space=pltpu.SMEM)],
        out_specs=pl.BlockSpec((None, 2, t, t), lambda h: (h, 0, 0, 0)),
        out_shape=jax.ShapeDtypeStruct((nh, 2, t, t), F32),
        compiler_params=_cparams(("arbitrary",)),
        name="bias_tiles",
    )(tab_t)


def _softmax_step(s, vt, m_ref, l_ref, acc_ref):
    m_old = m_ref[...]
    m_new = jnp.maximum(m_old, jnp.max(s, axis=0, keepdims=True))
    alpha = jnp.exp2(m_old - m_new)
    p = jnp.exp2(s - m_new)
    l_ref[...] = alpha * l_ref[...] + jnp.sum(p, axis=0, keepdims=True)
    acc_ref[...] = acc_ref[...] * alpha + jnp.dot(vt, p.astype(BF16), preferred_element_type=F32)
    m_ref[...] = m_new


def _pipelined_tiles(n_far, produce, consume, buf0, buf1):
    produce(0, buf0)
    pairs = jnp.maximum(n_far, 0) // 2

    def body(i, carry):
        j = 2 * i
        produce(j + 1, buf1)
        consume(j, buf0, None)
        produce(j + 2, buf0)
        consume(j + 1, buf1, None)
        return carry

    lax.fori_loop(0, pairs, body, 0)
    done = 2 * pairs
    rem = n_far - done

    @pl.when(rem == -1)
    def _():
        consume(done, buf0, 1)

    @pl.when(rem == 0)
    def _():
        produce(done + 1, buf1)
        consume(done, buf0, 0)
        consume(done + 1, buf1, 1)

    @pl.when(rem == 1)
    def _():
        produce(done + 1, buf1)
        consume(done, buf0, None)
        produce(done + 2, buf0)
        consume(done + 1, buf1, 0)
        consume(done + 2, buf0, 1)


def _attn_a_body(t, q0, out_scale, lam_ref, q_ref, k_ref, vt_ref, bias_ref, g_ref, o_ref,
                 qz_ref, m_ref, l_ref, acc_ref, s0_ref, s1_ref):
    i = pl.program_id(2)
    n_far = q0 // t + i - 1
    q = q_ref[...]
    lane = lax.broadcasted_iota(I32, (t, LANES), 1)
    zero = jnp.zeros_like(q)
    qz_ref[:t] = jnp.where(lane < HD_A, q, zero)
    qz_ref[t:] = jnp.where(lane < HD_A, zero, q)
    m_ref[...] = jnp.full(m_ref.shape, NEG, F32)
    l_ref[...] = jnp.zeros(l_ref.shape, F32)
    acc_ref[...] = jnp.zeros(acc_ref.shape, F32)

    def produce(j, s_ref):
        ks = pl.multiple_of(j * t, t)
        s_ref[...] = lax.dot_general(k_ref[pl.ds(ks, t), :], qz_ref[...], _NT, preferred_element_type=F32)

    def consume(j, s_ref, kind):
        s = s_ref[...]
        if kind is not None:
            b = bias_ref[kind]
            s = jnp.concatenate([s[:, :t] + b, s[:, t:] + b], axis=1)
        _softmax_step(s, vt_ref[:, pl.ds(pl.multiple_of(j * t, t), t)], m_ref, l_ref, acc_ref)

    _pipelined_tiles(n_far, produce, consume, s0_ref, s1_ref)

    lam = lam_ref[0]
    o = acc_ref[...] / l_ref[...]
    o = o[:, :t] - lam * o[:, t:]
    y = o * lax.rsqrt(jnp.mean(o * o, axis=0, keepdims=True) + EPS)
    y = (y * g_ref[...]) * out_scale
    o_ref[...] = y.T.astype(o_ref.dtype)


def _attn_a(q, k, vt, bias, subln_g, lam, *, t, q0, out_scale):
    b, nq, _ = q.shape
    nk = k.shape[1]
    return pl.pallas_call(
        functools.partial(_attn_a_body, t, q0, out_scale),
        grid=(b, H_A, nq // t),
        in_specs=[
            pl.BlockSpec(memory_space=pltpu.SMEM),
            pl.BlockSpec((None, t, LANES), lambda bb, h, i: (bb, i, h)),
            pl.BlockSpec((None, nk, LANES), lambda bb, h, i: (bb, 0, h)),
            pl.BlockSpec((None, DV_A, nk), lambda bb, h, i: (bb, h, 0)),
            pl.BlockSpec((None, 2, t, t), lambda bb, h, i: (h, 0, 0, 0)),
            pl.BlockSpec((DV_A, 1), lambda bb, h, i: (0, 0)),
        ],
        out_specs=pl.BlockSpec((None, t, LANES), lambda bb, h, i: (bb, i, h)),
        out_shape=jax.ShapeDtypeStruct((b, nq, H_A * DV_A), BF16),
        scratch_shapes=[
            pltpu.VMEM((2 * t, LANES), BF16),
            pltpu.VMEM((1, 2 * t), F32),
            pltpu.VMEM((1, 2 * t), F32),
            pltpu.VMEM((DV_A, 2 * t), F32),
            pltpu.VMEM((t, 2 * t), F32),
            pltpu.VMEM((t, 2 * t), F32),
        ],
        compiler_params=_cparams(("parallel", "parallel", "arbitrary")),
        name="diff_attn",
    )(lam, q, k, vt, bias, subln_g)


def _order_key(x):
    b = pltpu.bitcast(x, I32)
    return jnp.where(b < 0, b ^ INT_MAX, b)


def _dsa_body(t, q0, topk, n_valid_diag, qb_ref, qi_ref, wi_ref, kb_ref, vbt_ref, ki_ref, bias_ref, o_ref,
              keys_ref, qz_ref, qs_ref, m_ref, l_ref, acc_ref, s0_ref, s1_ref):
    i = pl.program_id(1)
    n_tiles = q0 // t + i + 1
    n_far = n_tiles - 2

    lane = lax.broadcasted_iota(I32, (t, LANES), 1)
    for hp in range(H_IDX // 2):
        qs = qi_ref[:, hp * LANES:(hp + 1) * LANES]
        zero = jnp.zeros_like(qs)
        qz_ref[2 * hp] = jnp.where(lane < D_IDX, qs, zero)
        qz_ref[2 * hp + 1] = jnp.where(lane < D_IDX, zero, qs)

    def score_tile(j):
        kt = ki_ref[pl.ds(pl.multiple_of(j * t, t), t), :]
        acc = jnp.zeros((t, t), F32)
        for h in range(H_IDX):
            s = lax.dot_general(kt, qz_ref[h], _NT, preferred_element_type=F32)
            acc = acc + jnp.maximum(s, 0.0) * wi_ref[h:h + 1, :]
        return acc

    def score_body(j, carry):
        keys_ref[pl.ds(pl.multiple_of(j * t, t), t), :] = _order_key(score_tile(j))
        return carry

    lax.fori_loop(0, n_tiles - 1, score_body, 0)
    jd = n_tiles - 1
    kl = lax.broadcasted_iota(I32, (t, t), 0)
    ql = lax.broadcasted_iota(I32, (t, t), 1)
    admissible = ((kl // CHUNK) <= (ql // CHUNK)) & (kl < n_valid_diag)
    keys_ref[pl.ds(pl.multiple_of(jd * t, t), t), :] = jnp.where(admissible, _order_key(score_tile(jd)), INT_MIN)

    def count_ge(thr):
        def body(j, c):
            kk = keys_ref[pl.ds(pl.multiple_of(j * t, t), t), :]
            hit = jnp.where(kk >= thr, 1.0, 0.0)
            return c + jnp.sum(hit.reshape(t // 8, 8, t), axis=0)
        c8 = lax.fori_loop(0, n_tiles, body, jnp.zeros((8, t), F32))
        return jnp.sum(c8, axis=0, keepdims=True)

    def bisect_cond(state):
        it, _, _, active = state
        return jnp.logical_and(it < 34, active > 0)

    def bisect_body(state):
        it, lo, hi, _ = state
        mid = (lo >> 1) + (hi >> 1) + (lo & hi & 1)
        c = count_ge(mid)
        ge = c >= topk
        exact = c == topk
        lo2 = jnp.where(ge, mid, lo)
        hi2 = jnp.where(exact, mid + 1, jnp.where(ge, hi, mid))
        active = jnp.max(jnp.where(lo2 + 1 < hi2, 1, 0))
        return it + 1, lo2, hi2, active

    lo0 = jnp.full((1, t), INT_MIN, I32)
    hi0 = jnp.full((1, t), INT_MAX, I32)
    _, thr, _, _ = lax.while_loop(bisect_cond, bisect_body, (jnp.int32(0), lo0, hi0, jnp.int32(1)))

    m_ref[...] = jnp.full(m_ref.shape, NEG, F32)
    l_ref[...] = jnp.zeros(l_ref.shape, F32)
    acc_ref[...] = jnp.zeros(acc_ref.shape, F32)

    for h in range(H_B):
        qs_ref[h // G_B, (h % G_B) * t:(h % G_B + 1) * t, :] = qb_ref[:, h * HD_B:(h + 1) * HD_B]

    def produce(j, s_ref):
        ks = pl.multiple_of(j * t, t)
        for n in range(HKV_B):
            s_ref[n] = lax.dot_general(kb_ref[pl.ds(ks, t), n * HD_B:(n + 1) * HD_B], qs_ref[n], _NT,
                                       preferred_element_type=F32)

    def consume(j, s_ref, kind):
        ks = pl.multiple_of(j * t, t)
        sel = keys_ref[pl.ds(ks, t), :] >= thr
        for n in range(HKV_B):
            s = s_ref[n]
            parts = []
            for g in range(G_B):
                sg = s[:, g * t:(g + 1) * t]
                if kind is not None:
                    sg = sg + bias_ref[n * G_B + g, kind]
                parts.append(jnp.where(sel, sg, NEG))
            _softmax_step(jnp.concatenate(parts, axis=1), vbt_ref[n * HD_B:(n + 1) * HD_B, pl.ds(ks, t)],
                          m_ref.at[n], l_ref.at[n], acc_ref.at[n])

    _pipelined_tiles(n_far, produce, consume, s0_ref, s1_ref)

    for n in range(HKV_B):
        o = acc_ref[n] / l_ref[n]
        for g in range(G_B):
            h = n * G_B + g
            o_ref[:, h * HD_B:(h + 1) * HD_B] = o[:, g * t:(g + 1) * t].T.astype(o_ref.dtype)


def _dsa(qb, qi, wi_t, kb, vbt, ki2, bias, *, t, q0, topk, n_valid_diag):
    b, nq, _ = qb.shape
    nk = kb.shape[1]
    return pl.pallas_call(
        functools.partial(_dsa_body, t, q0, topk, n_valid_diag),
        grid=(b, nq // t),
        in_specs=[
            pl.BlockSpec((None, t, H_B * HD_B), lambda bb, i: (bb, i, 0)),
            pl.BlockSpec((None, t, H_IDX * D_IDX), lambda bb, i: (bb, i, 0)),
            pl.BlockSpec((None, H_IDX, t), lambda bb, i: (bb, 0, i)),
            _resident((None, nk, HKV_B * HD_B), lambda bb, i: (bb, 0, 0)),
            _resident((None, HKV_B * HD_B, nk), lambda bb, i: (bb, 0, 0)),
            _resident((None, nk, LANES), lambda bb, i: (bb, 0, 0)),
            _resident((H_B, 2, t, t), lambda bb, i: (0, 0, 0, 0)),
        ],
        out_specs=pl.BlockSpec((None, t, H_B * HD_B), lambda bb, i: (bb, i, 0)),
        out_shape=jax.ShapeDtypeStruct((b, nq, H_B * HD_B), BF16),
        scratch_shapes=[
            pltpu.VMEM((nk, t), I32),
            pltpu.VMEM((H_IDX, t, LANES), BF16),
            pltpu.VMEM((HKV_B, G_B * t, HD_B), BF16),
            pltpu.VMEM((HKV_B, 1, G_B * t), F32),
            pltpu.VMEM((HKV_B, 1, G_B * t), F32),
            pltpu.VMEM((HKV_B, HD_B, G_B * t), F32),
            pltpu.VMEM((HKV_B, t, G_B * t), F32),
            pltpu.VMEM((HKV_B, t, G_B * t), F32),
        ],
        compiler_params=_cparams(("parallel", "arbitrary")),
        name="dsa",
    )(qb, qi, wi_t, kb, vbt, ki2, bias)


def _sigmoid(x):
    return 1.0 / (1.0 + jnp.exp(-x))


def _merge_body(h_ref, oa_ref, ob_ref, wg_ref, woa_ref, wob_ref, o_ref):
    h = h_ref[...]
    ga = jnp.dot(h, wg_ref[:, :D_MODEL], preferred_element_type=F32)
    ya = jnp.dot(oa_ref[...], woa_ref[...], preferred_element_type=F32)
    merged = _sigmoid(ga) * ya
    gb = jnp.dot(h, wg_ref[:, D_MODEL:], preferred_element_type=F32)
    yb = jnp.dot(ob_ref[...], wob_ref[...], preferred_element_type=F32)
    o_ref[...] = (merged + _sigmoid(gb) * yb).astype(o_ref.dtype)


def _merge(h, oa, ob, wg, woa, wob, tm):
    n = h.shape[0]
    return pl.pallas_call(
        _merge_body,
        grid=(n // tm,),
        in_specs=[
            pl.BlockSpec((tm, D_MODEL), lambda i: (i, 0)),
            pl.BlockSpec((tm, H_A * DV_A), lambda i: (i, 0)),
            pl.BlockSpec((tm, H_B * HD_B), lambda i: (i, 0)),
            _resident((D_MODEL, 2 * D_MODEL), lambda i: (0, 0)),
            _resident((H_A * DV_A, D_MODEL), lambda i: (0, 0)),
            _resident((H_B * HD_B, D_MODEL), lambda i: (0, 0)),
        ],
        out_specs=pl.BlockSpec((tm, D_MODEL), lambda i: (i, 0)),
        out_shape=jax.ShapeDtypeStruct((n, D_MODEL), BF16),
        compiler_params=_cparams(("parallel",)),
        name="gated_merge",
    )(h, oa, ob, wg, woa, wob)


def _outproj_body(x_ref, mg_ref, w_ref, g_ref, x1_ref, h2_ref):
    x1 = x_ref[...] + jnp.dot(mg_ref[...], w_ref[...], preferred_element_type=F32)
    x1_ref[...] = x1
    y = x1 * lax.rsqrt(jnp.mean(x1 * x1, axis=-1, keepdims=True) + EPS)
    h2_ref[...] = (y * g_ref[...]).astype(h2_ref.dtype)


def _outproj(x, merged, w_out, g2, tm):
    n = x.shape[0]
    return pl.pallas_call(
        _outproj_body,
        grid=(n // tm,),
        in_specs=[
            pl.BlockSpec((tm, D_MODEL), lambda i: (i, 0)),
            pl.BlockSpec((tm, D_MODEL), lambda i: (i, 0)),
            _resident((D_MODEL, D_MODEL), lambda i: (0, 0)),
            pl.BlockSpec((1, D_MODEL), lambda i: (0, 0)),
        ],
        out_specs=[pl.BlockSpec((tm, D_MODEL), lambda i: (i, 0)), pl.BlockSpec((tm, D_MODEL), lambda i: (i, 0))],
        out_shape=[jax.ShapeDtypeStruct((n, D_MODEL), F32), jax.ShapeDtypeStruct((n, D_MODEL), BF16)],
        compiler_params=_cparams(("parallel",)),
        name="out_proj",
    )(x, merged, w_out, g2.reshape(1, D_MODEL))


def _ffn_body(x1_ref, h2_ref, w1_ref, w2_ref, o_ref):
    f = pl.program_id(1)

    @pl.when(f == 0)
    def _():
        o_ref[...] = x1_ref[...]

    u = jnp.maximum(jnp.dot(h2_ref[...], w1_ref[...], preferred_element_type=F32), 0.0)
    o_ref[...] += jnp.dot((u * u).astype(BF16), w2_ref[...], preferred_element_type=F32)


def _ffn(x1, h2, w1, w2, tm, tf):
    n = x1.shape[0]
    return pl.pallas_call(
        _ffn_body,
        grid=(n // tm, D_FF // tf),
        in_specs=[
            pl.BlockSpec((tm, D_MODEL), lambda i, f: (i, 0)),
            pl.BlockSpec((tm, D_MODEL), lambda i, f: (i, 0)),
            pl.BlockSpec((D_MODEL, tf), lambda i, f: (0, f)),
            pl.BlockSpec((tf, D_MODEL), lambda i, f: (f, 0)),
        ],
        out_specs=pl.BlockSpec((tm, D_MODEL), lambda i, f: (i, 0)),
        out_shape=jax.ShapeDtypeStruct((n, D_MODEL), F32),
        compiler_params=_cparams(("parallel", "arbitrary")),
        name="ffn",
    )(x1, h2, w1, w2)


def _prep_weights(w_in, w_o_a, w_o_b, w_out, w_ff1, w_ff2):
    sizes = (H_A * 2 * HD_A, H_A * 2 * HD_A, H_A * DV_A, H_B * HD_B, HKV_B * HD_B, HKV_B * HD_B,
             H_IDX * D_IDX, D_IDX, H_IDX, D_MODEL, D_MODEL)
    offs = [0]
    for s in sizes:
        offs.append(offs[-1] + s)
    col = lambda a, b: w_in[:, offs[a]:offs[b]]
    w_misc = jnp.concatenate(
        [col(4, 6), col(7, 8), col(7, 8), col(8, 9),
         jnp.zeros((D_MODEL, _MISC_COLS - _WI0 - H_IDX), w_in.dtype)], axis=1)
    return dict(
        qa=col(0, 1).astype(BF16), ka=col(1, 2).astype(BF16), va=col(2, 3).astype(BF16),
        qb=col(3, 4).astype(BF16), misc=w_misc.astype(BF16), qi=col(6, 7).astype(BF16),
        gate=col(9, 11).astype(BF16), oa=w_o_a.astype(BF16), ob=w_o_b.astype(BF16),
        out=w_out.astype(BF16), ff1=w_ff1.astype(BF16), ff2=w_ff2.astype(BF16))


def _layer(x, past, w, p, lam, lam_init, tab_t, *, t_a, t_b, tm):
    b, t, _ = x.shape
    n = b * t
    xf = x.reshape(n, D_MODEL)
    h = _rmsnorm(xf, p["norm1_g"], tm)

    gain_a = lambda g: g.reshape(1, 2 * HD_A)
    (qa,) = _proj(h, w["qa"], [gain_a(p["qn_a_g"])], _epi_qa, [(1024, BF16)], tm, "proj_qa")
    ka, ka_h = _proj(h, w["ka"], [gain_a(p["kn_a_g"])], _epi_ka, [(1024, F32), (1024, BF16)], tm, "proj_ka")
    va, va_h = _proj(h, w["va"], [], _epi_copy2, [(1024, F32), (1024, BF16)], tm, "proj_va")
    (qb,) = _proj(h, w["qb"], [p["qn_b_g"].reshape(1, HD_B)], _epi_qb, [(1024, BF16)], tm, "proj_qb")
    kb, kb_h, vb, vb_h, ki, ki2_h, wi = _proj(
        h, w["misc"], [p["kn_b_g"].reshape(1, HD_B)], _epi_misc,
        [(256, F32), (256, BF16), (256, F32), (256, BF16), (D_IDX, F32), (2 * D_IDX, BF16), (H_IDX, F32)],
        tm, "proj_misc")
    (qi,) = _proj(h, w["qi"], [], _epi_copy1, [(1024, BF16)], tm, "proj_qi")

    new_rows = (ka.reshape(b, t, H_A, 2 * HD_A), va.reshape(b, t, H_A, DV_A), kb.reshape(b, t, HKV_B, HD_B),
                vb.reshape(b, t, HKV_B, HD_B), ki.reshape(b, t, D_IDX))

    def full_keys(new_h, past_arr):
        new_h = new_h.reshape(b, t, -1)
        if past_arr is None:
            return new_h
        past_h = past_arr.reshape(b, past_arr.shape[1], -1).astype(BF16)
        return jnp.concatenate([past_h, new_h], axis=1)

    if past is None:
        pa_k = pa_v = pb_k = pb_v = pb_i = None
    else:
        pa_k, pa_v, pb_k, pb_v, pb_i = past
        pb_i = jnp.concatenate([pb_i, pb_i], axis=-1)
    ka_f, va_f = full_keys(ka_h, pa_k), full_keys(va_h, pa_v)
    kb_f, vb_f, ki_f = full_keys(kb_h, pb_k), full_keys(vb_h, pb_v), full_keys(ki2_h, pb_i)
    n_keys = ka_f.shape[1]
    q0 = n_keys - t
    topk = min(TOPK_MAX, n_keys // 4)
    t_pad = max(t_a, t_b)
    assert t_pad % t_a == 0 and t_pad % t_b == 0 and q0 % t_pad == 0
    nk_pad = -(-n_keys // t_pad) * t_pad
    nq_pad = -(-t // t_pad) * t_pad
    pad_k = lambda a: jnp.pad(a, ((0, 0), (0, nk_pad - n_keys), (0, 0)))
    pad_q = lambda a: jnp.pad(a.reshape(b, t, -1), ((0, 0), (0, nq_pad - t), (0, 0)))
    ka_f, va_f, kb_f, vb_f, ki_f = (pad_k(a) for a in (ka_f, va_f, kb_f, vb_f, ki_f))
    n_valid = lambda tt: n_keys - (n_keys - 1) // tt * tt
    assert all(nq_pad == tt or n_valid(tt) == tt for tt in (t_a, t_b))

    bias_a = _bias_tiles(tab_t[:H_A], t_a, n_valid(t_a))
    oa = _attn_a(pad_q(qa), ka_f, jnp.swapaxes(va_f, 1, 2), bias_a, p["subln_a_g"].reshape(DV_A, 1), lam,
                 t=t_a, q0=q0, out_scale=1.0 - lam_init)
    wi_t = jnp.swapaxes(pad_q(wi), 1, 2)
    bias_b = _bias_tiles(tab_t[H_A:], t_b, n_valid(t_b))
    ob = _dsa(pad_q(qb), pad_q(qi), wi_t, kb_f, jnp.swapaxes(vb_f, 1, 2), ki_f, bias_b,
              t=t_b, q0=q0, topk=topk, n_valid_diag=n_valid(t_b))
    oa = oa[:, :t].reshape(n, H_A * DV_A)
    ob = ob[:, :t].reshape(n, H_B * HD_B)

    merged = _merge(h, oa, ob, w["gate"], w["oa"], w["ob"], min(tm, 256))
    x1, h2 = _outproj(xf, merged, w["out"], p["norm2_g"], min(tm, 256))
    y = _ffn(x1, h2, w["ff1"], w["ff2"], tm, 512)
    return y.reshape(b, t, D_MODEL), new_rows


def kernel(x_prompt, x_sample, cache_a_k, cache_a_v, cache_b_k, cache_b_v, cache_b_kidx, rel_bias, norm1_g, w_in, qn_a_g, kn_a_g, lam_q1, lam_k1, lam_q2, lam_k2, subln_a_g, qn_b_g, kn_b_g, w_o_a, w_o_b, w_out, norm2_g, w_ff1, w_ff2):
    depth = w_in.shape[0]
    tab_t = rel_bias.T.astype(F32)
    y_prompt, y_sample = x_prompt, x_sample
    prompt_rows, sample_rows = [], []
    for l in range(depth):
        lam_init = 0.8 - 0.6 * math.exp(-0.3 * l)
        lam = (jnp.exp(jnp.sum(lam_q1[l].astype(F32) * lam_k1[l].astype(F32)))
               - jnp.exp(jnp.sum(lam_q2[l].astype(F32) * lam_k2[l].astype(F32))) + lam_init).reshape(1)
        w = _prep_weights(w_in[l], w_o_a[l], w_o_b[l], w_out[l], w_ff1[l], w_ff2[l])
        p = dict(norm1_g=norm1_g[l], qn_a_g=qn_a_g[l], kn_a_g=kn_a_g[l], subln_a_g=subln_a_g[l],
                 qn_b_g=qn_b_g[l], kn_b_g=kn_b_g[l], norm2_g=norm2_g[l])
        y_prompt, rp = _layer(y_prompt, None, w, p, lam, lam_init, tab_t, t_a=512, t_b=256, tm=512)
        past = (cache_a_k[l], cache_a_v[l], cache_b_k[l], cache_b_v[l], cache_b_kidx[l])
        y_sample, rs = _layer(y_sample, past, w, p, lam, lam_init, tab_t, t_a=128, t_b=128, tm=512)
        prompt_rows.append(rp)
        sample_rows.append(rs)
    p_rows = tuple(jnp.stack(r, axis=0) for r in zip(*prompt_rows))
    s_rows = tuple(jnp.stack(r, axis=0) for r in zip(*sample_rows))
    return (y_prompt, y_sample) + p_rows + s_rows
```

```python
import functools
import math

import jax
import jax.numpy as jnp
from jax import lax
from jax.experimental import pallas as pl
from jax.experimental.pallas import tpu as pltpu

F32 = jnp.float32
BF16 = jnp.bfloat16
I32 = jnp.int32

D_MODEL = 2048
CHUNK = 64
H_A = 8
DV_A = 128
HD_A = 64
H_B = 8
HD_B = 128
HKV_B = 2
G_B = H_B // HKV_B
H_IDX = 16
D_IDX = 64
TOPK_MAX = 256
N_BUCKETS = 32
D_FF = 4 * D_MODEL
EPS = 1e-6
LANES = 128
NEG = -1e30
LOG2E = math.log2(math.e)
INT_MIN = -(2 ** 31)
INT_MAX = 2 ** 31 - 1
VMEM_LIMIT = 56 * 1024 * 1024

_NT = (((1,), (1,)), ((), ()))


def _cparams(sem):
    return pltpu.CompilerParams(dimension_semantics=sem, vmem_limit_bytes=VMEM_LIMIT)


def _resident(block_shape, index_map):
    return pl.BlockSpec(block_shape, index_map, pipeline_mode=pl.Buffered(1))


def _rmsnorm_body(x_ref, g_ref, o_ref):
    x = x_ref[...]
    y = x * lax.rsqrt(jnp.mean(x * x, axis=-1, keepdims=True) + EPS)
    o_ref[...] = (y * g_ref[...]).astype(o_ref.dtype)


def _rmsnorm(x, g, tm):
    n, d = x.shape
    return pl.pallas_call(
        _rmsnorm_body,
        grid=(n // tm,),
        in_specs=[pl.BlockSpec((tm, d), lambda i: (i, 0)), pl.BlockSpec((1, d), lambda i: (0, 0))],
        out_specs=pl.BlockSpec((tm, d), lambda i: (i, 0)),
        out_shape=jax.ShapeDtypeStruct((n, d), BF16),
        compiler_params=_cparams(("parallel",)),
        name="rmsnorm",
    )(x, g.reshape(1, d))


def _head_norm(z, gain, split):
    lane = lax.broadcasted_iota(I32, (1, LANES), 1)
    lo = lane < (LANES // 2)
    outs = []
    for h in range(z.shape[1] // LANES):
        zh = z[:, h * LANES:(h + 1) * LANES]
        sq = zh * zh
        if split:
            s_lo = jnp.sum(jnp.where(lo, sq, 0.0), axis=-1, keepdims=True)
            s_hi = jnp.sum(jnp.where(lo, 0.0, sq), axis=-1, keepdims=True)
            r = jnp.where(lo, lax.rsqrt(s_lo * (2.0 / LANES) + EPS), lax.rsqrt(s_hi * (2.0 / LANES) + EPS))
        else:
            r = lax.rsqrt(jnp.mean(sq, axis=-1, keepdims=True) + EPS)
        outs.append(zh * r * gain)
    return jnp.concatenate(outs, axis=-1)


def _proj_body(epilogue, n_aux, h_ref, w_ref, *rest):
    z = jnp.dot(h_ref[...], w_ref[...], preferred_element_type=F32)
    outs = epilogue(z, *[r[...] for r in rest[:n_aux]])
    for o_ref, o in zip(rest[n_aux:], outs):
        o_ref[...] = o.astype(o_ref.dtype)


def _proj(h, w, aux, epilogue, out_defs, tm, name):
    n, k = h.shape
    c = w.shape[1]
    in_specs = [pl.BlockSpec((tm, k), lambda i: (i, 0)), _resident((k, c), lambda i: (0, 0))]
    in_specs += [pl.BlockSpec(a.shape, lambda i: (0, 0)) for a in aux]
    return pl.pallas_call(
        functools.partial(_proj_body, epilogue, len(aux)),
        grid=(n // tm,),
        in_specs=in_specs,
        out_specs=[pl.BlockSpec((tm, oc), lambda i: (i, 0)) for oc, _ in out_defs],
        out_shape=[jax.ShapeDtypeStruct((n, oc), dt) for oc, dt in out_defs],
        compiler_params=_cparams(("parallel",)),
        name=name,
    )(h, w, *aux)


def _epi_qa(z, gain):
    return (_head_norm(z, gain, True) * (HD_A ** -0.5 * LOG2E),)


def _epi_ka(z, gain):
    y = _head_norm(z, gain, True)
    return y, y


def _epi_copy2(z):
    return z, z


def _epi_qb(z, gain):
    return (_head_norm(z, gain, False) * (HD_B ** -0.5 * LOG2E),)


def _epi_copy1(z):
    return (z,)


_KB0, _VB0, _KI0, _WI0, _MISC_COLS = 0, 256, 512, 640, 768


def _epi_misc(z, gain):
    kb = _head_norm(z[:, _KB0:_VB0], gain, False)
    vb = z[:, _VB0:_KI0]
    ki2 = z[:, _KI0:_WI0]
    return kb, kb, vb, vb, ki2[:, :D_IDX], ki2, z[:, _WI0:_WI0 + H_IDX]


def _bias_body(t, n_valid_diag, tab_ref, o_ref):
    h = pl.program_id(0)
    half = N_BUCKETS // 2
    exact = half // 2
    kl = lax.broadcasted_iota(I32, (t, t), 0)
    ql = lax.broadcasted_iota(I32, (t, t), 1)
    for tile, off in ((0, -t), (1, 0)):
        rel = kl + off - ql
        n = jnp.abs(rel)
        n2 = n * n
        v_neg = jnp.full((t, t), tab_ref[h, 0], F32)
        v_pos = jnp.full((t, t), tab_ref[h, half], F32)
        for b in range(1, half):
            cond = (n >= b) if b < exact else (n2 >= exact * exact * 2 ** (b - exact))
            v_neg = jnp.where(cond, tab_ref[h, b], v_neg)
            v_pos = jnp.where(cond, tab_ref[h, half + b], v_pos)
        val = (jnp.where(rel > 0, v_pos, v_neg) - tab_ref[h, half - 1]) * LOG2E
        if tile == 1:
            visible = ((kl // CHUNK) <= (ql // CHUNK)) & (kl < n_valid_diag)
            val = jnp.where(visible, val, NEG)
        o_ref[tile] = val


def _bias_tiles(tab_t, t, n_valid_diag):
    nh = tab_t.shape[0]
    return pl.pallas_call(
        functools.partial(_bias_body, t, n_valid_diag),
        grid=(nh,),
        in_specs=[pl.BlockSpec(memory_space=pltpu.SMEM)],
        out_specs=pl.BlockSpec((None, 2, t, t), lambda h: (h, 0, 0, 0)),
        out_shape=jax.ShapeDtypeStruct((nh, 2, t, t), F32),
        compiler_params=_cparams(("arbitrary",)),
        name="bias_tiles",
    )(tab_t)


def _softmax_step(s, vt, m_ref, l_ref, acc_ref):
    m_old = m_ref[...]
    m_new = jnp.maximum(m_old, jnp.max(s, axis=0, keepdims=True))
    alpha = jnp.exp2(m_old - m_new)
    p = jnp.exp2(s - m_new)
    l_ref[...] = alpha * l_ref[...] + jnp.sum(p, axis=0, keepdims=True)
    acc_ref[...] = acc_ref[...] * alpha + jnp.dot(vt, p.astype(BF16), preferred_element_type=F32)
    m_ref[...] = m_new


def _pipelined_tiles(n_far, produce, consume, buf0, buf1):
    produce(0, buf0)
    pairs = jnp.maximum(n_far, 0) // 2

    def body(i, carry):
        j = 2 * i
        produce(j + 1, buf1)
        consume(j, buf0, None)
        produce(j + 2, buf0)
        consume(j + 1, buf1, None)
        return carry

    lax.fori_loop(0, pairs, body, 0)
    done = 2 * pairs
    rem = n_far - done

    @pl.when(rem == -1)
    def _():
        consume(done, buf0, 1)

    @pl.when(rem == 0)
    def _():
        produce(done + 1, buf1)
        consume(done, buf0, 0)
        consume(done + 1, buf1, 1)

    @pl.when(rem == 1)
    def _():
        produce(done + 1, buf1)
        consume(done, buf0, None)
        produce(done + 2, buf0)
        consume(done + 1, buf1, 0)
        consume(done + 2, buf0, 1)


def _attn_a_body(t, q0, out_scale, lam_ref, q_ref, k_ref, vt_ref, bias_ref, g_ref, o_ref,
                 qz_ref, m_ref, l_ref, acc_ref, s0_ref, s1_ref):
    i = pl.program_id(2)
    n_far = q0 // t + i - 1
    q = q_ref[...]
    lane = lax.broadcasted_iota(I32, (t, LANES), 1)
    zero = jnp.zeros_like(q)
    qz_ref[:t] = jnp.where(lane < HD_A, q, zero)
    qz_ref[t:] = jnp.where(lane < HD_A, zero, q)
    m_ref[...] = jnp.full(m_ref.shape, NEG, F32)
    l_ref[...] = jnp.zeros(l_ref.shape, F32)
    acc_ref[...] = jnp.zeros(acc_ref.shape, F32)

    def produce(j, s_ref):
        ks = pl.multiple_of(j * t, t)
        s_ref[...] = lax.dot_general(k_ref[pl.ds(ks, t), :], qz_ref[...], _NT, preferred_element_type=F32)

    def consume(j, s_ref, kind):
        s = s_ref[...]
        if kind is not None:
            b = bias_ref[kind]
            s = jnp.concatenate([s[:, :t] + b, s[:, t:] + b], axis=1)
        _softmax_step(s, vt_ref[:, pl.ds(pl.multiple_of(j * t, t), t)], m_ref, l_ref, acc_ref)

    _pipelined_tiles(n_far, produce, consume, s0_ref, s1_ref)

    lam = lam_ref[0]
    o = acc_ref[...] / l_ref[...]
    o = o[:, :t] - lam * o[:, t:]
    y = o * lax.rsqrt(jnp.mean(o * o, axis=0, keepdims=True) + EPS)
    y = (y * g_ref[...]) * out_scale
    o_ref[...] = y.T.astype(o_ref.dtype)


def _attn_a(q, k, vt, bias, subln_g, lam, *, t, q0, out_scale):
    b, nq, _ = q.shape
    nk = k.shape[1]
    return pl.pallas_call(
        functools.partial(_attn_a_body, t, q0, out_scale),
        grid=(b, H_A, nq // t),
        in_specs=[
            pl.BlockSpec(memory_space=pltpu.SMEM),
            pl.BlockSpec((None, t, LANES), lambda bb, h, i: (bb, i, h)),
            pl.BlockSpec((None, nk, LANES), lambda bb, h, i: (bb, 0, h)),
            pl.BlockSpec((None, DV_A, nk), lambda bb, h, i: (bb, h, 0)),
            pl.BlockSpec((None, 2, t, t), lambda bb, h, i: (h, 0, 0, 0)),
            pl.BlockSpec((DV_A, 1), lambda bb, h, i: (0, 0)),
        ],
        out_specs=pl.BlockSpec((None, t, LANES), lambda bb, h, i: (bb, i, h)),
        out_shape=jax.ShapeDtypeStruct((b, nq, H_A * DV_A), BF16),
        scratch_shapes=[
            pltpu.VMEM((2 * t, LANES), BF16),
            pltpu.VMEM((1, 2 * t), F32),
            pltpu.VMEM((1, 2 * t), F32),
            pltpu.VMEM((DV_A, 2 * t), F32),
            pltpu.VMEM((t, 2 * t), F32),
            pltpu.VMEM((t, 2 * t), F32),
        ],
        compiler_params=_cparams(("parallel", "parallel", "arbitrary")),
        name="diff_attn",
    )(lam, q, k, vt, bias, subln_g)


def _order_key(x):
    b = pltpu.bitcast(x, I32)
    return jnp.where(b < 0, b ^ INT_MAX, b)


def _order_unkey(k):
    return pltpu.bitcast(jnp.where(k < 0, k ^ INT_MAX, k), F32)


def _dsa_body(t, q0, topk, n_keys, n_valid_diag, qb_ref, qi_ref, wi_ref, kb_ref, vbt_ref, ki_ref, bias_ref, o_ref,
              keys_ref, qz_ref, qs_ref, m_ref, l_ref, acc_ref, s0_ref, s1_ref):
    i = pl.program_id(1)
    n_tiles = q0 // t + i + 1
    n_far = n_tiles - 2

    lane = lax.broadcasted_iota(I32, (t, LANES), 1)
    for hp in range(H_IDX // 2):
        qs = qi_ref[:, hp * LANES:(hp + 1) * LANES]
        zero = jnp.zeros_like(qs)
        qz_ref[2 * hp] = jnp.where(lane < D_IDX, qs, zero)
        qz_ref[2 * hp + 1] = jnp.where(lane < D_IDX, zero, qs)

    def score_tile(j):
        kt = ki_ref[pl.ds(pl.multiple_of(j * t, t), t), :]
        acc = jnp.zeros((t, t), F32)
        for h in range(H_IDX):
            s = lax.dot_general(kt, qz_ref[h], _NT, preferred_element_type=F32)
            acc = acc + jnp.maximum(s, 0.0) * wi_ref[h:h + 1, :]
        return acc

    def fold8(x, op):
        return op(x.reshape(x.shape[0] // 8, 8, t), axis=0)

    def score_body(j, carry):
        smin, smax = carry
        sc = score_tile(j)
        keys_ref[pl.ds(pl.multiple_of(j * t, t), t), :] = _order_key(sc)
        return jnp.minimum(smin, fold8(sc, jnp.min)), jnp.maximum(smax, fold8(sc, jnp.max))

    smin, smax = lax.fori_loop(0, n_tiles - 1, score_body,
                               (jnp.full((8, t), jnp.inf, F32), jnp.full((8, t), -jnp.inf, F32)))
    jd = n_tiles - 1
    kl = lax.broadcasted_iota(I32, (t, t), 0)
    ql = lax.broadcasted_iota(I32, (t, t), 1)
    admissible = ((kl // CHUNK) <= (ql // CHUNK)) & (kl < n_valid_diag)
    sc = score_tile(jd)
    keys_ref[pl.ds(pl.multiple_of(jd * t, t), t), :] = jnp.where(admissible, _order_key(sc), INT_MIN)
    smin = jnp.minimum(smin, fold8(jnp.where(admissible, sc, jnp.inf), jnp.min))
    smax = jnp.maximum(smax, fold8(jnp.where(admissible, sc, -jnp.inf), jnp.max))
    keys_ref[pl.ds(pl.multiple_of(n_tiles * t, t), t), :] = jnp.full((t, t), INT_MIN, I32)
    n_pairs = (n_tiles + 1) // 2

    def count_pairs(hit_fn):
        def body(j, c):
            r0 = pl.multiple_of(j * 2 * t, 2 * t)
            hit = hit_fn(keys_ref[pl.ds(r0, 2 * t), :], r0)
            return c + jnp.sum(hit.reshape(2 * t // 32, 32, t), axis=0)
        part = lax.fori_loop(0, n_pairs, body, jnp.zeros((32, t), F32))
        return jnp.sum(part, axis=0, keepdims=True)

    def search_cond(state):
        return jnp.logical_and(state[0] < 4 * 34, state[-1] > 0)

    def search_body(state):
        it, lo, hi, c_lo, c_hi, _ = state
        f_lo, f_hi = _order_unkey(lo), _order_unkey(hi)
        a, b = jnp.log(c_lo + 0.5), jnp.log(c_hi + 0.5)
        frac = jnp.clip((a - math.log(topk)) / jnp.maximum(a - b, 1e-9), 0.0, 1.0)
        guess = jnp.minimum(jnp.maximum(_order_key(f_lo + (f_hi - f_lo) * frac), lo + 1), hi - 1)
        mid = jnp.where(it % 4 == 3, (lo >> 1) + (hi >> 1) + (lo & hi & 1), guess)
        c = count_pairs(lambda kk, r0: jnp.where(kk >= mid, 1.0, 0.0))
        live = lo + 1 < hi
        ge = c >= topk
        up = jnp.logical_and(live, ge)
        down = jnp.logical_and(live, jnp.logical_not(ge))
        lo = jnp.where(up, mid, lo)
        c_lo = jnp.where(up, c, c_lo)
        hi = jnp.where(jnp.logical_and(live, c == topk), mid + 1, jnp.where(down, mid, hi))
        c_hi = jnp.where(down, c, c_hi)
        return it + 1, lo, hi, c_lo, c_hi, jnp.max(jnp.where(lo + 1 < hi, 1, 0))

    qpos = q0 + i * t + lax.broadcasted_iota(I32, (1, t), 1)
    n_adm = jnp.minimum((qpos // CHUNK + 1) * CHUNK, n_keys).astype(F32)
    lo0 = _order_key(jnp.min(smin, axis=0, keepdims=True))
    hi0 = jnp.where(n_adm <= topk, lo0 + 1, _order_key(jnp.max(smax, axis=0, keepdims=True)) + 1)
    _, thr, _, c_thr, c_above, _ = lax.while_loop(
        search_cond, search_body, (jnp.int32(0), lo0, hi0, n_adm, jnp.zeros((1, t), F32), jnp.int32(1)))

    @pl.when(jnp.max(jnp.where(c_thr > topk, 1, 0)) > 0)
    def _():
        need = topk - c_above
        row = lax.broadcasted_iota(I32, (2 * t, t), 0)

        def cut_body(_, st):
            p_lo, p_hi = st
            p_mid = (p_lo + p_hi) >> 1
            c = count_pairs(lambda kk, r0: jnp.where(
                jnp.logical_and(kk == thr, row + r0 <= p_mid), 1.0, 0.0))
            ok = c >= need
            return jnp.where(ok, p_lo, p_mid), jnp.where(ok, p_mid, p_hi)

        n_pos = keys_ref.shape[0]
        _, cut = lax.fori_loop(0, n_pos.bit_length(), cut_body,
                               (jnp.full((1, t), -1, I32), jnp.full((1, t), n_pos - 1, I32)))

        def demote_body(j, carry):
            r0 = pl.multiple_of(j * 2 * t, 2 * t)
            kk = keys_ref[pl.ds(r0, 2 * t), :]
            keys_ref[pl.ds(r0, 2 * t), :] = jnp.where(jnp.logical_and(kk == thr, row + r0 > cut), thr - 1, kk)
            return carry

        lax.fori_loop(0, n_pairs, demote_body, 0)

    m_ref[...] = jnp.full(m_ref.shape, NEG, F32)
    l_ref[...] = jnp.zeros(l_ref.shape, F32)
    acc_ref[...] = jnp.zeros(acc_ref.shape, F32)

    for h in range(H_B):
        qs_ref[h // G_B, (h % G_B) * t:(h % G_B + 1) * t, :] = qb_ref[:, h * HD_B:(h + 1) * HD_B]

    def produce(j, s_ref):
        ks = pl.multiple_of(j * t, t)
        for n in range(HKV_B):
            s_ref[n] = lax.dot_general(kb_ref[pl.ds(ks, t), n * HD_B:(n + 1) * HD_B], qs_ref[n], _NT,
                                       preferred_element_type=F32)

    def consume(j, s_ref, kind):
        ks = pl.multiple_of(j * t, t)
        sel = keys_ref[pl.ds(ks, t), :] >= thr
        for n in range(HKV_B):
            s = s_ref[n]
            parts = []
            for g in range(G_B):
                sg = s[:, g * t:(g + 1) * t]
                if kind is not None:
                    sg = sg + bias_ref[n * G_B + g, kind]
                parts.append(jnp.where(sel, sg, NEG))
            _softmax_step(jnp.concatenate(parts, axis=1), vbt_ref[n * HD_B:(n + 1) * HD_B, pl.ds(ks, t)],
                          m_ref.at[n], l_ref.at[n], acc_ref.at[n])

    _pipelined_tiles(n_far, produce, consume, s0_ref, s1_ref)

    for n in range(HKV_B):
        o = acc_ref[n] / l_ref[n]
        for g in range(G_B):
            h = n * G_B + g
            o_ref[:, h * HD_B:(h + 1) * HD_B] = o[:, g * t:(g + 1) * t].T.astype(o_ref.dtype)


def _dsa(qb, qi, wi_t, kb, vbt, ki2, bias, *, t, q0, topk, n_keys, n_valid_diag):
    b, nq, _ = qb.shape
    nk = kb.shape[1]
    return pl.pallas_call(
        functools.partial(_dsa_body, t, q0, topk, n_keys, n_valid_diag),
        grid=(b, nq // t),
        in_specs=[
            pl.BlockSpec((None, t, H_B * HD_B), lambda bb, i: (bb, i, 0)),
            pl.BlockSpec((None, t, H_IDX * D_IDX), lambda bb, i: (bb, i, 0)),
            pl.BlockSpec((None, H_IDX, t), lambda bb, i: (bb, 0, i)),
            _resident((None, nk, HKV_B * HD_B), lambda bb, i: (bb, 0, 0)),
            _resident((None, HKV_B * HD_B, nk), lambda bb, i: (bb, 0, 0)),
            _resident((None, nk, LANES), lambda bb, i: (bb, 0, 0)),
            _resident((H_B, 2, t, t), lambda bb, i: (0, 0, 0, 0)),
        ],
        out_specs=pl.BlockSpec((None, t, H_B * HD_B), lambda bb, i: (bb, i, 0)),
        out_shape=jax.ShapeDtypeStruct((b, nq, H_B * HD_B), BF16),
        scratch_shapes=[
            pltpu.VMEM((nk + t, t), I32),
            pltpu.VMEM((H_IDX, t, LANES), BF16),
            pltpu.VMEM((HKV_B, G_B * t, HD_B), BF16),
            pltpu.VMEM((HKV_B, 1, G_B * t), F32),
            pltpu.VMEM((HKV_B, 1, G_B * t), F32),
            pltpu.VMEM((HKV_B, HD_B, G_B * t), F32),
            pltpu.VMEM((HKV_B, t, G_B * t), F32),
            pltpu.VMEM((HKV_B, t, G_B * t), F32),
        ],
        compiler_params=_cparams(("parallel", "arbitrary")),
        name="dsa",
    )(qb, qi, wi_t, kb, vbt, ki2, bias)


def _sigmoid(x):
    return 1.0 / (1.0 + jnp.exp(-x))


def _merge_body(h_ref, oa_ref, ob_ref, wg_ref, woa_ref, wob_ref, o_ref):
    h = h_ref[...]
    ga = jnp.dot(h, wg_ref[:, :D_MODEL], preferred_element_type=F32)
    ya = jnp.dot(oa_ref[...], woa_ref[...], preferred_element_type=F32)
    merged = _sigmoid(ga) * ya
    gb = jnp.dot(h, wg_ref[:, D_MODEL:], preferred_element_type=F32)
    yb = jnp.dot(ob_ref[...], wob_ref[...], preferred_element_type=F32)
    o_ref[...] = (merged + _sigmoid(gb) * yb).astype(o_ref.dtype)


def _merge(h, oa, ob, wg, woa, wob, tm):
    n = h.shape[0]
    return pl.pallas_call(
        _merge_body,
        grid=(n // tm,),
        in_specs=[
            pl.BlockSpec((tm, D_MODEL), lambda i: (i, 0)),
            pl.BlockSpec((tm, H_A * DV_A), lambda i: (i, 0)),
            pl.BlockSpec((tm, H_B * HD_B), lambda i: (i, 0)),
            _resident((D_MODEL, 2 * D_MODEL), lambda i: (0, 0)),
            _resident((H_A * DV_A, D_MODEL), lambda i: (0, 0)),
            _resident((H_B * HD_B, D_MODEL), lambda i: (0, 0)),
        ],
        out_specs=pl.BlockSpec((tm, D_MODEL), lambda i: (i, 0)),
        out_shape=jax.ShapeDtypeStruct((n, D_MODEL), BF16),
        compiler_params=_cparams(("parallel",)),
        name="gated_merge",
    )(h, oa, ob, wg, woa, wob)


def _outproj_body(x_ref, mg_ref, w_ref, g_ref, x1_ref, h2_ref):
    x1 = x_ref[...] + jnp.dot(mg_ref[...], w_ref[...], preferred_element_type=F32)
    x1_ref[...] = x1
    y = x1 * lax.rsqrt(jnp.mean(x1 * x1, axis=-1, keepdims=True) + EPS)
    h2_ref[...] = (y * g_ref[...]).astype(h2_ref.dtype)


def _outproj(x, merged, w_out, g2, tm):
    n = x.shape[0]
    return pl.pallas_call(
        _outproj_body,
        grid=(n // tm,),
        in_specs=[
            pl.BlockSpec((tm, D_MODEL), lambda i: (i, 0)),
            pl.BlockSpec((tm, D_MODEL), lambda i: (i, 0)),
            _resident((D_MODEL, D_MODEL), lambda i: (0, 0)),
            pl.BlockSpec((1, D_MODEL), lambda i: (0, 0)),
        ],
        out_specs=[pl.BlockSpec((tm, D_MODEL), lambda i: (i, 0)), pl.BlockSpec((tm, D_MODEL), lambda i: (i, 0))],
        out_shape=[jax.ShapeDtypeStruct((n, D_MODEL), F32), jax.ShapeDtypeStruct((n, D_MODEL), BF16)],
        compiler_params=_cparams(("parallel",)),
        name="out_proj",
    )(x, merged, w_out, g2.reshape(1, D_MODEL))


def _ffn_body(x1_ref, h2_ref, w1_ref, w2_ref, o_ref):
    f = pl.program_id(1)

    @pl.when(f == 0)
    def _():
        o_ref[...] = x1_ref[...]

    u = jnp.maximum(jnp.dot(h2_ref[...], w1_ref[...], preferred_element_type=F32), 0.0)
    o_ref[...] += jnp.dot((u * u).astype(BF16), w2_ref[...], preferred_element_type=F32)


def _ffn(x1, h2, w1, w2, tm, tf):
    n = x1.shape[0]
    return pl.pallas_call(
        _ffn_body,
        grid=(n // tm, D_FF // tf),
        in_specs=[
            pl.BlockSpec((tm, D_MODEL), lambda i, f: (i, 0)),
            pl.BlockSpec((tm, D_MODEL), lambda i, f: (i, 0)),
            pl.BlockSpec((D_MODEL, tf), lambda i, f: (0, f)),
            pl.BlockSpec((tf, D_MODEL), lambda i, f: (f, 0)),
        ],
        out_specs=pl.BlockSpec((tm, D_MODEL), lambda i, f: (i, 0)),
        out_shape=jax.ShapeDtypeStruct((n, D_MODEL), F32),
        compiler_params=_cparams(("parallel", "arbitrary")),
        name="ffn",
    )(x1, h2, w1, w2)


def _prep_weights(w_in, w_o_a, w_o_b, w_out, w_ff1, w_ff2):
    sizes = (H_A * 2 * HD_A, H_A * 2 * HD_A, H_A * DV_A, H_B * HD_B, HKV_B * HD_B, HKV_B * HD_B,
             H_IDX * D_IDX, D_IDX, H_IDX, D_MODEL, D_MODEL)
    offs = [0]
    for s in sizes:
        offs.append(offs[-1] + s)
    col = lambda a, b: w_in[:, offs[a]:offs[b]]
    w_misc = jnp.concatenate(
        [col(4, 6), col(7, 8), col(7, 8), col(8, 9),
         jnp.zeros((D_MODEL, _MISC_COLS - _WI0 - H_IDX), w_in.dtype)], axis=1)
    return dict(
        qa=col(0, 1).astype(BF16), ka=col(1, 2).astype(BF16), va=col(2, 3).astype(BF16),
        qb=col(3, 4).astype(BF16), misc=w_misc.astype(BF16), qi=col(6, 7).astype(BF16),
        gate=col(9, 11).astype(BF16), oa=w_o_a.astype(BF16), ob=w_o_b.astype(BF16),
        out=w_out.astype(BF16), ff1=w_ff1.astype(BF16), ff2=w_ff2.astype(BF16))


def _layer(x, past, w, p, lam, lam_init, tab_t, *, t_a, t_b, tm):
    b, t, _ = x.shape
    n = b * t
    xf = x.reshape(n, D_MODEL)
    h = _rmsnorm(xf, p["norm1_g"], tm)

    gain_a = lambda g: g.reshape(1, 2 * HD_A)
    (qa,) = _proj(h, w["qa"], [gain_a(p["qn_a_g"])], _epi_qa, [(1024, BF16)], tm, "proj_qa")
    ka, ka_h = _proj(h, w["ka"], [gain_a(p["kn_a_g"])], _epi_ka, [(1024, F32), (1024, BF16)], tm, "proj_ka")
    va, va_h = _proj(h, w["va"], [], _epi_copy2, [(1024, F32), (1024, BF16)], tm, "proj_va")
    (qb,) = _proj(h, w["qb"], [p["qn_b_g"].reshape(1, HD_B)], _epi_qb, [(1024, BF16)], tm, "proj_qb")
    kb, kb_h, vb, vb_h, ki, ki2_h, wi = _proj(
        h, w["misc"], [p["kn_b_g"].reshape(1, HD_B)], _epi_misc,
        [(256, F32), (256, BF16), (256, F32), (256, BF16), (D_IDX, F32), (2 * D_IDX, BF16), (H_IDX, F32)],
        tm, "proj_misc")
    (qi,) = _proj(h, w["qi"], [], _epi_copy1, [(1024, BF16)], tm, "proj_qi")

    new_rows = (ka.reshape(b, t, H_A, 2 * HD_A), va.reshape(b, t, H_A, DV_A), kb.reshape(b, t, HKV_B, HD_B),
                vb.reshape(b, t, HKV_B, HD_B), ki.reshape(b, t, D_IDX))

    def full_keys(new_h, past_arr):
        new_h = new_h.reshape(b, t, -1)
        if past_arr is None:
            return new_h
        past_h = past_arr.reshape(b, past_arr.shape[1], -1).astype(BF16)
        return jnp.concatenate([past_h, new_h], axis=1)

    if past is None:
        pa_k = pa_v = pb_k = pb_v = pb_i = None
    else:
        pa_k, pa_v, pb_k, pb_v, pb_i = past
        pb_i = jnp.concatenate([pb_i, pb_i], axis=-1)
    ka_f, va_f = full_keys(ka_h, pa_k), full_keys(va_h, pa_v)
    kb_f, vb_f, ki_f = full_keys(kb_h, pb_k), full_keys(vb_h, pb_v), full_keys(ki2_h, pb_i)
    n_keys = ka_f.shape[1]
    q0 = n_keys - t
    topk = min(TOPK_MAX, n_keys // 4)
    t_pad = max(t_a, t_b)
    assert t_pad % t_a == 0 and t_pad % t_b == 0 and q0 % t_pad == 0
    nk_pad = -(-n_keys // t_pad) * t_pad
    nq_pad = -(-t // t_pad) * t_pad
    pad_k = lambda a: jnp.pad(a, ((0, 0), (0, nk_pad - n_keys), (0, 0)))
    pad_q = lambda a: jnp.pad(a.reshape(b, t, -1), ((0, 0), (0, nq_pad - t), (0, 0)))
    ka_f, va_f, kb_f, vb_f, ki_f = (pad_k(a) for a in (ka_f, va_f, kb_f, vb_f, ki_f))
    n_valid = lambda tt: n_keys - (n_keys - 1) // tt * tt
    assert all(nq_pad == tt or n_valid(tt) == tt for tt in (t_a, t_b))

    bias_a = _bias_tiles(tab_t[:H_A], t_a, n_valid(t_a))
    oa = _attn_a(pad_q(qa), ka_f, jnp.swapaxes(va_f, 1, 2), bias_a, p["subln_a_g"].reshape(DV_A, 1), lam,
                 t=t_a, q0=q0, out_scale=1.0 - lam_init)
    wi_t = jnp.swapaxes(pad_q(wi), 1, 2)
    bias_b = _bias_tiles(tab_t[H_A:], t_b, n_valid(t_b))
    ob = _dsa(pad_q(qb), pad_q(qi), wi_t, kb_f, jnp.swapaxes(vb_f, 1, 2), ki_f, bias_b,
              t=t_b, q0=q0, topk=topk, n_keys=n_keys, n_valid_diag=n_valid(t_b))
    oa = oa[:, :t].reshape(n, H_A * DV_A)
    ob = ob[:, :t].reshape(n, H_B * HD_B)

    merged = _merge(h, oa, ob, w["gate"], w["oa"], w["ob"], min(tm, 256))
    x1, h2 = _outproj(xf, merged, w["out"], p["norm2_g"], min(tm, 256))
    y = _ffn(x1, h2, w["ff1"], w["ff2"], tm, 512)
    return y.reshape(b, t, D_MODEL), new_rows


def kernel(x_prompt, x_sample, cache_a_k, cache_a_v, cache_b_k, cache_b_v, cache_b_kidx, rel_bias, norm1_g, w_in, qn_a_g, kn_a_g, lam_q1, lam_k1, lam_q2, lam_k2, subln_a_g, qn_b_g, kn_b_g, w_o_a, w_o_b, w_out, norm2_g, w_ff1, w_ff2):
    depth = w_in.shape[0]
    tab_t = rel_bias.T.astype(F32)
    y_prompt, y_sample = x_prompt, x_sample
    prompt_rows, sample_rows = [], []
    for l in range(depth):
        lam_init = 0.8 - 0.6 * math.exp(-0.3 * l)
        lam = (jnp.exp(jnp.sum(lam_q1[l].astype(F32) * lam_k1[l].astype(F32)))
               - jnp.exp(jnp.sum(lam_q2[l].astype(F32) * lam_k2[l].astype(F32))) + lam_init).reshape(1)
        w = _prep_weights(w_in[l], w_o_a[l], w_o_b[l], w_out[l], w_ff1[l], w_ff2[l])
        p = dict(norm1_g=norm1_g[l], qn_a_g=qn_a_g[l], kn_a_g=kn_a_g[l], subln_a_g=subln_a_g[l],
                 qn_b_g=qn_b_g[l], kn_b_g=kn_b_g[l], norm2_g=norm2_g[l])
        y_prompt, rp = _layer(y_prompt, None, w, p, lam, lam_init, tab_t, t_a=512, t_b=256, tm=512)
        past = (cache_a_k[l], cache_a_v[l], cache_b_k[l], cache_b_v[l], cache_b_kidx[l])
        y_sample, rs = _layer(y_sample, past, w, p, lam, lam_init, tab_t, t_a=128, t_b=128, tm=512)
        prompt_rows.append(rp)
        sample_rows.append(rs)
    p_rows = tuple(jnp.stack(r, axis=0) for r in zip(*prompt_rows))
    s_rows = tuple(jnp.stack(r, axis=0) for r in zip(*sample_rows))
    return (y_prompt, y_sample) + p_rows + s_rows
```

```python
import functools
import math

import jax
import jax.numpy as jnp
from jax import lax
from jax.experimental import pallas as pl
from jax.experimental.pallas import tpu as pltpu

F32 = jnp.float32
BF16 = jnp.bfloat16
I32 = jnp.int32

D_MODEL = 2048
CHUNK = 64
H_A = 8
DV_A = 128
HD_A = 64
H_B = 8
HD_B = 128
HKV_B = 2
G_B = H_B // HKV_B
H_IDX = 16
D_IDX = 64
TOPK_MAX = 256
N_BUCKETS = 32
D_FF = 4 * D_MODEL
EPS = 1e-6
LANES = 128
NEG = -1e30
LOG2E = math.log2(math.e)
INT_MIN = -(2 ** 31)
INT_MAX = 2 ** 31 - 1
VMEM_LIMIT = 56 * 1024 * 1024

_NT = (((1,), (1,)), ((), ()))


def _cparams(sem):
    return pltpu.CompilerParams(dimension_semantics=sem, vmem_limit_bytes=VMEM_LIMIT)


def _resident(block_shape, index_map):
    return pl.BlockSpec(block_shape, index_map, pipeline_mode=pl.Buffered(1))


def _rmsnorm_body(x_ref, g_ref, o_ref):
    x = x_ref[...]
    y = x * lax.rsqrt(jnp.mean(x * x, axis=-1, keepdims=True) + EPS)
    o_ref[...] = (y * g_ref[...]).astype(o_ref.dtype)


def _rmsnorm(x, g, tm):
    n, d = x.shape
    return pl.pallas_call(
        _rmsnorm_body,
        grid=(n // tm,),
        in_specs=[pl.BlockSpec((tm, d), lambda i: (i, 0)), pl.BlockSpec((1, d), lambda i: (0, 0))],
        out_specs=pl.BlockSpec((tm, d), lambda i: (i, 0)),
        out_shape=jax.ShapeDtypeStruct((n, d), BF16),
        compiler_params=_cparams(("parallel",)),
        name="rmsnorm",
    )(x, g.reshape(1, d))


def _head_norm(z, gain, split):
    lane = lax.broadcasted_iota(I32, (1, LANES), 1)
    lo = lane < (LANES // 2)
    outs = []
    for h in range(z.shape[1] // LANES):
        zh = z[:, h * LANES:(h + 1) * LANES]
        sq = zh * zh
        if split:
            s_lo = jnp.sum(jnp.where(lo, sq, 0.0), axis=-1, keepdims=True)
            s_hi = jnp.sum(jnp.where(lo, 0.0, sq), axis=-1, keepdims=True)
            r = jnp.where(lo, lax.rsqrt(s_lo * (2.0 / LANES) + EPS), lax.rsqrt(s_hi * (2.0 / LANES) + EPS))
        else:
            r = lax.rsqrt(jnp.mean(sq, axis=-1, keepdims=True) + EPS)
        outs.append(zh * r * gain)
    return jnp.concatenate(outs, axis=-1)


def _proj_body(epilogue, n_aux, h_ref, w_ref, *rest):
    z = jnp.dot(h_ref[...], w_ref[...], preferred_element_type=F32)
    outs = epilogue(z, *[r[...] for r in rest[:n_aux]])
    for o_ref, o in zip(rest[n_aux:], outs):
        o_ref[...] = o.astype(o_ref.dtype)


def _proj(h, w, aux, epilogue, out_defs, tm, name):
    n, k = h.shape
    c = w.shape[1]
    in_specs = [pl.BlockSpec((tm, k), lambda i: (i, 0)), _resident((k, c), lambda i: (0, 0))]
    in_specs += [pl.BlockSpec(a.shape, lambda i: (0, 0)) for a in aux]
    return pl.pallas_call(
        functools.partial(_proj_body, epilogue, len(aux)),
        grid=(n // tm,),
        in_specs=in_specs,
        out_specs=[pl.BlockSpec((tm, oc), lambda i: (i, 0)) for oc, _ in out_defs],
        out_shape=[jax.ShapeDtypeStruct((n, oc), dt) for oc, dt in out_defs],
        compiler_params=_cparams(("parallel",)),
        name=name,
    )(h, w, *aux)


def _epi_qa(z, gain):
    return (_head_norm(z, gain, True) * (HD_A ** -0.5 * LOG2E),)


def _epi_ka(z, gain):
    y = _head_norm(z, gain, True)
    return y, y


def _epi_copy2(z):
    return z, z


def _epi_qb(z, gain):
    return (_head_norm(z, gain, False) * (HD_B ** -0.5 * LOG2E),)


def _epi_copy1(z):
    return (z,)


_KB0, _VB0, _KI0, _WI0, _MISC_COLS = 0, 256, 512, 640, 768


def _epi_misc(z, gain):
    kb = _head_norm(z[:, _KB0:_VB0], gain, False)
    vb = z[:, _VB0:_KI0]
    ki2 = z[:, _KI0:_WI0]
    return kb, kb, vb, vb, ki2[:, :D_IDX], ki2, z[:, _WI0:_WI0 + H_IDX]


def _bias_body(t, n_valid_diag, tab_ref, o_ref):
    h = pl.program_id(0)
    half = N_BUCKETS // 2
    exact = half // 2
    kl = lax.broadcasted_iota(I32, (t, t), 0)
    ql = lax.broadcasted_iota(I32, (t, t), 1)
    for tile, off in ((0, -t), (1, 0)):
        rel = kl + off - ql
        n = jnp.abs(rel)
        n2 = n * n
        v_neg = jnp.full((t, t), tab_ref[h, 0], F32)
        v_pos = jnp.full((t, t), tab_ref[h, half], F32)
        for b in range(1, half):
            cond = (n >= b) if b < exact else (n2 >= exact * exact * 2 ** (b - exact))
            v_neg = jnp.where(cond, tab_ref[h, b], v_neg)
            v_pos = jnp.where(cond, tab_ref[h, half + b], v_pos)
        val = (jnp.where(rel > 0, v_pos, v_neg) - tab_ref[h, half - 1]) * LOG2E
        if tile == 1:
            visible = ((kl // CHUNK) <= (ql // CHUNK)) & (kl < n_valid_diag)
            val = jnp.where(visible, val, NEG)
        o_ref[tile] = val


def _bias_tiles(tab_t, t, n_valid_diag):
    nh = tab_t.shape[0]
    return pl.pallas_call(
        functools.partial(_bias_body, t, n_valid_diag),
        grid=(nh,),
        in_specs=[pl.BlockSpec(memory_space=pltpu.SMEM)],
        out_specs=pl.BlockSpec((None, 2, t, t), lambda h: (h, 0, 0, 0)),
        out_shape=jax.ShapeDtypeStruct((nh, 2, t, t), F32),
        compiler_params=_cparams(("arbitrary",)),
        name="bias_tiles",
    )(tab_t)


ONES_ROWS = 16


def _with_ones_rows(vt):
    return jnp.concatenate([vt, jnp.ones(vt.shape[:-2] + (ONES_ROWS, vt.shape[-1]), vt.dtype)], axis=-2)


def _softmax_step(s, vt, m_ref, acc_ref):
    m_old = m_ref[...]
    m_new = jnp.maximum(m_old, jnp.max(s, axis=0, keepdims=True))
    alpha = jnp.exp2(m_old - m_new)
    p = jnp.exp2(s - m_new)
    acc_ref[...] = acc_ref[...] * alpha + jnp.dot(vt, p.astype(BF16), preferred_element_type=F32)
    m_ref[...] = m_new


def _normalised(acc, dv):
    return acc[:dv] / acc[dv:dv + 1]


def _pipelined_tiles(n_far, produce, consume, buf0, buf1):
    @pl.when(n_far == -1)
    def _():
        produce(0, buf0, 1)
        consume(0, buf0)

    @pl.when(n_far == 0)
    def _():
        produce(0, buf0, 0)
        produce(1, buf1, 1)
        consume(0, buf0)
        consume(1, buf1)

    @pl.when(n_far >= 1)
    def _():
        produce(0, buf0, None)

    pairs = jnp.maximum(n_far - 1, 0) // 2

    def body(i, carry):
        j = 2 * i
        produce(j + 1, buf1, None)
        consume(j, buf0)
        produce(j + 2, buf0, None)
        consume(j + 1, buf1)
        return carry

    lax.fori_loop(0, pairs, body, 0)
    d = 2 * pairs
    rem = n_far - d

    @pl.when(rem == 1)
    def _():
        produce(d + 1, buf1, 0)
        consume(d, buf0)
        produce(d + 2, buf0, 1)
        consume(d + 1, buf1)
        consume(d + 2, buf0)

    @pl.when(rem == 2)
    def _():
        produce(d + 1, buf1, None)
        consume(d, buf0)
        produce(d + 2, buf0, 0)
        consume(d + 1, buf1)
        produce(d + 3, buf1, 1)
        consume(d + 2, buf0)
        consume(d + 3, buf1)


def _attn_a_body(t, q0, out_scale, lam_ref, q_ref, k_ref, vt_ref, bias_ref, g_ref, o_ref,
                 qz_ref, m_ref, acc_ref, s0_ref, s1_ref):
    i = pl.program_id(2)
    n_far = q0 // t + i - 1
    q = q_ref[...]
    lane = lax.broadcasted_iota(I32, (t, LANES), 1)
    zero = jnp.zeros_like(q)
    qz_ref[:t] = jnp.where(lane < HD_A, q, zero)
    qz_ref[t:] = jnp.where(lane < HD_A, zero, q)
    m_ref[...] = jnp.full(m_ref.shape, NEG, F32)
    acc_ref[...] = jnp.zeros(acc_ref.shape, F32)

    def produce(j, s_ref, kind):
        ks = pl.multiple_of(j * t, t)
        s = lax.dot_general(k_ref[pl.ds(ks, t), :], qz_ref[...], _NT, preferred_element_type=F32)
        if kind is not None:
            b = bias_ref[kind]
            s = jnp.concatenate([s[:, :t] + b, s[:, t:] + b], axis=1)
        s_ref[...] = s

    def consume(j, s_ref):
        _softmax_step(s_ref[...], vt_ref[:, pl.ds(pl.multiple_of(j * t, t), t)], m_ref, acc_ref)

    _pipelined_tiles(n_far, produce, consume, s0_ref, s1_ref)

    lam = lam_ref[0]
    o = _normalised(acc_ref[...], DV_A)
    o = o[:, :t] - lam * o[:, t:]
    y = o * lax.rsqrt(jnp.mean(o * o, axis=0, keepdims=True) + EPS)
    y = (y * g_ref[...]) * out_scale
    o_ref[...] = y.T.astype(o_ref.dtype)


def _attn_a(q, k, vt, bias, subln_g, lam, *, t, q0, out_scale):
    b, nq, _ = q.shape
    nk = k.shape[1]
    return pl.pallas_call(
        functools.partial(_attn_a_body, t, q0, out_scale),
        grid=(b, H_A, nq // t),
        in_specs=[
            pl.BlockSpec(memory_space=pltpu.SMEM),
            pl.BlockSpec((None, t, LANES), lambda bb, h, i: (bb, i, h)),
            pl.BlockSpec((None, nk, LANES), lambda bb, h, i: (bb, 0, h)),
            pl.BlockSpec((None, None, DV_A + ONES_ROWS, nk), lambda bb, h, i: (bb, h, 0, 0)),
            pl.BlockSpec((None, 2, t, t), lambda bb, h, i: (h, 0, 0, 0)),
            pl.BlockSpec((DV_A, 1), lambda bb, h, i: (0, 0)),
        ],
        out_specs=pl.BlockSpec((None, t, LANES), lambda bb, h, i: (bb, i, h)),
        out_shape=jax.ShapeDtypeStruct((b, nq, H_A * DV_A), BF16),
        scratch_shapes=[
            pltpu.VMEM((2 * t, LANES), BF16),
            pltpu.VMEM((1, 2 * t), F32),
            pltpu.VMEM((DV_A + ONES_ROWS, 2 * t), F32),
            pltpu.VMEM((t, 2 * t), F32),
            pltpu.VMEM((t, 2 * t), F32),
        ],
        compiler_params=_cparams(("parallel", "parallel", "arbitrary")),
        name="diff_attn",
    )(lam, q, k, vt, bias, subln_g)


def _order_key(x):
    b = pltpu.bitcast(x, I32)
    return jnp.where(b < 0, b ^ INT_MAX, b)


def _order_unkey(k):
    return pltpu.bitcast(jnp.where(k < 0, k ^ INT_MAX, k), F32)


def _dsa_body(t, q0, topk, n_keys, n_valid_diag, qb_ref, qi_ref, wi_ref, kb_ref, vbt_ref, ki_ref, bias_ref, o_ref,
              keys_ref, qz_ref, qs_ref, m_ref, acc_ref, s0_ref, s1_ref):
    i = pl.program_id(1)
    n_tiles = q0 // t + i + 1
    n_far = n_tiles - 2

    lane = lax.broadcasted_iota(I32, (t, LANES), 1)
    for hp in range(H_IDX // 2):
        qs = qi_ref[:, hp * LANES:(hp + 1) * LANES]
        zero = jnp.zeros_like(qs)
        qz_ref[2 * hp] = jnp.where(lane < D_IDX, qs, zero)
        qz_ref[2 * hp + 1] = jnp.where(lane < D_IDX, zero, qs)

    def score_tile(j):
        kt = ki_ref[pl.ds(pl.multiple_of(j * t, t), t), :]
        acc = jnp.zeros((t, t), F32)
        for h in range(H_IDX):
            s = lax.dot_general(kt, qz_ref[h], _NT, preferred_element_type=F32)
            acc = acc + jnp.maximum(s, 0.0) * wi_ref[h:h + 1, :]
        return acc

    def fold8(x, op):
        return op(x.reshape(x.shape[0] // 8, 8, t), axis=0)

    def score_body(j, carry):
        smin, smax = carry
        sc = score_tile(j)
        keys_ref[pl.ds(pl.multiple_of(j * t, t), t), :] = _order_key(sc)
        return jnp.minimum(smin, fold8(sc, jnp.min)), jnp.maximum(smax, fold8(sc, jnp.max))

    smin, smax = lax.fori_loop(0, n_tiles - 1, score_body,
                               (jnp.full((8, t), jnp.inf, F32), jnp.full((8, t), -jnp.inf, F32)))
    jd = n_tiles - 1
    kl = lax.broadcasted_iota(I32, (t, t), 0)
    ql = lax.broadcasted_iota(I32, (t, t), 1)
    admissible = ((kl // CHUNK) <= (ql // CHUNK)) & (kl < n_valid_diag)
    sc = score_tile(jd)
    keys_ref[pl.ds(pl.multiple_of(jd * t, t), t), :] = jnp.where(admissible, _order_key(sc), INT_MIN)
    smin = jnp.minimum(smin, fold8(jnp.where(admissible, sc, jnp.inf), jnp.min))
    smax = jnp.maximum(smax, fold8(jnp.where(admissible, sc, -jnp.inf), jnp.max))
    keys_ref[pl.ds(pl.multiple_of(n_tiles * t, t), t), :] = jnp.full((t, t), INT_MIN, I32)
    n_pairs = (n_tiles + 1) // 2

    def count_pairs(hit_fn):
        def body(j, c):
            r0 = pl.multiple_of(j * 2 * t, 2 * t)
            hit = hit_fn(keys_ref[pl.ds(r0, 2 * t), :], r0)
            return c + jnp.sum(hit.reshape(2 * t // 32, 32, t), axis=0)
        part = lax.fori_loop(0, n_pairs, body, jnp.zeros((32, t), F32))
        return jnp.sum(part, axis=0, keepdims=True)

    def search_cond(state):
        return jnp.logical_and(state[0] < 4 * 34, state[-1] > 0)

    def search_body(state):
        it, lo, hi, c_lo, c_hi, _ = state
        f_lo, f_hi = _order_unkey(lo), _order_unkey(hi)
        a, b = jnp.log(c_lo + 0.5), jnp.log(c_hi + 0.5)
        frac = jnp.clip((a - math.log(topk)) / jnp.maximum(a - b, 1e-9), 0.0, 1.0)
        guess = jnp.minimum(jnp.maximum(_order_key(f_lo + (f_hi - f_lo) * frac), lo + 1), hi - 1)
        mid = jnp.where(it % 4 == 3, (lo >> 1) + (hi >> 1) + (lo & hi & 1), guess)
        c = count_pairs(lambda kk, r0: jnp.where(kk >= mid, 1.0, 0.0))
        live = lo + 1 < hi
        ge = c >= topk
        up = jnp.logical_and(live, ge)
        down = jnp.logical_and(live, jnp.logical_not(ge))
        lo = jnp.where(up, mid, lo)
        c_lo = jnp.where(up, c, c_lo)
        hi = jnp.where(jnp.logical_and(live, c == topk), mid + 1, jnp.where(down, mid, hi))
        c_hi = jnp.where(down, c, c_hi)
        return it + 1, lo, hi, c_lo, c_hi, jnp.max(jnp.where(lo + 1 < hi, 1, 0))

    qpos = q0 + i * t + lax.broadcasted_iota(I32, (1, t), 1)
    n_adm = jnp.minimum((qpos // CHUNK + 1) * CHUNK, n_keys).astype(F32)
    lo0 = _order_key(jnp.min(smin, axis=0, keepdims=True))
    hi0 = jnp.where(n_adm <= topk, lo0 + 1, _order_key(jnp.max(smax, axis=0, keepdims=True)) + 1)
    _, thr, _, c_thr, c_above, _ = lax.while_loop(
        search_cond, search_body, (jnp.int32(0), lo0, hi0, n_adm, jnp.zeros((1, t), F32), jnp.int32(1)))

    @pl.when(jnp.max(jnp.where(c_thr > topk, 1, 0)) > 0)
    def _():
        need = topk - c_above
        row = lax.broadcasted_iota(I32, (2 * t, t), 0)

        def cut_body(_, st):
            p_lo, p_hi = st
            p_mid = (p_lo + p_hi) >> 1
            c = count_pairs(lambda kk, r0: jnp.where(
                jnp.logical_and(kk == thr, row + r0 <= p_mid), 1.0, 0.0))
            ok = c >= need
            return jnp.where(ok, p_lo, p_mid), jnp.where(ok, p_mid, p_hi)

        n_pos = keys_ref.shape[0]
        _, cut = lax.fori_loop(0, n_pos.bit_length(), cut_body,
                               (jnp.full((1, t), -1, I32), jnp.full((1, t), n_pos - 1, I32)))

        def demote_body(j, carry):
            r0 = pl.multiple_of(j * 2 * t, 2 * t)
            kk = keys_ref[pl.ds(r0, 2 * t), :]
            keys_ref[pl.ds(r0, 2 * t), :] = jnp.where(jnp.logical_and(kk == thr, row + r0 > cut), thr - 1, kk)
            return carry

        lax.fori_loop(0, n_pairs, demote_body, 0)

    m_ref[...] = jnp.full(m_ref.shape, NEG, F32)
    acc_ref[...] = jnp.zeros(acc_ref.shape, F32)

    for h in range(H_B):
        qs_ref[h // G_B, (h % G_B) * t:(h % G_B + 1) * t, :] = qb_ref[:, h * HD_B:(h + 1) * HD_B]

    def produce(j, s_ref, kind):
        ks = pl.multiple_of(j * t, t)
        off = jnp.where(keys_ref[pl.ds(ks, t), :] >= thr, 0.0, NEG)
        for n in range(HKV_B):
            s = lax.dot_general(kb_ref[pl.ds(ks, t), n * HD_B:(n + 1) * HD_B], qs_ref[n], _NT,
                                preferred_element_type=F32)
            parts = []
            for g in range(G_B):
                off_g = off if kind is None else off + bias_ref[n * G_B + g, kind]
                parts.append(s[:, g * t:(g + 1) * t] + off_g)
            s_ref[n] = jnp.concatenate(parts, axis=1)

    def consume(j, s_ref):
        ks = pl.multiple_of(j * t, t)
        for n in range(HKV_B):
            _softmax_step(s_ref[n], vbt_ref[n, :, pl.ds(ks, t)], m_ref.at[n], acc_ref.at[n])

    _pipelined_tiles(n_far, produce, consume, s0_ref, s1_ref)

    for n in range(HKV_B):
        o = _normalised(acc_ref[n], HD_B)
        for g in range(G_B):
            h = n * G_B + g
            o_ref[:, h * HD_B:(h + 1) * HD_B] = o[:, g * t:(g + 1) * t].T.astype(o_ref.dtype)


def _dsa(qb, qi, wi_t, kb, vbt, ki2, bias, *, t, q0, topk, n_keys, n_valid_diag):
    b, nq, _ = qb.shape
    nk = kb.shape[1]
    return pl.pallas_call(
        functools.partial(_dsa_body, t, q0, topk, n_keys, n_valid_diag),
        grid=(b, nq // t),
        in_specs=[
            pl.BlockSpec((None, t, H_B * HD_B), lambda bb, i: (bb, i, 0)),
            pl.BlockSpec((None, t, H_IDX * D_IDX), lambda bb, i: (bb, i, 0)),
            pl.BlockSpec((None, H_IDX, t), lambda bb, i: (bb, 0, i)),
            _resident((None, nk, HKV_B * HD_B), lambda bb, i: (bb, 0, 0)),
            _resident((None, HKV_B, HD_B + ONES_ROWS, nk), lambda bb, i: (bb, 0, 0, 0)),
            _resident((None, nk, LANES), lambda bb, i: (bb, 0, 0)),
            _resident((H_B, 2, t, t), lambda bb, i: (0, 0, 0, 0)),
        ],
        out_specs=pl.BlockSpec((None, t, H_B * HD_B), lambda bb, i: (bb, i, 0)),
        out_shape=jax.ShapeDtypeStruct((b, nq, H_B * HD_B), BF16),
        scratch_shapes=[
            pltpu.VMEM((nk + t, t), I32),
            pltpu.VMEM((H_IDX, t, LANES), BF16),
            pltpu.VMEM((HKV_B, G_B * t, HD_B), BF16),
            pltpu.VMEM((HKV_B, 1, G_B * t), F32),
            pltpu.VMEM((HKV_B, HD_B + ONES_ROWS, G_B * t), F32),
            pltpu.VMEM((HKV_B, t, G_B * t), F32),
            pltpu.VMEM((HKV_B, t, G_B * t), F32),
        ],
        compiler_params=_cparams(("parallel", "arbitrary")),
        name="dsa",
    )(qb, qi, wi_t, kb, vbt, ki2, bias)


def _sigmoid(x):
    return 1.0 / (1.0 + jnp.exp(-x))


def _merge_body(h_ref, oa_ref, ob_ref, wg_ref, woa_ref, wob_ref, o_ref):
    h = h_ref[...]
    ga = jnp.dot(h, wg_ref[:, :D_MODEL], preferred_element_type=F32)
    ya = jnp.dot(oa_ref[...], woa_ref[...], preferred_element_type=F32)
    merged = _sigmoid(ga) * ya
    gb = jnp.dot(h, wg_ref[:, D_MODEL:], preferred_element_type=F32)
    yb = jnp.dot(ob_ref[...], wob_ref[...], preferred_element_type=F32)
    o_ref[...] = (merged + _sigmoid(gb) * yb).astype(o_ref.dtype)


def _merge(h, oa, ob, wg, woa, wob, tm):
    n = h.shape[0]
    return pl.pallas_call(
        _merge_body,
        grid=(n // tm,),
        in_specs=[
            pl.BlockSpec((tm, D_MODEL), lambda i: (i, 0)),
            pl.BlockSpec((tm, H_A * DV_A), lambda i: (i, 0)),
            pl.BlockSpec((tm, H_B * HD_B), lambda i: (i, 0)),
            _resident((D_MODEL, 2 * D_MODEL), lambda i: (0, 0)),
            _resident((H_A * DV_A, D_MODEL), lambda i: (0, 0)),
            _resident((H_B * HD_B, D_MODEL), lambda i: (0, 0)),
        ],
        out_specs=pl.BlockSpec((tm, D_MODEL), lambda i: (i, 0)),
        out_shape=jax.ShapeDtypeStruct((n, D_MODEL), BF16),
        compiler_params=_cparams(("parallel",)),
        name="gated_merge",
    )(h, oa, ob, wg, woa, wob)


def _outproj_body(x_ref, mg_ref, w_ref, g_ref, x1_ref, h2_ref):
    x1 = x_ref[...] + jnp.dot(mg_ref[...], w_ref[...], preferred_element_type=F32)
    x1_ref[...] = x1
    y = x1 * lax.rsqrt(jnp.mean(x1 * x1, axis=-1, keepdims=True) + EPS)
    h2_ref[...] = (y * g_ref[...]).astype(h2_ref.dtype)


def _outproj(x, merged, w_out, g2, tm):
    n = x.shape[0]
    return pl.pallas_call(
        _outproj_body,
        grid=(n // tm,),
        in_specs=[
            pl.BlockSpec((tm, D_MODEL), lambda i: (i, 0)),
            pl.BlockSpec((tm, D_MODEL), lambda i: (i, 0)),
            _resident((D_MODEL, D_MODEL), lambda i: (0, 0)),
            pl.BlockSpec((1, D_MODEL), lambda i: (0, 0)),
        ],
        out_specs=[pl.BlockSpec((tm, D_MODEL), lambda i: (i, 0)), pl.BlockSpec((tm, D_MODEL), lambda i: (i, 0))],
        out_shape=[jax.ShapeDtypeStruct((n, D_MODEL), F32), jax.ShapeDtypeStruct((n, D_MODEL), BF16)],
        compiler_params=_cparams(("parallel",)),
        name="out_proj",
    )(x, merged, w_out, g2.reshape(1, D_MODEL))


def _ffn_body(x1_ref, h2_ref, w1_ref, w2_ref, o_ref):
    f = pl.program_id(1)

    @pl.when(f == 0)
    def _():
        o_ref[...] = x1_ref[...]

    u = jnp.maximum(jnp.dot(h2_ref[...], w1_ref[...], preferred_element_type=F32), 0.0)
    o_ref[...] += jnp.dot((u * u).astype(BF16), w2_ref[...], preferred_element_type=F32)


def _ffn(x1, h2, w1, w2, tm, tf):
    n = x1.shape[0]
    return pl.pallas_call(
        _ffn_body,
        grid=(n // tm, D_FF // tf),
        in_specs=[
            pl.BlockSpec((tm, D_MODEL), lambda i, f: (i, 0)),
            pl.BlockSpec((tm, D_MODEL), lambda i, f: (i, 0)),
            pl.BlockSpec((D_MODEL, tf), lambda i, f: (0, f)),
            pl.BlockSpec((tf, D_MODEL), lambda i, f: (f, 0)),
        ],
        out_specs=pl.BlockSpec((tm, D_MODEL), lambda i, f: (i, 0)),
        out_shape=jax.ShapeDtypeStruct((n, D_MODEL), F32),
        compiler_params=_cparams(("parallel", "arbitrary")),
        name="ffn",
    )(x1, h2, w1, w2)


def _prep_weights(w_in, w_o_a, w_o_b, w_out, w_ff1, w_ff2):
    sizes = (H_A * 2 * HD_A, H_A * 2 * HD_A, H_A * DV_A, H_B * HD_B, HKV_B * HD_B, HKV_B * HD_B,
             H_IDX * D_IDX, D_IDX, H_IDX, D_MODEL, D_MODEL)
    offs = [0]
    for s in sizes:
        offs.append(offs[-1] + s)
    col = lambda a, b: w_in[:, offs[a]:offs[b]]
    w_misc = jnp.concatenate(
        [col(4, 6), col(7, 8), col(7, 8), col(8, 9),
         jnp.zeros((D_MODEL, _MISC_COLS - _WI0 - H_IDX), w_in.dtype)], axis=1)
    return dict(
        qa=col(0, 1).astype(BF16), ka=col(1, 2).astype(BF16), va=col(2, 3).astype(BF16),
        qb=col(3, 4).astype(BF16), misc=w_misc.astype(BF16), qi=col(6, 7).astype(BF16),
        gate=col(9, 11).astype(BF16), oa=w_o_a.astype(BF16), ob=w_o_b.astype(BF16),
        out=w_out.astype(BF16), ff1=w_ff1.astype(BF16), ff2=w_ff2.astype(BF16))


def _layer(x, past, w, p, lam, lam_init, tab_t, *, t_a, t_b, tm):
    b, t, _ = x.shape
    n = b * t
    xf = x.reshape(n, D_MODEL)
    h = _rmsnorm(xf, p["norm1_g"], tm)

    gain_a = lambda g: g.reshape(1, 2 * HD_A)
    (qa,) = _proj(h, w["qa"], [gain_a(p["qn_a_g"])], _epi_qa, [(1024, BF16)], tm, "proj_qa")
    ka, ka_h = _proj(h, w["ka"], [gain_a(p["kn_a_g"])], _epi_ka, [(1024, F32), (1024, BF16)], tm, "proj_ka")
    va, va_h = _proj(h, w["va"], [], _epi_copy2, [(1024, F32), (1024, BF16)], tm, "proj_va")
    (qb,) = _proj(h, w["qb"], [p["qn_b_g"].reshape(1, HD_B)], _epi_qb, [(1024, BF16)], tm, "proj_qb")
    kb, kb_h, vb, vb_h, ki, ki2_h, wi = _proj(
        h, w["misc"], [p["kn_b_g"].reshape(1, HD_B)], _epi_misc,
        [(256, F32), (256, BF16), (256, F32), (256, BF16), (D_IDX, F32), (2 * D_IDX, BF16), (H_IDX, F32)],
        tm, "proj_misc")
    (qi,) = _proj(h, w["qi"], [], _epi_copy1, [(1024, BF16)], tm, "proj_qi")

    new_rows = (ka.reshape(b, t, H_A, 2 * HD_A), va.reshape(b, t, H_A, DV_A), kb.reshape(b, t, HKV_B, HD_B),
                vb.reshape(b, t, HKV_B, HD_B), ki.reshape(b, t, D_IDX))

    def full_keys(new_h, past_arr):
        new_h = new_h.reshape(b, t, -1)
        if past_arr is None:
            return new_h
        past_h = past_arr.reshape(b, past_arr.shape[1], -1).astype(BF16)
        return jnp.concatenate([past_h, new_h], axis=1)

    if past is None:
        pa_k = pa_v = pb_k = pb_v = pb_i = None
    else:
        pa_k, pa_v, pb_k, pb_v, pb_i = past
        pb_i = jnp.concatenate([pb_i, pb_i], axis=-1)
    ka_f, va_f = full_keys(ka_h, pa_k), full_keys(va_h, pa_v)
    kb_f, vb_f, ki_f = full_keys(kb_h, pb_k), full_keys(vb_h, pb_v), full_keys(ki2_h, pb_i)
    n_keys = ka_f.shape[1]
    q0 = n_keys - t
    topk = min(TOPK_MAX, n_keys // 4)
    t_pad = max(t_a, t_b)
    assert t_pad % t_a == 0 and t_pad % t_b == 0 and q0 % t_pad == 0
    nk_pad = -(-n_keys // t_pad) * t_pad
    nq_pad = -(-t // t_pad) * t_pad
    pad_k = lambda a: jnp.pad(a, ((0, 0), (0, nk_pad - n_keys), (0, 0)))
    pad_q = lambda a: jnp.pad(a.reshape(b, t, -1), ((0, 0), (0, nq_pad - t), (0, 0)))
    ka_f, va_f, kb_f, vb_f, ki_f = (pad_k(a) for a in (ka_f, va_f, kb_f, vb_f, ki_f))
    n_valid = lambda tt: n_keys - (n_keys - 1) // tt * tt
    assert all(nq_pad == tt or n_valid(tt) == tt for tt in (t_a, t_b))

    bias_a = _bias_tiles(tab_t[:H_A], t_a, n_valid(t_a))
    heads_t = lambda a, nh: _with_ones_rows(jnp.transpose(a.reshape(b, nk_pad, nh, -1), (0, 2, 3, 1)))
    oa = _attn_a(pad_q(qa), ka_f, heads_t(va_f, H_A), bias_a, p["subln_a_g"].reshape(DV_A, 1), lam,
                 t=t_a, q0=q0, out_scale=1.0 - lam_init)
    wi_t = jnp.swapaxes(pad_q(wi), 1, 2)
    bias_b = _bias_tiles(tab_t[H_A:], t_b, n_valid(t_b))
    ob = _dsa(pad_q(qb), pad_q(qi), wi_t, kb_f, heads_t(vb_f, HKV_B), ki_f, bias_b,
              t=t_b, q0=q0, topk=topk, n_keys=n_keys, n_valid_diag=n_valid(t_b))
    oa = oa[:, :t].reshape(n, H_A * DV_A)
    ob = ob[:, :t].reshape(n, H_B * HD_B)

    merged = _merge(h, oa, ob, w["gate"], w["oa"], w["ob"], min(tm, 256))
    x1, h2 = _outproj(xf, merged, w["out"], p["norm2_g"], min(tm, 256))
    y = _ffn(x1, h2, w["ff1"], w["ff2"], tm, 512)
    return y.reshape(b, t, D_MODEL), new_rows


def kernel(x_prompt, x_sample, cache_a_k, cache_a_v, cache_b_k, cache_b_v, cache_b_kidx, rel_bias, norm1_g, w_in, qn_a_g, kn_a_g, lam_q1, lam_k1, lam_q2, lam_k2, subln_a_g, qn_b_g, kn_b_g, w_o_a, w_o_b, w_out, norm2_g, w_ff1, w_ff2):
    depth = w_in.shape[0]
    tab_t = rel_bias.T.astype(F32)
    y_prompt, y_sample = x_prompt, x_sample
    prompt_rows, sample_rows = [], []
    for l in range(depth):
        lam_init = 0.8 - 0.6 * math.exp(-0.3 * l)
        lam = (jnp.exp(jnp.sum(lam_q1[l].astype(F32) * lam_k1[l].astype(F32)))
               - jnp.exp(jnp.sum(lam_q2[l].astype(F32) * lam_k2[l].astype(F32))) + lam_init).reshape(1)
        w = _prep_weights(w_in[l], w_o_a[l], w_o_b[l], w_out[l], w_ff1[l], w_ff2[l])
        p = dict(norm1_g=norm1_g[l], qn_a_g=qn_a_g[l], kn_a_g=kn_a_g[l], subln_a_g=subln_a_g[l],
                 qn_b_g=qn_b_g[l], kn_b_g=kn_b_g[l], norm2_g=norm2_g[l])
        y_prompt, rp = _layer(y_prompt, None, w, p, lam, lam_init, tab_t, t_a=512, t_b=256, tm=512)
        past = (cache_a_k[l], cache_a_v[l], cache_b_k[l], cache_b_v[l], cache_b_kidx[l])
        y_sample, rs = _layer(y_sample, past, w, p, lam, lam_init, tab_t, t_a=256, t_b=256, tm=512)
        prompt_rows.append(rp)
        sample_rows.append(rs)
    p_rows = tuple(jnp.stack(r, axis=0) for r in zip(*prompt_rows))
    s_rows = tuple(jnp.stack(r, axis=0) for r in zip(*sample_rows))
    return (y_prompt, y_sample) + p_rows + s_rows
```

```python
import functools
import math

import jax
import jax.numpy as jnp
from jax import lax
from jax.experimental import pallas as pl
from jax.experimental.pallas import tpu as pltpu

F32 = jnp.float32
BF16 = jnp.bfloat16
I32 = jnp.int32

D_MODEL = 2048
CHUNK = 64
H_A = 8
DV_A = 128
HD_A = 64
H_B = 8
HD_B = 128
HKV_B = 2
G_B = H_B // HKV_B
H_IDX = 16
D_IDX = 64
TOPK_MAX = 256
N_BUCKETS = 32
D_FF = 4 * D_MODEL
EPS = 1e-6
LANES = 128
NEG = -1e30
LOG2E = math.log2(math.e)
INT_MIN = -(2 ** 31)
INT_MAX = 2 ** 31 - 1
VMEM_LIMIT = 56 * 1024 * 1024

_NT = (((1,), (1,)), ((), ()))


def _cparams(sem):
    return pltpu.CompilerParams(dimension_semantics=sem, vmem_limit_bytes=VMEM_LIMIT)


def _resident(block_shape, index_map):
    return pl.BlockSpec(block_shape, index_map, pipeline_mode=pl.Buffered(1))


def _rmsnorm_body(x_ref, g_ref, o_ref):
    x = x_ref[...]
    y = x * lax.rsqrt(jnp.mean(x * x, axis=-1, keepdims=True) + EPS)
    o_ref[...] = (y * g_ref[...]).astype(o_ref.dtype)


def _rmsnorm(x, g, tm):
    n, d = x.shape
    return pl.pallas_call(
        _rmsnorm_body,
        grid=(n // tm,),
        in_specs=[pl.BlockSpec((tm, d), lambda i: (i, 0)), pl.BlockSpec((1, d), lambda i: (0, 0))],
        out_specs=pl.BlockSpec((tm, d), lambda i: (i, 0)),
        out_shape=jax.ShapeDtypeStruct((n, d), BF16),
        compiler_params=_cparams(("parallel",)),
        name="rmsnorm",
    )(x, g.reshape(1, d))


def _head_norm(z, gain, split):
    lane = lax.broadcasted_iota(I32, (1, LANES), 1)
    lo = lane < (LANES // 2)
    outs = []
    for h in range(z.shape[1] // LANES):
        zh = z[:, h * LANES:(h + 1) * LANES]
        sq = zh * zh
        if split:
            s_lo = jnp.sum(jnp.where(lo, sq, 0.0), axis=-1, keepdims=True)
            s_hi = jnp.sum(jnp.where(lo, 0.0, sq), axis=-1, keepdims=True)
            r = jnp.where(lo, lax.rsqrt(s_lo * (2.0 / LANES) + EPS), lax.rsqrt(s_hi * (2.0 / LANES) + EPS))
        else:
            r = lax.rsqrt(jnp.mean(sq, axis=-1, keepdims=True) + EPS)
        outs.append(zh * r * gain)
    return jnp.concatenate(outs, axis=-1)


def _proj_body(epilogue, n_aux, h_ref, w_ref, *rest):
    z = jnp.dot(h_ref[...], w_ref[...], preferred_element_type=F32)
    outs = epilogue(z, *[r[...] for r in rest[:n_aux]])
    for o_ref, o in zip(rest[n_aux:], outs):
        o_ref[...] = o.astype(o_ref.dtype)


def _proj(h, w, aux, epilogue, out_defs, tm, name):
    n, k = h.shape
    c = w.shape[1]
    in_specs = [pl.BlockSpec((tm, k), lambda i: (i, 0)), _resident((k, c), lambda i: (0, 0))]
    in_specs += [pl.BlockSpec(a.shape, lambda i: (0, 0)) for a in aux]
    return pl.pallas_call(
        functools.partial(_proj_body, epilogue, len(aux)),
        grid=(n // tm,),
        in_specs=in_specs,
        out_specs=[pl.BlockSpec((tm, oc), lambda i: (i, 0)) for oc, _ in out_defs],
        out_shape=[jax.ShapeDtypeStruct((n, oc), dt) for oc, dt in out_defs],
        compiler_params=_cparams(("parallel",)),
        name=name,
    )(h, w, *aux)


def _epi_qa(z, gain):
    return (_head_norm(z, gain, True) * (HD_A ** -0.5 * LOG2E),)


def _epi_ka(z, gain):
    y = _head_norm(z, gain, True)
    return y, y


def _epi_copy2(z):
    return z, z


def _epi_qb(z, gain):
    return (_head_norm(z, gain, False) * (HD_B ** -0.5 * LOG2E),)


def _epi_copy1(z):
    return (z,)


_KB0, _VB0, _KI0, _WI0, _MISC_COLS = 0, 256, 512, 640, 768


def _epi_misc(z, gain):
    kb = _head_norm(z[:, _KB0:_VB0], gain, False)
    vb = z[:, _VB0:_KI0]
    ki2 = z[:, _KI0:_WI0]
    return kb, kb, vb, vb, ki2[:, :D_IDX], ki2, z[:, _WI0:_WI0 + H_IDX]


def _rel_bias(tab_ref, h, rel):
    half = N_BUCKETS // 2
    exact = half // 2
    n = jnp.abs(rel)
    n2 = n * n
    v_neg = jnp.full(rel.shape, tab_ref[h, 0], F32)
    v_pos = jnp.full(rel.shape, tab_ref[h, half], F32)
    for b in range(1, half):
        cond = (n >= b) if b < exact else (n2 >= exact * exact * 2 ** (b - exact))
        v_neg = jnp.where(cond, tab_ref[h, b], v_neg)
        v_pos = jnp.where(cond, tab_ref[h, half + b], v_pos)
    return (jnp.where(rel > 0, v_pos, v_neg) - tab_ref[h, half - 1]) * LOG2E


def _bias_body(t, n_valid_diag, tab_ref, o_ref):
    h = pl.program_id(0)
    kl = lax.broadcasted_iota(I32, (t, t), 0)
    ql = lax.broadcasted_iota(I32, (t, t), 1)
    o_ref[0] = _rel_bias(tab_ref, h, kl - t - ql)
    visible = ((kl // CHUNK) <= (ql // CHUNK)) & (kl < n_valid_diag)
    o_ref[1] = jnp.where(visible, _rel_bias(tab_ref, h, kl - ql), NEG)


def _bias_dense_body(q0, n_keys, tab_ref, o_ref):
    h = pl.program_id(0)
    qpos = q0 + lax.broadcasted_iota(I32, o_ref.shape, 0)
    kpos = lax.broadcasted_iota(I32, o_ref.shape, 1)
    visible = ((kpos // CHUNK) <= (qpos // CHUNK)) & (kpos < n_keys)
    o_ref[...] = jnp.where(visible, _rel_bias(tab_ref, h, kpos - qpos), NEG)


def _bias_dense(tab_t, nq, nk_pad, q0, n_keys):
    nh = tab_t.shape[0]
    return pl.pallas_call(
        functools.partial(_bias_dense_body, q0, n_keys),
        grid=(nh,),
        in_specs=[pl.BlockSpec(memory_space=pltpu.SMEM)],
        out_specs=pl.BlockSpec((None, nq, nk_pad), lambda h: (h, 0, 0)),
        out_shape=jax.ShapeDtypeStruct((nh, nq, nk_pad), F32),
        compiler_params=_cparams(("arbitrary",)),
        name="bias_dense",
    )(tab_t)


def _bias_tiles(tab_t, t, n_valid_diag):
    nh = tab_t.shape[0]
    return pl.pallas_call(
        functools.partial(_bias_body, t, n_valid_diag),
        grid=(nh,),
        in_specs=[pl.BlockSpec(memory_space=pltpu.SMEM)],
        out_specs=pl.BlockSpec((None, 2, t, t), lambda h: (h, 0, 0, 0)),
        out_shape=jax.ShapeDtypeStruct((nh, 2, t, t), F32),
        compiler_params=_cparams(("arbitrary",)),
        name="bias_tiles",
    )(tab_t)


ONES_ROWS = 16


def _with_ones_rows(vt):
    return jnp.concatenate([vt, jnp.ones(vt.shape[:-2] + (ONES_ROWS, vt.shape[-1]), vt.dtype)], axis=-2)


def _softmax_step(s, vt, m_ref, acc_ref):
    m_old = m_ref[...]
    m_new = jnp.maximum(m_old, jnp.max(s, axis=0, keepdims=True))
    alpha = jnp.exp2(m_old - m_new)
    p = jnp.exp2(s - m_new)
    acc_ref[...] = acc_ref[...] * alpha + jnp.dot(vt, p.astype(BF16), preferred_element_type=F32)
    m_ref[...] = m_new


def _normalised(acc, dv):
    return acc[:dv] / acc[dv:dv + 1]


def _pipelined_tiles(n_far, produce, consume, buf0, buf1):
    @pl.when(n_far == -1)
    def _():
        produce(0, buf0, 1)
        consume(0, buf0)

    @pl.when(n_far == 0)
    def _():
        produce(0, buf0, 0)
        produce(1, buf1, 1)
        consume(0, buf0)
        consume(1, buf1)

    @pl.when(n_far >= 1)
    def _():
        produce(0, buf0, None)

    pairs = jnp.maximum(n_far - 1, 0) // 2

    def body(i, carry):
        j = 2 * i
        produce(j + 1, buf1, None)
        consume(j, buf0)
        produce(j + 2, buf0, None)
        consume(j + 1, buf1)
        return carry

    lax.fori_loop(0, pairs, body, 0)
    d = 2 * pairs
    rem = n_far - d

    @pl.when(rem == 1)
    def _():
        produce(d + 1, buf1, 0)
        consume(d, buf0)
        produce(d + 2, buf0, 1)
        consume(d + 1, buf1)
        consume(d + 2, buf0)

    @pl.when(rem == 2)
    def _():
        produce(d + 1, buf1, None)
        consume(d, buf0)
        produce(d + 2, buf0, 0)
        consume(d + 1, buf1)
        produce(d + 3, buf1, 1)
        consume(d + 2, buf0)
        consume(d + 3, buf1)


def _attn_a_body(t, q0, out_scale, lam_ref, q_ref, k_ref, vt_ref, bias_ref, g_ref, o_ref,
                 qz_ref, m_ref, acc_ref, s0_ref, s1_ref):
    i = pl.program_id(2)
    n_far = q0 // t + i - 1
    q = q_ref[...]
    lane = lax.broadcasted_iota(I32, (t, LANES), 1)
    zero = jnp.zeros_like(q)
    qz_ref[:t] = jnp.where(lane < HD_A, q, zero)
    qz_ref[t:] = jnp.where(lane < HD_A, zero, q)
    m_ref[...] = jnp.full(m_ref.shape, NEG, F32)
    acc_ref[...] = jnp.zeros(acc_ref.shape, F32)

    def produce(j, s_ref, kind):
        ks = pl.multiple_of(j * t, t)
        s = lax.dot_general(k_ref[pl.ds(ks, t), :], qz_ref[...], _NT, preferred_element_type=F32)
        if kind is not None:
            b = bias_ref[kind]
            s = jnp.concatenate([s[:, :t] + b, s[:, t:] + b], axis=1)
        s_ref[...] = s

    def consume(j, s_ref):
        _softmax_step(s_ref[...], vt_ref[:, pl.ds(pl.multiple_of(j * t, t), t)], m_ref, acc_ref)

    _pipelined_tiles(n_far, produce, consume, s0_ref, s1_ref)

    lam = lam_ref[0]
    o = _normalised(acc_ref[...], DV_A)
    o = o[:, :t] - lam * o[:, t:]
    y = o * lax.rsqrt(jnp.mean(o * o, axis=0, keepdims=True) + EPS)
    y = (y * g_ref[...]) * out_scale
    o_ref[...] = y.T.astype(o_ref.dtype)


def _attn_a(q, k, vt, bias, subln_g, lam, *, t, q0, out_scale):
    b, nq, _ = q.shape
    nk = k.shape[1]
    return pl.pallas_call(
        functools.partial(_attn_a_body, t, q0, out_scale),
        grid=(b, H_A, nq // t),
        in_specs=[
            pl.BlockSpec(memory_space=pltpu.SMEM),
            pl.BlockSpec((None, t, LANES), lambda bb, h, i: (bb, i, h)),
            pl.BlockSpec((None, nk, LANES), lambda bb, h, i: (bb, 0, h)),
            pl.BlockSpec((None, None, DV_A + ONES_ROWS, nk), lambda bb, h, i: (bb, h, 0, 0)),
            pl.BlockSpec((None, 2, t, t), lambda bb, h, i: (h, 0, 0, 0)),
            pl.BlockSpec((DV_A, 1), lambda bb, h, i: (0, 0)),
        ],
        out_specs=pl.BlockSpec((None, t, LANES), lambda bb, h, i: (bb, i, h)),
        out_shape=jax.ShapeDtypeStruct((b, nq, H_A * DV_A), BF16),
        scratch_shapes=[
            pltpu.VMEM((2 * t, LANES), BF16),
            pltpu.VMEM((1, 2 * t), F32),
            pltpu.VMEM((DV_A + ONES_ROWS, 2 * t), F32),
            pltpu.VMEM((t, 2 * t), F32),
            pltpu.VMEM((t, 2 * t), F32),
        ],
        compiler_params=_cparams(("parallel", "parallel", "arbitrary")),
        name="diff_attn",
    )(lam, q, k, vt, bias, subln_g)


def _order_key(x):
    b = pltpu.bitcast(x, I32)
    return jnp.where(b < 0, b ^ INT_MAX, b)


def _order_unkey(k):
    return pltpu.bitcast(jnp.where(k < 0, k ^ INT_MAX, k), F32)


def _dsa_body(t, q0, topk, n_keys, n_valid_diag, qb_ref, qi_ref, wi_ref, kb_ref, vbt_ref, ki_ref, bias_ref, o_ref,
              keys_ref, qz_ref, qs_ref, m_ref, acc_ref, s0_ref, s1_ref):
    i = pl.program_id(1)
    n_tiles = q0 // t + i + 1
    n_far = n_tiles - 2

    lane = lax.broadcasted_iota(I32, (t, LANES), 1)
    for hp in range(H_IDX // 2):
        qs = qi_ref[:, hp * LANES:(hp + 1) * LANES]
        zero = jnp.zeros_like(qs)
        qz_ref[2 * hp] = jnp.where(lane < D_IDX, qs, zero)
        qz_ref[2 * hp + 1] = jnp.where(lane < D_IDX, zero, qs)

    def score_tile(j):
        kt = ki_ref[pl.ds(pl.multiple_of(j * t, t), t), :]
        acc = jnp.zeros((t, t), F32)
        for h in range(H_IDX):
            s = lax.dot_general(kt, qz_ref[h], _NT, preferred_element_type=F32)
            acc = acc + jnp.maximum(s, 0.0) * wi_ref[h:h + 1, :]
        return acc

    def fold8(x, op):
        return op(x.reshape(x.shape[0] // 8, 8, t), axis=0)

    def score_body(j, carry):
        smin, smax = carry
        sc = score_tile(j)
        keys_ref[pl.ds(pl.multiple_of(j * t, t), t), :] = _order_key(sc)
        return jnp.minimum(smin, fold8(sc, jnp.min)), jnp.maximum(smax, fold8(sc, jnp.max))

    smin, smax = lax.fori_loop(0, n_tiles - 1, score_body,
                               (jnp.full((8, t), jnp.inf, F32), jnp.full((8, t), -jnp.inf, F32)))
    jd = n_tiles - 1
    kl = lax.broadcasted_iota(I32, (t, t), 0)
    ql = lax.broadcasted_iota(I32, (t, t), 1)
    admissible = ((kl // CHUNK) <= (ql // CHUNK)) & (kl < n_valid_diag)
    sc = score_tile(jd)
    keys_ref[pl.ds(pl.multiple_of(jd * t, t), t), :] = jnp.where(admissible, _order_key(sc), INT_MIN)
    smin = jnp.minimum(smin, fold8(jnp.where(admissible, sc, jnp.inf), jnp.min))
    smax = jnp.maximum(smax, fold8(jnp.where(admissible, sc, -jnp.inf), jnp.max))
    keys_ref[pl.ds(pl.multiple_of(n_tiles * t, t), t), :] = jnp.full((t, t), INT_MIN, I32)
    n_pairs = (n_tiles + 1) // 2

    def count_pairs(hit_fn):
        def body(j, c):
            r0 = pl.multiple_of(j * 2 * t, 2 * t)
            hit = hit_fn(keys_ref[pl.ds(r0, 2 * t), :], r0)
            return c + jnp.sum(hit.reshape(2 * t // 32, 32, t), axis=0)
        part = lax.fori_loop(0, n_pairs, body, jnp.zeros((32, t), F32))
        return jnp.sum(part, axis=0, keepdims=True)

    def search_cond(state):
        return jnp.logical_and(state[0] < 4 * 34, state[-1] > 0)

    def search_body(state):
        it, lo, hi, c_lo, c_hi, _ = state
        f_lo, f_hi = _order_unkey(lo), _order_unkey(hi)
        a, b = jnp.log(c_lo + 0.5), jnp.log(c_hi + 0.5)
        frac = jnp.clip((a - math.log(topk)) / jnp.maximum(a - b, 1e-9), 0.0, 1.0)
        guess = jnp.minimum(jnp.maximum(_order_key(f_lo + (f_hi - f_lo) * frac), lo + 1), hi - 1)
        mid = jnp.where(it % 4 == 3, (lo >> 1) + (hi >> 1) + (lo & hi & 1), guess)
        c = count_pairs(lambda kk, r0: jnp.where(kk >= mid, 1.0, 0.0))
        live = lo + 1 < hi
        ge = c >= topk
        up = jnp.logical_and(live, ge)
        down = jnp.logical_and(live, jnp.logical_not(ge))
        lo = jnp.where(up, mid, lo)
        c_lo = jnp.where(up, c, c_lo)
        hi = jnp.where(jnp.logical_and(live, c == topk), mid + 1, jnp.where(down, mid, hi))
        c_hi = jnp.where(down, c, c_hi)
        return it + 1, lo, hi, c_lo, c_hi, jnp.max(jnp.where(lo + 1 < hi, 1, 0))

    qpos = q0 + i * t + lax.broadcasted_iota(I32, (1, t), 1)
    n_adm = jnp.minimum((qpos // CHUNK + 1) * CHUNK, n_keys).astype(F32)
    lo0 = _order_key(jnp.min(smin, axis=0, keepdims=True))
    hi0 = jnp.where(n_adm <= topk, lo0 + 1, _order_key(jnp.max(smax, axis=0, keepdims=True)) + 1)
    _, thr, _, c_thr, c_above, _ = lax.while_loop(
        search_cond, search_body, (jnp.int32(0), lo0, hi0, n_adm, jnp.zeros((1, t), F32), jnp.int32(1)))

    @pl.when(jnp.max(jnp.where(jnp.logical_and(c_thr > topk, qpos < n_keys), 1, 0)) > 0)
    def _():
        need = topk - c_above
        row = lax.broadcasted_iota(I32, (2 * t, t), 0)

        def cut_body(_, st):
            p_lo, p_hi = st
            p_mid = (p_lo + p_hi) >> 1
            c = count_pairs(lambda kk, r0: jnp.where(
                jnp.logical_and(kk == thr, row + r0 <= p_mid), 1.0, 0.0))
            ok = c >= need
            return jnp.where(ok, p_lo, p_mid), jnp.where(ok, p_mid, p_hi)

        n_pos = keys_ref.shape[0]
        _, cut = lax.fori_loop(0, n_pos.bit_length(), cut_body,
                               (jnp.full((1, t), -1, I32), jnp.full((1, t), n_pos - 1, I32)))

        def demote_body(j, carry):
            r0 = pl.multiple_of(j * 2 * t, 2 * t)
            kk = keys_ref[pl.ds(r0, 2 * t), :]
            keys_ref[pl.ds(r0, 2 * t), :] = jnp.where(jnp.logical_and(kk == thr, row + r0 > cut), thr - 1, kk)
            return carry

        lax.fori_loop(0, n_pairs, demote_body, 0)

    m_ref[...] = jnp.full(m_ref.shape, NEG, F32)
    acc_ref[...] = jnp.zeros(acc_ref.shape, F32)

    for h in range(H_B):
        qs_ref[h // G_B, (h % G_B) * t:(h % G_B + 1) * t, :] = qb_ref[:, h * HD_B:(h + 1) * HD_B]

    def produce(j, s_ref, kind):
        ks = pl.multiple_of(j * t, t)
        off = jnp.where(keys_ref[pl.ds(ks, t), :] >= thr, 0.0, NEG)
        for n in range(HKV_B):
            s = lax.dot_general(kb_ref[pl.ds(ks, t), n * HD_B:(n + 1) * HD_B], qs_ref[n], _NT,
                                preferred_element_type=F32)
            parts = []
            for g in range(G_B):
                off_g = off if kind is None else off + bias_ref[n * G_B + g, kind]
                parts.append(s[:, g * t:(g + 1) * t] + off_g)
            s_ref[n] = jnp.concatenate(parts, axis=1)

    def consume(j, s_ref):
        ks = pl.multiple_of(j * t, t)
        for n in range(HKV_B):
            _softmax_step(s_ref[n], vbt_ref[n, :, pl.ds(ks, t)], m_ref.at[n], acc_ref.at[n])

    _pipelined_tiles(n_far, produce, consume, s0_ref, s1_ref)

    for n in range(HKV_B):
        o = _normalised(acc_ref[n], HD_B)
        for g in range(G_B):
            h = n * G_B + g
            o_ref[:, h * HD_B:(h + 1) * HD_B] = o[:, g * t:(g + 1) * t].T.astype(o_ref.dtype)


def _dsa(qb, qi, wi_t, kb, vbt, ki2, bias, *, t, q0, topk, n_keys, n_valid_diag):
    b, nq, _ = qb.shape
    nk = kb.shape[1]
    return pl.pallas_call(
        functools.partial(_dsa_body, t, q0, topk, n_keys, n_valid_diag),
        grid=(b, nq // t),
        in_specs=[
            pl.BlockSpec((None, t, H_B * HD_B), lambda bb, i: (bb, i, 0)),
            pl.BlockSpec((None, t, H_IDX * D_IDX), lambda bb, i: (bb, i, 0)),
            pl.BlockSpec((None, H_IDX, t), lambda bb, i: (bb, 0, i)),
            _resident((None, nk, HKV_B * HD_B), lambda bb, i: (bb, 0, 0)),
            _resident((None, HKV_B, HD_B + ONES_ROWS, nk), lambda bb, i: (bb, 0, 0, 0)),
            _resident((None, nk, LANES), lambda bb, i: (bb, 0, 0)),
            _resident((H_B, 2, t, t), lambda bb, i: (0, 0, 0, 0)),
        ],
        out_specs=pl.BlockSpec((None, t, H_B * HD_B), lambda bb, i: (bb, i, 0)),
        out_shape=jax.ShapeDtypeStruct((b, nq, H_B * HD_B), BF16),
        scratch_shapes=[
            pltpu.VMEM((nk + t, t), I32),
            pltpu.VMEM((H_IDX, t, LANES), BF16),
            pltpu.VMEM((HKV_B, G_B * t, HD_B), BF16),
            pltpu.VMEM((HKV_B, 1, G_B * t), F32),
            pltpu.VMEM((HKV_B, HD_B + ONES_ROWS, G_B * t), F32),
            pltpu.VMEM((HKV_B, t, G_B * t), F32),
            pltpu.VMEM((HKV_B, t, G_B * t), F32),
        ],
        compiler_params=_cparams(("parallel", "arbitrary")),
        name="dsa",
    )(qb, qi, wi_t, kb, vbt, ki2, bias)


def _softmax_rows(s, v):
    p = jnp.exp2(s - jnp.max(s, axis=-1, keepdims=True))
    o = jnp.dot(p.astype(BF16), v, preferred_element_type=F32)
    return o / jnp.sum(p, axis=-1, keepdims=True)


def _short_a_body(nq, out_scale, lam_ref, q_ref, k_ref, v_ref, bias_ref, g_ref, o_ref):
    lam = lam_ref[0]
    lane = lax.broadcasted_iota(I32, (nq, LANES), 1)
    for h in range(H_A):
        cols = slice(h * LANES, (h + 1) * LANES)
        q = q_ref[:, cols]
        zero = jnp.zeros_like(q)
        qz = jnp.concatenate([jnp.where(lane < HD_A, q, zero), jnp.where(lane < HD_A, zero, q)], axis=0)
        s = lax.dot_general(qz, k_ref[:, cols], _NT, preferred_element_type=F32)
        b = bias_ref[h]
        o = _softmax_rows(s + jnp.concatenate([b, b], axis=0), v_ref[:, cols])
        o = o[:nq] - lam * o[nq:]
        y = o * lax.rsqrt(jnp.mean(o * o, axis=-1, keepdims=True) + EPS)
        o_ref[:, cols] = ((y * g_ref[...]) * out_scale).astype(o_ref.dtype)


def _short_a(q, k, v, bias, subln_g, lam, *, out_scale):
    b, nq, d = q.shape
    nk = k.shape[1]
    return pl.pallas_call(
        functools.partial(_short_a_body, nq, out_scale),
        grid=(b,),
        in_specs=[
            pl.BlockSpec(memory_space=pltpu.SMEM),
            pl.BlockSpec((None, nq, d), lambda bb: (bb, 0, 0)),
            pl.BlockSpec((None, nk, d), lambda bb: (bb, 0, 0)),
            pl.BlockSpec((None, nk, d), lambda bb: (bb, 0, 0)),
            pl.BlockSpec((H_A, nq, nk), lambda bb: (0, 0, 0)),
            pl.BlockSpec((1, DV_A), lambda bb: (0, 0)),
        ],
        out_specs=pl.BlockSpec((None, nq, d), lambda bb: (bb, 0, 0)),
        out_shape=jax.ShapeDtypeStruct((b, nq, d), BF16),
        compiler_params=_cparams(("parallel",)),
        name="diff_attn_short",
    )(lam, q, k, v, bias, subln_g)


def _short_dsa_body(nq, q0, topk, n_keys, qb_ref, qi_ref, wi_ref, kb_ref, vb_ref, ki_ref, bias_ref, o_ref):
    nk = ki_ref.shape[0]
    lane = lax.broadcasted_iota(I32, (nq, LANES), 1)

    qz = []
    for h in range(H_IDX):
        qs = qi_ref[:, (h // 2) * LANES:(h // 2 + 1) * LANES]
        zero = jnp.zeros_like(qs)
        qz.append(jnp.where(lane < D_IDX, qs, zero) if h % 2 == 0 else jnp.where(lane < D_IDX, zero, qs))
    s_idx = lax.dot_general(jnp.concatenate(qz, axis=0), ki_ref[...], _NT, preferred_element_type=F32)
    score = jnp.zeros((nq, nk), F32)
    for h in range(H_IDX):
        score = score + jnp.maximum(s_idx[h * nq:(h + 1) * nq], 0.0) * wi_ref[:, h:h + 1]
    qpos = q0 + lax.broadcasted_iota(I32, (nq, nk), 0)
    kpos = lax.broadcasted_iota(I32, (nq, nk), 1)
    admissible = ((kpos // CHUNK) <= (qpos // CHUNK)) & (kpos < n_keys)
    keys = jnp.where(admissible, _order_key(score), INT_MIN)

    def count(hit):
        return jnp.sum(jnp.where(hit, 1.0, 0.0), axis=-1, keepdims=True)

    def search_body(state):
        it, lo, hi, c_lo, c_hi, _ = state
        mid = (lo >> 1) + (hi >> 1) + (lo & hi & 1)
        c = count(keys >= mid)
        live = lo + 1 < hi
        ge = c >= topk
        up = jnp.logical_and(live, ge)
        down = jnp.logical_and(live, jnp.logical_not(ge))
        lo = jnp.where(up, mid, lo)
        c_lo = jnp.where(up, c, c_lo)
        hi = jnp.where(jnp.logical_and(live, c == topk), mid + 1, jnp.where(down, mid, hi))
        c_hi = jnp.where(down, c, c_hi)
        return it + 1, lo, hi, c_lo, c_hi, jnp.max(jnp.where(lo + 1 < hi, 1, 0))

    n_adm = count(admissible)
    lo0 = jnp.full((nq, 1), INT_MIN + 1, I32)
    hi0 = jnp.where(n_adm <= topk, lo0 + 1, INT_MAX)
    _, thr, _, c_thr, c_above, _ = lax.while_loop(
        lambda st: jnp.logical_and(st[0] < 34, st[-1] > 0), search_body,
        (jnp.int32(0), lo0, hi0, n_adm, jnp.zeros((nq, 1), F32), jnp.int32(1)))

    need = topk - c_above

    def cut_body(_, st):
        p_lo, p_hi = st
        p_mid = (p_lo + p_hi) >> 1
        ok = count(jnp.logical_and(keys == thr, kpos <= p_mid)) >= need
        return jnp.where(ok, p_lo, p_mid), jnp.where(ok, p_mid, p_hi)

    _, cut = lax.fori_loop(0, nk.bit_length(), cut_body,
                           (jnp.full((nq, 1), -1, I32), jnp.full((nq, 1), nk - 1, I32)))
    selected = jnp.logical_and(keys >= thr, jnp.logical_not(jnp.logical_and(keys == thr, kpos > cut)))
    off = jnp.where(selected, 0.0, NEG)

    for n in range(HKV_B):
        heads = range(n * G_B, (n + 1) * G_B)
        qs = jnp.concatenate([qb_ref[:, h * HD_B:(h + 1) * HD_B] for h in heads], axis=0)
        s = lax.dot_general(qs, kb_ref[:, n * HD_B:(n + 1) * HD_B], _NT, preferred_element_type=F32)
        s = s + jnp.concatenate([bias_ref[h] + off for h in heads], axis=0)
        o = _softmax_rows(s, vb_ref[:, n * HD_B:(n + 1) * HD_B])
        for g, h in enumerate(heads):
            o_ref[:, h * HD_B:(h + 1) * HD_B] = o[g * nq:(g + 1) * nq].astype(o_ref.dtype)


def _short_dsa(qb, qi, wi, kb, vb, ki2, bias, *, q0, topk, n_keys):
    b, nq, d = qb.shape
    nk = kb.shape[1]
    per_batch = lambda shape: pl.BlockSpec((None,) + shape, lambda bb: (bb, 0, 0))
    return pl.pallas_call(
        functools.partial(_short_dsa_body, nq, q0, topk, n_keys),
        grid=(b,),
        in_specs=[per_batch((nq, d)), per_batch((nq, d)), per_batch((nq, H_IDX)),
                  per_batch((nk, HKV_B * HD_B)), per_batch((nk, HKV_B * HD_B)), per_batch((nk, LANES)),
                  pl.BlockSpec((H_B, nq, nk), lambda bb: (0, 0, 0))],
        out_specs=per_batch((nq, d)),
        out_shape=jax.ShapeDtypeStruct((b, nq, d), BF16),
        compiler_params=_cparams(("parallel",)),
        name="dsa_short",
    )(qb, qi, wi, kb, vb, ki2, bias)


def _sigmoid(x):
    return 1.0 / (1.0 + jnp.exp(-x))


def _merge_body(h_ref, oa_ref, ob_ref, wg_ref, woa_ref, wob_ref, o_ref):
    h = h_ref[...]
    ga = jnp.dot(h, wg_ref[:, :D_MODEL], preferred_element_type=F32)
    ya = jnp.dot(oa_ref[...], woa_ref[...], preferred_element_type=F32)
    merged = _sigmoid(ga) * ya
    gb = jnp.dot(h, wg_ref[:, D_MODEL:], preferred_element_type=F32)
    yb = jnp.dot(ob_ref[...], wob_ref[...], preferred_element_type=F32)
    o_ref[...] = (merged + _sigmoid(gb) * yb).astype(o_ref.dtype)


def _merge(h, oa, ob, wg, woa, wob, tm):
    n = h.shape[0]
    return pl.pallas_call(
        _merge_body,
        grid=(n // tm,),
        in_specs=[
            pl.BlockSpec((tm, D_MODEL), lambda i: (i, 0)),
            pl.BlockSpec((tm, H_A * DV_A), lambda i: (i, 0)),
            pl.BlockSpec((tm, H_B * HD_B), lambda i: (i, 0)),
            _resident((D_MODEL, 2 * D_MODEL), lambda i: (0, 0)),
            _resident((H_A * DV_A, D_MODEL), lambda i: (0, 0)),
            _resident((H_B * HD_B, D_MODEL), lambda i: (0, 0)),
        ],
        out_specs=pl.BlockSpec((tm, D_MODEL), lambda i: (i, 0)),
        out_shape=jax.ShapeDtypeStruct((n, D_MODEL), BF16),
        compiler_params=_cparams(("parallel",)),
        name="gated_merge",
    )(h, oa, ob, wg, woa, wob)


def _outproj_body(x_ref, mg_ref, w_ref, g_ref, x1_ref, h2_ref):
    x1 = x_ref[...] + jnp.dot(mg_ref[...], w_ref[...], preferred_element_type=F32)
    x1_ref[...] = x1
    y = x1 * lax.rsqrt(jnp.mean(x1 * x1, axis=-1, keepdims=True) + EPS)
    h2_ref[...] = (y * g_ref[...]).astype(h2_ref.dtype)


def _outproj(x, merged, w_out, g2, tm):
    n = x.shape[0]
    return pl.pallas_call(
        _outproj_body,
        grid=(n // tm,),
        in_specs=[
            pl.BlockSpec((tm, D_MODEL), lambda i: (i, 0)),
            pl.BlockSpec((tm, D_MODEL), lambda i: (i, 0)),
            _resident((D_MODEL, D_MODEL), lambda i: (0, 0)),
            pl.BlockSpec((1, D_MODEL), lambda i: (0, 0)),
        ],
        out_specs=[pl.BlockSpec((tm, D_MODEL), lambda i: (i, 0)), pl.BlockSpec((tm, D_MODEL), lambda i: (i, 0))],
        out_shape=[jax.ShapeDtypeStruct((n, D_MODEL), F32), jax.ShapeDtypeStruct((n, D_MODEL), BF16)],
        compiler_params=_cparams(("parallel",)),
        name="out_proj",
    )(x, merged, w_out, g2.reshape(1, D_MODEL))


def _ffn_body(x1_ref, h2_ref, w1_ref, w2_ref, o_ref):
    f = pl.program_id(1)

    @pl.when(f == 0)
    def _():
        o_ref[...] = x1_ref[...]

    u = jnp.maximum(jnp.dot(h2_ref[...], w1_ref[...], preferred_element_type=F32), 0.0)
    o_ref[...] += jnp.dot((u * u).astype(BF16), w2_ref[...], preferred_element_type=F32)


def _ffn(x1, h2, w1, w2, tm, tf):
    n = x1.shape[0]
    return pl.pallas_call(
        _ffn_body,
        grid=(n // tm, D_FF // tf),
        in_specs=[
            pl.BlockSpec((tm, D_MODEL), lambda i, f: (i, 0)),
            pl.BlockSpec((tm, D_MODEL), lambda i, f: (i, 0)),
            pl.BlockSpec((D_MODEL, tf), lambda i, f: (0, f)),
            pl.BlockSpec((tf, D_MODEL), lambda i, f: (f, 0)),
        ],
        out_specs=pl.BlockSpec((tm, D_MODEL), lambda i, f: (i, 0)),
        out_shape=jax.ShapeDtypeStruct((n, D_MODEL), F32),
        compiler_params=_cparams(("parallel", "arbitrary")),
        name="ffn",
    )(x1, h2, w1, w2)


def _prep_weights(w_in, w_o_a, w_o_b, w_out, w_ff1, w_ff2):
    sizes = (H_A * 2 * HD_A, H_A * 2 * HD_A, H_A * DV_A, H_B * HD_B, HKV_B * HD_B, HKV_B * HD_B,
             H_IDX * D_IDX, D_IDX, H_IDX, D_MODEL, D_MODEL)
    offs = [0]
    for s in sizes:
        offs.append(offs[-1] + s)
    col = lambda a, b: w_in[:, offs[a]:offs[b]]
    w_misc = jnp.concatenate(
        [col(4, 6), col(7, 8), col(7, 8), col(8, 9),
         jnp.zeros((D_MODEL, _MISC_COLS - _WI0 - H_IDX), w_in.dtype)], axis=1)
    return dict(
        qa=col(0, 1).astype(BF16), ka=col(1, 2).astype(BF16), va=col(2, 3).astype(BF16),
        qb=col(3, 4).astype(BF16), misc=w_misc.astype(BF16), qi=col(6, 7).astype(BF16),
        gate=col(9, 11).astype(BF16), oa=w_o_a.astype(BF16), ob=w_o_b.astype(BF16),
        out=w_out.astype(BF16), ff1=w_ff1.astype(BF16), ff2=w_ff2.astype(BF16))


def _layer(x, past, w, p, lam, lam_init, tab_t, *, tm, t_a=None, t_b=None):
    short = t_a is None
    b, t, _ = x.shape
    n = b * t
    xf = x.reshape(n, D_MODEL)
    h = _rmsnorm(xf, p["norm1_g"], tm)

    gain_a = lambda g: g.reshape(1, 2 * HD_A)
    (qa,) = _proj(h, w["qa"], [gain_a(p["qn_a_g"])], _epi_qa, [(1024, BF16)], tm, "proj_qa")
    ka, ka_h = _proj(h, w["ka"], [gain_a(p["kn_a_g"])], _epi_ka, [(1024, F32), (1024, BF16)], tm, "proj_ka")
    va, va_h = _proj(h, w["va"], [], _epi_copy2, [(1024, F32), (1024, BF16)], tm, "proj_va")
    (qb,) = _proj(h, w["qb"], [p["qn_b_g"].reshape(1, HD_B)], _epi_qb, [(1024, BF16)], tm, "proj_qb")
    kb, kb_h, vb, vb_h, ki, ki2_h, wi = _proj(
        h, w["misc"], [p["kn_b_g"].reshape(1, HD_B)], _epi_misc,
        [(256, F32), (256, BF16), (256, F32), (256, BF16), (D_IDX, F32), (2 * D_IDX, BF16), (H_IDX, F32)],
        tm, "proj_misc")
    (qi,) = _proj(h, w["qi"], [], _epi_copy1, [(1024, BF16)], tm, "proj_qi")

    new_rows = (ka.reshape(b, t, H_A, 2 * HD_A), va.reshape(b, t, H_A, DV_A), kb.reshape(b, t, HKV_B, HD_B),
                vb.reshape(b, t, HKV_B, HD_B), ki.reshape(b, t, D_IDX))

    def full_keys(new_h, past_arr):
        new_h = new_h.reshape(b, t, -1)
        if past_arr is None:
            return new_h
        past_h = past_arr.reshape(b, past_arr.shape[1], -1).astype(BF16)
        return jnp.concatenate([past_h, new_h], axis=1)

    if past is None:
        pa_k = pa_v = pb_k = pb_v = pb_i = None
    else:
        pa_k, pa_v, pb_k, pb_v, pb_i = past
        pb_i = jnp.concatenate([pb_i, pb_i], axis=-1)
    ka_f, va_f = full_keys(ka_h, pa_k), full_keys(va_h, pa_v)
    kb_f, vb_f, ki_f = full_keys(kb_h, pb_k), full_keys(vb_h, pb_v), full_keys(ki2_h, pb_i)
    n_keys = ka_f.shape[1]
    q0 = n_keys - t
    topk = min(TOPK_MAX, n_keys // 4)
    if short:
        nk_pad = -(-n_keys // LANES) * LANES
        pad_k = lambda a: jnp.pad(a, ((0, 0), (0, nk_pad - n_keys), (0, 0)))
        ka_f, va_f, kb_f, vb_f, ki_f = (pad_k(a) for a in (ka_f, va_f, kb_f, vb_f, ki_f))
        per_b = lambda a: a.reshape(b, t, -1)
        oa = _short_a(per_b(qa), ka_f, va_f, _bias_dense(tab_t[:H_A], t, nk_pad, q0, n_keys),
                      p["subln_a_g"].reshape(1, DV_A), lam, out_scale=1.0 - lam_init)
        ob = _short_dsa(per_b(qb), per_b(qi), per_b(wi), kb_f, vb_f, ki_f,
                        _bias_dense(tab_t[H_A:], t, nk_pad, q0, n_keys), q0=q0, topk=topk, n_keys=n_keys)
    else:
        oa, ob = _tiled_mixers(qa, qb, qi, wi, ka_f, va_f, kb_f, vb_f, ki_f, tab_t, p, lam, lam_init,
                               b=b, t=t, n_keys=n_keys, topk=topk, t_a=t_a, t_b=t_b)
    oa = oa.reshape(n, H_A * DV_A)
    ob = ob.reshape(n, H_B * HD_B)

    merged = _merge(h, oa, ob, w["gate"], w["oa"], w["ob"], min(tm, 256))
    x1, h2 = _outproj(xf, merged, w["out"], p["norm2_g"], min(tm, 256))
    y = _ffn(x1, h2, w["ff1"], w["ff2"], tm, 1024)
    return y.reshape(b, t, D_MODEL), new_rows


def _tiled_mixers(qa, qb, qi, wi, ka_f, va_f, kb_f, vb_f, ki_f, tab_t, p, lam, lam_init,
                  *, b, t, n_keys, topk, t_a, t_b):
    q0 = n_keys - t
    t_pad = max(t_a, t_b)
    assert t_pad % t_a == 0 and t_pad % t_b == 0 and q0 % t_pad == 0
    nk_pad = -(-n_keys // t_pad) * t_pad
    pad_k = lambda a: jnp.pad(a, ((0, 0), (0, nk_pad - n_keys), (0, 0)))
    pad_q = lambda a, tt: jnp.pad(a.reshape(b, t, -1), ((0, 0), (0, -(-t // tt) * tt - t), (0, 0)))
    ka_f, va_f, kb_f, vb_f, ki_f = (pad_k(a) for a in (ka_f, va_f, kb_f, vb_f, ki_f))
    n_valid = lambda tt: n_keys - (n_keys - 1) // tt * tt
    assert all(t <= tt or n_valid(tt) == tt for tt in (t_a, t_b))

    bias_a = _bias_tiles(tab_t[:H_A], t_a, n_valid(t_a))
    heads_t = lambda a, nh: _with_ones_rows(jnp.transpose(a.reshape(b, nk_pad, nh, -1), (0, 2, 3, 1)))
    oa = _attn_a(pad_q(qa, t_a), ka_f, heads_t(va_f, H_A), bias_a, p["subln_a_g"].reshape(DV_A, 1), lam,
                 t=t_a, q0=q0, out_scale=1.0 - lam_init)
    wi_t = jnp.swapaxes(pad_q(wi, t_b), 1, 2)
    bias_b = _bias_tiles(tab_t[H_A:], t_b, n_valid(t_b))
    ob = _dsa(pad_q(qb, t_b), pad_q(qi, t_b), wi_t, kb_f, heads_t(vb_f, HKV_B), ki_f, bias_b,
              t=t_b, q0=q0, topk=topk, n_keys=n_keys, n_valid_diag=n_valid(t_b))
    return oa[:, :t], ob[:, :t]


def kernel(x_prompt, x_sample, cache_a_k, cache_a_v, cache_b_k, cache_b_v, cache_b_kidx, rel_bias, norm1_g, w_in, qn_a_g, kn_a_g, lam_q1, lam_k1, lam_q2, lam_k2, subln_a_g, qn_b_g, kn_b_g, w_o_a, w_o_b, w_out, norm2_g, w_ff1, w_ff2):
    depth = w_in.shape[0]
    tab_t = rel_bias.T.astype(F32)
    y_prompt, y_sample = x_prompt, x_sample
    prompt_rows, sample_rows = [], []
    for l in range(depth):
        lam_init = 0.8 - 0.6 * math.exp(-0.3 * l)
        lam = (jnp.exp(jnp.sum(lam_q1[l].astype(F32) * lam_k1[l].astype(F32)))
               - jnp.exp(jnp.sum(lam_q2[l].astype(F32) * lam_k2[l].astype(F32))) + lam_init).reshape(1)
        w = _prep_weights(w_in[l], w_o_a[l], w_o_b[l], w_out[l], w_ff1[l], w_ff2[l])
        p = dict(norm1_g=norm1_g[l], qn_a_g=qn_a_g[l], kn_a_g=kn_a_g[l], subln_a_g=subln_a_g[l],
                 qn_b_g=qn_b_g[l], kn_b_g=kn_b_g[l], norm2_g=norm2_g[l])
        y_prompt, rp = _layer(y_prompt, None, w, p, lam, lam_init, tab_t, t_a=512, t_b=256, tm=512)
        past = (cache_a_k[l], cache_a_v[l], cache_b_k[l], cache_b_v[l], cache_b_kidx[l])
        y_sample, rs = _layer(y_sample, past, w, p, lam, lam_init, tab_t, tm=512)
        prompt_rows.append(rp)
        sample_rows.append(rs)
    p_rows = tuple(jnp.stack(r, axis=0) for r in zip(*prompt_rows))
    s_rows = tuple(jnp.stack(r, axis=0) for r in zip(*sample_rows))
    return (y_prompt, y_sample) + p_rows + s_rows
```

```python
import functools
import math

import jax
import jax.numpy as jnp
from jax import lax
from jax.experimental import pallas as pl
from jax.experimental.pallas import tpu as pltpu

F32 = jnp.float32
BF16 = jnp.bfloat16
I32 = jnp.int32

D_MODEL = 2048
CHUNK = 64
H_A = 8
DV_A = 128
HD_A = 64
H_B = 8
HD_B = 128
HKV_B = 2
G_B = H_B // HKV_B
H_IDX = 16
D_IDX = 64
TOPK_MAX = 256
N_BUCKETS = 32
D_FF = 4 * D_MODEL
EPS = 1e-6
LANES = 128
NEG = -1e30
LOG2E = math.log2(math.e)
INT_MIN = -(2 ** 31)
INT_MAX = 2 ** 31 - 1
VMEM_LIMIT = 56 * 1024 * 1024

_NT = (((1,), (1,)), ((), ()))


def _cparams(sem):
    return pltpu.CompilerParams(dimension_semantics=sem, vmem_limit_bytes=VMEM_LIMIT)


def _resident(block_shape, index_map):
    return pl.BlockSpec(block_shape, index_map, pipeline_mode=pl.Buffered(1))


def _rmsnorm_body(x_ref, g_ref, o_ref):
    x = x_ref[...]
    y = x * lax.rsqrt(jnp.mean(x * x, axis=-1, keepdims=True) + EPS)
    o_ref[...] = (y * g_ref[...]).astype(o_ref.dtype)


def _rmsnorm(x, g, tm):
    n, d = x.shape
    return pl.pallas_call(
        _rmsnorm_body,
        grid=(n // tm,),
        in_specs=[pl.BlockSpec((tm, d), lambda i: (i, 0)), pl.BlockSpec((1, d), lambda i: (0, 0))],
        out_specs=pl.BlockSpec((tm, d), lambda i: (i, 0)),
        out_shape=jax.ShapeDtypeStruct((n, d), BF16),
        compiler_params=_cparams(("parallel",)),
        name="rmsnorm",
    )(x, g.reshape(1, d))


def _head_norm(z, gain, split):
    lane = lax.broadcasted_iota(I32, (1, LANES), 1)
    lo = lane < (LANES // 2)
    outs = []
    for h in range(z.shape[1] // LANES):
        zh = z[:, h * LANES:(h + 1) * LANES]
        sq = zh * zh
        if split:
            s_lo = jnp.sum(jnp.where(lo, sq, 0.0), axis=-1, keepdims=True)
            s_hi = jnp.sum(jnp.where(lo, 0.0, sq), axis=-1, keepdims=True)
            r = jnp.where(lo, lax.rsqrt(s_lo * (2.0 / LANES) + EPS), lax.rsqrt(s_hi * (2.0 / LANES) + EPS))
        else:
            r = lax.rsqrt(jnp.mean(sq, axis=-1, keepdims=True) + EPS)
        outs.append(zh * r * gain)
    return jnp.concatenate(outs, axis=-1)


FLAT, HEADS, HEADS_T = "tokens x cols", "tokens x heads x 128", "heads x 128 x tokens"


def _proj_body(epilogue, n_aux, layouts, h_ref, w_ref, *rest):
    z = jnp.dot(h_ref[...], w_ref[...], preferred_element_type=F32)
    outs = epilogue(z, *[r[...] for r in rest[:n_aux]])
    for o_ref, o, layout in zip(rest[n_aux:], outs, layouts):
        if layout == FLAT:
            o_ref[...] = o.astype(o_ref.dtype)
            continue
        for hd in range(o.shape[1] // LANES):
            slab = o[:, hd * LANES:(hd + 1) * LANES]
            if layout == HEADS:
                o_ref[:, hd, :] = slab.astype(o_ref.dtype)
            else:
                o_ref[hd] = slab.T.astype(o_ref.dtype)


def _proj(h, w, aux, epilogue, out_defs, tm, name):
    n, k = h.shape
    c = w.shape[1]
    in_specs = [pl.BlockSpec((tm, k), lambda i: (i, 0)), _resident((k, c), lambda i: (0, 0))]
    in_specs += [pl.BlockSpec(a.shape, lambda i: (0, 0)) for a in aux]
    out_specs, out_shape = [], []
    for oc, dt, layout in out_defs:
        nh = oc // LANES
        if layout == FLAT:
            out_specs.append(pl.BlockSpec((tm, oc), lambda i: (i, 0)))
            out_shape.append(jax.ShapeDtypeStruct((n, oc), dt))
        elif layout == HEADS:
            out_specs.append(pl.BlockSpec((tm, nh, LANES), lambda i: (i, 0, 0)))
            out_shape.append(jax.ShapeDtypeStruct((n, nh, LANES), dt))
        else:
            out_specs.append(pl.BlockSpec((nh, LANES, tm), lambda i: (0, 0, i)))
            out_shape.append(jax.ShapeDtypeStruct((nh, LANES, n), dt))
    return pl.pallas_call(
        functools.partial(_proj_body, epilogue, len(aux), tuple(d[2] for d in out_defs)),
        grid=(n // tm,),
        in_specs=in_specs,
        out_specs=out_specs,
        out_shape=out_shape,
        compiler_params=_cparams(("parallel",)),
        name=name,
    )(h, w, *aux)


def _epi_qa(z, gain):
    return (_head_norm(z, gain, True) * (HD_A ** -0.5 * LOG2E),)


def _epi_ka(z, gain):
    y = _head_norm(z, gain, True)
    return y, y


def _epi_copy2(z):
    return z, z


def _epi_qb(z, gain):
    return (_head_norm(z, gain, False) * (HD_B ** -0.5 * LOG2E),)


def _epi_copy1(z):
    return (z,)


_KB0, _VB0, _KI0, _WI0, _MISC_COLS = 0, 256, 512, 640, 768


def _epi_misc(z, gain):
    kb = _head_norm(z[:, _KB0:_VB0], gain, False)
    vb = z[:, _VB0:_KI0]
    ki2 = z[:, _KI0:_WI0]
    return kb, kb, vb, vb, ki2[:, :D_IDX], ki2, z[:, _WI0:_WI0 + H_IDX]


def _rel_bias(tab_ref, h, rel):
    half = N_BUCKETS // 2
    exact = half // 2
    n = jnp.abs(rel)
    n2 = n * n
    v_neg = jnp.full(rel.shape, tab_ref[h, 0], F32)
    v_pos = jnp.full(rel.shape, tab_ref[h, half], F32)
    for b in range(1, half):
        cond = (n >= b) if b < exact else (n2 >= exact * exact * 2 ** (b - exact))
        v_neg = jnp.where(cond, tab_ref[h, b], v_neg)
        v_pos = jnp.where(cond, tab_ref[h, half + b], v_pos)
    return (jnp.where(rel > 0, v_pos, v_neg) - tab_ref[h, half - 1]) * LOG2E


def _bias_body(t, n_valid_diag, tab_ref, o_ref):
    h = pl.program_id(0)
    kl = lax.broadcasted_iota(I32, (t, t), 0)
    ql = lax.broadcasted_iota(I32, (t, t), 1)
    o_ref[0] = _rel_bias(tab_ref, h, kl - t - ql)
    visible = ((kl // CHUNK) <= (ql // CHUNK)) & (kl < n_valid_diag)
    o_ref[1] = jnp.where(visible, _rel_bias(tab_ref, h, kl - ql), NEG)


def _bias_dense_body(q0, n_keys, tab_ref, o_ref):
    h = pl.program_id(0)
    qpos = q0 + lax.broadcasted_iota(I32, o_ref.shape, 0)
    kpos = lax.broadcasted_iota(I32, o_ref.shape, 1)
    visible = ((kpos // CHUNK) <= (qpos // CHUNK)) & (kpos < n_keys)
    o_ref[...] = jnp.where(visible, _rel_bias(tab_ref, h, kpos - qpos), NEG)


def _bias_dense(tab_t, nq, nk_pad, q0, n_keys):
    nh = tab_t.shape[0]
    return pl.pallas_call(
        functools.partial(_bias_dense_body, q0, n_keys),
        grid=(nh,),
        in_specs=[pl.BlockSpec(memory_space=pltpu.SMEM)],
        out_specs=pl.BlockSpec((None, nq, nk_pad), lambda h: (h, 0, 0)),
        out_shape=jax.ShapeDtypeStruct((nh, nq, nk_pad), F32),
        compiler_params=_cparams(("arbitrary",)),
        name="bias_dense",
    )(tab_t)


def _bias_tiles(tab_t, t, n_valid_diag):
    nh = tab_t.shape[0]
    return pl.pallas_call(
        functools.partial(_bias_body, t, n_valid_diag),
        grid=(nh,),
        in_specs=[pl.BlockSpec(memory_space=pltpu.SMEM)],
        out_specs=pl.BlockSpec((None, 2, t, t), lambda h: (h, 0, 0, 0)),
        out_shape=jax.ShapeDtypeStruct((nh, 2, t, t), F32),
        compiler_params=_cparams(("arbitrary",)),
        name="bias_tiles",
    )(tab_t)


ONES_ROWS = 16


def _softmax_step(s, vt, m_ref, acc_ref):
    m_old = m_ref[...]
    m_new = jnp.maximum(m_old, jnp.max(s, axis=0, keepdims=True))
    alpha = jnp.exp2(m_old - m_new)
    p = jnp.exp2(s - m_new)
    vt_ones = jnp.concatenate([vt, jnp.ones((ONES_ROWS, vt.shape[1]), vt.dtype)], axis=0)
    acc_ref[...] = acc_ref[...] * alpha + jnp.dot(vt_ones, p.astype(BF16), preferred_element_type=F32)
    m_ref[...] = m_new


def _normalised(acc, dv):
    return acc[:dv] / acc[dv:dv + 1]


def _pipelined_tiles(n_far, produce, consume, buf0, buf1):
    @pl.when(n_far == -1)
    def _():
        produce(0, buf0, 1)
        consume(0, buf0)

    @pl.when(n_far == 0)
    def _():
        produce(0, buf0, 0)
        produce(1, buf1, 1)
        consume(0, buf0)
        consume(1, buf1)

    @pl.when(n_far >= 1)
    def _():
        produce(0, buf0, None)

    pairs = jnp.maximum(n_far - 1, 0) // 2

    def body(i, carry):
        j = 2 * i
        produce(j + 1, buf1, None)
        consume(j, buf0)
        produce(j + 2, buf0, None)
        consume(j + 1, buf1)
        return carry

    lax.fori_loop(0, pairs, body, 0)
    d = 2 * pairs
    rem = n_far - d

    @pl.when(rem == 1)
    def _():
        produce(d + 1, buf1, 0)
        consume(d, buf0)
        produce(d + 2, buf0, 1)
        consume(d + 1, buf1)
        consume(d + 2, buf0)

    @pl.when(rem == 2)
    def _():
        produce(d + 1, buf1, None)
        consume(d, buf0)
        produce(d + 2, buf0, 0)
        consume(d + 1, buf1)
        produce(d + 3, buf1, 1)
        consume(d + 2, buf0)
        consume(d + 3, buf1)


def _attn_a_body(t, q0, out_scale, lam_ref, q_ref, k_ref, vt_ref, bias_ref, g_ref, o_ref,
                 qz_ref, m_ref, acc_ref, s0_ref, s1_ref):
    i = pl.program_id(2)
    n_far = q0 // t + i - 1
    q = q_ref[...]
    lane = lax.broadcasted_iota(I32, (t, LANES), 1)
    zero = jnp.zeros_like(q)
    qz_ref[:t] = jnp.where(lane < HD_A, q, zero)
    qz_ref[t:] = jnp.where(lane < HD_A, zero, q)
    m_ref[...] = jnp.full(m_ref.shape, NEG, F32)
    acc_ref[...] = jnp.zeros(acc_ref.shape, F32)

    def produce(j, s_ref, kind):
        ks = pl.multiple_of(j * t, t)
        s = lax.dot_general(k_ref[pl.ds(ks, t), :], qz_ref[...], _NT, preferred_element_type=F32)
        if kind is not None:
            b = bias_ref[kind]
            s = jnp.concatenate([s[:, :t] + b, s[:, t:] + b], axis=1)
        s_ref[...] = s

    def consume(j, s_ref):
        _softmax_step(s_ref[...], vt_ref[:, pl.ds(pl.multiple_of(j * t, t), t)], m_ref, acc_ref)

    _pipelined_tiles(n_far, produce, consume, s0_ref, s1_ref)

    lam = lam_ref[0]
    o = _normalised(acc_ref[...], DV_A)
    o = o[:, :t] - lam * o[:, t:]
    y = o * lax.rsqrt(jnp.mean(o * o, axis=0, keepdims=True) + EPS)
    y = (y * g_ref[...]) * out_scale
    o_ref[...] = y.T.astype(o_ref.dtype)


def _attn_a(q, k, vt, bias, subln_g, lam, *, t, q0, out_scale):
    b, nq, _ = q.shape
    nk = k.shape[1]
    return pl.pallas_call(
        functools.partial(_attn_a_body, t, q0, out_scale),
        grid=(b, H_A, nq // t),
        in_specs=[
            pl.BlockSpec(memory_space=pltpu.SMEM),
            pl.BlockSpec((None, t, LANES), lambda bb, h, i: (bb, i, h)),
            pl.BlockSpec((None, nk, LANES), lambda bb, h, i: (bb, 0, h)),
            pl.BlockSpec((None, None, DV_A, nk), lambda bb, h, i: (bb, h, 0, 0)),
            pl.BlockSpec((None, 2, t, t), lambda bb, h, i: (h, 0, 0, 0)),
            pl.BlockSpec((DV_A, 1), lambda bb, h, i: (0, 0)),
        ],
        out_specs=pl.BlockSpec((None, t, LANES), lambda bb, h, i: (bb, i, h)),
        out_shape=jax.ShapeDtypeStruct((b, nq, H_A * DV_A), BF16),
        scratch_shapes=[
            pltpu.VMEM((2 * t, LANES), BF16),
            pltpu.VMEM((1, 2 * t), F32),
            pltpu.VMEM((DV_A + ONES_ROWS, 2 * t), F32),
            pltpu.VMEM((t, 2 * t), F32),
            pltpu.VMEM((t, 2 * t), F32),
        ],
        compiler_params=_cparams(("parallel", "parallel", "arbitrary")),
        name="diff_attn",
    )(lam, q, k, vt, bias, subln_g)


def _order_key(x):
    b = pltpu.bitcast(x, I32)
    return jnp.where(b < 0, b ^ INT_MAX, b)


def _order_unkey(k):
    return pltpu.bitcast(jnp.where(k < 0, k ^ INT_MAX, k), F32)


def _dsa_body(t, q0, topk, n_keys, n_valid_diag, qb_ref, qi_ref, wi_ref, kb_ref, vbt_ref, ki_ref, bias_ref, o_ref,
              keys_ref, qz_ref, qs_ref, m_ref, acc_ref, s0_ref, s1_ref):
    i = pl.program_id(1)
    n_tiles = q0 // t + i + 1
    n_far = n_tiles - 2

    lane = lax.broadcasted_iota(I32, (t, LANES), 1)
    for hp in range(H_IDX // 2):
        qs = qi_ref[:, hp * LANES:(hp + 1) * LANES]
        zero = jnp.zeros_like(qs)
        qz_ref[2 * hp] = jnp.where(lane < D_IDX, qs, zero)
        qz_ref[2 * hp + 1] = jnp.where(lane < D_IDX, zero, qs)

    def score_tile(j):
        kt = ki_ref[pl.ds(pl.multiple_of(j * t, t), t), :]
        acc = jnp.zeros((t, t), F32)
        for h in range(H_IDX):
            s = lax.dot_general(kt, qz_ref[h], _NT, preferred_element_type=F32)
            acc = acc + jnp.maximum(s, 0.0) * wi_ref[h:h + 1, :]
        return acc

    def fold8(x, op):
        return op(x.reshape(x.shape[0] // 8, 8, t), axis=0)

    def score_body(j, carry):
        smin, smax = carry
        sc = score_tile(j)
        keys_ref[pl.ds(pl.multiple_of(j * t, t), t), :] = _order_key(sc)
        return jnp.minimum(smin, fold8(sc, jnp.min)), jnp.maximum(smax, fold8(sc, jnp.max))

    smin, smax = lax.fori_loop(0, n_tiles - 1, score_body,
                               (jnp.full((8, t), jnp.inf, F32), jnp.full((8, t), -jnp.inf, F32)))
    jd = n_tiles - 1
    kl = lax.broadcasted_iota(I32, (t, t), 0)
    ql = lax.broadcasted_iota(I32, (t, t), 1)
    admissible = ((kl // CHUNK) <= (ql // CHUNK)) & (kl < n_valid_diag)
    sc = score_tile(jd)
    keys_ref[pl.ds(pl.multiple_of(jd * t, t), t), :] = jnp.where(admissible, _order_key(sc), INT_MIN)
    smin = jnp.minimum(smin, fold8(jnp.where(admissible, sc, jnp.inf), jnp.min))
    smax = jnp.maximum(smax, fold8(jnp.where(admissible, sc, -jnp.inf), jnp.max))
    keys_ref[pl.ds(pl.multiple_of(n_tiles * t, t), t), :] = jnp.full((t, t), INT_MIN, I32)
    n_pairs = (n_tiles + 1) // 2

    def count_pairs(hit_fn):
        def body(j, c):
            r0 = pl.multiple_of(j * 2 * t, 2 * t)
            hit = hit_fn(keys_ref[pl.ds(r0, 2 * t), :], r0)
            return c + jnp.sum(hit.reshape(2 * t // 32, 32, t), axis=0)
        part = lax.fori_loop(0, n_pairs, body, jnp.zeros((32, t), F32))
        return jnp.sum(part, axis=0, keepdims=True)

    def search_cond(state):
        return jnp.logical_and(state[0] < 4 * 34, state[-1] > 0)

    def search_body(state):
        it, lo, hi, c_lo, c_hi, _ = state
        f_lo, f_hi = _order_unkey(lo), _order_unkey(hi)
        a, b = jnp.log(c_lo + 0.5), jnp.log(c_hi + 0.5)
        frac = jnp.clip((a - math.log(topk)) / jnp.maximum(a - b, 1e-9), 0.0, 1.0)
        guess = jnp.minimum(jnp.maximum(_order_key(f_lo + (f_hi - f_lo) * frac), lo + 1), hi - 1)
        mid = jnp.where(it % 4 == 3, (lo >> 1) + (hi >> 1) + (lo & hi & 1), guess)
        c = count_pairs(lambda kk, r0: jnp.where(kk >= mid, 1.0, 0.0))
        live = lo + 1 < hi
        ge = c >= topk
        up = jnp.logical_and(live, ge)
        down = jnp.logical_and(live, jnp.logical_not(ge))
        lo = jnp.where(up, mid, lo)
        c_lo = jnp.where(up, c, c_lo)
        hi = jnp.where(jnp.logical_and(live, c == topk), mid + 1, jnp.where(down, mid, hi))
        c_hi = jnp.where(down, c, c_hi)
        return it + 1, lo, hi, c_lo, c_hi, jnp.max(jnp.where(lo + 1 < hi, 1, 0))

    qpos = q0 + i * t + lax.broadcasted_iota(I32, (1, t), 1)
    n_adm = jnp.minimum((qpos // CHUNK + 1) * CHUNK, n_keys).astype(F32)
    lo0 = _order_key(jnp.min(smin, axis=0, keepdims=True))
    hi0 = jnp.where(n_adm <= topk, lo0 + 1, _order_key(jnp.max(smax, axis=0, keepdims=True)) + 1)
    _, thr, _, c_thr, c_above, _ = lax.while_loop(
        search_cond, search_body, (jnp.int32(0), lo0, hi0, n_adm, jnp.zeros((1, t), F32), jnp.int32(1)))

    @pl.when(jnp.max(jnp.where(jnp.logical_and(c_thr > topk, qpos < n_keys), 1, 0)) > 0)
    def _():
        need = topk - c_above
        row = lax.broadcasted_iota(I32, (2 * t, t), 0)

        def cut_body(_, st):
            p_lo, p_hi = st
            p_mid = (p_lo + p_hi) >> 1
            c = count_pairs(lambda kk, r0: jnp.where(
                jnp.logical_and(kk == thr, row + r0 <= p_mid), 1.0, 0.0))
            ok = c >= need
            return jnp.where(ok, p_lo, p_mid), jnp.where(ok, p_mid, p_hi)

        n_pos = keys_ref.shape[0]
        _, cut = lax.fori_loop(0, n_pos.bit_length(), cut_body,
                               (jnp.full((1, t), -1, I32), jnp.full((1, t), n_pos - 1, I32)))

        def demote_body(j, carry):
            r0 = pl.multiple_of(j * 2 * t, 2 * t)
            kk = keys_ref[pl.ds(r0, 2 * t), :]
            keys_ref[pl.ds(r0, 2 * t), :] = jnp.where(jnp.logical_and(kk == thr, row + r0 > cut), thr - 1, kk)
            return carry

        lax.fori_loop(0, n_pairs, demote_body, 0)

    m_ref[...] = jnp.full(m_ref.shape, NEG, F32)
    acc_ref[...] = jnp.zeros(acc_ref.shape, F32)

    for h in range(H_B):
        qs_ref[h // G_B, (h % G_B) * t:(h % G_B + 1) * t, :] = qb_ref[:, h * HD_B:(h + 1) * HD_B]

    def produce(j, s_ref, kind):
        ks = pl.multiple_of(j * t, t)
        off = jnp.where(keys_ref[pl.ds(ks, t), :] >= thr, 0.0, NEG)
        for n in range(HKV_B):
            s = lax.dot_general(kb_ref[pl.ds(ks, t), n * HD_B:(n + 1) * HD_B], qs_ref[n], _NT,
                                preferred_element_type=F32)
            parts = []
            for g in range(G_B):
                off_g = off if kind is None else off + bias_ref[n * G_B + g, kind]
                parts.append(s[:, g * t:(g + 1) * t] + off_g)
            s_ref[n] = jnp.concatenate(parts, axis=1)

    def consume(j, s_ref):
        ks = pl.multiple_of(j * t, t)
        for n in range(HKV_B):
            _softmax_step(s_ref[n], vbt_ref[n, :, pl.ds(ks, t)], m_ref.at[n], acc_ref.at[n])

    _pipelined_tiles(n_far, produce, consume, s0_ref, s1_ref)

    for n in range(HKV_B):
        o = _normalised(acc_ref[n], HD_B)
        for g in range(G_B):
            h = n * G_B + g
            o_ref[:, h * HD_B:(h + 1) * HD_B] = o[:, g * t:(g + 1) * t].T.astype(o_ref.dtype)


def _dsa(qb, qi, wi_t, kb, vbt, ki2, bias, *, t, q0, topk, n_keys, n_valid_diag):
    b, nq, _ = qb.shape
    nk = kb.shape[1]
    return pl.pallas_call(
        functools.partial(_dsa_body, t, q0, topk, n_keys, n_valid_diag),
        grid=(b, nq // t),
        in_specs=[
            pl.BlockSpec((None, t, H_B * HD_B), lambda bb, i: (bb, i, 0)),
            pl.BlockSpec((None, t, H_IDX * D_IDX), lambda bb, i: (bb, i, 0)),
            pl.BlockSpec((None, H_IDX, t), lambda bb, i: (bb, 0, i)),
            _resident((None, nk, HKV_B * HD_B), lambda bb, i: (bb, 0, 0)),
            _resident((None, HKV_B, HD_B, nk), lambda bb, i: (bb, 0, 0, 0)),
            _resident((None, nk, LANES), lambda bb, i: (bb, 0, 0)),
            _resident((H_B, 2, t, t), lambda bb, i: (0, 0, 0, 0)),
        ],
        out_specs=pl.BlockSpec((None, t, H_B * HD_B), lambda bb, i: (bb, i, 0)),
        out_shape=jax.ShapeDtypeStruct((b, nq, H_B * HD_B), BF16),
        scratch_shapes=[
            pltpu.VMEM((nk + t, t), I32),
            pltpu.VMEM((H_IDX, t, LANES), BF16),
            pltpu.VMEM((HKV_B, G_B * t, HD_B), BF16),
            pltpu.VMEM((HKV_B, 1, G_B * t), F32),
            pltpu.VMEM((HKV_B, HD_B + ONES_ROWS, G_B * t), F32),
            pltpu.VMEM((HKV_B, t, G_B * t), F32),
            pltpu.VMEM((HKV_B, t, G_B * t), F32),
        ],
        compiler_params=_cparams(("parallel", "arbitrary")),
        name="dsa",
    )(qb, qi, wi_t, kb, vbt, ki2, bias)


def _softmax_rows(s, v):
    p = jnp.exp2(s - jnp.max(s, axis=-1, keepdims=True))
    o = jnp.dot(p.astype(BF16), v, preferred_element_type=F32)
    return o / jnp.sum(p, axis=-1, keepdims=True)


def _short_a_body(nq, out_scale, lam_ref, q_ref, k_ref, v_ref, bias_ref, g_ref, o_ref):
    lam = lam_ref[0]
    lane = lax.broadcasted_iota(I32, (nq, LANES), 1)
    for h in range(H_A):
        cols = slice(h * LANES, (h + 1) * LANES)
        q = q_ref[:, cols]
        zero = jnp.zeros_like(q)
        qz = jnp.concatenate([jnp.where(lane < HD_A, q, zero), jnp.where(lane < HD_A, zero, q)], axis=0)
        s = lax.dot_general(qz, k_ref[:, cols], _NT, preferred_element_type=F32)
        b = bias_ref[h]
        o = _softmax_rows(s + jnp.concatenate([b, b], axis=0), v_ref[:, cols])
        o = o[:nq] - lam * o[nq:]
        y = o * lax.rsqrt(jnp.mean(o * o, axis=-1, keepdims=True) + EPS)
        o_ref[:, cols] = ((y * g_ref[...]) * out_scale).astype(o_ref.dtype)


def _short_a(q, k, v, bias, subln_g, lam, *, out_scale):
    b, nq, d = q.shape
    nk = k.shape[1]
    return pl.pallas_call(
        functools.partial(_short_a_body, nq, out_scale),
        grid=(b,),
        in_specs=[
            pl.BlockSpec(memory_space=pltpu.SMEM),
            pl.BlockSpec((None, nq, d), lambda bb: (bb, 0, 0)),
            pl.BlockSpec((None, nk, d), lambda bb: (bb, 0, 0)),
            pl.BlockSpec((None, nk, d), lambda bb: (bb, 0, 0)),
            pl.BlockSpec((H_A, nq, nk), lambda bb: (0, 0, 0)),
            pl.BlockSpec((1, DV_A), lambda bb: (0, 0)),
        ],
        out_specs=pl.BlockSpec((None, nq, d), lambda bb: (bb, 0, 0)),
        out_shape=jax.ShapeDtypeStruct((b, nq, d), BF16),
        compiler_params=_cparams(("parallel",)),
        name="diff_attn_short",
    )(lam, q, k, v, bias, subln_g)


def _short_dsa_body(nq, q0, topk, n_keys, qb_ref, qi_ref, wi_ref, kb_ref, vb_ref, ki_ref, bias_ref, o_ref):
    nk = ki_ref.shape[0]
    lane = lax.broadcasted_iota(I32, (nq, LANES), 1)

    qz = []
    for h in range(H_IDX):
        qs = qi_ref[:, (h // 2) * LANES:(h // 2 + 1) * LANES]
        zero = jnp.zeros_like(qs)
        qz.append(jnp.where(lane < D_IDX, qs, zero) if h % 2 == 0 else jnp.where(lane < D_IDX, zero, qs))
    s_idx = lax.dot_general(jnp.concatenate(qz, axis=0), ki_ref[...], _NT, preferred_element_type=F32)
    score = jnp.zeros((nq, nk), F32)
    for h in range(H_IDX):
        score = score + jnp.maximum(s_idx[h * nq:(h + 1) * nq], 0.0) * wi_ref[:, h:h + 1]
    qpos = q0 + lax.broadcasted_iota(I32, (nq, nk), 0)
    kpos = lax.broadcasted_iota(I32, (nq, nk), 1)
    admissible = ((kpos // CHUNK) <= (qpos // CHUNK)) & (kpos < n_keys)
    keys = jnp.where(admissible, _order_key(score), INT_MIN)

    def count(hit):
        return jnp.sum(jnp.where(hit, 1.0, 0.0), axis=-1, keepdims=True)

    def search_body(state):
        it, lo, hi, c_lo, c_hi, _ = state
        mid = (lo >> 1) + (hi >> 1) + (lo & hi & 1)
        c = count(keys >= mid)
        live = lo + 1 < hi
        ge = c >= topk
        up = jnp.logical_and(live, ge)
        down = jnp.logical_and(live, jnp.logical_not(ge))
        lo = jnp.where(up, mid, lo)
        c_lo = jnp.where(up, c, c_lo)
        hi = jnp.where(jnp.logical_and(live, c == topk), mid + 1, jnp.where(down, mid, hi))
        c_hi = jnp.where(down, c, c_hi)
        return it + 1, lo, hi, c_lo, c_hi, jnp.max(jnp.where(lo + 1 < hi, 1, 0))

    n_adm = count(admissible)
    lo0 = jnp.full((nq, 1), INT_MIN + 1, I32)
    hi0 = jnp.where(n_adm <= topk, lo0 + 1, INT_MAX)
    _, thr, _, c_thr, c_above, _ = lax.while_loop(
        lambda st: jnp.logical_and(st[0] < 34, st[-1] > 0), search_body,
        (jnp.int32(0), lo0, hi0, n_adm, jnp.zeros((nq, 1), F32), jnp.int32(1)))

    need = topk - c_above

    def cut_body(_, st):
        p_lo, p_hi = st
        p_mid = (p_lo + p_hi) >> 1
        ok = count(jnp.logical_and(keys == thr, kpos <= p_mid)) >= need
        return jnp.where(ok, p_lo, p_mid), jnp.where(ok, p_mid, p_hi)

    _, cut = lax.fori_loop(0, nk.bit_length(), cut_body,
                           (jnp.full((nq, 1), -1, I32), jnp.full((nq, 1), nk - 1, I32)))
    selected = jnp.logical_and(keys >= thr, jnp.logical_not(jnp.logical_and(keys == thr, kpos > cut)))
    off = jnp.where(selected, 0.0, NEG)

    for n in range(HKV_B):
        heads = range(n * G_B, (n + 1) * G_B)
        qs = jnp.concatenate([qb_ref[:, h * HD_B:(h + 1) * HD_B] for h in heads], axis=0)
        s = lax.dot_general(qs, kb_ref[:, n * HD_B:(n + 1) * HD_B], _NT, preferred_element_type=F32)
        s = s + jnp.concatenate([bias_ref[h] + off for h in heads], axis=0)
        o = _softmax_rows(s, vb_ref[:, n * HD_B:(n + 1) * HD_B])
        for g, h in enumerate(heads):
            o_ref[:, h * HD_B:(h + 1) * HD_B] = o[g * nq:(g + 1) * nq].astype(o_ref.dtype)


def _short_dsa(qb, qi, wi, kb, vb, ki2, bias, *, q0, topk, n_keys):
    b, nq, d = qb.shape
    nk = kb.shape[1]
    per_batch = lambda shape: pl.BlockSpec((None,) + shape, lambda bb: (bb, 0, 0))
    return pl.pallas_call(
        functools.partial(_short_dsa_body, nq, q0, topk, n_keys),
        grid=(b,),
        in_specs=[per_batch((nq, d)), per_batch((nq, d)), per_batch((nq, H_IDX)),
                  per_batch((nk, HKV_B * HD_B)), per_batch((nk, HKV_B * HD_B)), per_batch((nk, LANES)),
                  pl.BlockSpec((H_B, nq, nk), lambda bb: (0, 0, 0))],
        out_specs=per_batch((nq, d)),
        out_shape=jax.ShapeDtypeStruct((b, nq, d), BF16),
        compiler_params=_cparams(("parallel",)),
        name="dsa_short",
    )(qb, qi, wi, kb, vb, ki2, bias)


def _sigmoid(x):
    return 1.0 / (1.0 + jnp.exp(-x))


def _merge_body(h_ref, oa_ref, ob_ref, wg_ref, woa_ref, wob_ref, o_ref):
    h = h_ref[...]
    ga = jnp.dot(h, wg_ref[:, :D_MODEL], preferred_element_type=F32)
    ya = jnp.dot(oa_ref[...], woa_ref[...], preferred_element_type=F32)
    merged = _sigmoid(ga) * ya
    gb = jnp.dot(h, wg_ref[:, D_MODEL:], preferred_element_type=F32)
    yb = jnp.dot(ob_ref[...], wob_ref[...], preferred_element_type=F32)
    o_ref[...] = (merged + _sigmoid(gb) * yb).astype(o_ref.dtype)


def _merge(h, oa, ob, wg, woa, wob, tm):
    n = h.shape[0]
    return pl.pallas_call(
        _merge_body,
        grid=(n // tm,),
        in_specs=[
            pl.BlockSpec((tm, D_MODEL), lambda i: (i, 0)),
            pl.BlockSpec((tm, H_A * DV_A), lambda i: (i, 0)),
            pl.BlockSpec((tm, H_B * HD_B), lambda i: (i, 0)),
            _resident((D_MODEL, 2 * D_MODEL), lambda i: (0, 0)),
            _resident((H_A * DV_A, D_MODEL), lambda i: (0, 0)),
            _resident((H_B * HD_B, D_MODEL), lambda i: (0, 0)),
        ],
        out_specs=pl.BlockSpec((tm, D_MODEL), lambda i: (i, 0)),
        out_shape=jax.ShapeDtypeStruct((n, D_MODEL), BF16),
        compiler_params=_cparams(("parallel",)),
        name="gated_merge",
    )(h, oa, ob, wg, woa, wob)


def _outproj_body(x_ref, mg_ref, w_ref, g_ref, x1_ref, h2_ref):
    x1 = x_ref[...] + jnp.dot(mg_ref[...], w_ref[...], preferred_element_type=F32)
    x1_ref[...] = x1
    y = x1 * lax.rsqrt(jnp.mean(x1 * x1, axis=-1, keepdims=True) + EPS)
    h2_ref[...] = (y * g_ref[...]).astype(h2_ref.dtype)


def _outproj(x, merged, w_out, g2, tm):
    n = x.shape[0]
    return pl.pallas_call(
        _outproj_body,
        grid=(n // tm,),
        in_specs=[
            pl.BlockSpec((tm, D_MODEL), lambda i: (i, 0)),
            pl.BlockSpec((tm, D_MODEL), lambda i: (i, 0)),
            _resident((D_MODEL, D_MODEL), lambda i: (0, 0)),
            pl.BlockSpec((1, D_MODEL), lambda i: (0, 0)),
        ],
        out_specs=[pl.BlockSpec((tm, D_MODEL), lambda i: (i, 0)), pl.BlockSpec((tm, D_MODEL), lambda i: (i, 0))],
        out_shape=[jax.ShapeDtypeStruct((n, D_MODEL), F32), jax.ShapeDtypeStruct((n, D_MODEL), BF16)],
        compiler_params=_cparams(("parallel",)),
        name="out_proj",
    )(x, merged, w_out, g2.reshape(1, D_MODEL))


def _ffn_body(x1_ref, h2_ref, w1_ref, w2_ref, o_ref):
    f = pl.program_id(1)

    @pl.when(f == 0)
    def _():
        o_ref[...] = x1_ref[...]

    u = jnp.maximum(jnp.dot(h2_ref[...], w1_ref[...], preferred_element_type=F32), 0.0)
    o_ref[...] += jnp.dot((u * u).astype(BF16), w2_ref[...], preferred_element_type=F32)


def _ffn(x1, h2, w1, w2, tm, tf):
    n = x1.shape[0]
    return pl.pallas_call(
        _ffn_body,
        grid=(n // tm, D_FF // tf),
        in_specs=[
            pl.BlockSpec((tm, D_MODEL), lambda i, f: (i, 0)),
            pl.BlockSpec((tm, D_MODEL), lambda i, f: (i, 0)),
            pl.BlockSpec((D_MODEL, tf), lambda i, f: (0, f)),
            pl.BlockSpec((tf, D_MODEL), lambda i, f: (f, 0)),
        ],
        out_specs=pl.BlockSpec((tm, D_MODEL), lambda i, f: (i, 0)),
        out_shape=jax.ShapeDtypeStruct((n, D_MODEL), F32),
        compiler_params=_cparams(("parallel", "arbitrary")),
        name="ffn",
    )(x1, h2, w1, w2)


def _prep_weights(w_in, w_o_a, w_o_b, w_out, w_ff1, w_ff2):
    sizes = (H_A * 2 * HD_A, H_A * 2 * HD_A, H_A * DV_A, H_B * HD_B, HKV_B * HD_B, HKV_B * HD_B,
             H_IDX * D_IDX, D_IDX, H_IDX, D_MODEL, D_MODEL)
    offs = [0]
    for s in sizes:
        offs.append(offs[-1] + s)
    col = lambda a, b: w_in[:, offs[a]:offs[b]]
    w_misc = jnp.concatenate(
        [col(4, 6), col(7, 8), col(7, 8), col(8, 9),
         jnp.zeros((D_MODEL, _MISC_COLS - _WI0 - H_IDX), w_in.dtype)], axis=1)
    return dict(
        qa=col(0, 1).astype(BF16), ka=col(1, 2).astype(BF16), va=col(2, 3).astype(BF16),
        qb=col(3, 4).astype(BF16), misc=w_misc.astype(BF16), qi=col(6, 7).astype(BF16),
        gate=col(9, 11).astype(BF16), oa=w_o_a.astype(BF16), ob=w_o_b.astype(BF16),
        out=w_out.astype(BF16), ff1=w_ff1.astype(BF16), ff2=w_ff2.astype(BF16))


def _layer(x, past, w, p, lam, lam_init, tab_t, *, tm, t_a=None, t_b=None):
    short = t_a is None
    b, t, _ = x.shape
    n = b * t
    xf = x.reshape(n, D_MODEL)
    h = _rmsnorm(xf, p["norm1_g"], tm)

    gain_a = lambda g: g.reshape(1, 2 * HD_A)
    v_layout = HEADS_T if (not short and past is None and b == 1) else FLAT
    (qa,) = _proj(h, w["qa"], [gain_a(p["qn_a_g"])], _epi_qa, [(1024, BF16, FLAT)], tm, "proj_qa")
    ka, ka_h = _proj(h, w["ka"], [gain_a(p["kn_a_g"])], _epi_ka, [(1024, F32, HEADS), (1024, BF16, FLAT)],
                     tm, "proj_ka")
    va, va_h = _proj(h, w["va"], [], _epi_copy2, [(1024, F32, HEADS), (1024, BF16, v_layout)], tm, "proj_va")
    (qb,) = _proj(h, w["qb"], [p["qn_b_g"].reshape(1, HD_B)], _epi_qb, [(1024, BF16, FLAT)], tm, "proj_qb")
    kb, kb_h, vb, vb_h, ki, ki2_h, wi = _proj(
        h, w["misc"], [p["kn_b_g"].reshape(1, HD_B)], _epi_misc,
        [(256, F32, HEADS), (256, BF16, FLAT), (256, F32, HEADS), (256, BF16, v_layout), (D_IDX, F32, FLAT),
         (2 * D_IDX, BF16, FLAT), (H_IDX, F32, FLAT)],
        tm, "proj_misc")
    (qi,) = _proj(h, w["qi"], [], _epi_copy1, [(1024, BF16, FLAT)], tm, "proj_qi")

    new_rows = (ka.reshape(b, t, H_A, 2 * HD_A), va.reshape(b, t, H_A, DV_A), kb.reshape(b, t, HKV_B, HD_B),
                vb.reshape(b, t, HKV_B, HD_B), ki.reshape(b, t, D_IDX))

    def full_keys(new_h, past_arr):
        if new_h.ndim == 3:
            return new_h[None]
        new_h = new_h.reshape(b, t, -1)
        if past_arr is None:
            return new_h
        past_h = past_arr.reshape(b, past_arr.shape[1], -1).astype(BF16)
        return jnp.concatenate([past_h, new_h], axis=1)

    if past is None:
        pa_k = pa_v = pb_k = pb_v = pb_i = None
    else:
        pa_k, pa_v, pb_k, pb_v, pb_i = past
        pb_i = jnp.concatenate([pb_i, pb_i], axis=-1)
    ka_f, va_f = full_keys(ka_h, pa_k), full_keys(va_h, pa_v)
    kb_f, vb_f, ki_f = full_keys(kb_h, pb_k), full_keys(vb_h, pb_v), full_keys(ki2_h, pb_i)
    n_keys = ka_f.shape[1]
    q0 = n_keys - t
    topk = min(TOPK_MAX, n_keys // 4)
    if short:
        nk_pad = -(-n_keys // LANES) * LANES
        pad_k = lambda a: jnp.pad(a, ((0, 0), (0, nk_pad - n_keys), (0, 0)))
        ka_f, va_f, kb_f, vb_f, ki_f = (pad_k(a) for a in (ka_f, va_f, kb_f, vb_f, ki_f))
        per_b = lambda a: a.reshape(b, t, -1)
        oa = _short_a(per_b(qa), ka_f, va_f, _bias_dense(tab_t[:H_A], t, nk_pad, q0, n_keys),
                      p["subln_a_g"].reshape(1, DV_A), lam, out_scale=1.0 - lam_init)
        ob = _short_dsa(per_b(qb), per_b(qi), per_b(wi), kb_f, vb_f, ki_f,
                        _bias_dense(tab_t[H_A:], t, nk_pad, q0, n_keys), q0=q0, topk=topk, n_keys=n_keys)
    else:
        oa, ob = _tiled_mixers(qa, qb, qi, wi, ka_f, va_f, kb_f, vb_f, ki_f, tab_t, p, lam, lam_init,
                               b=b, t=t, n_keys=n_keys, topk=topk, t_a=t_a, t_b=t_b)
    oa = oa.reshape(n, H_A * DV_A)
    ob = ob.reshape(n, H_B * HD_B)

    merged = _merge(h, oa, ob, w["gate"], w["oa"], w["ob"], min(tm, 256))
    x1, h2 = _outproj(xf, merged, w["out"], p["norm2_g"], min(tm, 256))
    y = _ffn(x1, h2, w["ff1"], w["ff2"], tm, 1024)
    return y.reshape(b, t, D_MODEL), new_rows


def _tiled_mixers(qa, qb, qi, wi, ka_f, va_f, kb_f, vb_f, ki_f, tab_t, p, lam, lam_init,
                  *, b, t, n_keys, topk, t_a, t_b):
    q0 = n_keys - t
    t_pad = max(t_a, t_b)
    assert t_pad % t_a == 0 and t_pad % t_b == 0 and q0 % t_pad == 0
    nk_pad = -(-n_keys // t_pad) * t_pad
    pad_k = lambda a: jnp.pad(a, ((0, 0), (0, nk_pad - n_keys), (0, 0)))
    pad_q = lambda a, tt: jnp.pad(a.reshape(b, t, -1), ((0, 0), (0, -(-t // tt) * tt - t), (0, 0)))
    ka_f, kb_f, ki_f = (pad_k(a) for a in (ka_f, kb_f, ki_f))
    n_valid = lambda tt: n_keys - (n_keys - 1) // tt * tt
    assert all(t <= tt or n_valid(tt) == tt for tt in (t_a, t_b))

    def heads_t(v, nh):
        if v.ndim == 3:
            v = jnp.transpose(v.reshape(b, n_keys, nh, -1), (0, 2, 3, 1))
        return jnp.pad(v, ((0, 0), (0, 0), (0, 0), (0, nk_pad - n_keys)))

    bias_a = _bias_tiles(tab_t[:H_A], t_a, n_valid(t_a))
    oa = _attn_a(pad_q(qa, t_a), ka_f, heads_t(va_f, H_A), bias_a, p["subln_a_g"].reshape(DV_A, 1), lam,
                 t=t_a, q0=q0, out_scale=1.0 - lam_init)
    wi_t = jnp.swapaxes(pad_q(wi, t_b), 1, 2)
    bias_b = _bias_tiles(tab_t[H_A:], t_b, n_valid(t_b))
    ob = _dsa(pad_q(qb, t_b), pad_q(qi, t_b), wi_t, kb_f, heads_t(vb_f, HKV_B), ki_f, bias_b,
              t=t_b, q0=q0, topk=topk, n_keys=n_keys, n_valid_diag=n_valid(t_b))
    return oa[:, :t], ob[:, :t]


def kernel(x_prompt, x_sample, cache_a_k, cache_a_v, cache_b_k, cache_b_v, cache_b_kidx, rel_bias, norm1_g, w_in, qn_a_g, kn_a_g, lam_q1, lam_k1, lam_q2, lam_k2, subln_a_g, qn_b_g, kn_b_g, w_o_a, w_o_b, w_out, norm2_g, w_ff1, w_ff2):
    depth = w_in.shape[0]
    tab_t = rel_bias.T.astype(F32)
    y_prompt, y_sample = x_prompt, x_sample
    prompt_rows, sample_rows = [], []
    for l in range(depth):
        lam_init = 0.8 - 0.6 * math.exp(-0.3 * l)
        lam = (jnp.exp(jnp.sum(lam_q1[l].astype(F32) * lam_k1[l].astype(F32)))
               - jnp.exp(jnp.sum(lam_q2[l].astype(F32) * lam_k2[l].astype(F32))) + lam_init).reshape(1)
        w = _prep_weights(w_in[l], w_o_a[l], w_o_b[l], w_out[l], w_ff1[l], w_ff2[l])
        p = dict(norm1_g=norm1_g[l], qn_a_g=qn_a_g[l], kn_a_g=kn_a_g[l], subln_a_g=subln_a_g[l],
                 qn_b_g=qn_b_g[l], kn_b_g=kn_b_g[l], norm2_g=norm2_g[l])
        y_prompt, rp = _layer(y_prompt, None, w, p, lam, lam_init, tab_t, t_a=512, t_b=256, tm=512)
        past = (cache_a_k[l], cache_a_v[l], cache_b_k[l], cache_b_v[l], cache_b_kidx[l])
        y_sample, rs = _layer(y_sample, past, w, p, lam, lam_init, tab_t, tm=512)
        prompt_rows.append(rp)
        sample_rows.append(rs)
    p_rows = tuple(jnp.stack(r, axis=0) for r in zip(*prompt_rows))
    s_rows = tuple(jnp.stack(r, axis=0) for r in zip(*sample_rows))
    return (y_prompt, y_sample) + p_rows + s_rows
```

```python
import functools
import math

import jax
import jax.numpy as jnp
from jax import lax
from jax.experimental import pallas as pl
from jax.experimental.pallas import tpu as pltpu

F32 = jnp.float32
BF16 = jnp.bfloat16
I32 = jnp.int32

D_MODEL = 2048
CHUNK = 64
H_A = 8
DV_A = 128
HD_A = 64
H_B = 8
HD_B = 128
HKV_B = 2
G_B = H_B // HKV_B
H_IDX = 16
D_IDX = 64
TOPK_MAX = 256
N_BUCKETS = 32
D_FF = 4 * D_MODEL
EPS = 1e-6
LANES = 128
NEG = -1e30
LOG2E = math.log2(math.e)
INT_MIN = -(2 ** 31)
INT_MAX = 2 ** 31 - 1
VMEM_LIMIT = 56 * 1024 * 1024

_NT = (((1,), (1,)), ((), ()))


def _cparams(sem):
    return pltpu.CompilerParams(dimension_semantics=sem, vmem_limit_bytes=VMEM_LIMIT)


def _resident(block_shape, index_map):
    return pl.BlockSpec(block_shape, index_map, pipeline_mode=pl.Buffered(1))


def _rmsnorm_body(x_ref, g_ref, o_ref):
    x = x_ref[...]
    y = x * lax.rsqrt(jnp.mean(x * x, axis=-1, keepdims=True) + EPS)
    o_ref[...] = (y * g_ref[...]).astype(o_ref.dtype)


def _rmsnorm(x, g, tm):
    n, d = x.shape
    return pl.pallas_call(
        _rmsnorm_body,
        grid=(n // tm,),
        in_specs=[pl.BlockSpec((tm, d), lambda i: (i, 0)), pl.BlockSpec((1, d), lambda i: (0, 0))],
        out_specs=pl.BlockSpec((tm, d), lambda i: (i, 0)),
        out_shape=jax.ShapeDtypeStruct((n, d), BF16),
        compiler_params=_cparams(("parallel",)),
        name="rmsnorm",
    )(x, g.reshape(1, d))


def _head_norm(z, gain, split):
    lane = lax.broadcasted_iota(I32, (1, LANES), 1)
    lo = lane < (LANES // 2)
    outs = []
    for h in range(z.shape[1] // LANES):
        zh = z[:, h * LANES:(h + 1) * LANES]
        sq = zh * zh
        if split:
            s_lo = jnp.sum(jnp.where(lo, sq, 0.0), axis=-1, keepdims=True)
            s_hi = jnp.sum(jnp.where(lo, 0.0, sq), axis=-1, keepdims=True)
            r = jnp.where(lo, lax.rsqrt(s_lo * (2.0 / LANES) + EPS), lax.rsqrt(s_hi * (2.0 / LANES) + EPS))
        else:
            r = lax.rsqrt(jnp.mean(sq, axis=-1, keepdims=True) + EPS)
        outs.append(zh * r * gain)
    return jnp.concatenate(outs, axis=-1)


FLAT, HEADS, HEADS_T = "tokens x cols", "tokens x heads x 128", "heads x 128 x tokens"


def _proj_body(epilogue, n_aux, layouts, h_ref, w_ref, *rest):
    z = jnp.dot(h_ref[...], w_ref[...], preferred_element_type=F32)
    outs = epilogue(z, *[r[...] for r in rest[:n_aux]])
    for o_ref, o, layout in zip(rest[n_aux:], outs, layouts):
        if layout == FLAT:
            o_ref[...] = o.astype(o_ref.dtype)
            continue
        for hd in range(o.shape[1] // LANES):
            slab = o[:, hd * LANES:(hd + 1) * LANES]
            if layout == HEADS:
                o_ref[:, hd, :] = slab.astype(o_ref.dtype)
            else:
                o_ref[hd] = slab.T.astype(o_ref.dtype)


def _proj(h, w, aux, epilogue, out_defs, tm, name):
    n, k = h.shape
    c = w.shape[1]
    in_specs = [pl.BlockSpec((tm, k), lambda i: (i, 0)), _resident((k, c), lambda i: (0, 0))]
    in_specs += [pl.BlockSpec(a.shape, lambda i: (0, 0)) for a in aux]
    out_specs, out_shape = [], []
    for oc, dt, layout in out_defs:
        nh = oc // LANES
        if layout == FLAT:
            out_specs.append(pl.BlockSpec((tm, oc), lambda i: (i, 0)))
            out_shape.append(jax.ShapeDtypeStruct((n, oc), dt))
        elif layout == HEADS:
            out_specs.append(pl.BlockSpec((tm, nh, LANES), lambda i: (i, 0, 0)))
            out_shape.append(jax.ShapeDtypeStruct((n, nh, LANES), dt))
        else:
            out_specs.append(pl.BlockSpec((nh, LANES, tm), lambda i: (0, 0, i)))
            out_shape.append(jax.ShapeDtypeStruct((nh, LANES, n), dt))
    return pl.pallas_call(
        functools.partial(_proj_body, epilogue, len(aux), tuple(d[2] for d in out_defs)),
        grid=(n // tm,),
        in_specs=in_specs,
        out_specs=out_specs,
        out_shape=out_shape,
        compiler_params=_cparams(("parallel",)),
        name=name,
    )(h, w, *aux)


def _epi_qa(z, gain):
    return (_head_norm(z, gain, True) * (HD_A ** -0.5 * LOG2E),)


def _epi_ka(z, gain):
    y = _head_norm(z, gain, True)
    return y, y


def _epi_copy2(z):
    return z, z


def _epi_qb(z, gain):
    return (_head_norm(z, gain, False) * (HD_B ** -0.5 * LOG2E),)


def _epi_copy1(z):
    return (z,)


_KB0, _VB0, _KI0, _WI0, _MISC_COLS = 0, 256, 512, 640, 768


def _epi_misc(z, gain):
    kb = _head_norm(z[:, _KB0:_VB0], gain, False)
    vb = z[:, _VB0:_KI0]
    ki2 = z[:, _KI0:_WI0]
    return kb, kb, vb, vb, ki2[:, :D_IDX], ki2, z[:, _WI0:_WI0 + H_IDX]


def _rel_bias(tab_ref, h, rel):
    half = N_BUCKETS // 2
    exact = half // 2
    n = jnp.abs(rel)
    n2 = n * n
    v_neg = jnp.full(rel.shape, tab_ref[h, 0], F32)
    v_pos = jnp.full(rel.shape, tab_ref[h, half], F32)
    for b in range(1, half):
        cond = (n >= b) if b < exact else (n2 >= exact * exact * 2 ** (b - exact))
        v_neg = jnp.where(cond, tab_ref[h, b], v_neg)
        v_pos = jnp.where(cond, tab_ref[h, half + b], v_pos)
    return (jnp.where(rel > 0, v_pos, v_neg) - tab_ref[h, half - 1]) * LOG2E


def _bias_body(t, n_valid_diag, tab_ref, o_ref):
    h = pl.program_id(0)
    kl = lax.broadcasted_iota(I32, (t, t), 0)
    ql = lax.broadcasted_iota(I32, (t, t), 1)
    o_ref[0] = _rel_bias(tab_ref, h, kl - t - ql)
    visible = ((kl // CHUNK) <= (ql // CHUNK)) & (kl < n_valid_diag)
    o_ref[1] = jnp.where(visible, _rel_bias(tab_ref, h, kl - ql), NEG)


def _bias_dense_body(q0, n_keys, tab_ref, o_ref):
    h = pl.program_id(0)
    qpos = q0 + lax.broadcasted_iota(I32, o_ref.shape, 0)
    kpos = lax.broadcasted_iota(I32, o_ref.shape, 1)
    visible = ((kpos // CHUNK) <= (qpos // CHUNK)) & (kpos < n_keys)
    o_ref[...] = jnp.where(visible, _rel_bias(tab_ref, h, kpos - qpos), NEG)


def _bias_dense(tab_t, nq, nk_pad, q0, n_keys):
    nh = tab_t.shape[0]
    return pl.pallas_call(
        functools.partial(_bias_dense_body, q0, n_keys),
        grid=(nh,),
        in_specs=[pl.BlockSpec(memory_space=pltpu.SMEM)],
        out_specs=pl.BlockSpec((None, nq, nk_pad), lambda h: (h, 0, 0)),
        out_shape=jax.ShapeDtypeStruct((nh, nq, nk_pad), F32),
        compiler_params=_cparams(("arbitrary",)),
        name="bias_dense",
    )(tab_t)


def _bias_tiles(tab_t, t, n_valid_diag):
    nh = tab_t.shape[0]
    return pl.pallas_call(
        functools.partial(_bias_body, t, n_valid_diag),
        grid=(nh,),
        in_specs=[pl.BlockSpec(memory_space=pltpu.SMEM)],
        out_specs=pl.BlockSpec((None, 2, t, t), lambda h: (h, 0, 0, 0)),
        out_shape=jax.ShapeDtypeStruct((nh, 2, t, t), F32),
        compiler_params=_cparams(("arbitrary",)),
        name="bias_tiles",
    )(tab_t)


ONES_ROWS = 16


def _softmax_step(s, vt, m_ref, acc_ref):
    m_old = m_ref[...]
    m_new = jnp.maximum(m_old, jnp.max(s, axis=0, keepdims=True))
    alpha = jnp.exp2(m_old - m_new)
    p = jnp.exp2(s - m_new)
    vt_ones = jnp.concatenate([vt, jnp.ones((ONES_ROWS, vt.shape[1]), vt.dtype)], axis=0)
    acc_ref[...] = acc_ref[...] * alpha + jnp.dot(vt_ones, p.astype(BF16), preferred_element_type=F32)
    m_ref[...] = m_new


def _normalised(acc, dv):
    return acc[:dv] / acc[dv:dv + 1]


def _pipelined_tiles(n_far, produce, consume, buf0, buf1):
    @pl.when(n_far == -1)
    def _():
        produce(0, buf0, 1)
        consume(0, buf0)

    @pl.when(n_far == 0)
    def _():
        produce(0, buf0, 0)
        produce(1, buf1, 1)
        consume(0, buf0)
        consume(1, buf1)

    @pl.when(n_far >= 1)
    def _():
        produce(0, buf0, None)

    pairs = jnp.maximum(n_far - 1, 0) // 2

    def body(i, carry):
        j = 2 * i
        produce(j + 1, buf1, None)
        consume(j, buf0)
        produce(j + 2, buf0, None)
        consume(j + 1, buf1)
        return carry

    lax.fori_loop(0, pairs, body, 0)
    d = 2 * pairs
    rem = n_far - d

    @pl.when(rem == 1)
    def _():
        produce(d + 1, buf1, 0)
        consume(d, buf0)
        produce(d + 2, buf0, 1)
        consume(d + 1, buf1)
        consume(d + 2, buf0)

    @pl.when(rem == 2)
    def _():
        produce(d + 1, buf1, None)
        consume(d, buf0)
        produce(d + 2, buf0, 0)
        consume(d + 1, buf1)
        produce(d + 3, buf1, 1)
        consume(d + 2, buf0)
        consume(d + 3, buf1)


def _attn_a_body(t, q0, out_scale, lam_ref, q_ref, k_ref, vt_ref, bias_ref, g_ref, o_ref,
                 qz_ref, m_ref, acc_ref, s0_ref, s1_ref):
    i = pl.program_id(2)
    n_far = q0 // t + i - 1
    q = q_ref[...]
    lane = lax.broadcasted_iota(I32, (t, LANES), 1)
    zero = jnp.zeros_like(q)
    qz_ref[:t] = jnp.where(lane < HD_A, q, zero)
    qz_ref[t:] = jnp.where(lane < HD_A, zero, q)
    m_ref[...] = jnp.full(m_ref.shape, NEG, F32)
    acc_ref[...] = jnp.zeros(acc_ref.shape, F32)

    def produce(j, s_ref, kind):
        ks = pl.multiple_of(j * t, t)
        s = lax.dot_general(k_ref[pl.ds(ks, t), :], qz_ref[...], _NT, preferred_element_type=F32)
        if kind is not None:
            b = bias_ref[kind]
            s = jnp.concatenate([s[:, :t] + b, s[:, t:] + b], axis=1)
        s_ref[...] = s

    def consume(j, s_ref):
        _softmax_step(s_ref[...], vt_ref[:, pl.ds(pl.multiple_of(j * t, t), t)], m_ref, acc_ref)

    _pipelined_tiles(n_far, produce, consume, s0_ref, s1_ref)

    lam = lam_ref[0]
    o = _normalised(acc_ref[...], DV_A)
    o = o[:, :t] - lam * o[:, t:]
    y = o * lax.rsqrt(jnp.mean(o * o, axis=0, keepdims=True) + EPS)
    y = (y * g_ref[...]) * out_scale
    o_ref[...] = y.T.astype(o_ref.dtype)


def _attn_a(q, k, vt, bias, subln_g, lam, *, t, q0, out_scale):
    b, nq, _ = q.shape
    nk = k.shape[1]
    return pl.pallas_call(
        functools.partial(_attn_a_body, t, q0, out_scale),
        grid=(b, H_A, nq // t),
        in_specs=[
            pl.BlockSpec(memory_space=pltpu.SMEM),
            pl.BlockSpec((None, t, LANES), lambda bb, h, i: (bb, i, h)),
            pl.BlockSpec((None, nk, LANES), lambda bb, h, i: (bb, 0, h)),
            pl.BlockSpec((None, None, DV_A, nk), lambda bb, h, i: (bb, h, 0, 0)),
            pl.BlockSpec((None, 2, t, t), lambda bb, h, i: (h, 0, 0, 0)),
            pl.BlockSpec((DV_A, 1), lambda bb, h, i: (0, 0)),
        ],
        out_specs=pl.BlockSpec((None, t, LANES), lambda bb, h, i: (bb, i, h)),
        out_shape=jax.ShapeDtypeStruct((b, nq, H_A * DV_A), BF16),
        scratch_shapes=[
            pltpu.VMEM((2 * t, LANES), BF16),
            pltpu.VMEM((1, 2 * t), F32),
            pltpu.VMEM((DV_A + ONES_ROWS, 2 * t), F32),
            pltpu.VMEM((t, 2 * t), F32),
            pltpu.VMEM((t, 2 * t), F32),
        ],
        compiler_params=_cparams(("parallel", "parallel", "arbitrary")),
        name="diff_attn",
    )(lam, q, k, vt, bias, subln_g)


def _order_key(x):
    b = pltpu.bitcast(x, I32)
    return jnp.where(b < 0, b ^ INT_MAX, b)


def _order_unkey(k):
    return pltpu.bitcast(jnp.where(k < 0, k ^ INT_MAX, k), F32)


def _dsa_body(t, q0, topk, n_keys, n_valid_diag, qb_ref, qi_ref, wi_ref, kb_ref, vbt_ref, ki_ref, bias_ref, o_ref,
              keys_ref, qz_ref, qs_ref, m_ref, acc_ref, s0_ref, s1_ref):
    i = pl.program_id(1)
    n_tiles = q0 // t + i + 1
    n_far = n_tiles - 2

    lane = lax.broadcasted_iota(I32, (t, LANES), 1)
    for hp in range(H_IDX // 2):
        qs = qi_ref[:, hp * LANES:(hp + 1) * LANES]
        zero = jnp.zeros_like(qs)
        qz_ref[2 * hp] = jnp.where(lane < D_IDX, qs, zero)
        qz_ref[2 * hp + 1] = jnp.where(lane < D_IDX, zero, qs)

    def score_tile(j):
        kt = ki_ref[pl.ds(pl.multiple_of(j * t, t), t), :]
        acc = jnp.zeros((t, t), F32)
        for h in range(H_IDX):
            s = lax.dot_general(kt, qz_ref[h], _NT, preferred_element_type=F32)
            acc = acc + jnp.maximum(s, 0.0) * wi_ref[h:h + 1, :]
        return acc

    def fold8(x, op):
        return op(x.reshape(x.shape[0] // 8, 8, t), axis=0)

    def score_body(j, carry):
        smin, smax = carry
        sc = score_tile(j)
        keys_ref[pl.ds(pl.multiple_of(j * t, t), t), :] = _order_key(sc)
        return jnp.minimum(smin, fold8(sc, jnp.min)), jnp.maximum(smax, fold8(sc, jnp.max))

    smin, smax = lax.fori_loop(0, n_tiles - 1, score_body,
                               (jnp.full((8, t), jnp.inf, F32), jnp.full((8, t), -jnp.inf, F32)))
    jd = n_tiles - 1
    kl = lax.broadcasted_iota(I32, (t, t), 0)
    ql = lax.broadcasted_iota(I32, (t, t), 1)
    admissible = ((kl // CHUNK) <= (ql // CHUNK)) & (kl < n_valid_diag)
    sc = score_tile(jd)
    keys_ref[pl.ds(pl.multiple_of(jd * t, t), t), :] = jnp.where(admissible, _order_key(sc), INT_MIN)
    smin = jnp.minimum(smin, fold8(jnp.where(admissible, sc, jnp.inf), jnp.min))
    smax = jnp.maximum(smax, fold8(jnp.where(admissible, sc, -jnp.inf), jnp.max))
    keys_ref[pl.ds(pl.multiple_of(n_tiles * t, t), t), :] = jnp.full((t, t), INT_MIN, I32)
    n_pairs = (n_tiles + 1) // 2

    def count_pairs(hit_fn):
        def body(j, c):
            r0 = pl.multiple_of(j * 2 * t, 2 * t)
            hit = hit_fn(keys_ref[pl.ds(r0, 2 * t), :], r0)
            return c + jnp.sum(hit.reshape(2 * t // 32, 32, t), axis=0)
        part = lax.fori_loop(0, n_pairs, body, jnp.zeros((32, t), F32))
        return jnp.sum(part, axis=0, keepdims=True)

    def search_cond(state):
        return jnp.logical_and(state[0] < 4 * 34, state[-1] > 0)

    def search_body(state):
        it, lo, hi, c_lo, c_hi, _ = state
        f_lo, f_hi = _order_unkey(lo), _order_unkey(hi)
        a, b = jnp.log(c_lo + 0.5), jnp.log(c_hi + 0.5)
        frac = jnp.clip((a - math.log(topk)) / jnp.maximum(a - b, 1e-9), 0.0, 1.0)
        guess = jnp.minimum(jnp.maximum(_order_key(f_lo + (f_hi - f_lo) * frac), lo + 1), hi - 1)
        mid = jnp.where(it % 4 == 3, (lo >> 1) + (hi >> 1) + (lo & hi & 1), guess)
        c = count_pairs(lambda kk, r0: jnp.where(kk >= mid, 1.0, 0.0))
        live = lo + 1 < hi
        ge = c >= topk
        up = jnp.logical_and(live, ge)
        down = jnp.logical_and(live, jnp.logical_not(ge))
        lo = jnp.where(up, mid, lo)
        c_lo = jnp.where(up, c, c_lo)
        hi = jnp.where(jnp.logical_and(live, c == topk), mid + 1, jnp.where(down, mid, hi))
        c_hi = jnp.where(down, c, c_hi)
        return it + 1, lo, hi, c_lo, c_hi, jnp.max(jnp.where(lo + 1 < hi, 1, 0))

    qpos = q0 + i * t + lax.broadcasted_iota(I32, (1, t), 1)
    n_adm = jnp.minimum((qpos // CHUNK + 1) * CHUNK, n_keys).astype(F32)
    lo0 = _order_key(jnp.min(smin, axis=0, keepdims=True))
    hi0 = jnp.where(n_adm <= topk, lo0 + 1, _order_key(jnp.max(smax, axis=0, keepdims=True)) + 1)
    _, thr, _, c_thr, c_above, _ = lax.while_loop(
        search_cond, search_body, (jnp.int32(0), lo0, hi0, n_adm, jnp.zeros((1, t), F32), jnp.int32(1)))

    @pl.when(jnp.max(jnp.where(jnp.logical_and(c_thr > topk, qpos < n_keys), 1, 0)) > 0)
    def _():
        need = topk - c_above
        row = lax.broadcasted_iota(I32, (2 * t, t), 0)

        def cut_body(_, st):
            p_lo, p_hi = st
            p_mid = (p_lo + p_hi) >> 1
            c = count_pairs(lambda kk, r0: jnp.where(
                jnp.logical_and(kk == thr, row + r0 <= p_mid), 1.0, 0.0))
            ok = c >= need
            return jnp.where(ok, p_lo, p_mid), jnp.where(ok, p_mid, p_hi)

        n_pos = keys_ref.shape[0]
        _, cut = lax.fori_loop(0, n_pos.bit_length(), cut_body,
                               (jnp.full((1, t), -1, I32), jnp.full((1, t), n_pos - 1, I32)))

        def demote_body(j, carry):
            r0 = pl.multiple_of(j * 2 * t, 2 * t)
            kk = keys_ref[pl.ds(r0, 2 * t), :]
            keys_ref[pl.ds(r0, 2 * t), :] = jnp.where(jnp.logical_and(kk == thr, row + r0 > cut), thr - 1, kk)
            return carry

        lax.fori_loop(0, n_pairs, demote_body, 0)

    m_ref[...] = jnp.full(m_ref.shape, NEG, F32)
    acc_ref[...] = jnp.zeros(acc_ref.shape, F32)

    for h in range(H_B):
        qs_ref[h // G_B, (h % G_B) * t:(h % G_B + 1) * t, :] = qb_ref[:, h * HD_B:(h + 1) * HD_B]

    def produce(j, s_ref, kind):
        ks = pl.multiple_of(j * t, t)
        off = jnp.where(keys_ref[pl.ds(ks, t), :] >= thr, 0.0, NEG)
        for n in range(HKV_B):
            s = lax.dot_general(kb_ref[pl.ds(ks, t), n * HD_B:(n + 1) * HD_B], qs_ref[n], _NT,
                                preferred_element_type=F32)
            parts = []
            for g in range(G_B):
                off_g = off if kind is None else off + bias_ref[n * G_B + g, kind]
                parts.append(s[:, g * t:(g + 1) * t] + off_g)
            s_ref[n] = jnp.concatenate(parts, axis=1)

    def consume(j, s_ref):
        ks = pl.multiple_of(j * t, t)
        for n in range(HKV_B):
            _softmax_step(s_ref[n], vbt_ref[n, :, pl.ds(ks, t)], m_ref.at[n], acc_ref.at[n])

    _pipelined_tiles(n_far, produce, consume, s0_ref, s1_ref)

    for n in range(HKV_B):
        o = _normalised(acc_ref[n], HD_B)
        for g in range(G_B):
            h = n * G_B + g
            o_ref[:, h * HD_B:(h + 1) * HD_B] = o[:, g * t:(g + 1) * t].T.astype(o_ref.dtype)


def _dsa(qb, qi, wi_t, kb, vbt, ki2, bias, *, t, q0, topk, n_keys, n_valid_diag):
    b, nq, _ = qb.shape
    nk = kb.shape[1]
    return pl.pallas_call(
        functools.partial(_dsa_body, t, q0, topk, n_keys, n_valid_diag),
        grid=(b, nq // t),
        in_specs=[
            pl.BlockSpec((None, t, H_B * HD_B), lambda bb, i: (bb, i, 0)),
            pl.BlockSpec((None, t, H_IDX * D_IDX), lambda bb, i: (bb, i, 0)),
            pl.BlockSpec((None, H_IDX, t), lambda bb, i: (bb, 0, i)),
            _resident((None, nk, HKV_B * HD_B), lambda bb, i: (bb, 0, 0)),
            _resident((None, HKV_B, HD_B, nk), lambda bb, i: (bb, 0, 0, 0)),
            _resident((None, nk, LANES), lambda bb, i: (bb, 0, 0)),
            _resident((H_B, 2, t, t), lambda bb, i: (0, 0, 0, 0)),
        ],
        out_specs=pl.BlockSpec((None, t, H_B * HD_B), lambda bb, i: (bb, i, 0)),
        out_shape=jax.ShapeDtypeStruct((b, nq, H_B * HD_B), BF16),
        scratch_shapes=[
            pltpu.VMEM((nk + t, t), I32),
            pltpu.VMEM((H_IDX, t, LANES), BF16),
            pltpu.VMEM((HKV_B, G_B * t, HD_B), BF16),
            pltpu.VMEM((HKV_B, 1, G_B * t), F32),
            pltpu.VMEM((HKV_B, HD_B + ONES_ROWS, G_B * t), F32),
            pltpu.VMEM((HKV_B, t, G_B * t), F32),
            pltpu.VMEM((HKV_B, t, G_B * t), F32),
        ],
        compiler_params=_cparams(("parallel", "arbitrary")),
        name="dsa",
    )(qb, qi, wi_t, kb, vbt, ki2, bias)


def _softmax_pieces(s_list, v_list):
    m = functools.reduce(jnp.maximum, [jnp.max(s, axis=-1, keepdims=True) for s in s_list])
    p_list = [jnp.exp2(s - m) for s in s_list]
    o = sum(jnp.dot(p.astype(BF16), v, preferred_element_type=F32) for p, v in zip(p_list, v_list))
    return o / sum(jnp.sum(p, axis=-1, keepdims=True) for p in p_list)


def _head_rows(ref, h, n_heads, n_pos):
    return ref[pl.ds(h, n_pos, stride=n_heads), :].astype(BF16)


def _short_a_body(nq, n_past, out_scale, lam_ref, q_ref, kp_ref, vp_ref, kn_ref, vn_ref, bias_ref, g_ref, o_ref):
    lam = lam_ref[0]
    lane = lax.broadcasted_iota(I32, (nq, LANES), 1)
    pieces = ((kp_ref, vp_ref, 0, n_past), (kn_ref, vn_ref, n_past, nq))
    for h in range(H_A):
        cols = slice(h * LANES, (h + 1) * LANES)
        q = q_ref[:, cols]
        zero = jnp.zeros_like(q)
        qz = jnp.concatenate([jnp.where(lane < HD_A, q, zero), jnp.where(lane < HD_A, zero, q)], axis=0)
        b = bias_ref[h]
        b2 = jnp.concatenate([b, b], axis=0)
        s_list = [lax.dot_general(qz, _head_rows(k_ref, h, H_A, size), _NT, preferred_element_type=F32)
                  + b2[:, lo:lo + size] for k_ref, _, lo, size in pieces]
        o = _softmax_pieces(s_list, [_head_rows(v_ref, h, H_A, size) for _, v_ref, _, size in pieces])
        o = o[:nq] - lam * o[nq:]
        y = o * lax.rsqrt(jnp.mean(o * o, axis=-1, keepdims=True) + EPS)
        o_ref[:, cols] = ((y * g_ref[...]) * out_scale).astype(o_ref.dtype)


def _short_a(q, k_past, v_past, k_new, v_new, bias, subln_g, lam, *, out_scale):
    b, nq, d = q.shape
    n_past = k_past.shape[1]
    assert n_past % LANES == 0
    rows = lambda n: pl.BlockSpec((None, n * H_A, DV_A), lambda bb: (bb, 0, 0))
    merge = lambda a: a.reshape(b, a.shape[1] * H_A, DV_A)
    return pl.pallas_call(
        functools.partial(_short_a_body, nq, n_past, out_scale),
        grid=(b,),
        in_specs=[
            pl.BlockSpec(memory_space=pltpu.SMEM),
            pl.BlockSpec((None, nq, d), lambda bb: (bb, 0, 0)),
            rows(n_past), rows(n_past), rows(nq), rows(nq),
            pl.BlockSpec(bias.shape, lambda bb: (0, 0, 0)),
            pl.BlockSpec((1, DV_A), lambda bb: (0, 0)),
        ],
        out_specs=pl.BlockSpec((None, nq, d), lambda bb: (bb, 0, 0)),
        out_shape=jax.ShapeDtypeStruct((b, nq, d), BF16),
        compiler_params=_cparams(("parallel",)),
        name="diff_attn_short",
    )(lam, q, merge(k_past), merge(v_past), merge(k_new), merge(v_new), bias, subln_g)


def _short_dsa_body(nq, n_past, topk, qb_ref, qi_ref, wi_ref, kbp_ref, vbp_ref, kip_ref, kbn_ref, vbn_ref,
                    kin_ref, bias_ref, o_ref):
    lane = lax.broadcasted_iota(I32, (nq, LANES), 1)
    spans = ((0, n_past), (n_past, nq))

    qz = []
    for h in range(H_IDX):
        qs = qi_ref[:, (h // 2) * LANES:(h // 2 + 1) * LANES]
        zero = jnp.zeros_like(qs)
        qz.append(jnp.where(lane < D_IDX, qs, zero) if h % 2 == 0 else jnp.where(lane < D_IDX, zero, qs))
    qz = jnp.concatenate(qz, axis=0)
    keys, kpos = [], []
    for ki_ref, (lo, size) in zip((kip_ref, kin_ref), spans):
        ki = ki_ref[...].astype(BF16)
        s_idx = lax.dot_general(qz, jnp.concatenate([ki, ki], axis=-1), _NT, preferred_element_type=F32)
        score = jnp.zeros((nq, size), F32)
        for h in range(H_IDX):
            score = score + jnp.maximum(s_idx[h * nq:(h + 1) * nq], 0.0) * wi_ref[:, h:h + 1]
        qp = n_past + lax.broadcasted_iota(I32, (nq, size), 0)
        kp = lo + lax.broadcasted_iota(I32, (nq, size), 1)
        keys.append(jnp.where((kp // CHUNK) <= (qp // CHUNK), _order_key(score), INT_MIN))
        kpos.append(kp)

    def count(hit_fn):
        return sum(jnp.sum(jnp.where(hit_fn(k, kp), 1.0, 0.0), axis=-1, keepdims=True)
                   for k, kp in zip(keys, kpos))

    def search_body(state):
        it, lo, hi, c_lo, c_hi, _ = state
        mid = (lo >> 1) + (hi >> 1) + (lo & hi & 1)
        c = count(lambda k, kp: k >= mid)
        live = lo + 1 < hi
        ge = c >= topk
        up = jnp.logical_and(live, ge)
        down = jnp.logical_and(live, jnp.logical_not(ge))
        lo = jnp.where(up, mid, lo)
        c_lo = jnp.where(up, c, c_lo)
        hi = jnp.where(jnp.logical_and(live, c == topk), mid + 1, jnp.where(down, mid, hi))
        c_hi = jnp.where(down, c, c_hi)
        return it + 1, lo, hi, c_lo, c_hi, jnp.max(jnp.where(lo + 1 < hi, 1, 0))

    n_keys = n_past + nq
    n_adm = count(lambda k, kp: k > INT_MIN)
    lo0 = jnp.full((nq, 1), INT_MIN + 1, I32)
    hi0 = jnp.where(n_adm <= topk, lo0 + 1, INT_MAX)
    _, thr, _, c_thr, c_above, _ = lax.while_loop(
        lambda st: jnp.logical_and(st[0] < 34, st[-1] > 0), search_body,
        (jnp.int32(0), lo0, hi0, n_adm, jnp.zeros((nq, 1), F32), jnp.int32(1)))

    need = topk - c_above

    def cut_body(_, st):
        p_lo, p_hi = st
        p_mid = (p_lo + p_hi) >> 1
        ok = count(lambda k, kp: jnp.logical_and(k == thr, kp <= p_mid)) >= need
        return jnp.where(ok, p_lo, p_mid), jnp.where(ok, p_mid, p_hi)

    _, cut = lax.fori_loop(0, n_keys.bit_length(), cut_body,
                           (jnp.full((nq, 1), -1, I32), jnp.full((nq, 1), n_keys - 1, I32)))
    off = [jnp.where(jnp.logical_and(k >= thr, jnp.logical_not(jnp.logical_and(k == thr, kp > cut))), 0.0, NEG)
           for k, kp in zip(keys, kpos)]

    for n in range(HKV_B):
        heads = range(n * G_B, (n + 1) * G_B)
        qs = jnp.concatenate([qb_ref[:, h * HD_B:(h + 1) * HD_B] for h in heads], axis=0)
        s_list = []
        for kb_ref, off_i, (lo, size) in zip((kbp_ref, kbn_ref), off, spans):
            s = lax.dot_general(qs, _head_rows(kb_ref, n, HKV_B, size), _NT, preferred_element_type=F32)
            s_list.append(s + jnp.concatenate([bias_ref[h][:, lo:lo + size] + off_i for h in heads], axis=0))
        o = _softmax_pieces(s_list, [_head_rows(vbp_ref, n, HKV_B, n_past), _head_rows(vbn_ref, n, HKV_B, nq)])
        for g, h in enumerate(heads):
            o_ref[:, h * HD_B:(h + 1) * HD_B] = o[g * nq:(g + 1) * nq].astype(o_ref.dtype)


def _short_dsa(qb, qi, wi, kb_past, vb_past, ki_past, kb_new, vb_new, ki_new, bias, *, topk):
    b, nq, d = qb.shape
    n_past = kb_past.shape[1]
    assert n_past % LANES == 0
    per_batch = lambda *shape: pl.BlockSpec((None,) + shape, lambda bb: (bb,) + (0,) * len(shape))
    merge = lambda a: a.reshape(b, a.shape[1] * HKV_B, HD_B)
    return pl.pallas_call(
        functools.partial(_short_dsa_body, nq, n_past, topk),
        grid=(b,),
        in_specs=[per_batch(nq, d), per_batch(nq, d), per_batch(nq, H_IDX),
                  per_batch(n_past * HKV_B, HD_B), per_batch(n_past * HKV_B, HD_B), per_batch(n_past, D_IDX),
                  per_batch(nq * HKV_B, HD_B), per_batch(nq * HKV_B, HD_B), per_batch(nq, D_IDX),
                  pl.BlockSpec(bias.shape, lambda bb: (0, 0, 0))],
        out_specs=per_batch(nq, d),
        out_shape=jax.ShapeDtypeStruct((b, nq, d), BF16),
        compiler_params=_cparams(("parallel",)),
        name="dsa_short",
    )(qb, qi, wi, merge(kb_past), merge(vb_past), ki_past, merge(kb_new), merge(vb_new), ki_new, bias)


def _sigmoid(x):
    return 1.0 / (1.0 + jnp.exp(-x))


def _merge_body(h_ref, oa_ref, ob_ref, wg_ref, woa_ref, wob_ref, o_ref):
    h = h_ref[...]
    ga = jnp.dot(h, wg_ref[:, :D_MODEL], preferred_element_type=F32)
    ya = jnp.dot(oa_ref[...], woa_ref[...], preferred_element_type=F32)
    merged = _sigmoid(ga) * ya
    gb = jnp.dot(h, wg_ref[:, D_MODEL:], preferred_element_type=F32)
    yb = jnp.dot(ob_ref[...], wob_ref[...], preferred_element_type=F32)
    o_ref[...] = (merged + _sigmoid(gb) * yb).astype(o_ref.dtype)


def _merge(h, oa, ob, wg, woa, wob, tm):
    n = h.shape[0]
    return pl.pallas_call(
        _merge_body,
        grid=(n // tm,),
        in_specs=[
            pl.BlockSpec((tm, D_MODEL), lambda i: (i, 0)),
            pl.BlockSpec((tm, H_A * DV_A), lambda i: (i, 0)),
            pl.BlockSpec((tm, H_B * HD_B), lambda i: (i, 0)),
            _resident((D_MODEL, 2 * D_MODEL), lambda i: (0, 0)),
            _resident((H_A * DV_A, D_MODEL), lambda i: (0, 0)),
            _resident((H_B * HD_B, D_MODEL), lambda i: (0, 0)),
        ],
        out_specs=pl.BlockSpec((tm, D_MODEL), lambda i: (i, 0)),
        out_shape=jax.ShapeDtypeStruct((n, D_MODEL), BF16),
        compiler_params=_cparams(("parallel",)),
        name="gated_merge",
    )(h, oa, ob, wg, woa, wob)


def _outproj_body(x_ref, mg_ref, w_ref, g_ref, x1_ref, h2_ref):
    x1 = x_ref[...] + jnp.dot(mg_ref[...], w_ref[...], preferred_element_type=F32)
    x1_ref[...] = x1
    y = x1 * lax.rsqrt(jnp.mean(x1 * x1, axis=-1, keepdims=True) + EPS)
    h2_ref[...] = (y * g_ref[...]).astype(h2_ref.dtype)


def _outproj(x, merged, w_out, g2, tm):
    n = x.shape[0]
    return pl.pallas_call(
        _outproj_body,
        grid=(n // tm,),
        in_specs=[
            pl.BlockSpec((tm, D_MODEL), lambda i: (i, 0)),
            pl.BlockSpec((tm, D_MODEL), lambda i: (i, 0)),
            _resident((D_MODEL, D_MODEL), lambda i: (0, 0)),
            pl.BlockSpec((1, D_MODEL), lambda i: (0, 0)),
        ],
        out_specs=[pl.BlockSpec((tm, D_MODEL), lambda i: (i, 0)), pl.BlockSpec((tm, D_MODEL), lambda i: (i, 0))],
        out_shape=[jax.ShapeDtypeStruct((n, D_MODEL), F32), jax.ShapeDtypeStruct((n, D_MODEL), BF16)],
        compiler_params=_cparams(("parallel",)),
        name="out_proj",
    )(x, merged, w_out, g2.reshape(1, D_MODEL))


def _ffn_body(x1_ref, h2_ref, w1_ref, w2_ref, o_ref):
    f = pl.program_id(1)

    @pl.when(f == 0)
    def _():
        o_ref[...] = x1_ref[...]

    u = jnp.maximum(jnp.dot(h2_ref[...], w1_ref[...], preferred_element_type=F32), 0.0)
    o_ref[...] += jnp.dot((u * u).astype(BF16), w2_ref[...], preferred_element_type=F32)


def _ffn(x1, h2, w1, w2, tm, tf):
    n = x1.shape[0]
    return pl.pallas_call(
        _ffn_body,
        grid=(n // tm, D_FF // tf),
        in_specs=[
            pl.BlockSpec((tm, D_MODEL), lambda i, f: (i, 0)),
            pl.BlockSpec((tm, D_MODEL), lambda i, f: (i, 0)),
            pl.BlockSpec((D_MODEL, tf), lambda i, f: (0, f)),
            pl.BlockSpec((tf, D_MODEL), lambda i, f: (f, 0)),
        ],
        out_specs=pl.BlockSpec((tm, D_MODEL), lambda i, f: (i, 0)),
        out_shape=jax.ShapeDtypeStruct((n, D_MODEL), F32),
        compiler_params=_cparams(("parallel", "arbitrary")),
        name="ffn",
    )(x1, h2, w1, w2)


def _prep_weights(w_in, w_o_a, w_o_b, w_out, w_ff1, w_ff2):
    sizes = (H_A * 2 * HD_A, H_A * 2 * HD_A, H_A * DV_A, H_B * HD_B, HKV_B * HD_B, HKV_B * HD_B,
             H_IDX * D_IDX, D_IDX, H_IDX, D_MODEL, D_MODEL)
    offs = [0]
    for s in sizes:
        offs.append(offs[-1] + s)
    col = lambda a, b: w_in[:, offs[a]:offs[b]]
    w_misc = jnp.concatenate(
        [col(4, 6), col(7, 8), col(7, 8), col(8, 9),
         jnp.zeros((D_MODEL, _MISC_COLS - _WI0 - H_IDX), w_in.dtype)], axis=1)
    return dict(
        qa=col(0, 1).astype(BF16), ka=col(1, 2).astype(BF16), va=col(2, 3).astype(BF16),
        qb=col(3, 4).astype(BF16), misc=w_misc.astype(BF16), qi=col(6, 7).astype(BF16),
        gate=col(9, 11).astype(BF16), oa=w_o_a.astype(BF16), ob=w_o_b.astype(BF16),
        out=w_out.astype(BF16), ff1=w_ff1.astype(BF16), ff2=w_ff2.astype(BF16))


def _layer(x, past, w, p, lam, lam_init, tab_t, *, tm, t_a=None, t_b=None):
    short = t_a is None
    b, t, _ = x.shape
    n = b * t
    xf = x.reshape(n, D_MODEL)
    h = _rmsnorm(xf, p["norm1_g"], tm)

    gain_a = lambda g: g.reshape(1, 2 * HD_A)
    v_layout = HEADS_T if (not short and past is None and b == 1) else FLAT
    (qa,) = _proj(h, w["qa"], [gain_a(p["qn_a_g"])], _epi_qa, [(1024, BF16, FLAT)], tm, "proj_qa")
    ka, ka_h = _proj(h, w["ka"], [gain_a(p["kn_a_g"])], _epi_ka, [(1024, F32, HEADS), (1024, BF16, FLAT)],
                     tm, "proj_ka")
    va, va_h = _proj(h, w["va"], [], _epi_copy2, [(1024, F32, HEADS), (1024, BF16, v_layout)], tm, "proj_va")
    (qb,) = _proj(h, w["qb"], [p["qn_b_g"].reshape(1, HD_B)], _epi_qb, [(1024, BF16, FLAT)], tm, "proj_qb")
    kb, kb_h, vb, vb_h, ki, ki2_h, wi = _proj(
        h, w["misc"], [p["kn_b_g"].reshape(1, HD_B)], _epi_misc,
        [(256, F32, HEADS), (256, BF16, FLAT), (256, F32, HEADS), (256, BF16, v_layout), (D_IDX, F32, FLAT),
         (2 * D_IDX, BF16, FLAT), (H_IDX, F32, FLAT)],
        tm, "proj_misc")
    (qi,) = _proj(h, w["qi"], [], _epi_copy1, [(1024, BF16, FLAT)], tm, "proj_qi")

    new_rows = (ka.reshape(b, t, H_A, 2 * HD_A), va.reshape(b, t, H_A, DV_A), kb.reshape(b, t, HKV_B, HD_B),
                vb.reshape(b, t, HKV_B, HD_B), ki.reshape(b, t, D_IDX))

    per_b = lambda a: a.reshape(b, t, -1)
    if short:
        n_keys = past[0].shape[1] + t
        topk = min(TOPK_MAX, n_keys // 4)
        oa = _short_a(per_b(qa), past[0], past[1], new_rows[0], new_rows[1],
                      _bias_dense(tab_t[:H_A], t, n_keys, n_keys - t, n_keys),
                      p["subln_a_g"].reshape(1, DV_A), lam, out_scale=1.0 - lam_init)
        ob = _short_dsa(per_b(qb), per_b(qi), per_b(wi), past[2], past[3], past[4], *new_rows[2:],
                        _bias_dense(tab_t[H_A:], t, n_keys, n_keys - t, n_keys), topk=topk)
    else:
        oa, ob = _tiled_mixers(qa, qb, qi, wi, ka_h, va_h, kb_h, vb_h, ki2_h, past, tab_t, p, lam, lam_init,
                               b=b, t=t, t_a=t_a, t_b=t_b)
    oa = oa.reshape(n, H_A * DV_A)
    ob = ob.reshape(n, H_B * HD_B)

    merged = _merge(h, oa, ob, w["gate"], w["oa"], w["ob"], min(tm, 256))
    x1, h2 = _outproj(xf, merged, w["out"], p["norm2_g"], min(tm, 256))
    y = _ffn(x1, h2, w["ff1"], w["ff2"], tm, 1024)
    return y.reshape(b, t, D_MODEL), new_rows


def _tiled_mixers(qa, qb, qi, wi, ka_h, va_h, kb_h, vb_h, ki2_h, past, tab_t, p, lam, lam_init,
                  *, b, t, t_a, t_b):
    def full_keys(new_h, past_arr):
        if new_h.ndim == 3:
            return new_h[None]
        new_h = new_h.reshape(b, t, -1)
        if past_arr is None:
            return new_h
        past_h = past_arr.reshape(b, past_arr.shape[1], -1).astype(BF16)
        return jnp.concatenate([past_h, new_h], axis=1)

    if past is None:
        pa_k = pa_v = pb_k = pb_v = pb_i = None
    else:
        pa_k, pa_v, pb_k, pb_v, pb_i = past
        pb_i = jnp.concatenate([pb_i, pb_i], axis=-1)
    ka_f, va_f = full_keys(ka_h, pa_k), full_keys(va_h, pa_v)
    kb_f, vb_f, ki_f = full_keys(kb_h, pb_k), full_keys(vb_h, pb_v), full_keys(ki2_h, pb_i)
    n_keys = ka_f.shape[1]
    q0 = n_keys - t
    topk = min(TOPK_MAX, n_keys // 4)
    t_pad = max(t_a, t_b)
    assert t_pad % t_a == 0 and t_pad % t_b == 0 and q0 % t_pad == 0
    nk_pad = -(-n_keys // t_pad) * t_pad
    pad_k = lambda a: jnp.pad(a, ((0, 0), (0, nk_pad - n_keys), (0, 0)))
    pad_q = lambda a, tt: jnp.pad(a.reshape(b, t, -1), ((0, 0), (0, -(-t // tt) * tt - t), (0, 0)))
    ka_f, kb_f, ki_f = (pad_k(a) for a in (ka_f, kb_f, ki_f))
    n_valid = lambda tt: n_keys - (n_keys - 1) // tt * tt
    assert all(t <= tt or n_valid(tt) == tt for tt in (t_a, t_b))

    def heads_t(v, nh):
        if v.ndim == 3:
            v = jnp.transpose(v.reshape(b, n_keys, nh, -1), (0, 2, 3, 1))
        return jnp.pad(v, ((0, 0), (0, 0), (0, 0), (0, nk_pad - n_keys)))

    bias_a = _bias_tiles(tab_t[:H_A], t_a, n_valid(t_a))
    oa = _attn_a(pad_q(qa, t_a), ka_f, heads_t(va_f, H_A), bias_a, p["subln_a_g"].reshape(DV_A, 1), lam,
                 t=t_a, q0=q0, out_scale=1.0 - lam_init)
    wi_t = jnp.swapaxes(pad_q(wi, t_b), 1, 2)
    bias_b = _bias_tiles(tab_t[H_A:], t_b, n_valid(t_b))
    ob = _dsa(pad_q(qb, t_b), pad_q(qi, t_b), wi_t, kb_f, heads_t(vb_f, HKV_B), ki_f, bias_b,
              t=t_b, q0=q0, topk=topk, n_keys=n_keys, n_valid_diag=n_valid(t_b))
    return oa[:, :t], ob[:, :t]


def kernel(x_prompt, x_sample, cache_a_k, cache_a_v, cache_b_k, cache_b_v, cache_b_kidx, rel_bias, norm1_g, w_in, qn_a_g, kn_a_g, lam_q1, lam_k1, lam_q2, lam_k2, subln_a_g, qn_b_g, kn_b_g, w_o_a, w_o_b, w_out, norm2_g, w_ff1, w_ff2):
    depth = w_in.shape[0]
    tab_t = rel_bias.T.astype(F32)
    y_prompt, y_sample = x_prompt, x_sample
    prompt_rows, sample_rows = [], []
    for l in range(depth):
        lam_init = 0.8 - 0.6 * math.exp(-0.3 * l)
        lam = (jnp.exp(jnp.sum(lam_q1[l].astype(F32) * lam_k1[l].astype(F32)))
               - jnp.exp(jnp.sum(lam_q2[l].astype(F32) * lam_k2[l].astype(F32))) + lam_init).reshape(1)
        w = _prep_weights(w_in[l], w_o_a[l], w_o_b[l], w_out[l], w_ff1[l], w_ff2[l])
        p = dict(norm1_g=norm1_g[l], qn_a_g=qn_a_g[l], kn_a_g=kn_a_g[l], subln_a_g=subln_a_g[l],
                 qn_b_g=qn_b_g[l], kn_b_g=kn_b_g[l], norm2_g=norm2_g[l])
        y_prompt, rp = _layer(y_prompt, None, w, p, lam, lam_init, tab_t, t_a=512, t_b=256, tm=512)
        past = (cache_a_k[l], cache_a_v[l], cache_b_k[l], cache_b_v[l], cache_b_kidx[l])
        y_sample, rs = _layer(y_sample, past, w, p, lam, lam_init, tab_t, tm=512)
        prompt_rows.append(rp)
        sample_rows.append(rs)
    p_rows = tuple(jnp.stack(r, axis=0) for r in zip(*prompt_rows))
    s_rows = tuple(jnp.stack(r, axis=0) for r in zip(*sample_rows))
    return (y_prompt, y_sample) + p_rows + s_rows
```

```python
import functools
import math

import jax
import jax.numpy as jnp
from jax import lax
from jax.experimental import pallas as pl
from jax.experimental.pallas import tpu as pltpu

F32 = jnp.float32
BF16 = jnp.bfloat16
I32 = jnp.int32

D_MODEL = 2048
CHUNK = 64
H_A = 8
DV_A = 128
HD_A = 64
H_B = 8
HD_B = 128
HKV_B = 2
G_B = H_B // HKV_B
H_IDX = 16
D_IDX = 64
TOPK_MAX = 256
N_BUCKETS = 32
D_FF = 4 * D_MODEL
EPS = 1e-6
LANES = 128
NEG = -1e30
LOG2E = math.log2(math.e)
INT_MIN = -(2 ** 31)
INT_MAX = 2 ** 31 - 1
VMEM_LIMIT = 56 * 1024 * 1024

_NT = (((1,), (1,)), ((), ()))


def _cparams(sem):
    return pltpu.CompilerParams(dimension_semantics=sem, vmem_limit_bytes=VMEM_LIMIT)


def _resident(block_shape, index_map):
    return pl.BlockSpec(block_shape, index_map, pipeline_mode=pl.Buffered(1))


def _rmsnorm_body(x_ref, g_ref, o_ref):
    x = x_ref[...]
    y = x * lax.rsqrt(jnp.mean(x * x, axis=-1, keepdims=True) + EPS)
    o_ref[...] = (y * g_ref[...]).astype(o_ref.dtype)


def _rmsnorm(x, g, tm):
    n, d = x.shape
    return pl.pallas_call(
        _rmsnorm_body,
        grid=(n // tm,),
        in_specs=[pl.BlockSpec((tm, d), lambda i: (i, 0)), pl.BlockSpec((1, d), lambda i: (0, 0))],
        out_specs=pl.BlockSpec((tm, d), lambda i: (i, 0)),
        out_shape=jax.ShapeDtypeStruct((n, d), BF16),
        compiler_params=_cparams(("parallel",)),
        name="rmsnorm",
    )(x, g.reshape(1, d))


def _head_norm(z, gain, split):
    lane = lax.broadcasted_iota(I32, (1, LANES), 1)
    lo = lane < (LANES // 2)
    outs = []
    for h in range(z.shape[1] // LANES):
        zh = z[:, h * LANES:(h + 1) * LANES]
        sq = zh * zh
        if split:
            s_lo = jnp.sum(jnp.where(lo, sq, 0.0), axis=-1, keepdims=True)
            s_hi = jnp.sum(jnp.where(lo, 0.0, sq), axis=-1, keepdims=True)
            r = jnp.where(lo, lax.rsqrt(s_lo * (2.0 / LANES) + EPS), lax.rsqrt(s_hi * (2.0 / LANES) + EPS))
        else:
            r = lax.rsqrt(jnp.mean(sq, axis=-1, keepdims=True) + EPS)
        outs.append(zh * r * gain)
    return jnp.concatenate(outs, axis=-1)


FLAT, HEADS, HEADS_T = "tokens x cols", "tokens x heads x 128", "heads x 128 x tokens"


def _proj_body(epilogue, n_aux, layouts, h_ref, w_ref, *rest):
    z = jnp.dot(h_ref[...], w_ref[...], preferred_element_type=F32)
    outs = epilogue(z, *[r[...] for r in rest[:n_aux]])
    for o_ref, o, layout in zip(rest[n_aux:], outs, layouts):
        if layout == FLAT:
            o_ref[...] = o.astype(o_ref.dtype)
            continue
        for hd in range(o.shape[1] // LANES):
            slab = o[:, hd * LANES:(hd + 1) * LANES]
            if layout == HEADS:
                o_ref[:, hd, :] = slab.astype(o_ref.dtype)
            else:
                o_ref[hd] = slab.T.astype(o_ref.dtype)


def _proj(h, w, aux, epilogue, out_defs, tm, name):
    n, k = h.shape
    c = w.shape[1]
    in_specs = [pl.BlockSpec((tm, k), lambda i: (i, 0)), _resident((k, c), lambda i: (0, 0))]
    in_specs += [pl.BlockSpec(a.shape, lambda i: (0, 0)) for a in aux]
    out_specs, out_shape = [], []
    for oc, dt, layout in out_defs:
        nh = oc // LANES
        if layout == FLAT:
            out_specs.append(pl.BlockSpec((tm, oc), lambda i: (i, 0)))
            out_shape.append(jax.ShapeDtypeStruct((n, oc), dt))
        elif layout == HEADS:
            out_specs.append(pl.BlockSpec((tm, nh, LANES), lambda i: (i, 0, 0)))
            out_shape.append(jax.ShapeDtypeStruct((n, nh, LANES), dt))
        else:
            out_specs.append(pl.BlockSpec((nh, LANES, tm), lambda i: (0, 0, i)))
            out_shape.append(jax.ShapeDtypeStruct((nh, LANES, n), dt))
    return pl.pallas_call(
        functools.partial(_proj_body, epilogue, len(aux), tuple(d[2] for d in out_defs)),
        grid=(n // tm,),
        in_specs=in_specs,
        out_specs=out_specs,
        out_shape=out_shape,
        compiler_params=_cparams(("parallel",)),
        name=name,
    )(h, w, *aux)


def _epi_qa(z, gain):
    return (_head_norm(z, gain, True) * (HD_A ** -0.5 * LOG2E),)


def _epi_ka(z, gain):
    y = _head_norm(z, gain, True)
    return y, y


def _epi_copy2(z):
    return z, z


def _epi_qb(z, gain):
    return (_head_norm(z, gain, False) * (HD_B ** -0.5 * LOG2E),)


def _epi_copy1(z):
    return (z,)


_KB0, _VB0, _KI0, _WI0, _MISC_COLS = 0, 256, 512, 640, 768


def _epi_misc(z, gain):
    kb = _head_norm(z[:, _KB0:_VB0], gain, False)
    vb = z[:, _VB0:_KI0]
    ki2 = z[:, _KI0:_WI0]
    return kb, kb, vb, vb, ki2[:, :D_IDX], ki2, z[:, _WI0:_WI0 + H_IDX]


def _rel_bias(tab_ref, h, rel):
    half = N_BUCKETS // 2
    exact = half // 2
    n = jnp.abs(rel)
    n2 = n * n
    v_neg = jnp.full(rel.shape, tab_ref[h, 0], F32)
    v_pos = jnp.full(rel.shape, tab_ref[h, half], F32)
    for b in range(1, half):
        cond = (n >= b) if b < exact else (n2 >= exact * exact * 2 ** (b - exact))
        v_neg = jnp.where(cond, tab_ref[h, b], v_neg)
        v_pos = jnp.where(cond, tab_ref[h, half + b], v_pos)
    return (jnp.where(rel > 0, v_pos, v_neg) - tab_ref[h, half - 1]) * LOG2E


def _bias_body(t, n_valid_diag, tab_ref, o_ref):
    h = pl.program_id(0)
    kl = lax.broadcasted_iota(I32, (t, t), 0)
    ql = lax.broadcasted_iota(I32, (t, t), 1)
    o_ref[0] = _rel_bias(tab_ref, h, kl - t - ql)
    visible = ((kl // CHUNK) <= (ql // CHUNK)) & (kl < n_valid_diag)
    o_ref[1] = jnp.where(visible, _rel_bias(tab_ref, h, kl - ql), NEG)


def _bias_dense_body(q0, n_keys, tab_ref, o_ref):
    h = pl.program_id(0)
    qpos = q0 + lax.broadcasted_iota(I32, o_ref.shape, 0)
    kpos = lax.broadcasted_iota(I32, o_ref.shape, 1)
    visible = ((kpos // CHUNK) <= (qpos // CHUNK)) & (kpos < n_keys)
    o_ref[...] = jnp.where(visible, _rel_bias(tab_ref, h, kpos - qpos), NEG)


def _bias_dense(tab_t, nq, nk_pad, q0, n_keys):
    nh = tab_t.shape[0]
    return pl.pallas_call(
        functools.partial(_bias_dense_body, q0, n_keys),
        grid=(nh,),
        in_specs=[pl.BlockSpec(memory_space=pltpu.SMEM)],
        out_specs=pl.BlockSpec((None, nq, nk_pad), lambda h: (h, 0, 0)),
        out_shape=jax.ShapeDtypeStruct((nh, nq, nk_pad), F32),
        compiler_params=_cparams(("arbitrary",)),
        name="bias_dense",
    )(tab_t)


def _bias_tiles(tab_t, t, n_valid_diag):
    nh = tab_t.shape[0]
    return pl.pallas_call(
        functools.partial(_bias_body, t, n_valid_diag),
        grid=(nh,),
        in_specs=[pl.BlockSpec(memory_space=pltpu.SMEM)],
        out_specs=pl.BlockSpec((None, 2, t, t), lambda h: (h, 0, 0, 0)),
        out_shape=jax.ShapeDtypeStruct((nh, 2, t, t), F32),
        compiler_params=_cparams(("arbitrary",)),
        name="bias_tiles",
    )(tab_t)


ONES_ROWS = 16


def _softmax_step(s, vt, m_ref, acc_ref):
    m_old = m_ref[...]
    m_new = jnp.maximum(m_old, jnp.max(s, axis=0, keepdims=True))
    alpha = jnp.exp2(m_old - m_new)
    p = jnp.exp2(s - m_new)
    vt_ones = jnp.concatenate([vt, jnp.ones((ONES_ROWS, vt.shape[1]), vt.dtype)], axis=0)
    acc_ref[...] = acc_ref[...] * alpha + jnp.dot(vt_ones, p.astype(BF16), preferred_element_type=F32)
    m_ref[...] = m_new


def _normalised(acc, dv):
    return acc[:dv] / acc[dv:dv + 1]


def _pipelined_tiles(n_far, produce, consume, buf0, buf1):
    @pl.when(n_far == -1)
    def _():
        produce(0, buf0, 1)
        consume(0, buf0)

    @pl.when(n_far == 0)
    def _():
        produce(0, buf0, 0)
        produce(1, buf1, 1)
        consume(0, buf0)
        consume(1, buf1)

    @pl.when(n_far >= 1)
    def _():
        produce(0, buf0, None)

    pairs = jnp.maximum(n_far - 1, 0) // 2

    def pair(j):
        produce(j + 1, buf1, None)
        consume(j, buf0)
        produce(j + 2, buf0, None)
        consume(j + 1, buf1)

    def body(i, carry):
        pair(4 * i)
        pair(4 * i + 2)
        return carry

    lax.fori_loop(0, pairs // 2, body, 0)

    @pl.when(pairs % 2 == 1)
    def _():
        pair(2 * (pairs - 1))
    d = 2 * pairs
    rem = n_far - d

    @pl.when(rem == 1)
    def _():
        produce(d + 1, buf1, 0)
        consume(d, buf0)
        produce(d + 2, buf0, 1)
        consume(d + 1, buf1)
        consume(d + 2, buf0)

    @pl.when(rem == 2)
    def _():
        produce(d + 1, buf1, None)
        consume(d, buf0)
        produce(d + 2, buf0, 0)
        consume(d + 1, buf1)
        produce(d + 3, buf1, 1)
        consume(d + 2, buf0)
        consume(d + 3, buf1)


def _attn_a_body(t, q0, out_scale, lam_ref, q_ref, k_ref, vt_ref, bias_ref, g_ref, o_ref,
                 qz_ref, m_ref, acc_ref, s0_ref, s1_ref):
    i = pl.program_id(2)
    n_far = q0 // t + i - 1
    q = q_ref[...]
    lane = lax.broadcasted_iota(I32, (t, LANES), 1)
    zero = jnp.zeros_like(q)
    qz_ref[:, :t] = jnp.where(lane < HD_A, q, zero).T
    qz_ref[:, t:] = jnp.where(lane < HD_A, zero, q).T
    m_ref[...] = jnp.full(m_ref.shape, NEG, F32)
    acc_ref[...] = jnp.zeros(acc_ref.shape, F32)

    def produce(j, s_ref, kind):
        ks = pl.multiple_of(j * t, t)
        s = jnp.dot(k_ref[pl.ds(ks, t), :], qz_ref[...], preferred_element_type=F32)
        if kind is not None:
            b = bias_ref[kind]
            s = jnp.concatenate([s[:, :t] + b, s[:, t:] + b], axis=1)
        s_ref[...] = s

    def consume(j, s_ref):
        _softmax_step(s_ref[...], vt_ref[:, pl.ds(pl.multiple_of(j * t, t), t)], m_ref, acc_ref)

    _pipelined_tiles(n_far, produce, consume, s0_ref, s1_ref)

    lam = lam_ref[0]
    o = _normalised(acc_ref[...], DV_A)
    o = o[:, :t] - lam * o[:, t:]
    y = o * lax.rsqrt(jnp.mean(o * o, axis=0, keepdims=True) + EPS)
    y = (y * g_ref[...]) * out_scale
    o_ref[...] = y.T.astype(o_ref.dtype)


def _attn_a(q, k, vt, bias, subln_g, lam, *, t, q0, out_scale):
    b, nq, _ = q.shape
    nk = k.shape[1]
    return pl.pallas_call(
        functools.partial(_attn_a_body, t, q0, out_scale),
        grid=(b, H_A, nq // t),
        in_specs=[
            pl.BlockSpec(memory_space=pltpu.SMEM),
            pl.BlockSpec((None, t, LANES), lambda bb, h, i: (bb, i, h)),
            pl.BlockSpec((None, nk, LANES), lambda bb, h, i: (bb, 0, h)),
            pl.BlockSpec((None, None, DV_A, nk), lambda bb, h, i: (bb, h, 0, 0)),
            pl.BlockSpec((None, 2, t, t), lambda bb, h, i: (h, 0, 0, 0)),
            pl.BlockSpec((DV_A, 1), lambda bb, h, i: (0, 0)),
        ],
        out_specs=pl.BlockSpec((None, t, LANES), lambda bb, h, i: (bb, i, h)),
        out_shape=jax.ShapeDtypeStruct((b, nq, H_A * DV_A), BF16),
        scratch_shapes=[
            pltpu.VMEM((LANES, 2 * t), BF16),
            pltpu.VMEM((1, 2 * t), F32),
            pltpu.VMEM((DV_A + ONES_ROWS, 2 * t), F32),
            pltpu.VMEM((t, 2 * t), F32),
            pltpu.VMEM((t, 2 * t), F32),
        ],
        compiler_params=_cparams(("parallel", "parallel", "arbitrary")),
        name="diff_attn",
    )(lam, q, k, vt, bias, subln_g)


def _order_key(x):
    b = pltpu.bitcast(x, I32)
    return jnp.where(b < 0, b ^ INT_MAX, b)


def _order_unkey(k):
    return pltpu.bitcast(jnp.where(k < 0, k ^ INT_MAX, k), F32)


def _dsa_body(t, q0, topk, n_keys, n_valid_diag, qb_ref, qi_ref, wi_ref, kb_ref, vbt_ref, ki_ref, bias_ref, o_ref,
              keys_ref, qz_ref, qs_ref, mm_ref, m_ref, acc_ref, s0_ref, s1_ref):
    i = pl.program_id(1)
    n_tiles = q0 // t + i + 1
    n_far = n_tiles - 2

    lane = lax.broadcasted_iota(I32, (t, LANES), 1)
    for hp in range(H_IDX // 2):
        qs = qi_ref[:, hp * LANES:(hp + 1) * LANES]
        zero = jnp.zeros_like(qs)
        qz_ref[2 * hp] = jnp.where(lane < D_IDX, qs, zero).T
        qz_ref[2 * hp + 1] = jnp.where(lane < D_IDX, zero, qs).T

    def score_tile(j):
        kt = ki_ref[pl.ds(pl.multiple_of(j * t, t), t), :]
        acc = jnp.zeros((t, t), F32)
        for h in range(H_IDX):
            s = jnp.dot(kt, qz_ref[h], preferred_element_type=F32)
            acc = acc + jnp.maximum(s, 0.0) * wi_ref[h:h + 1, :]
        return acc

    def fold8(x, op):
        return op(x.reshape(x.shape[0] // 8, 8, t), axis=0)

    mm_ref[0] = jnp.full((8, t), jnp.inf, F32)
    mm_ref[1] = jnp.full((8, t), -jnp.inf, F32)

    def score_store(j):
        sc = score_tile(j)
        keys_ref[pl.ds(pl.multiple_of(j * t, t), t), :] = _order_key(sc)
        mm_ref[0] = jnp.minimum(mm_ref[0], fold8(sc, jnp.min))
        mm_ref[1] = jnp.maximum(mm_ref[1], fold8(sc, jnp.max))

    def score_pair(jj, carry):
        score_store(2 * jj)
        score_store(2 * jj + 1)
        return carry

    jd = n_tiles - 1
    lax.fori_loop(0, jd // 2, score_pair, 0)

    @pl.when(jd % 2 == 1)
    def _():
        score_store(jd - 1)

    kl = lax.broadcasted_iota(I32, (t, t), 0)
    ql = lax.broadcasted_iota(I32, (t, t), 1)
    admissible = ((kl // CHUNK) <= (ql // CHUNK)) & (kl < n_valid_diag)
    sc = score_tile(jd)
    keys_ref[pl.ds(pl.multiple_of(jd * t, t), t), :] = jnp.where(admissible, _order_key(sc), INT_MIN)
    smin = jnp.minimum(mm_ref[0], fold8(jnp.where(admissible, sc, jnp.inf), jnp.min))
    smax = jnp.maximum(mm_ref[1], fold8(jnp.where(admissible, sc, -jnp.inf), jnp.max))
    keys_ref[pl.ds(pl.multiple_of(n_tiles * t, t), t), :] = jnp.full((t, t), INT_MIN, I32)
    n_pairs = (n_tiles + 1) // 2

    def count_pairs(hit_fn):
        def body(j, c):
            r0 = pl.multiple_of(j * 2 * t, 2 * t)
            hit = hit_fn(keys_ref[pl.ds(r0, 2 * t), :], r0)
            return c + jnp.sum(hit.reshape(2 * t // 32, 32, t), axis=0)
        part = lax.fori_loop(0, n_pairs, body, jnp.zeros((32, t), F32))
        return jnp.sum(part, axis=0, keepdims=True)

    def search_cond(state):
        return jnp.logical_and(state[0] < 4 * 34, state[-1] > 0)

    def search_body(state):
        it, lo, hi, c_lo, c_hi, _ = state
        f_lo, f_hi = _order_unkey(lo), _order_unkey(hi)
        a, b = jnp.log(c_lo + 0.5), jnp.log(c_hi + 0.5)
        frac = jnp.clip((a - math.log(topk)) / jnp.maximum(a - b, 1e-9), 0.0, 1.0)
        guess = jnp.minimum(jnp.maximum(_order_key(f_lo + (f_hi - f_lo) * frac), lo + 1), hi - 1)
        mid = jnp.where(it % 4 == 3, (lo >> 1) + (hi >> 1) + (lo & hi & 1), guess)
        c = count_pairs(lambda kk, r0: jnp.where(kk >= mid, 1.0, 0.0))
        live = lo + 1 < hi
        ge = c >= topk
        up = jnp.logical_and(live, ge)
        down = jnp.logical_and(live, jnp.logical_not(ge))
        lo = jnp.where(up, mid, lo)
        c_lo = jnp.where(up, c, c_lo)
        hi = jnp.where(jnp.logical_and(live, c == topk), mid + 1, jnp.where(down, mid, hi))
        c_hi = jnp.where(down, c, c_hi)
        return it + 1, lo, hi, c_lo, c_hi, jnp.max(jnp.where(lo + 1 < hi, 1, 0))

    qpos = q0 + i * t + lax.broadcasted_iota(I32, (1, t), 1)
    n_adm = jnp.minimum((qpos // CHUNK + 1) * CHUNK, n_keys).astype(F32)
    lo0 = _order_key(jnp.min(smin, axis=0, keepdims=True))
    hi0 = jnp.where(n_adm <= topk, lo0 + 1, _order_key(jnp.max(smax, axis=0, keepdims=True)) + 1)
    _, thr, _, c_thr, c_above, _ = lax.while_loop(
        search_cond, search_body, (jnp.int32(0), lo0, hi0, n_adm, jnp.zeros((1, t), F32), jnp.int32(1)))

    @pl.when(jnp.max(jnp.where(jnp.logical_and(c_thr > topk, qpos < n_keys), 1, 0)) > 0)
    def _():
        need = topk - c_above
        row = lax.broadcasted_iota(I32, (2 * t, t), 0)

        def cut_body(_, st):
            p_lo, p_hi = st
            p_mid = (p_lo + p_hi) >> 1
            c = count_pairs(lambda kk, r0: jnp.where(
                jnp.logical_and(kk == thr, row + r0 <= p_mid), 1.0, 0.0))
            ok = c >= need
            return jnp.where(ok, p_lo, p_mid), jnp.where(ok, p_mid, p_hi)

        n_pos = keys_ref.shape[0]
        _, cut = lax.fori_loop(0, n_pos.bit_length(), cut_body,
                               (jnp.full((1, t), -1, I32), jnp.full((1, t), n_pos - 1, I32)))

        def demote_body(j, carry):
            r0 = pl.multiple_of(j * 2 * t, 2 * t)
            kk = keys_ref[pl.ds(r0, 2 * t), :]
            keys_ref[pl.ds(r0, 2 * t), :] = jnp.where(jnp.logical_and(kk == thr, row + r0 > cut), thr - 1, kk)
            return carry

        lax.fori_loop(0, n_pairs, demote_body, 0)

    m_ref[...] = jnp.full(m_ref.shape, NEG, F32)
    acc_ref[...] = jnp.zeros(acc_ref.shape, F32)

    for h in range(H_B):
        qs_ref[h // G_B, :, (h % G_B) * t:(h % G_B + 1) * t] = qb_ref[:, h * HD_B:(h + 1) * HD_B].T

    def produce(j, s_ref, kind):
        ks = pl.multiple_of(j * t, t)
        off = jnp.where(keys_ref[pl.ds(ks, t), :] >= thr, 0.0, NEG)
        for n in range(HKV_B):
            s = jnp.dot(kb_ref[pl.ds(ks, t), n * HD_B:(n + 1) * HD_B], qs_ref[n], preferred_element_type=F32)
            parts = []
            for g in range(G_B):
                off_g = off if kind is None else off + bias_ref[n * G_B + g, kind]
                parts.append(s[:, g * t:(g + 1) * t] + off_g)
            s_ref[n] = jnp.concatenate(parts, axis=1)

    def consume(j, s_ref):
        ks = pl.multiple_of(j * t, t)
        for n in range(HKV_B):
            _softmax_step(s_ref[n], vbt_ref[n, :, pl.ds(ks, t)], m_ref.at[n], acc_ref.at[n])

    _pipelined_tiles(n_far, produce, consume, s0_ref, s1_ref)

    for n in range(HKV_B):
        o = _normalised(acc_ref[n], HD_B)
        for g in range(G_B):
            h = n * G_B + g
            o_ref[:, h * HD_B:(h + 1) * HD_B] = o[:, g * t:(g + 1) * t].T.astype(o_ref.dtype)


def _dsa(qb, qi, wi_t, kb, vbt, ki2, bias, *, t, q0, topk, n_keys, n_valid_diag):
    b, nq, _ = qb.shape
    nk = kb.shape[1]
    return pl.pallas_call(
        functools.partial(_dsa_body, t, q0, topk, n_keys, n_valid_diag),
        grid=(b, nq // t),
        in_specs=[
            pl.BlockSpec((None, t, H_B * HD_B), lambda bb, i: (bb, i, 0)),
            pl.BlockSpec((None, t, H_IDX * D_IDX), lambda bb, i: (bb, i, 0)),
            pl.BlockSpec((None, H_IDX, t), lambda bb, i: (bb, 0, i)),
            _resident((None, nk, HKV_B * HD_B), lambda bb, i: (bb, 0, 0)),
            _resident((None, HKV_B, HD_B, nk), lambda bb, i: (bb, 0, 0, 0)),
            _resident((None, nk, LANES), lambda bb, i: (bb, 0, 0)),
            _resident((H_B, 2, t, t), lambda bb, i: (0, 0, 0, 0)),
        ],
        out_specs=pl.BlockSpec((None, t, H_B * HD_B), lambda bb, i: (bb, i, 0)),
        out_shape=jax.ShapeDtypeStruct((b, nq, H_B * HD_B), BF16),
        scratch_shapes=[
            pltpu.VMEM((nk + t, t), I32),
            pltpu.VMEM((H_IDX, LANES, t), BF16),
            pltpu.VMEM((HKV_B, HD_B, G_B * t), BF16),
            pltpu.VMEM((2, 8, t), F32),
            pltpu.VMEM((HKV_B, 1, G_B * t), F32),
            pltpu.VMEM((HKV_B, HD_B + ONES_ROWS, G_B * t), F32),
            pltpu.VMEM((HKV_B, t, G_B * t), F32),
            pltpu.VMEM((HKV_B, t, G_B * t), F32),
        ],
        compiler_params=_cparams(("parallel", "arbitrary")),
        name="dsa",
    )(qb, qi, wi_t, kb, vbt, ki2, bias)


def _softmax_pieces(s_list, v_list):
    m = functools.reduce(jnp.maximum, [jnp.max(s, axis=-1, keepdims=True) for s in s_list])
    p_list = [jnp.exp2(s - m) for s in s_list]
    o = sum(jnp.dot(p.astype(BF16), v, preferred_element_type=F32) for p, v in zip(p_list, v_list))
    return o / sum(jnp.sum(p, axis=-1, keepdims=True) for p in p_list)


def _head_rows(ref, h, n_heads, n_pos):
    return ref[pl.ds(h, n_pos, stride=n_heads), :].astype(BF16)


def _short_a_body(nq, n_past, out_scale, lam_ref, q_ref, kp_ref, vp_ref, kn_ref, vn_ref, bias_ref, g_ref, o_ref):
    lam = lam_ref[0]
    lane = lax.broadcasted_iota(I32, (nq, LANES), 1)
    pieces = ((kp_ref, vp_ref, 0, n_past), (kn_ref, vn_ref, n_past, nq))
    for h in range(H_A):
        cols = slice(h * LANES, (h + 1) * LANES)
        q = q_ref[:, cols]
        zero = jnp.zeros_like(q)
        qz = jnp.concatenate([jnp.where(lane < HD_A, q, zero), jnp.where(lane < HD_A, zero, q)], axis=0)
        b = bias_ref[h]
        b2 = jnp.concatenate([b, b], axis=0)
        s_list = [lax.dot_general(qz, _head_rows(k_ref, h, H_A, size), _NT, preferred_element_type=F32)
                  + b2[:, lo:lo + size] for k_ref, _, lo, size in pieces]
        o = _softmax_pieces(s_list, [_head_rows(v_ref, h, H_A, size) for _, v_ref, _, size in pieces])
        o = o[:nq] - lam * o[nq:]
        y = o * lax.rsqrt(jnp.mean(o * o, axis=-1, keepdims=True) + EPS)
        o_ref[:, cols] = ((y * g_ref[...]) * out_scale).astype(o_ref.dtype)


def _short_a(q, k_past, v_past, k_new, v_new, bias, subln_g, lam, *, out_scale):
    b, nq, d = q.shape
    n_past = k_past.shape[1]
    assert n_past % LANES == 0
    rows = lambda n: pl.BlockSpec((None, n * H_A, DV_A), lambda bb: (bb, 0, 0))
    merge = lambda a: a.reshape(b, a.shape[1] * H_A, DV_A)
    return pl.pallas_call(
        functools.partial(_short_a_body, nq, n_past, out_scale),
        grid=(b,),
        in_specs=[
            pl.BlockSpec(memory_space=pltpu.SMEM),
            pl.BlockSpec((None, nq, d), lambda bb: (bb, 0, 0)),
            rows(n_past), rows(n_past), rows(nq), rows(nq),
            pl.BlockSpec(bias.shape, lambda bb: (0, 0, 0)),
            pl.BlockSpec((1, DV_A), lambda bb: (0, 0)),
        ],
        out_specs=pl.BlockSpec((None, nq, d), lambda bb: (bb, 0, 0)),
        out_shape=jax.ShapeDtypeStruct((b, nq, d), BF16),
        compiler_params=_cparams(("parallel",)),
        name="diff_attn_short",
    )(lam, q, merge(k_past), merge(v_past), merge(k_new), merge(v_new), bias, subln_g)


def _short_dsa_body(nq, n_past, topk, qb_ref, qi_ref, wi_ref, kbp_ref, vbp_ref, kip_ref, kbn_ref, vbn_ref,
                    kin_ref, bias_ref, o_ref):
    lane = lax.broadcasted_iota(I32, (nq, LANES), 1)
    spans = ((0, n_past), (n_past, nq))

    qz = []
    for h in range(H_IDX):
        qs = qi_ref[:, (h // 2) * LANES:(h // 2 + 1) * LANES]
        zero = jnp.zeros_like(qs)
        qz.append(jnp.where(lane < D_IDX, qs, zero) if h % 2 == 0 else jnp.where(lane < D_IDX, zero, qs))
    qz = jnp.concatenate(qz, axis=0)
    keys, kpos = [], []
    for ki_ref, (lo, size) in zip((kip_ref, kin_ref), spans):
        ki = ki_ref[...].astype(BF16)
        s_idx = lax.dot_general(qz, jnp.concatenate([ki, ki], axis=-1), _NT, preferred_element_type=F32)
        score = jnp.zeros((nq, size), F32)
        for h in range(H_IDX):
            score = score + jnp.maximum(s_idx[h * nq:(h + 1) * nq], 0.0) * wi_ref[:, h:h + 1]
        qp = n_past + lax.broadcasted_iota(I32, (nq, size), 0)
        kp = lo + lax.broadcasted_iota(I32, (nq, size), 1)
        keys.append(jnp.where((kp // CHUNK) <= (qp // CHUNK), _order_key(score), INT_MIN))
        kpos.append(kp)

    def count(hit_fn):
        return sum(jnp.sum(jnp.where(hit_fn(k, kp), 1.0, 0.0), axis=-1, keepdims=True)
                   for k, kp in zip(keys, kpos))

    def search_body(state):
        it, lo, hi, c_lo, c_hi, _ = state
        mid = (lo >> 1) + (hi >> 1) + (lo & hi & 1)
        c = count(lambda k, kp: k >= mid)
        live = lo + 1 < hi
        ge = c >= topk
        up = jnp.logical_and(live, ge)
        down = jnp.logical_and(live, jnp.logical_not(ge))
        lo = jnp.where(up, mid, lo)
        c_lo = jnp.where(up, c, c_lo)
        hi = jnp.where(jnp.logical_and(live, c == topk), mid + 1, jnp.where(down, mid, hi))
        c_hi = jnp.where(down, c, c_hi)
        return it + 1, lo, hi, c_lo, c_hi, jnp.max(jnp.where(lo + 1 < hi, 1, 0))

    n_keys = n_past + nq
    n_adm = count(lambda k, kp: k > INT_MIN)
    lo0 = jnp.full((nq, 1), INT_MIN + 1, I32)
    hi0 = jnp.where(n_adm <= topk, lo0 + 1, INT_MAX)
    _, thr, _, c_thr, c_above, _ = lax.while_loop(
        lambda st: jnp.logical_and(st[0] < 34, st[-1] > 0), search_body,
        (jnp.int32(0), lo0, hi0, n_adm, jnp.zeros((nq, 1), F32), jnp.int32(1)))

    need = topk - c_above

    def cut_body(_, st):
        p_lo, p_hi = st
        p_mid = (p_lo + p_hi) >> 1
        ok = count(lambda k, kp: jnp.logical_and(k == thr, kp <= p_mid)) >= need
        return jnp.where(ok, p_lo, p_mid), jnp.where(ok, p_mid, p_hi)

    _, cut = lax.fori_loop(0, n_keys.bit_length(), cut_body,
                           (jnp.full((nq, 1), -1, I32), jnp.full((nq, 1), n_keys - 1, I32)))
    off = [jnp.where(jnp.logical_and(k >= thr, jnp.logical_not(jnp.logical_and(k == thr, kp > cut))), 0.0, NEG)
           for k, kp in zip(keys, kpos)]

    for n in range(HKV_B):
        heads = range(n * G_B, (n + 1) * G_B)
        qs = jnp.concatenate([qb_ref[:, h * HD_B:(h + 1) * HD_B] for h in heads], axis=0)
        s_list = []
        for kb_ref, off_i, (lo, size) in zip((kbp_ref, kbn_ref), off, spans):
            s = lax.dot_general(qs, _head_rows(kb_ref, n, HKV_B, size), _NT, preferred_element_type=F32)
            s_list.append(s + jnp.concatenate([bias_ref[h][:, lo:lo + size] + off_i for h in heads], axis=0))
        o = _softmax_pieces(s_list, [_head_rows(vbp_ref, n, HKV_B, n_past), _head_rows(vbn_ref, n, HKV_B, nq)])
        for g, h in enumerate(heads):
            o_ref[:, h * HD_B:(h + 1) * HD_B] = o[g * nq:(g + 1) * nq].astype(o_ref.dtype)


def _short_dsa(qb, qi, wi, kb_past, vb_past, ki_past, kb_new, vb_new, ki_new, bias, *, topk):
    b, nq, d = qb.shape
    n_past = kb_past.shape[1]
    assert n_past % LANES == 0
    per_batch = lambda *shape: pl.BlockSpec((None,) + shape, lambda bb: (bb,) + (0,) * len(shape))
    merge = lambda a: a.reshape(b, a.shape[1] * HKV_B, HD_B)
    return pl.pallas_call(
        functools.partial(_short_dsa_body, nq, n_past, topk),
        grid=(b,),
        in_specs=[per_batch(nq, d), per_batch(nq, d), per_batch(nq, H_IDX),
                  per_batch(n_past * HKV_B, HD_B), per_batch(n_past * HKV_B, HD_B), per_batch(n_past, D_IDX),
                  per_batch(nq * HKV_B, HD_B), per_batch(nq * HKV_B, HD_B), per_batch(nq, D_IDX),
                  pl.BlockSpec(bias.shape, lambda bb: (0, 0, 0))],
        out_specs=per_batch(nq, d),
        out_shape=jax.ShapeDtypeStruct((b, nq, d), BF16),
        compiler_params=_cparams(("parallel",)),
        name="dsa_short",
    )(qb, qi, wi, merge(kb_past), merge(vb_past), ki_past, merge(kb_new), merge(vb_new), ki_new, bias)


def _sigmoid(x):
    return 1.0 / (1.0 + jnp.exp(-x))


def _merge_body(h_ref, oa_ref, ob_ref, wg_ref, woa_ref, wob_ref, o_ref):
    h = h_ref[...]
    ga = jnp.dot(h, wg_ref[:, :D_MODEL], preferred_element_type=F32)
    ya = jnp.dot(oa_ref[...], woa_ref[...], preferred_element_type=F32)
    merged = _sigmoid(ga) * ya
    gb = jnp.dot(h, wg_ref[:, D_MODEL:], preferred_element_type=F32)
    yb = jnp.dot(ob_ref[...], wob_ref[...], preferred_element_type=F32)
    o_ref[...] = (merged + _sigmoid(gb) * yb).astype(o_ref.dtype)


def _merge(h, oa, ob, wg, woa, wob, tm):
    n = h.shape[0]
    return pl.pallas_call(
        _merge_body,
        grid=(n // tm,),
        in_specs=[
            pl.BlockSpec((tm, D_MODEL), lambda i: (i, 0)),
            pl.BlockSpec((tm, H_A * DV_A), lambda i: (i, 0)),
            pl.BlockSpec((tm, H_B * HD_B), lambda i: (i, 0)),
            _resident((D_MODEL, 2 * D_MODEL), lambda i: (0, 0)),
            _resident((H_A * DV_A, D_MODEL), lambda i: (0, 0)),
            _resident((H_B * HD_B, D_MODEL), lambda i: (0, 0)),
        ],
        out_specs=pl.BlockSpec((tm, D_MODEL), lambda i: (i, 0)),
        out_shape=jax.ShapeDtypeStruct((n, D_MODEL), BF16),
        compiler_params=_cparams(("parallel",)),
        name="gated_merge",
    )(h, oa, ob, wg, woa, wob)


def _outproj_body(x_ref, mg_ref, w_ref, g_ref, x1_ref, h2_ref):
    x1 = x_ref[...] + jnp.dot(mg_ref[...], w_ref[...], preferred_element_type=F32)
    x1_ref[...] = x1
    y = x1 * lax.rsqrt(jnp.mean(x1 * x1, axis=-1, keepdims=True) + EPS)
    h2_ref[...] = (y * g_ref[...]).astype(h2_ref.dtype)


def _outproj(x, merged, w_out, g2, tm):
    n = x.shape[0]
    return pl.pallas_call(
        _outproj_body,
        grid=(n // tm,),
        in_specs=[
            pl.BlockSpec((tm, D_MODEL), lambda i: (i, 0)),
            pl.BlockSpec((tm, D_MODEL), lambda i: (i, 0)),
            _resident((D_MODEL, D_MODEL), lambda i: (0, 0)),
            pl.BlockSpec((1, D_MODEL), lambda i: (0, 0)),
        ],
        out_specs=[pl.BlockSpec((tm, D_MODEL), lambda i: (i, 0)), pl.BlockSpec((tm, D_MODEL), lambda i: (i, 0))],
        out_shape=[jax.ShapeDtypeStruct((n, D_MODEL), F32), jax.ShapeDtypeStruct((n, D_MODEL), BF16)],
        compiler_params=_cparams(("parallel",)),
        name="out_proj",
    )(x, merged, w_out, g2.reshape(1, D_MODEL))


def _ffn_body(x1_ref, h2_ref, w1_ref, w2_ref, o_ref):
    f = pl.program_id(1)

    @pl.when(f == 0)
    def _():
        o_ref[...] = x1_ref[...]

    u = jnp.maximum(jnp.dot(h2_ref[...], w1_ref[...], preferred_element_type=F32), 0.0)
    o_ref[...] += jnp.dot((u * u).astype(BF16), w2_ref[...], preferred_element_type=F32)


def _ffn(x1, h2, w1, w2, tm, tf):
    n = x1.shape[0]
    return pl.pallas_call(
        _ffn_body,
        grid=(n // tm, D_FF // tf),
        in_specs=[
            pl.BlockSpec((tm, D_MODEL), lambda i, f: (i, 0)),
            pl.BlockSpec((tm, D_MODEL), lambda i, f: (i, 0)),
            pl.BlockSpec((D_MODEL, tf), lambda i, f: (0, f)),
            pl.BlockSpec((tf, D_MODEL), lambda i, f: (f, 0)),
        ],
        out_specs=pl.BlockSpec((tm, D_MODEL), lambda i, f: (i, 0)),
        out_shape=jax.ShapeDtypeStruct((n, D_MODEL), F32),
        compiler_params=_cparams(("parallel", "arbitrary")),
        name="ffn",
    )(x1, h2, w1, w2)


def _prep_weights(w_in, w_o_a, w_o_b, w_out, w_ff1, w_ff2):
    sizes = (H_A * 2 * HD_A, H_A * 2 * HD_A, H_A * DV_A, H_B * HD_B, HKV_B * HD_B, HKV_B * HD_B,
             H_IDX * D_IDX, D_IDX, H_IDX, D_MODEL, D_MODEL)
    offs = [0]
    for s in sizes:
        offs.append(offs[-1] + s)
    col = lambda a, b: w_in[:, offs[a]:offs[b]]
    w_misc = jnp.concatenate(
        [col(4, 6), col(7, 8), col(7, 8), col(8, 9),
         jnp.zeros((D_MODEL, _MISC_COLS - _WI0 - H_IDX), w_in.dtype)], axis=1)
    return dict(
        qa=col(0, 1).astype(BF16), ka=col(1, 2).astype(BF16), va=col(2, 3).astype(BF16),
        qb=col(3, 4).astype(BF16), misc=w_misc.astype(BF16), qi=col(6, 7).astype(BF16),
        gate=col(9, 11).astype(BF16), oa=w_o_a.astype(BF16), ob=w_o_b.astype(BF16),
        out=w_out.astype(BF16), ff1=w_ff1.astype(BF16), ff2=w_ff2.astype(BF16))


def _layer(x, past, w, p, lam, lam_init, tab_t, *, tm, t_a=None, t_b=None):
    short = t_a is None
    b, t, _ = x.shape
    n = b * t
    xf = x.reshape(n, D_MODEL)
    h = _rmsnorm(xf, p["norm1_g"], tm)

    gain_a = lambda g: g.reshape(1, 2 * HD_A)
    v_layout = HEADS_T if (not short and past is None and b == 1) else FLAT
    (qa,) = _proj(h, w["qa"], [gain_a(p["qn_a_g"])], _epi_qa, [(1024, BF16, FLAT)], tm, "proj_qa")
    ka, ka_h = _proj(h, w["ka"], [gain_a(p["kn_a_g"])], _epi_ka, [(1024, F32, HEADS), (1024, BF16, FLAT)],
                     tm, "proj_ka")
    va, va_h = _proj(h, w["va"], [], _epi_copy2, [(1024, F32, HEADS), (1024, BF16, v_layout)], tm, "proj_va")
    (qb,) = _proj(h, w["qb"], [p["qn_b_g"].reshape(1, HD_B)], _epi_qb, [(1024, BF16, FLAT)], tm, "proj_qb")
    kb, kb_h, vb, vb_h, ki, ki2_h, wi = _proj(
        h, w["misc"], [p["kn_b_g"].reshape(1, HD_B)], _epi_misc,
        [(256, F32, HEADS), (256, BF16, FLAT), (256, F32, HEADS), (256, BF16, v_layout), (D_IDX, F32, FLAT),
         (2 * D_IDX, BF16, FLAT), (H_IDX, F32, FLAT)],
        tm, "proj_misc")
    (qi,) = _proj(h, w["qi"], [], _epi_copy1, [(1024, BF16, FLAT)], tm, "proj_qi")

    new_rows = (ka.reshape(b, t, H_A, 2 * HD_A), va.reshape(b, t, H_A, DV_A), kb.reshape(b, t, HKV_B, HD_B),
                vb.reshape(b, t, HKV_B, HD_B), ki.reshape(b, t, D_IDX))

    per_b = lambda a: a.reshape(b, t, -1)
    if short:
        n_keys = past[0].shape[1] + t
        topk = min(TOPK_MAX, n_keys // 4)
        oa = _short_a(per_b(qa), past[0], past[1], new_rows[0], new_rows[1],
                      _bias_dense(tab_t[:H_A], t, n_keys, n_keys - t, n_keys),
                      p["subln_a_g"].reshape(1, DV_A), lam, out_scale=1.0 - lam_init)
        ob = _short_dsa(per_b(qb), per_b(qi), per_b(wi), past[2], past[3], past[4], *new_rows[2:],
                        _bias_dense(tab_t[H_A:], t, n_keys, n_keys - t, n_keys), topk=topk)
    else:
        oa, ob = _tiled_mixers(qa, qb, qi, wi, ka_h, va_h, kb_h, vb_h, ki2_h, past, tab_t, p, lam, lam_init,
                               b=b, t=t, t_a=t_a, t_b=t_b)
    oa = oa.reshape(n, H_A * DV_A)
    ob = ob.reshape(n, H_B * HD_B)

    merged = _merge(h, oa, ob, w["gate"], w["oa"], w["ob"], min(tm, 256))
    x1, h2 = _outproj(xf, merged, w["out"], p["norm2_g"], min(tm, 256))
    y = _ffn(x1, h2, w["ff1"], w["ff2"], tm, 1024)
    return y.reshape(b, t, D_MODEL), new_rows


def _tiled_mixers(qa, qb, qi, wi, ka_h, va_h, kb_h, vb_h, ki2_h, past, tab_t, p, lam, lam_init,
                  *, b, t, t_a, t_b):
    def full_keys(new_h, past_arr):
        if new_h.ndim == 3:
            return new_h[None]
        new_h = new_h.reshape(b, t, -1)
        if past_arr is None:
            return new_h
        past_h = past_arr.reshape(b, past_arr.shape[1], -1).astype(BF16)
        return jnp.concatenate([past_h, new_h], axis=1)

    if past is None:
        pa_k = pa_v = pb_k = pb_v = pb_i = None
    else:
        pa_k, pa_v, pb_k, pb_v, pb_i = past
        pb_i = jnp.concatenate([pb_i, pb_i], axis=-1)
    ka_f, va_f = full_keys(ka_h, pa_k), full_keys(va_h, pa_v)
    kb_f, vb_f, ki_f = full_keys(kb_h, pb_k), full_keys(vb_h, pb_v), full_keys(ki2_h, pb_i)
    n_keys = ka_f.shape[1]
    q0 = n_keys - t
    topk = min(TOPK_MAX, n_keys // 4)
    t_pad = max(t_a, t_b)
    assert t_pad % t_a == 0 and t_pad % t_b == 0 and q0 % t_pad == 0
    nk_pad = -(-n_keys // t_pad) * t_pad
    pad_k = lambda a: jnp.pad(a, ((0, 0), (0, nk_pad - n_keys), (0, 0)))
    pad_q = lambda a, tt: jnp.pad(a.reshape(b, t, -1), ((0, 0), (0, -(-t // tt) * tt - t), (0, 0)))
    ka_f, kb_f, ki_f = (pad_k(a) for a in (ka_f, kb_f, ki_f))
    n_valid = lambda tt: n_keys - (n_keys - 1) // tt * tt
    assert all(t <= tt or n_valid(tt) == tt for tt in (t_a, t_b))

    def heads_t(v, nh):
        if v.ndim == 3:
            v = jnp.transpose(v.reshape(b, n_keys, nh, -1), (0, 2, 3, 1))
        return jnp.pad(v, ((0, 0), (0, 0), (0, 0), (0, nk_pad - n_keys)))

    bias_a = _bias_tiles(tab_t[:H_A], t_a, n_valid(t_a))
    oa = _attn_a(pad_q(qa, t_a), ka_f, heads_t(va_f, H_A), bias_a, p["subln_a_g"].reshape(DV_A, 1), lam,
                 t=t_a, q0=q0, out_scale=1.0 - lam_init)
    wi_t = jnp.swapaxes(pad_q(wi, t_b), 1, 2)
    bias_b = _bias_tiles(tab_t[H_A:], t_b, n_valid(t_b))
    ob = _dsa(pad_q(qb, t_b), pad_q(qi, t_b), wi_t, kb_f, heads_t(vb_f, HKV_B), ki_f, bias_b,
              t=t_b, q0=q0, topk=topk, n_keys=n_keys, n_valid_diag=n_valid(t_b))
    return oa[:, :t], ob[:, :t]


def kernel(x_prompt, x_sample, cache_a_k, cache_a_v, cache_b_k, cache_b_v, cache_b_kidx, rel_bias, norm1_g, w_in, qn_a_g, kn_a_g, lam_q1, lam_k1, lam_q2, lam_k2, subln_a_g, qn_b_g, kn_b_g, w_o_a, w_o_b, w_out, norm2_g, w_ff1, w_ff2):
    depth = w_in.shape[0]
    tab_t = rel_bias.T.astype(F32)
    y_prompt, y_sample = x_prompt, x_sample
    prompt_rows, sample_rows = [], []
    for l in range(depth):
        lam_init = 0.8 - 0.6 * math.exp(-0.3 * l)
        lam = (jnp.exp(jnp.sum(lam_q1[l].astype(F32) * lam_k1[l].astype(F32)))
               - jnp.exp(jnp.sum(lam_q2[l].astype(F32) * lam_k2[l].astype(F32))) + lam_init).reshape(1)
        w = _prep_weights(w_in[l], w_o_a[l], w_o_b[l], w_out[l], w_ff1[l], w_ff2[l])
        p = dict(norm1_g=norm1_g[l], qn_a_g=qn_a_g[l], kn_a_g=kn_a_g[l], subln_a_g=subln_a_g[l],
                 qn_b_g=qn_b_g[l], kn_b_g=kn_b_g[l], norm2_g=norm2_g[l])
        y_prompt, rp = _layer(y_prompt, None, w, p, lam, lam_init, tab_t, t_a=512, t_b=256, tm=512)
        past = (cache_a_k[l], cache_a_v[l], cache_b_k[l], cache_b_v[l], cache_b_kidx[l])
        y_sample, rs = _layer(y_sample, past, w, p, lam, lam_init, tab_t, tm=512)
        prompt_rows.append(rp)
        sample_rows.append(rs)
    p_rows = tuple(jnp.stack(r, axis=0) for r in zip(*prompt_rows))
    s_rows = tuple(jnp.stack(r, axis=0) for r in zip(*sample_rows))
    return (y_prompt, y_sample) + p_rows + s_rows
```

```python
import functools
import math

import jax
import jax.numpy as jnp
from jax import lax
from jax.experimental import pallas as pl
from jax.experimental.pallas import tpu as pltpu

F32 = jnp.float32
BF16 = jnp.bfloat16
I32 = jnp.int32

D_MODEL = 2048
CHUNK = 64
H_A = 8
DV_A = 128
HD_A = 64
H_B = 8
HD_B = 128
HKV_B = 2
G_B = H_B // HKV_B
H_IDX = 16
D_IDX = 64
TOPK_MAX = 256
N_BUCKETS = 32
D_FF = 4 * D_MODEL
EPS = 1e-6
LANES = 128
NEG = -1e30
LOG2E = math.log2(math.e)
INT_MIN = -(2 ** 31)
INT_MAX = 2 ** 31 - 1
VMEM_LIMIT = 56 * 1024 * 1024

_NT = (((1,), (1,)), ((), ()))


def _cparams(sem):
    return pltpu.CompilerParams(dimension_semantics=sem, vmem_limit_bytes=VMEM_LIMIT)


def _resident(block_shape, index_map):
    return pl.BlockSpec(block_shape, index_map, pipeline_mode=pl.Buffered(1))


def _rmsnorm_body(x_ref, g_ref, o_ref):
    x = x_ref[...]
    y = x * lax.rsqrt(jnp.mean(x * x, axis=-1, keepdims=True) + EPS)
    o_ref[...] = (y * g_ref[...]).astype(o_ref.dtype)


def _rmsnorm(x, g, tm):
    n, d = x.shape
    return pl.pallas_call(
        _rmsnorm_body,
        grid=(n // tm,),
        in_specs=[pl.BlockSpec((tm, d), lambda i: (i, 0)), pl.BlockSpec((1, d), lambda i: (0, 0))],
        out_specs=pl.BlockSpec((tm, d), lambda i: (i, 0)),
        out_shape=jax.ShapeDtypeStruct((n, d), BF16),
        compiler_params=_cparams(("parallel",)),
        name="rmsnorm",
    )(x, g.reshape(1, d))


def _head_norm(z, gain, split):
    lane = lax.broadcasted_iota(I32, (1, LANES), 1)
    lo = lane < (LANES // 2)
    outs = []
    for h in range(z.shape[1] // LANES):
        zh = z[:, h * LANES:(h + 1) * LANES]
        sq = zh * zh
        if split:
            s_lo = jnp.sum(jnp.where(lo, sq, 0.0), axis=-1, keepdims=True)
            s_hi = jnp.sum(jnp.where(lo, 0.0, sq), axis=-1, keepdims=True)
            r = jnp.where(lo, lax.rsqrt(s_lo * (2.0 / LANES) + EPS), lax.rsqrt(s_hi * (2.0 / LANES) + EPS))
        else:
            r = lax.rsqrt(jnp.mean(sq, axis=-1, keepdims=True) + EPS)
        outs.append(zh * r * gain)
    return jnp.concatenate(outs, axis=-1)


FLAT, HEADS, HEADS_T = "tokens x cols", "tokens x heads x 128", "heads x 128 x tokens"


def _proj_body(epilogue, n_aux, layouts, h_ref, w_ref, *rest):
    z = jnp.dot(h_ref[...], w_ref[...], preferred_element_type=F32)
    outs = epilogue(z, *[r[...] for r in rest[:n_aux]])
    for o_ref, o, layout in zip(rest[n_aux:], outs, layouts):
        if layout == FLAT:
            o_ref[...] = o.astype(o_ref.dtype)
            continue
        for hd in range(o.shape[1] // LANES):
            slab = o[:, hd * LANES:(hd + 1) * LANES]
            if layout == HEADS:
                o_ref[:, hd, :] = slab.astype(o_ref.dtype)
            else:
                o_ref[hd] = slab.T.astype(o_ref.dtype)


def _proj(h, w, aux, epilogue, out_defs, tm, name):
    n, k = h.shape
    c = w.shape[1]
    in_specs = [pl.BlockSpec((tm, k), lambda i: (i, 0)), _resident((k, c), lambda i: (0, 0))]
    in_specs += [pl.BlockSpec(a.shape, lambda i: (0, 0)) for a in aux]
    out_specs, out_shape = [], []
    for oc, dt, layout in out_defs:
        nh = oc // LANES
        if layout == FLAT:
            out_specs.append(pl.BlockSpec((tm, oc), lambda i: (i, 0)))
            out_shape.append(jax.ShapeDtypeStruct((n, oc), dt))
        elif layout == HEADS:
            out_specs.append(pl.BlockSpec((tm, nh, LANES), lambda i: (i, 0, 0)))
            out_shape.append(jax.ShapeDtypeStruct((n, nh, LANES), dt))
        else:
            out_specs.append(pl.BlockSpec((nh, LANES, tm), lambda i: (0, 0, i)))
            out_shape.append(jax.ShapeDtypeStruct((nh, LANES, n), dt))
    return pl.pallas_call(
        functools.partial(_proj_body, epilogue, len(aux), tuple(d[2] for d in out_defs)),
        grid=(n // tm,),
        in_specs=in_specs,
        out_specs=out_specs,
        out_shape=out_shape,
        compiler_params=_cparams(("parallel",)),
        name=name,
    )(h, w, *aux)


def _epi_qa(z, gain):
    return (_head_norm(z, gain, True) * (HD_A ** -0.5 * LOG2E),)


def _epi_ka(z, gain):
    y = _head_norm(z, gain, True)
    return y, y


def _epi_copy2(z):
    return z, z


def _epi_qb(z, gain):
    return (_head_norm(z, gain, False) * (HD_B ** -0.5 * LOG2E),)


def _epi_copy1(z):
    return (z,)


_KB0, _VB0, _KI0, _WI0, _MISC_COLS = 0, 256, 512, 640, 768


def _epi_misc(z, gain):
    kb = _head_norm(z[:, _KB0:_VB0], gain, False)
    vb = z[:, _VB0:_KI0]
    ki2 = z[:, _KI0:_WI0]
    return kb, kb, vb, vb, ki2[:, :D_IDX], ki2, z[:, _WI0:_WI0 + H_IDX]


def _rel_bias(tab_ref, h, rel):
    half = N_BUCKETS // 2
    exact = half // 2
    n = jnp.abs(rel)
    n2 = n * n
    v_neg = jnp.full(rel.shape, tab_ref[h, 0], F32)
    v_pos = jnp.full(rel.shape, tab_ref[h, half], F32)
    for b in range(1, half):
        cond = (n >= b) if b < exact else (n2 >= exact * exact * 2 ** (b - exact))
        v_neg = jnp.where(cond, tab_ref[h, b], v_neg)
        v_pos = jnp.where(cond, tab_ref[h, half + b], v_pos)
    return (jnp.where(rel > 0, v_pos, v_neg) - tab_ref[h, half - 1]) * LOG2E


def _bias_body(t, n_valid_diag, tab_ref, o_ref):
    h = pl.program_id(0)
    kl = lax.broadcasted_iota(I32, (t, t), 0)
    ql = lax.broadcasted_iota(I32, (t, t), 1)
    o_ref[0] = _rel_bias(tab_ref, h, kl - t - ql)
    visible = ((kl // CHUNK) <= (ql // CHUNK)) & (kl < n_valid_diag)
    o_ref[1] = jnp.where(visible, _rel_bias(tab_ref, h, kl - ql), NEG)


def _bias_dense_body(q0, n_keys, tab_ref, o_ref):
    h = pl.program_id(0)
    qpos = q0 + lax.broadcasted_iota(I32, o_ref.shape, 0)
    kpos = lax.broadcasted_iota(I32, o_ref.shape, 1)
    visible = ((kpos // CHUNK) <= (qpos // CHUNK)) & (kpos < n_keys)
    o_ref[...] = jnp.where(visible, _rel_bias(tab_ref, h, kpos - qpos), NEG)


def _bias_dense(tab_t, nq, nk_pad, q0, n_keys):
    nh = tab_t.shape[0]
    return pl.pallas_call(
        functools.partial(_bias_dense_body, q0, n_keys),
        grid=(nh,),
        in_specs=[pl.BlockSpec(memory_space=pltpu.SMEM)],
        out_specs=pl.BlockSpec((None, nq, nk_pad), lambda h: (h, 0, 0)),
        out_shape=jax.ShapeDtypeStruct((nh, nq, nk_pad), F32),
        compiler_params=_cparams(("arbitrary",)),
        name="bias_dense",
    )(tab_t)


def _bias_tiles(tab_t, t, n_valid_diag):
    nh = tab_t.shape[0]
    return pl.pallas_call(
        functools.partial(_bias_body, t, n_valid_diag),
        grid=(nh,),
        in_specs=[pl.BlockSpec(memory_space=pltpu.SMEM)],
        out_specs=pl.BlockSpec((None, 2, t, t), lambda h: (h, 0, 0, 0)),
        out_shape=jax.ShapeDtypeStruct((nh, 2, t, t), F32),
        compiler_params=_cparams(("arbitrary",)),
        name="bias_tiles",
    )(tab_t)


ONES_ROWS = 16


def _softmax_step(s, vt, m_ref, acc_ref):
    m_old = m_ref[...]
    m_new = jnp.maximum(m_old, jnp.max(s, axis=0, keepdims=True))
    alpha = jnp.exp2(m_old - m_new)
    p = jnp.exp2(s - m_new)
    vt_ones = jnp.concatenate([vt, jnp.ones((ONES_ROWS, vt.shape[1]), vt.dtype)], axis=0)
    acc_ref[...] = acc_ref[...] * alpha + jnp.dot(vt_ones, p.astype(BF16), preferred_element_type=F32)
    m_ref[...] = m_new


def _normalised(acc, dv):
    return acc[:dv] / acc[dv:dv + 1]


def _pipelined_tiles(n_far, produce, consume, buf0, buf1):
    @pl.when(n_far == -1)
    def _():
        produce(0, buf0, 1)
        consume(0, buf0)

    @pl.when(n_far == 0)
    def _():
        produce(0, buf0, 0)
        produce(1, buf1, 1)
        consume(0, buf0)
        consume(1, buf1)

    @pl.when(n_far >= 1)
    def _():
        produce(0, buf0, None)

    pairs = jnp.maximum(n_far - 1, 0) // 2

    def pair(j):
        produce(j + 1, buf1, None)
        consume(j, buf0)
        produce(j + 2, buf0, None)
        consume(j + 1, buf1)

    def body(i, carry):
        for u in range(4):
            pair(8 * i + 2 * u)
        return carry

    def leftover(q, carry):
        pair(2 * q)
        return carry

    lax.fori_loop(0, pairs // 4, body, 0)
    lax.fori_loop(pairs // 4 * 4, pairs, leftover, 0)
    d = 2 * pairs
    rem = n_far - d

    @pl.when(rem == 1)
    def _():
        produce(d + 1, buf1, 0)
        consume(d, buf0)
        produce(d + 2, buf0, 1)
        consume(d + 1, buf1)
        consume(d + 2, buf0)

    @pl.when(rem == 2)
    def _():
        produce(d + 1, buf1, None)
        consume(d, buf0)
        produce(d + 2, buf0, 0)
        consume(d + 1, buf1)
        produce(d + 3, buf1, 1)
        consume(d + 2, buf0)
        consume(d + 3, buf1)


def _attn_a_body(t, q0, out_scale, lam_ref, q_ref, k_ref, vt_ref, bias_ref, g_ref, o_ref,
                 qz_ref, m_ref, acc_ref, s0_ref, s1_ref):
    i = pl.program_id(2)
    n_far = q0 // t + i - 1
    q = q_ref[...]
    lane = lax.broadcasted_iota(I32, (t, LANES), 1)
    zero = jnp.zeros_like(q)
    qz_ref[:, :t] = jnp.where(lane < HD_A, q, zero).T
    qz_ref[:, t:] = jnp.where(lane < HD_A, zero, q).T
    m_ref[...] = jnp.full(m_ref.shape, NEG, F32)
    acc_ref[...] = jnp.zeros(acc_ref.shape, F32)

    def produce(j, s_ref, kind):
        ks = pl.multiple_of(j * t, t)
        s = jnp.dot(k_ref[pl.ds(ks, t), :], qz_ref[...], preferred_element_type=F32)
        if kind is not None:
            b = bias_ref[kind]
            s = jnp.concatenate([s[:, :t] + b, s[:, t:] + b], axis=1)
        s_ref[...] = s

    def consume(j, s_ref):
        _softmax_step(s_ref[...], vt_ref[:, pl.ds(pl.multiple_of(j * t, t), t)], m_ref, acc_ref)

    _pipelined_tiles(n_far, produce, consume, s0_ref, s1_ref)

    lam = lam_ref[0]
    o = _normalised(acc_ref[...], DV_A)
    o = o[:, :t] - lam * o[:, t:]
    y = o * lax.rsqrt(jnp.mean(o * o, axis=0, keepdims=True) + EPS)
    y = (y * g_ref[...]) * out_scale
    o_ref[...] = y.T.astype(o_ref.dtype)


def _attn_a(q, k, vt, bias, subln_g, lam, *, t, q0, out_scale):
    b, nq, _ = q.shape
    nk = k.shape[1]
    return pl.pallas_call(
        functools.partial(_attn_a_body, t, q0, out_scale),
        grid=(b, H_A, nq // t),
        in_specs=[
            pl.BlockSpec(memory_space=pltpu.SMEM),
            pl.BlockSpec((None, t, LANES), lambda bb, h, i: (bb, i, h)),
            pl.BlockSpec((None, nk, LANES), lambda bb, h, i: (bb, 0, h)),
            pl.BlockSpec((None, None, DV_A, nk), lambda bb, h, i: (bb, h, 0, 0)),
            pl.BlockSpec((None, 2, t, t), lambda bb, h, i: (h, 0, 0, 0)),
            pl.BlockSpec((DV_A, 1), lambda bb, h, i: (0, 0)),
        ],
        out_specs=pl.BlockSpec((None, t, LANES), lambda bb, h, i: (bb, i, h)),
        out_shape=jax.ShapeDtypeStruct((b, nq, H_A * DV_A), BF16),
        scratch_shapes=[
            pltpu.VMEM((LANES, 2 * t), BF16),
            pltpu.VMEM((1, 2 * t), F32),
            pltpu.VMEM((DV_A + ONES_ROWS, 2 * t), F32),
            pltpu.VMEM((t, 2 * t), F32),
            pltpu.VMEM((t, 2 * t), F32),
        ],
        compiler_params=_cparams(("parallel", "parallel", "arbitrary")),
        name="diff_attn",
    )(lam, q, k, vt, bias, subln_g)


GROUP = 4


def _order_key(x):
    b = pltpu.bitcast(x, I32)
    return jnp.where(b < 0, b ^ INT_MAX, b)


def _order_unkey(k):
    return pltpu.bitcast(jnp.where(k < 0, k ^ INT_MAX, k), F32)


def _dsa_body(t, q0, topk, n_keys, n_valid_diag, qb_ref, qi_ref, wi_ref, kb_ref, vbt_ref, ki_ref, bias_ref, o_ref,
              keys_ref, qz_ref, qs_ref, mm_ref, m_ref, acc_ref, s0_ref, s1_ref):
    i = pl.program_id(1)
    n_tiles = q0 // t + i + 1
    n_far = n_tiles - 2

    lane = lax.broadcasted_iota(I32, (t, LANES), 1)
    for hp in range(H_IDX // 2):
        qs = qi_ref[:, hp * LANES:(hp + 1) * LANES]
        zero = jnp.zeros_like(qs)
        qz_ref[2 * hp] = jnp.where(lane < D_IDX, qs, zero).T
        qz_ref[2 * hp + 1] = jnp.where(lane < D_IDX, zero, qs).T

    def score_tile(j):
        kt = ki_ref[pl.ds(pl.multiple_of(j * t, t), t), :]
        acc = jnp.zeros((t, t), F32)
        for h in range(H_IDX):
            s = jnp.dot(kt, qz_ref[h], preferred_element_type=F32)
            acc = acc + jnp.maximum(s, 0.0) * wi_ref[h:h + 1, :]
        return acc

    def fold8(x, op):
        return op(x.reshape(x.shape[0] // 8, 8, t), axis=0)

    mm_ref[0] = jnp.full((8, t), jnp.inf, F32)
    mm_ref[1] = jnp.full((8, t), -jnp.inf, F32)

    def score_store(j):
        sc = score_tile(j)
        keys_ref[pl.ds(pl.multiple_of(j * t, t), t), :] = _order_key(sc)
        mm_ref[0] = jnp.minimum(mm_ref[0], fold8(sc, jnp.min))
        mm_ref[1] = jnp.maximum(mm_ref[1], fold8(sc, jnp.max))

    def score_group(jj, carry):
        for u in range(GROUP):
            score_store(GROUP * jj + u)
        return carry

    def score_single(j, carry):
        score_store(j)
        return carry

    jd = n_tiles - 1
    lax.fori_loop(0, jd // GROUP, score_group, 0)
    lax.fori_loop(jd // GROUP * GROUP, jd, score_single, 0)

    kl = lax.broadcasted_iota(I32, (t, t), 0)
    ql = lax.broadcasted_iota(I32, (t, t), 1)
    admissible = ((kl // CHUNK) <= (ql // CHUNK)) & (kl < n_valid_diag)
    sc = score_tile(jd)
    keys_ref[pl.ds(pl.multiple_of(jd * t, t), t), :] = jnp.where(admissible, _order_key(sc), INT_MIN)
    smin = jnp.minimum(mm_ref[0], fold8(jnp.where(admissible, sc, jnp.inf), jnp.min))
    smax = jnp.maximum(mm_ref[1], fold8(jnp.where(admissible, sc, -jnp.inf), jnp.max))
    for u in range(GROUP - 1):
        keys_ref[pl.ds(pl.multiple_of((n_tiles + u) * t, t), t), :] = jnp.full((t, t), INT_MIN, I32)
    n_groups = (n_tiles + GROUP - 1) // GROUP
    gt = GROUP * t

    def count_groups(hit_fn):
        def body(j, c):
            r0 = pl.multiple_of(j * gt, gt)
            hit = hit_fn(keys_ref[pl.ds(r0, gt), :], r0)
            return c + jnp.sum(hit.reshape(gt // 32, 32, t), axis=0)
        part = lax.fori_loop(0, n_groups, body, jnp.zeros((32, t), F32))
        return jnp.sum(part, axis=0, keepdims=True)

    def search_cond(state):
        return jnp.logical_and(state[0] < 4 * 34, state[-1] > 0)

    def search_body(state):
        it, lo, hi, c_lo, c_hi, _ = state
        f_lo, f_hi = _order_unkey(lo), _order_unkey(hi)
        a, b = jnp.log(c_lo + 0.5), jnp.log(c_hi + 0.5)
        frac = jnp.clip((a - math.log(topk)) / jnp.maximum(a - b, 1e-9), 0.0, 1.0)
        guess = jnp.minimum(jnp.maximum(_order_key(f_lo + (f_hi - f_lo) * frac), lo + 1), hi - 1)
        mid = jnp.where(it % 4 == 3, (lo >> 1) + (hi >> 1) + (lo & hi & 1), guess)
        c = count_groups(lambda kk, r0: jnp.where(kk >= mid, 1.0, 0.0))
        live = lo + 1 < hi
        ge = c >= topk
        up = jnp.logical_and(live, ge)
        down = jnp.logical_and(live, jnp.logical_not(ge))
        lo = jnp.where(up, mid, lo)
        c_lo = jnp.where(up, c, c_lo)
        hi = jnp.where(jnp.logical_and(live, c == topk), mid + 1, jnp.where(down, mid, hi))
        c_hi = jnp.where(down, c, c_hi)
        return it + 1, lo, hi, c_lo, c_hi, jnp.max(jnp.where(lo + 1 < hi, 1, 0))

    qpos = q0 + i * t + lax.broadcasted_iota(I32, (1, t), 1)
    n_adm = jnp.minimum((qpos // CHUNK + 1) * CHUNK, n_keys).astype(F32)
    lo0 = _order_key(jnp.min(smin, axis=0, keepdims=True))
    hi0 = jnp.where(n_adm <= topk, lo0 + 1, _order_key(jnp.max(smax, axis=0, keepdims=True)) + 1)
    _, thr, _, c_thr, c_above, _ = lax.while_loop(
        search_cond, search_body, (jnp.int32(0), lo0, hi0, n_adm, jnp.zeros((1, t), F32), jnp.int32(1)))

    @pl.when(jnp.max(jnp.where(jnp.logical_and(c_thr > topk, qpos < n_keys), 1, 0)) > 0)
    def _():
        need = topk - c_above
        row = lax.broadcasted_iota(I32, (gt, t), 0)

        def cut_body(_, st):
            p_lo, p_hi = st
            p_mid = (p_lo + p_hi) >> 1
            c = count_groups(lambda kk, r0: jnp.where(
                jnp.logical_and(kk == thr, row + r0 <= p_mid), 1.0, 0.0))
            ok = c >= need
            return jnp.where(ok, p_lo, p_mid), jnp.where(ok, p_mid, p_hi)

        n_pos = keys_ref.shape[0]
        _, cut = lax.fori_loop(0, n_pos.bit_length(), cut_body,
                               (jnp.full((1, t), -1, I32), jnp.full((1, t), n_pos - 1, I32)))

        def demote_body(j, carry):
            r0 = pl.multiple_of(j * gt, gt)
            kk = keys_ref[pl.ds(r0, gt), :]
            keys_ref[pl.ds(r0, gt), :] = jnp.where(jnp.logical_and(kk == thr, row + r0 > cut), thr - 1, kk)
            return carry

        lax.fori_loop(0, n_groups, demote_body, 0)

    m_ref[...] = jnp.full(m_ref.shape, NEG, F32)
    acc_ref[...] = jnp.zeros(acc_ref.shape, F32)

    for h in range(H_B):
        qs_ref[h // G_B, :, (h % G_B) * t:(h % G_B + 1) * t] = qb_ref[:, h * HD_B:(h + 1) * HD_B].T

    def produce(j, s_ref, kind):
        ks = pl.multiple_of(j * t, t)
        off = jnp.where(keys_ref[pl.ds(ks, t), :] >= thr, 0.0, NEG)
        for n in range(HKV_B):
            s = jnp.dot(kb_ref[pl.ds(ks, t), n * HD_B:(n + 1) * HD_B], qs_ref[n], preferred_element_type=F32)
            parts = []
            for g in range(G_B):
                off_g = off if kind is None else off + bias_ref[n * G_B + g, kind]
                parts.append(s[:, g * t:(g + 1) * t] + off_g)
            s_ref[n] = jnp.concatenate(parts, axis=1)

    def consume(j, s_ref):
        ks = pl.multiple_of(j * t, t)
        for n in range(HKV_B):
            _softmax_step(s_ref[n], vbt_ref[n, :, pl.ds(ks, t)], m_ref.at[n], acc_ref.at[n])

    _pipelined_tiles(n_far, produce, consume, s0_ref, s1_ref)

    for n in range(HKV_B):
        o = _normalised(acc_ref[n], HD_B)
        for g in range(G_B):
            h = n * G_B + g
            o_ref[:, h * HD_B:(h + 1) * HD_B] = o[:, g * t:(g + 1) * t].T.astype(o_ref.dtype)


def _dsa(qb, qi, wi_t, kb, vbt, ki2, bias, *, t, q0, topk, n_keys, n_valid_diag):
    b, nq, _ = qb.shape
    nk = kb.shape[1]
    return pl.pallas_call(
        functools.partial(_dsa_body, t, q0, topk, n_keys, n_valid_diag),
        grid=(b, nq // t),
        in_specs=[
            pl.BlockSpec((None, t, H_B * HD_B), lambda bb, i: (bb, i, 0)),
            pl.BlockSpec((None, t, H_IDX * D_IDX), lambda bb, i: (bb, i, 0)),
            pl.BlockSpec((None, H_IDX, t), lambda bb, i: (bb, 0, i)),
            _resident((None, nk, HKV_B * HD_B), lambda bb, i: (bb, 0, 0)),
            _resident((None, HKV_B, HD_B, nk), lambda bb, i: (bb, 0, 0, 0)),
            _resident((None, nk, LANES), lambda bb, i: (bb, 0, 0)),
            _resident((H_B, 2, t, t), lambda bb, i: (0, 0, 0, 0)),
        ],
        out_specs=pl.BlockSpec((None, t, H_B * HD_B), lambda bb, i: (bb, i, 0)),
        out_shape=jax.ShapeDtypeStruct((b, nq, H_B * HD_B), BF16),
        scratch_shapes=[
            pltpu.VMEM((nk + (GROUP - 1) * t, t), I32),
            pltpu.VMEM((H_IDX, LANES, t), BF16),
            pltpu.VMEM((HKV_B, HD_B, G_B * t), BF16),
            pltpu.VMEM((2, 8, t), F32),
            pltpu.VMEM((HKV_B, 1, G_B * t), F32),
            pltpu.VMEM((HKV_B, HD_B + ONES_ROWS, G_B * t), F32),
            pltpu.VMEM((HKV_B, t, G_B * t), F32),
            pltpu.VMEM((HKV_B, t, G_B * t), F32),
        ],
        compiler_params=_cparams(("parallel", "arbitrary")),
        name="dsa",
    )(qb, qi, wi_t, kb, vbt, ki2, bias)


def _softmax_pieces(s_list, v_list):
    m = functools.reduce(jnp.maximum, [jnp.max(s, axis=-1, keepdims=True) for s in s_list])
    p_list = [jnp.exp2(s - m) for s in s_list]
    o = sum(jnp.dot(p.astype(BF16), v, preferred_element_type=F32) for p, v in zip(p_list, v_list))
    return o / sum(jnp.sum(p, axis=-1, keepdims=True) for p in p_list)


def _head_rows(ref, h, n_heads, n_pos):
    return ref[pl.ds(h, n_pos, stride=n_heads), :].astype(BF16)


def _short_a_body(nq, n_past, out_scale, lam_ref, q_ref, kp_ref, vp_ref, kn_ref, vn_ref, bias_ref, g_ref, o_ref):
    lam = lam_ref[0]
    lane = lax.broadcasted_iota(I32, (nq, LANES), 1)
    pieces = ((kp_ref, vp_ref, 0, n_past), (kn_ref, vn_ref, n_past, nq))
    for h in range(H_A):
        cols = slice(h * LANES, (h + 1) * LANES)
        q = q_ref[:, cols]
        zero = jnp.zeros_like(q)
        qz = jnp.concatenate([jnp.where(lane < HD_A, q, zero), jnp.where(lane < HD_A, zero, q)], axis=0)
        b = bias_ref[h]
        b2 = jnp.concatenate([b, b], axis=0)
        s_list = [lax.dot_general(qz, _head_rows(k_ref, h, H_A, size), _NT, preferred_element_type=F32)
                  + b2[:, lo:lo + size] for k_ref, _, lo, size in pieces]
        o = _softmax_pieces(s_list, [_head_rows(v_ref, h, H_A, size) for _, v_ref, _, size in pieces])
        o = o[:nq] - lam * o[nq:]
        y = o * lax.rsqrt(jnp.mean(o * o, axis=-1, keepdims=True) + EPS)
        o_ref[:, cols] = ((y * g_ref[...]) * out_scale).astype(o_ref.dtype)


def _short_a(q, k_past, v_past, k_new, v_new, bias, subln_g, lam, *, out_scale):
    b, nq, d = q.shape
    n_past = k_past.shape[1]
    assert n_past % LANES == 0
    rows = lambda n: pl.BlockSpec((None, n * H_A, DV_A), lambda bb: (bb, 0, 0))
    merge = lambda a: a.reshape(b, a.shape[1] * H_A, DV_A)
    return pl.pallas_call(
        functools.partial(_short_a_body, nq, n_past, out_scale),
        grid=(b,),
        in_specs=[
            pl.BlockSpec(memory_space=pltpu.SMEM),
            pl.BlockSpec((None, nq, d), lambda bb: (bb, 0, 0)),
            rows(n_past), rows(n_past), rows(nq), rows(nq),
            pl.BlockSpec(bias.shape, lambda bb: (0, 0, 0)),
            pl.BlockSpec((1, DV_A), lambda bb: (0, 0)),
        ],
        out_specs=pl.BlockSpec((None, nq, d), lambda bb: (bb, 0, 0)),
        out_shape=jax.ShapeDtypeStruct((b, nq, d), BF16),
        compiler_params=_cparams(("parallel",)),
        name="diff_attn_short",
    )(lam, q, merge(k_past), merge(v_past), merge(k_new), merge(v_new), bias, subln_g)


def _short_dsa_body(nq, n_past, topk, qb_ref, qi_ref, wi_ref, kbp_ref, vbp_ref, kip_ref, kbn_ref, vbn_ref,
                    kin_ref, bias_ref, o_ref):
    lane = lax.broadcasted_iota(I32, (nq, LANES), 1)
    spans = ((0, n_past), (n_past, nq))

    qz = []
    for h in range(H_IDX):
        qs = qi_ref[:, (h // 2) * LANES:(h // 2 + 1) * LANES]
        zero = jnp.zeros_like(qs)
        qz.append(jnp.where(lane < D_IDX, qs, zero) if h % 2 == 0 else jnp.where(lane < D_IDX, zero, qs))
    qz = jnp.concatenate(qz, axis=0)
    keys, kpos = [], []
    for ki_ref, (lo, size) in zip((kip_ref, kin_ref), spans):
        ki = ki_ref[...].astype(BF16)
        s_idx = lax.dot_general(qz, jnp.concatenate([ki, ki], axis=-1), _NT, preferred_element_type=F32)
        score = jnp.zeros((nq, size), F32)
        for h in range(H_IDX):
            score = score + jnp.maximum(s_idx[h * nq:(h + 1) * nq], 0.0) * wi_ref[:, h:h + 1]
        qp = n_past + lax.broadcasted_iota(I32, (nq, size), 0)
        kp = lo + lax.broadcasted_iota(I32, (nq, size), 1)
        keys.append(jnp.where((kp // CHUNK) <= (qp // CHUNK), _order_key(score), INT_MIN))
        kpos.append(kp)

    def count(hit_fn):
        return sum(jnp.sum(jnp.where(hit_fn(k, kp), 1.0, 0.0), axis=-1, keepdims=True)
                   for k, kp in zip(keys, kpos))

    def search_body(state):
        it, lo, hi, c_lo, c_hi, _ = state
        mid = (lo >> 1) + (hi >> 1) + (lo & hi & 1)
        c = count(lambda k, kp: k >= mid)
        live = lo + 1 < hi
        ge = c >= topk
        up = jnp.logical_and(live, ge)
        down = jnp.logical_and(live, jnp.logical_not(ge))
        lo = jnp.where(up, mid, lo)
        c_lo = jnp.where(up, c, c_lo)
        hi = jnp.where(jnp.logical_and(live, c == topk), mid + 1, jnp.where(down, mid, hi))
        c_hi = jnp.where(down, c, c_hi)
        return it + 1, lo, hi, c_lo, c_hi, jnp.max(jnp.where(lo + 1 < hi, 1, 0))

    n_keys = n_past + nq
    n_adm = count(lambda k, kp: k > INT_MIN)
    lo0 = jnp.full((nq, 1), INT_MIN + 1, I32)
    hi0 = jnp.where(n_adm <= topk, lo0 + 1, INT_MAX)
    _, thr, _, c_thr, c_above, _ = lax.while_loop(
        lambda st: jnp.logical_and(st[0] < 34, st[-1] > 0), search_body,
        (jnp.int32(0), lo0, hi0, n_adm, jnp.zeros((nq, 1), F32), jnp.int32(1)))

    need = topk - c_above

    def cut_body(_, st):
        p_lo, p_hi = st
        p_mid = (p_lo + p_hi) >> 1
        ok = count(lambda k, kp: jnp.logical_and(k == thr, kp <= p_mid)) >= need
        return jnp.where(ok, p_lo, p_mid), jnp.where(ok, p_mid, p_hi)

    _, cut = lax.fori_loop(0, n_keys.bit_length(), cut_body,
                           (jnp.full((nq, 1), -1, I32), jnp.full((nq, 1), n_keys - 1, I32)))
    off = [jnp.where(jnp.logical_and(k >= thr, jnp.logical_not(jnp.logical_and(k == thr, kp > cut))), 0.0, NEG)
           for k, kp in zip(keys, kpos)]

    for n in range(HKV_B):
        heads = range(n * G_B, (n + 1) * G_B)
        qs = jnp.concatenate([qb_ref[:, h * HD_B:(h + 1) * HD_B] for h in heads], axis=0)
        s_list = []
        for kb_ref, off_i, (lo, size) in zip((kbp_ref, kbn_ref), off, spans):
            s = lax.dot_general(qs, _head_rows(kb_ref, n, HKV_B, size), _NT, preferred_element_type=F32)
            s_list.append(s + jnp.concatenate([bias_ref[h][:, lo:lo + size] + off_i for h in heads], axis=0))
        o = _softmax_pieces(s_list, [_head_rows(vbp_ref, n, HKV_B, n_past), _head_rows(vbn_ref, n, HKV_B, nq)])
        for g, h in enumerate(heads):
            o_ref[:, h * HD_B:(h + 1) * HD_B] = o[g * nq:(g + 1) * nq].astype(o_ref.dtype)


def _short_dsa(qb, qi, wi, kb_past, vb_past, ki_past, kb_new, vb_new, ki_new, bias, *, topk):
    b, nq, d = qb.shape
    n_past = kb_past.shape[1]
    assert n_past % LANES == 0
    per_batch = lambda *shape: pl.BlockSpec((None,) + shape, lambda bb: (bb,) + (0,) * len(shape))
    merge = lambda a: a.reshape(b, a.shape[1] * HKV_B, HD_B)
    return pl.pallas_call(
        functools.partial(_short_dsa_body, nq, n_past, topk),
        grid=(b,),
        in_specs=[per_batch(nq, d), per_batch(nq, d), per_batch(nq, H_IDX),
                  per_batch(n_past * HKV_B, HD_B), per_batch(n_past * HKV_B, HD_B), per_batch(n_past, D_IDX),
                  per_batch(nq * HKV_B, HD_B), per_batch(nq * HKV_B, HD_B), per_batch(nq, D_IDX),
                  pl.BlockSpec(bias.shape, lambda bb: (0, 0, 0))],
        out_specs=per_batch(nq, d),
        out_shape=jax.ShapeDtypeStruct((b, nq, d), BF16),
        compiler_params=_cparams(("parallel",)),
        name="dsa_short",
    )(qb, qi, wi, merge(kb_past), merge(vb_past), ki_past, merge(kb_new), merge(vb_new), ki_new, bias)


def _sigmoid(x):
    return 1.0 / (1.0 + jnp.exp(-x))


def _merge_body(h_ref, oa_ref, ob_ref, wg_ref, woa_ref, wob_ref, o_ref):
    h = h_ref[...]
    ga = jnp.dot(h, wg_ref[:, :D_MODEL], preferred_element_type=F32)
    ya = jnp.dot(oa_ref[...], woa_ref[...], preferred_element_type=F32)
    merged = _sigmoid(ga) * ya
    gb = jnp.dot(h, wg_ref[:, D_MODEL:], preferred_element_type=F32)
    yb = jnp.dot(ob_ref[...], wob_ref[...], preferred_element_type=F32)
    o_ref[...] = (merged + _sigmoid(gb) * yb).astype(o_ref.dtype)


def _merge(h, oa, ob, wg, woa, wob, tm):
    n = h.shape[0]
    return pl.pallas_call(
        _merge_body,
        grid=(n // tm,),
        in_specs=[
            pl.BlockSpec((tm, D_MODEL), lambda i: (i, 0)),
            pl.BlockSpec((tm, H_A * DV_A), lambda i: (i, 0)),
            pl.BlockSpec((tm, H_B * HD_B), lambda i: (i, 0)),
            _resident((D_MODEL, 2 * D_MODEL), lambda i: (0, 0)),
            _resident((H_A * DV_A, D_MODEL), lambda i: (0, 0)),
            _resident((H_B * HD_B, D_MODEL), lambda i: (0, 0)),
        ],
        out_specs=pl.BlockSpec((tm, D_MODEL), lambda i: (i, 0)),
        out_shape=jax.ShapeDtypeStruct((n, D_MODEL), BF16),
        compiler_params=_cparams(("parallel",)),
        name="gated_merge",
    )(h, oa, ob, wg, woa, wob)


def _outproj_body(x_ref, mg_ref, w_ref, g_ref, x1_ref, h2_ref):
    x1 = x_ref[...] + jnp.dot(mg_ref[...], w_ref[...], preferred_element_type=F32)
    x1_ref[...] = x1
    y = x1 * lax.rsqrt(jnp.mean(x1 * x1, axis=-1, keepdims=True) + EPS)
    h2_ref[...] = (y * g_ref[...]).astype(h2_ref.dtype)


def _outproj(x, merged, w_out, g2, tm):
    n = x.shape[0]
    return pl.pallas_call(
        _outproj_body,
        grid=(n // tm,),
        in_specs=[
            pl.BlockSpec((tm, D_MODEL), lambda i: (i, 0)),
            pl.BlockSpec((tm, D_MODEL), lambda i: (i, 0)),
            _resident((D_MODEL, D_MODEL), lambda i: (0, 0)),
            pl.BlockSpec((1, D_MODEL), lambda i: (0, 0)),
        ],
        out_specs=[pl.BlockSpec((tm, D_MODEL), lambda i: (i, 0)), pl.BlockSpec((tm, D_MODEL), lambda i: (i, 0))],
        out_shape=[jax.ShapeDtypeStruct((n, D_MODEL), F32), jax.ShapeDtypeStruct((n, D_MODEL), BF16)],
        compiler_params=_cparams(("parallel",)),
        name="out_proj",
    )(x, merged, w_out, g2.reshape(1, D_MODEL))


def _ffn_body(x1_ref, h2_ref, w1_ref, w2_ref, o_ref):
    f = pl.program_id(1)

    @pl.when(f == 0)
    def _():
        o_ref[...] = x1_ref[...]

    u = jnp.maximum(jnp.dot(h2_ref[...], w1_ref[...], preferred_element_type=F32), 0.0)
    o_ref[...] += jnp.dot((u * u).astype(BF16), w2_ref[...], preferred_element_type=F32)


def _ffn(x1, h2, w1, w2, tm, tf):
    n = x1.shape[0]
    return pl.pallas_call(
        _ffn_body,
        grid=(n // tm, D_FF // tf),
        in_specs=[
            pl.BlockSpec((tm, D_MODEL), lambda i, f: (i, 0)),
            pl.BlockSpec((tm, D_MODEL), lambda i, f: (i, 0)),
            pl.BlockSpec((D_MODEL, tf), lambda i, f: (0, f)),
            pl.BlockSpec((tf, D_MODEL), lambda i, f: (f, 0)),
        ],
        out_specs=pl.BlockSpec((tm, D_MODEL), lambda i, f: (i, 0)),
        out_shape=jax.ShapeDtypeStruct((n, D_MODEL), F32),
        compiler_params=_cparams(("parallel", "arbitrary")),
        name="ffn",
    )(x1, h2, w1, w2)


def _prep_weights(w_in, w_o_a, w_o_b, w_out, w_ff1, w_ff2):
    sizes = (H_A * 2 * HD_A, H_A * 2 * HD_A, H_A * DV_A, H_B * HD_B, HKV_B * HD_B, HKV_B * HD_B,
             H_IDX * D_IDX, D_IDX, H_IDX, D_MODEL, D_MODEL)
    offs = [0]
    for s in sizes:
        offs.append(offs[-1] + s)
    col = lambda a, b: w_in[:, offs[a]:offs[b]]
    w_misc = jnp.concatenate(
        [col(4, 6), col(7, 8), col(7, 8), col(8, 9),
         jnp.zeros((D_MODEL, _MISC_COLS - _WI0 - H_IDX), w_in.dtype)], axis=1)
    return dict(
        qa=col(0, 1).astype(BF16), ka=col(1, 2).astype(BF16), va=col(2, 3).astype(BF16),
        qb=col(3, 4).astype(BF16), misc=w_misc.astype(BF16), qi=col(6, 7).astype(BF16),
        gate=col(9, 11).astype(BF16), oa=w_o_a.astype(BF16), ob=w_o_b.astype(BF16),
        out=w_out.astype(BF16), ff1=w_ff1.astype(BF16), ff2=w_ff2.astype(BF16))


def _layer(x, past, w, p, lam, lam_init, tab_t, *, tm, t_a=None, t_b=None):
    short = t_a is None
    b, t, _ = x.shape
    n = b * t
    xf = x.reshape(n, D_MODEL)
    h = _rmsnorm(xf, p["norm1_g"], tm)

    gain_a = lambda g: g.reshape(1, 2 * HD_A)
    v_layout = HEADS_T if (not short and past is None and b == 1) else FLAT
    (qa,) = _proj(h, w["qa"], [gain_a(p["qn_a_g"])], _epi_qa, [(1024, BF16, FLAT)], tm, "proj_qa")
    ka, ka_h = _proj(h, w["ka"], [gain_a(p["kn_a_g"])], _epi_ka, [(1024, F32, HEADS), (1024, BF16, FLAT)],
                     tm, "proj_ka")
    va, va_h = _proj(h, w["va"], [], _epi_copy2, [(1024, F32, HEADS), (1024, BF16, v_layout)], tm, "proj_va")
    (qb,) = _proj(h, w["qb"], [p["qn_b_g"].reshape(1, HD_B)], _epi_qb, [(1024, BF16, FLAT)], tm, "proj_qb")
    kb, kb_h, vb, vb_h, ki, ki2_h, wi = _proj(
        h, w["misc"], [p["kn_b_g"].reshape(1, HD_B)], _epi_misc,
        [(256, F32, HEADS), (256, BF16, FLAT), (256, F32, HEADS), (256, BF16, v_layout), (D_IDX, F32, FLAT),
         (2 * D_IDX, BF16, FLAT), (H_IDX, F32, FLAT)],
        tm, "proj_misc")
    (qi,) = _proj(h, w["qi"], [], _epi_copy1, [(1024, BF16, FLAT)], tm, "proj_qi")

    new_rows = (ka.reshape(b, t, H_A, 2 * HD_A), va.reshape(b, t, H_A, DV_A), kb.reshape(b, t, HKV_B, HD_B),
                vb.reshape(b, t, HKV_B, HD_B), ki.reshape(b, t, D_IDX))

    per_b = lambda a: a.reshape(b, t, -1)
    if short:
        n_keys = past[0].shape[1] + t
        topk = min(TOPK_MAX, n_keys // 4)
        oa = _short_a(per_b(qa), past[0], past[1], new_rows[0], new_rows[1],
                      _bias_dense(tab_t[:H_A], t, n_keys, n_keys - t, n_keys),
                      p["subln_a_g"].reshape(1, DV_A), lam, out_scale=1.0 - lam_init)
        ob = _short_dsa(per_b(qb), per_b(qi), per_b(wi), past[2], past[3], past[4], *new_rows[2:],
                        _bias_dense(tab_t[H_A:], t, n_keys, n_keys - t, n_keys), topk=topk)
    else:
        oa, ob = _tiled_mixers(qa, qb, qi, wi, ka_h, va_h, kb_h, vb_h, ki2_h, past, tab_t, p, lam, lam_init,
                               b=b, t=t, t_a=t_a, t_b=t_b)
    oa = oa.reshape(n, H_A * DV_A)
    ob = ob.reshape(n, H_B * HD_B)

    merged = _merge(h, oa, ob, w["gate"], w["oa"], w["ob"], min(tm, 256))
    x1, h2 = _outproj(xf, merged, w["out"], p["norm2_g"], min(tm, 256))
    y = _ffn(x1, h2, w["ff1"], w["ff2"], tm, 1024)
    return y.reshape(b, t, D_MODEL), new_rows


def _tiled_mixers(qa, qb, qi, wi, ka_h, va_h, kb_h, vb_h, ki2_h, past, tab_t, p, lam, lam_init,
                  *, b, t, t_a, t_b):
    def full_keys(new_h, past_arr):
        if new_h.ndim == 3:
            return new_h[None]
        new_h = new_h.reshape(b, t, -1)
        if past_arr is None:
            return new_h
        past_h = past_arr.reshape(b, past_arr.shape[1], -1).astype(BF16)
        return jnp.concatenate([past_h, new_h], axis=1)

    if past is None:
        pa_k = pa_v = pb_k = pb_v = pb_i = None
    else:
        pa_k, pa_v, pb_k, pb_v, pb_i = past
        pb_i = jnp.concatenate([pb_i, pb_i], axis=-1)
    ka_f, va_f = full_keys(ka_h, pa_k), full_keys(va_h, pa_v)
    kb_f, vb_f, ki_f = full_keys(kb_h, pb_k), full_keys(vb_h, pb_v), full_keys(ki2_h, pb_i)
    n_keys = ka_f.shape[1]
    q0 = n_keys - t
    topk = min(TOPK_MAX, n_keys // 4)
    t_pad = max(t_a, t_b)
    assert t_pad % t_a == 0 and t_pad % t_b == 0 and q0 % t_pad == 0
    nk_pad = -(-n_keys // t_pad) * t_pad
    pad_k = lambda a: jnp.pad(a, ((0, 0), (0, nk_pad - n_keys), (0, 0)))
    pad_q = lambda a, tt: jnp.pad(a.reshape(b, t, -1), ((0, 0), (0, -(-t // tt) * tt - t), (0, 0)))
    ka_f, kb_f, ki_f = (pad_k(a) for a in (ka_f, kb_f, ki_f))
    n_valid = lambda tt: n_keys - (n_keys - 1) // tt * tt
    assert all(t <= tt or n_valid(tt) == tt for tt in (t_a, t_b))

    def heads_t(v, nh):
        if v.ndim == 3:
            v = jnp.transpose(v.reshape(b, n_keys, nh, -1), (0, 2, 3, 1))
        return jnp.pad(v, ((0, 0), (0, 0), (0, 0), (0, nk_pad - n_keys)))

    bias_a = _bias_tiles(tab_t[:H_A], t_a, n_valid(t_a))
    oa = _attn_a(pad_q(qa, t_a), ka_f, heads_t(va_f, H_A), bias_a, p["subln_a_g"].reshape(DV_A, 1), lam,
                 t=t_a, q0=q0, out_scale=1.0 - lam_init)
    wi_t = jnp.swapaxes(pad_q(wi, t_b), 1, 2)
    bias_b = _bias_tiles(tab_t[H_A:], t_b, n_valid(t_b))
    ob = _dsa(pad_q(qb, t_b), pad_q(qi, t_b), wi_t, kb_f, heads_t(vb_f, HKV_B), ki_f, bias_b,
              t=t_b, q0=q0, topk=topk, n_keys=n_keys, n_valid_diag=n_valid(t_b))
    return oa[:, :t], ob[:, :t]


def kernel(x_prompt, x_sample, cache_a_k, cache_a_v, cache_b_k, cache_b_v, cache_b_kidx, rel_bias, norm1_g, w_in, qn_a_g, kn_a_g, lam_q1, lam_k1, lam_q2, lam_k2, subln_a_g, qn_b_g, kn_b_g, w_o_a, w_o_b, w_out, norm2_g, w_ff1, w_ff2):
    depth = w_in.shape[0]
    tab_t = rel_bias.T.astype(F32)
    y_prompt, y_sample = x_prompt, x_sample
    prompt_rows, sample_rows = [], []
    for l in range(depth):
        lam_init = 0.8 - 0.6 * math.exp(-0.3 * l)
        lam = (jnp.exp(jnp.sum(lam_q1[l].astype(F32) * lam_k1[l].astype(F32)))
               - jnp.exp(jnp.sum(lam_q2[l].astype(F32) * lam_k2[l].astype(F32))) + lam_init).reshape(1)
        w = _prep_weights(w_in[l], w_o_a[l], w_o_b[l], w_out[l], w_ff1[l], w_ff2[l])
        p = dict(norm1_g=norm1_g[l], qn_a_g=qn_a_g[l], kn_a_g=kn_a_g[l], subln_a_g=subln_a_g[l],
                 qn_b_g=qn_b_g[l], kn_b_g=kn_b_g[l], norm2_g=norm2_g[l])
        y_prompt, rp = _layer(y_prompt, None, w, p, lam, lam_init, tab_t, t_a=512, t_b=256, tm=512)
        past = (cache_a_k[l], cache_a_v[l], cache_b_k[l], cache_b_v[l], cache_b_kidx[l])
        y_sample, rs = _layer(y_sample, past, w, p, lam, lam_init, tab_t, tm=512)
        prompt_rows.append(rp)
        sample_rows.append(rs)
    p_rows = tuple(jnp.stack(r, axis=0) for r in zip(*prompt_rows))
    s_rows = tuple(jnp.stack(r, axis=0) for r in zip(*sample_rows))
    return (y_prompt, y_sample) + p_rows + s_rows
```

```python
import functools
import math

import jax
import jax.numpy as jnp
from jax import lax
from jax.experimental import pallas as pl
from jax.experimental.pallas import tpu as pltpu

F32 = jnp.float32
BF16 = jnp.bfloat16
I32 = jnp.int32

D_MODEL = 2048
CHUNK = 64
H_A = 8
DV_A = 128
HD_A = 64
H_B = 8
HD_B = 128
HKV_B = 2
G_B = H_B // HKV_B
H_IDX = 16
D_IDX = 64
TOPK_MAX = 256
N_BUCKETS = 32
D_FF = 4 * D_MODEL
EPS = 1e-6
LANES = 128
NEG = -1e30
LOG2E = math.log2(math.e)
INT_MIN = -(2 ** 31)
INT_MAX = 2 ** 31 - 1
VMEM_LIMIT = 56 * 1024 * 1024

_NT = (((1,), (1,)), ((), ()))


def _cparams(sem):
    return pltpu.CompilerParams(dimension_semantics=sem, vmem_limit_bytes=VMEM_LIMIT)


def _resident(block_shape, index_map):
    return pl.BlockSpec(block_shape, index_map, pipeline_mode=pl.Buffered(1))


def _rmsnorm_body(x_ref, g_ref, o_ref):
    x = x_ref[...]
    y = x * lax.rsqrt(jnp.mean(x * x, axis=-1, keepdims=True) + EPS)
    o_ref[...] = (y * g_ref[...]).astype(o_ref.dtype)


def _rmsnorm(x, g, tm):
    n, d = x.shape
    return pl.pallas_call(
        _rmsnorm_body,
        grid=(n // tm,),
        in_specs=[pl.BlockSpec((tm, d), lambda i: (i, 0)), pl.BlockSpec((1, d), lambda i: (0, 0))],
        out_specs=pl.BlockSpec((tm, d), lambda i: (i, 0)),
        out_shape=jax.ShapeDtypeStruct((n, d), BF16),
        compiler_params=_cparams(("parallel",)),
        name="rmsnorm",
    )(x, g.reshape(1, d))


def _head_norm(z, gain, split):
    lane = lax.broadcasted_iota(I32, (1, LANES), 1)
    lo = lane < (LANES // 2)
    outs = []
    for h in range(z.shape[1] // LANES):
        zh = z[:, h * LANES:(h + 1) * LANES]
        sq = zh * zh
        if split:
            s_lo = jnp.sum(jnp.where(lo, sq, 0.0), axis=-1, keepdims=True)
            s_hi = jnp.sum(jnp.where(lo, 0.0, sq), axis=-1, keepdims=True)
            r = jnp.where(lo, lax.rsqrt(s_lo * (2.0 / LANES) + EPS), lax.rsqrt(s_hi * (2.0 / LANES) + EPS))
        else:
            r = lax.rsqrt(jnp.mean(sq, axis=-1, keepdims=True) + EPS)
        outs.append(zh * r * gain)
    return jnp.concatenate(outs, axis=-1)


FLAT, HEADS, HEADS_T = "tokens x cols", "tokens x heads x 128", "heads x 128 x tokens"


def _proj_body(epilogue, n_aux, layouts, h_ref, w_ref, *rest):
    z = jnp.dot(h_ref[...], w_ref[...], preferred_element_type=F32)
    outs = epilogue(z, *[r[...] for r in rest[:n_aux]])
    for o_ref, o, layout in zip(rest[n_aux:], outs, layouts):
        if layout == FLAT:
            o_ref[...] = o.astype(o_ref.dtype)
            continue
        for hd in range(o.shape[1] // LANES):
            slab = o[:, hd * LANES:(hd + 1) * LANES]
            if layout == HEADS:
                o_ref[:, hd, :] = slab.astype(o_ref.dtype)
            else:
                o_ref[hd] = slab.T.astype(o_ref.dtype)


def _proj(h, w, aux, epilogue, out_defs, tm, name):
    n, k = h.shape
    c = w.shape[1]
    in_specs = [pl.BlockSpec((tm, k), lambda i: (i, 0)), _resident((k, c), lambda i: (0, 0))]
    in_specs += [pl.BlockSpec(a.shape, lambda i: (0, 0)) for a in aux]
    out_specs, out_shape = [], []
    for oc, dt, layout in out_defs:
        nh = oc // LANES
        if layout == FLAT:
            out_specs.append(pl.BlockSpec((tm, oc), lambda i: (i, 0)))
            out_shape.append(jax.ShapeDtypeStruct((n, oc), dt))
        elif layout == HEADS:
            out_specs.append(pl.BlockSpec((tm, nh, LANES), lambda i: (i, 0, 0)))
            out_shape.append(jax.ShapeDtypeStruct((n, nh, LANES), dt))
        else:
            out_specs.append(pl.BlockSpec((nh, LANES, tm), lambda i: (0, 0, i)))
            out_shape.append(jax.ShapeDtypeStruct((nh, LANES, n), dt))
    return pl.pallas_call(
        functools.partial(_proj_body, epilogue, len(aux), tuple(d[2] for d in out_defs)),
        grid=(n // tm,),
        in_specs=in_specs,
        out_specs=out_specs,
        out_shape=out_shape,
        compiler_params=_cparams(("parallel",)),
        name=name,
    )(h, w, *aux)


def _epi_qa(z, gain):
    return (_head_norm(z, gain, True) * (HD_A ** -0.5 * LOG2E),)


def _epi_ka(z, gain):
    y = _head_norm(z, gain, True)
    return y, y


def _epi_copy2(z):
    return z, z


def _epi_qb(z, gain):
    return (_head_norm(z, gain, False) * (HD_B ** -0.5 * LOG2E),)


def _epi_copy1(z):
    return (z,)


_KB0, _VB0, _KI0, _WI0, _MISC_COLS = 0, 256, 512, 640, 768


def _epi_misc(z, gain):
    kb = _head_norm(z[:, _KB0:_VB0], gain, False)
    vb = z[:, _VB0:_KI0]
    ki2 = z[:, _KI0:_WI0]
    return kb, kb, vb, vb, ki2[:, :D_IDX], ki2, z[:, _WI0:_WI0 + H_IDX]


def _rel_bias(tab_ref, h, rel):
    half = N_BUCKETS // 2
    exact = half // 2
    n = jnp.abs(rel)
    n2 = n * n
    v_neg = jnp.full(rel.shape, tab_ref[h, 0], F32)
    v_pos = jnp.full(rel.shape, tab_ref[h, half], F32)
    for b in range(1, half):
        cond = (n >= b) if b < exact else (n2 >= exact * exact * 2 ** (b - exact))
        v_neg = jnp.where(cond, tab_ref[h, b], v_neg)
        v_pos = jnp.where(cond, tab_ref[h, half + b], v_pos)
    return (jnp.where(rel > 0, v_pos, v_neg) - tab_ref[h, half - 1]) * LOG2E


def _bias_body(t, n_valid_diag, tab_ref, o_ref):
    h = pl.program_id(0)
    kl = lax.broadcasted_iota(I32, (t, t), 0)
    ql = lax.broadcasted_iota(I32, (t, t), 1)
    o_ref[0] = _rel_bias(tab_ref, h, kl - t - ql)
    visible = ((kl // CHUNK) <= (ql // CHUNK)) & (kl < n_valid_diag)
    o_ref[1] = jnp.where(visible, _rel_bias(tab_ref, h, kl - ql), NEG)


def _bias_dense_body(q0, n_keys, tab_ref, o_ref):
    h = pl.program_id(0)
    qpos = q0 + lax.broadcasted_iota(I32, o_ref.shape, 0)
    kpos = lax.broadcasted_iota(I32, o_ref.shape, 1)
    visible = ((kpos // CHUNK) <= (qpos // CHUNK)) & (kpos < n_keys)
    o_ref[...] = jnp.where(visible, _rel_bias(tab_ref, h, kpos - qpos), NEG)


def _bias_dense(tab_t, nq, nk_pad, q0, n_keys):
    nh = tab_t.shape[0]
    return pl.pallas_call(
        functools.partial(_bias_dense_body, q0, n_keys),
        grid=(nh,),
        in_specs=[pl.BlockSpec(memory_space=pltpu.SMEM)],
        out_specs=pl.BlockSpec((None, nq, nk_pad), lambda h: (h, 0, 0)),
        out_shape=jax.ShapeDtypeStruct((nh, nq, nk_pad), F32),
        compiler_params=_cparams(("arbitrary",)),
        name="bias_dense",
    )(tab_t)


def _bias_tiles(tab_t, t, n_valid_diag):
    nh = tab_t.shape[0]
    return pl.pallas_call(
        functools.partial(_bias_body, t, n_valid_diag),
        grid=(nh,),
        in_specs=[pl.BlockSpec(memory_space=pltpu.SMEM)],
        out_specs=pl.BlockSpec((None, 2, t, t), lambda h: (h, 0, 0, 0)),
        out_shape=jax.ShapeDtypeStruct((nh, 2, t, t), F32),
        compiler_params=_cparams(("arbitrary",)),
        name="bias_tiles",
    )(tab_t)


ONES_ROWS = 16


def _softmax_step(s_ref, vt, m_ref, acc_ref, width):
    vt_ones = jnp.concatenate([vt, jnp.ones((ONES_ROWS, vt.shape[1]), vt.dtype)], axis=0)
    for c0 in range(0, s_ref.shape[1], width):
        cols = slice(c0, c0 + width)
        s = s_ref[:, cols]
        m_old = m_ref[:, cols]
        m_new = jnp.maximum(m_old, jnp.max(s, axis=0, keepdims=True))
        alpha = jnp.exp2(m_old - m_new)
        p = jnp.exp2(s - m_new)
        acc_ref[:, cols] = acc_ref[:, cols] * alpha + jnp.dot(vt_ones, p.astype(BF16), preferred_element_type=F32)
        m_ref[:, cols] = m_new


def _normalised(acc, dv):
    return acc[:dv] / acc[dv:dv + 1]


def _pipelined_tiles(n_far, produce, consume, buf0, buf1):
    @pl.when(n_far == -1)
    def _():
        produce(0, buf0, 1)
        consume(0, buf0)

    @pl.when(n_far == 0)
    def _():
        produce(0, buf0, 0)
        produce(1, buf1, 1)
        consume(0, buf0)
        consume(1, buf1)

    @pl.when(n_far >= 1)
    def _():
        produce(0, buf0, None)

    pairs = jnp.maximum(n_far - 1, 0) // 2

    def pair(j):
        produce(j + 1, buf1, None)
        consume(j, buf0)
        produce(j + 2, buf0, None)
        consume(j + 1, buf1)

    def body(i, carry):
        for u in range(4):
            pair(8 * i + 2 * u)
        return carry

    def leftover(q, carry):
        pair(2 * q)
        return carry

    lax.fori_loop(0, pairs // 4, body, 0)
    lax.fori_loop(pairs // 4 * 4, pairs, leftover, 0)
    d = 2 * pairs
    rem = n_far - d

    @pl.when(rem == 1)
    def _():
        produce(d + 1, buf1, 0)
        consume(d, buf0)
        produce(d + 2, buf0, 1)
        consume(d + 1, buf1)
        consume(d + 2, buf0)

    @pl.when(rem == 2)
    def _():
        produce(d + 1, buf1, None)
        consume(d, buf0)
        produce(d + 2, buf0, 0)
        consume(d + 1, buf1)
        produce(d + 3, buf1, 1)
        consume(d + 2, buf0)
        consume(d + 3, buf1)


def _attn_a_body(t, q0, out_scale, lam_ref, q_ref, k_ref, vt_ref, bias_ref, g_ref, o_ref,
                 qz_ref, m_ref, acc_ref, s0_ref, s1_ref):
    i = pl.program_id(2)
    n_far = q0 // t + i - 1
    q = q_ref[...]
    lane = lax.broadcasted_iota(I32, (t, LANES), 1)
    zero = jnp.zeros_like(q)
    qz_ref[:, :t] = jnp.where(lane < HD_A, q, zero).T
    qz_ref[:, t:] = jnp.where(lane < HD_A, zero, q).T
    m_ref[...] = jnp.full(m_ref.shape, NEG, F32)
    acc_ref[...] = jnp.zeros(acc_ref.shape, F32)

    def produce(j, s_ref, kind):
        ks = pl.multiple_of(j * t, t)
        s = jnp.dot(k_ref[pl.ds(ks, t), :], qz_ref[...], preferred_element_type=F32)
        if kind is not None:
            b = bias_ref[kind]
            s = jnp.concatenate([s[:, :t] + b, s[:, t:] + b], axis=1)
        s_ref[...] = s

    def consume(j, s_ref):
        _softmax_step(s_ref, vt_ref[:, pl.ds(pl.multiple_of(j * t, t), t)], m_ref, acc_ref, t)

    _pipelined_tiles(n_far, produce, consume, s0_ref, s1_ref)

    lam = lam_ref[0]
    o = _normalised(acc_ref[...], DV_A)
    o = o[:, :t] - lam * o[:, t:]
    y = o * lax.rsqrt(jnp.mean(o * o, axis=0, keepdims=True) + EPS)
    y = (y * g_ref[...]) * out_scale
    o_ref[...] = y.T.astype(o_ref.dtype)


def _attn_a(q, k, vt, bias, subln_g, lam, *, t, q0, out_scale):
    b, nq, _ = q.shape
    nk = k.shape[1]
    return pl.pallas_call(
        functools.partial(_attn_a_body, t, q0, out_scale),
        grid=(b, H_A, nq // t),
        in_specs=[
            pl.BlockSpec(memory_space=pltpu.SMEM),
            pl.BlockSpec((None, t, LANES), lambda bb, h, i: (bb, i, h)),
            pl.BlockSpec((None, nk, LANES), lambda bb, h, i: (bb, 0, h)),
            pl.BlockSpec((None, None, DV_A, nk), lambda bb, h, i: (bb, h, 0, 0)),
            pl.BlockSpec((None, 2, t, t), lambda bb, h, i: (h, 0, 0, 0)),
            pl.BlockSpec((DV_A, 1), lambda bb, h, i: (0, 0)),
        ],
        out_specs=pl.BlockSpec((None, t, LANES), lambda bb, h, i: (bb, i, h)),
        out_shape=jax.ShapeDtypeStruct((b, nq, H_A * DV_A), BF16),
        scratch_shapes=[
            pltpu.VMEM((LANES, 2 * t), BF16),
            pltpu.VMEM((1, 2 * t), F32),
            pltpu.VMEM((DV_A + ONES_ROWS, 2 * t), F32),
            pltpu.VMEM((t, 2 * t), F32),
            pltpu.VMEM((t, 2 * t), F32),
        ],
        compiler_params=_cparams(("parallel", "parallel", "arbitrary")),
        name="diff_attn",
    )(lam, q, k, vt, bias, subln_g)


GROUP = 4


def _order_key(x):
    b = pltpu.bitcast(x, I32)
    return jnp.where(b < 0, b ^ INT_MAX, b)


def _order_unkey(k):
    return pltpu.bitcast(jnp.where(k < 0, k ^ INT_MAX, k), F32)


def _dsa_body(t, q0, topk, n_keys, n_valid_diag, qb_ref, qi_ref, wi_ref, kb_ref, vbt_ref, ki_ref, bias_ref, o_ref,
              keys_ref, qz_ref, qs_ref, mm_ref, m_ref, acc_ref, s0_ref, s1_ref):
    i = pl.program_id(1)
    n_tiles = q0 // t + i + 1
    n_far = n_tiles - 2

    lane = lax.broadcasted_iota(I32, (t, LANES), 1)
    for hp in range(H_IDX // 2):
        qs = qi_ref[:, hp * LANES:(hp + 1) * LANES]
        zero = jnp.zeros_like(qs)
        qz_ref[2 * hp] = jnp.where(lane < D_IDX, qs, zero).T
        qz_ref[2 * hp + 1] = jnp.where(lane < D_IDX, zero, qs).T

    def score_tile(j):
        kt = ki_ref[pl.ds(pl.multiple_of(j * t, t), t), :]
        acc = jnp.zeros((t, t), F32)
        for h in range(H_IDX):
            s = jnp.dot(kt, qz_ref[h], preferred_element_type=F32)
            acc = acc + jnp.maximum(s, 0.0) * wi_ref[h:h + 1, :]
        return acc

    def fold8(x, op):
        return op(x.reshape(x.shape[0] // 8, 8, t), axis=0)

    mm_ref[0] = jnp.full((8, t), jnp.inf, F32)
    mm_ref[1] = jnp.full((8, t), -jnp.inf, F32)

    def score_store(j):
        sc = score_tile(j)
        keys_ref[pl.ds(pl.multiple_of(j * t, t), t), :] = _order_key(sc)
        mm_ref[0] = jnp.minimum(mm_ref[0], fold8(sc, jnp.min))
        mm_ref[1] = jnp.maximum(mm_ref[1], fold8(sc, jnp.max))

    def score_group(jj, carry):
        for u in range(GROUP):
            score_store(GROUP * jj + u)
        return carry

    def score_single(j, carry):
        score_store(j)
        return carry

    jd = n_tiles - 1
    lax.fori_loop(0, jd // GROUP, score_group, 0)
    lax.fori_loop(jd // GROUP * GROUP, jd, score_single, 0)

    kl = lax.broadcasted_iota(I32, (t, t), 0)
    ql = lax.broadcasted_iota(I32, (t, t), 1)
    admissible = ((kl // CHUNK) <= (ql // CHUNK)) & (kl < n_valid_diag)
    sc = score_tile(jd)
    keys_ref[pl.ds(pl.multiple_of(jd * t, t), t), :] = jnp.where(admissible, _order_key(sc), INT_MIN)
    smin = jnp.minimum(mm_ref[0], fold8(jnp.where(admissible, sc, jnp.inf), jnp.min))
    smax = jnp.maximum(mm_ref[1], fold8(jnp.where(admissible, sc, -jnp.inf), jnp.max))
    for u in range(GROUP - 1):
        keys_ref[pl.ds(pl.multiple_of((n_tiles + u) * t, t), t), :] = jnp.full((t, t), INT_MIN, I32)
    n_groups = (n_tiles + GROUP - 1) // GROUP
    gt = GROUP * t

    def count_groups(hit_fn):
        def body(j, c):
            r0 = pl.multiple_of(j * gt, gt)
            hit = hit_fn(keys_ref[pl.ds(r0, gt), :], r0)
            return c + jnp.sum(hit.reshape(gt // 32, 32, t), axis=0)
        part = lax.fori_loop(0, n_groups, body, jnp.zeros((32, t), F32))
        return jnp.sum(part, axis=0, keepdims=True)

    def search_cond(state):
        return jnp.logical_and(state[0] < 4 * 34, state[-1] > 0)

    def search_body(state):
        it, lo, hi, c_lo, c_hi, _ = state
        f_lo, f_hi = _order_unkey(lo), _order_unkey(hi)
        a, b = jnp.log(c_lo + 0.5), jnp.log(c_hi + 0.5)
        frac = jnp.clip((a - math.log(topk)) / jnp.maximum(a - b, 1e-9), 0.0, 1.0)
        guess = jnp.minimum(jnp.maximum(_order_key(f_lo + (f_hi - f_lo) * frac), lo + 1), hi - 1)
        mid = jnp.where(it % 4 == 3, (lo >> 1) + (hi >> 1) + (lo & hi & 1), guess)
        c = count_groups(lambda kk, r0: jnp.where(kk >= mid, 1.0, 0.0))
        live = lo + 1 < hi
        ge = c >= topk
        up = jnp.logical_and(live, ge)
        down = jnp.logical_and(live, jnp.logical_not(ge))
        lo = jnp.where(up, mid, lo)
        c_lo = jnp.where(up, c, c_lo)
        hi = jnp.where(jnp.logical_and(live, c == topk), mid + 1, jnp.where(down, mid, hi))
        c_hi = jnp.where(down, c, c_hi)
        open_ = jnp.logical_and(lo + 1 < hi, c_hi != topk - 1)
        return it + 1, lo, hi, c_lo, c_hi, jnp.max(jnp.where(open_, 1, 0))

    qpos = q0 + i * t + lax.broadcasted_iota(I32, (1, t), 1)
    n_adm = jnp.minimum((qpos // CHUNK + 1) * CHUNK, n_keys).astype(F32)
    lo0 = _order_key(jnp.min(smin, axis=0, keepdims=True))
    hi0 = jnp.where(n_adm <= topk, lo0 + 1, _order_key(jnp.max(smax, axis=0, keepdims=True)) + 1)
    _, lo, hi, c_lo, c_above, _ = lax.while_loop(
        search_cond, search_body, (jnp.int32(0), lo0, hi0, n_adm, jnp.zeros((1, t), F32), jnp.int32(1)))

    def below_hi_max(j, m):
        kk = keys_ref[pl.ds(pl.multiple_of(j * gt, gt), gt), :]
        return jnp.maximum(m, jnp.max(jnp.where(kk < hi, kk, INT_MIN).reshape(gt // 8, 8, t), axis=0))

    m8 = lax.fori_loop(0, n_groups, below_hi_max, jnp.full((8, t), INT_MIN, I32))
    for shift in (4, 2, 1):
        m8 = jnp.maximum(m8, pltpu.roll(m8, shift, axis=0))
    wide = lo + 1 < hi
    thr = jnp.where(wide, m8[:1], lo)
    c_eq = count_groups(lambda kk, r0: jnp.where(kk == thr, 1.0, 0.0))
    c_thr = jnp.where(wide, c_above + c_eq, c_lo)

    @pl.when(jnp.max(jnp.where(jnp.logical_and(c_thr > topk, qpos < n_keys), 1, 0)) > 0)
    def _():
        need = topk - c_above
        row = lax.broadcasted_iota(I32, (gt, t), 0)

        def cut_body(_, st):
            p_lo, p_hi = st
            p_mid = (p_lo + p_hi) >> 1
            c = count_groups(lambda kk, r0: jnp.where(
                jnp.logical_and(kk == thr, row + r0 <= p_mid), 1.0, 0.0))
            ok = c >= need
            return jnp.where(ok, p_lo, p_mid), jnp.where(ok, p_mid, p_hi)

        n_pos = keys_ref.shape[0]
        _, cut = lax.fori_loop(0, n_pos.bit_length(), cut_body,
                               (jnp.full((1, t), -1, I32), jnp.full((1, t), n_pos - 1, I32)))

        def demote_body(j, carry):
            r0 = pl.multiple_of(j * gt, gt)
            kk = keys_ref[pl.ds(r0, gt), :]
            keys_ref[pl.ds(r0, gt), :] = jnp.where(jnp.logical_and(kk == thr, row + r0 > cut), thr - 1, kk)
            return carry

        lax.fori_loop(0, n_groups, demote_body, 0)

    m_ref[...] = jnp.full(m_ref.shape, NEG, F32)
    acc_ref[...] = jnp.zeros(acc_ref.shape, F32)

    for h in range(H_B):
        qs_ref[h // G_B, :, (h % G_B) * t:(h % G_B + 1) * t] = qb_ref[:, h * HD_B:(h + 1) * HD_B].T

    def produce(j, s_ref, kind):
        ks = pl.multiple_of(j * t, t)
        off = jnp.where(keys_ref[pl.ds(ks, t), :] >= thr, 0.0, NEG)
        for n in range(HKV_B):
            s = jnp.dot(kb_ref[pl.ds(ks, t), n * HD_B:(n + 1) * HD_B], qs_ref[n], preferred_element_type=F32)
            parts = []
            for g in range(G_B):
                off_g = off if kind is None else off + bias_ref[n * G_B + g, kind]
                parts.append(s[:, g * t:(g + 1) * t] + off_g)
            s_ref[n] = jnp.concatenate(parts, axis=1)

    def consume(j, s_ref):
        ks = pl.multiple_of(j * t, t)
        for n in range(HKV_B):
            _softmax_step(s_ref.at[n], vbt_ref[n, :, pl.ds(ks, t)], m_ref.at[n], acc_ref.at[n], 2 * t)

    _pipelined_tiles(n_far, produce, consume, s0_ref, s1_ref)

    for n in range(HKV_B):
        o = _normalised(acc_ref[n], HD_B)
        for g in range(G_B):
            h = n * G_B + g
            o_ref[:, h * HD_B:(h + 1) * HD_B] = o[:, g * t:(g + 1) * t].T.astype(o_ref.dtype)


def _dsa(qb, qi, wi_t, kb, vbt, ki2, bias, *, t, q0, topk, n_keys, n_valid_diag):
    b, nq, _ = qb.shape
    nk = kb.shape[1]
    return pl.pallas_call(
        functools.partial(_dsa_body, t, q0, topk, n_keys, n_valid_diag),
        grid=(b, nq // t),
        in_specs=[
            pl.BlockSpec((None, t, H_B * HD_B), lambda bb, i: (bb, i, 0)),
            pl.BlockSpec((None, t, H_IDX * D_IDX), lambda bb, i: (bb, i, 0)),
            pl.BlockSpec((None, H_IDX, t), lambda bb, i: (bb, 0, i)),
            _resident((None, nk, HKV_B * HD_B), lambda bb, i: (bb, 0, 0)),
            _resident((None, HKV_B, HD_B, nk), lambda bb, i: (bb, 0, 0, 0)),
            _resident((None, nk, LANES), lambda bb, i: (bb, 0, 0)),
            _resident((H_B, 2, t, t), lambda bb, i: (0, 0, 0, 0)),
        ],
        out_specs=pl.BlockSpec((None, t, H_B * HD_B), lambda bb, i: (bb, i, 0)),
        out_shape=jax.ShapeDtypeStruct((b, nq, H_B * HD_B), BF16),
        scratch_shapes=[
            pltpu.VMEM((nk + (GROUP - 1) * t, t), I32),
            pltpu.VMEM((H_IDX, LANES, t), BF16),
            pltpu.VMEM((HKV_B, HD_B, G_B * t), BF16),
            pltpu.VMEM((2, 8, t), F32),
            pltpu.VMEM((HKV_B, 1, G_B * t), F32),
            pltpu.VMEM((HKV_B, HD_B + ONES_ROWS, G_B * t), F32),
            pltpu.VMEM((HKV_B, t, G_B * t), F32),
            pltpu.VMEM((HKV_B, t, G_B * t), F32),
        ],
        compiler_params=_cparams(("parallel", "arbitrary")),
        name="dsa",
    )(qb, qi, wi_t, kb, vbt, ki2, bias)


def _softmax_pieces(s_list, v_list):
    m = functools.reduce(jnp.maximum, [jnp.max(s, axis=-1, keepdims=True) for s in s_list])
    p_list = [jnp.exp2(s - m) for s in s_list]
    o = sum(jnp.dot(p.astype(BF16), v, preferred_element_type=F32) for p, v in zip(p_list, v_list))
    return o / sum(jnp.sum(p, axis=-1, keepdims=True) for p in p_list)


def _head_rows(ref, h, n_heads, n_pos):
    return ref[pl.ds(h, n_pos, stride=n_heads), :].astype(BF16)


def _short_a_body(nq, n_past, out_scale, lam_ref, q_ref, kp_ref, vp_ref, kn_ref, vn_ref, bias_ref, g_ref, o_ref):
    lam = lam_ref[0]
    lane = lax.broadcasted_iota(I32, (nq, LANES), 1)
    pieces = ((kp_ref, vp_ref, 0, n_past), (kn_ref, vn_ref, n_past, nq))
    for h in range(H_A):
        cols = slice(h * LANES, (h + 1) * LANES)
        q = q_ref[:, cols]
        zero = jnp.zeros_like(q)
        qz = jnp.concatenate([jnp.where(lane < HD_A, q, zero), jnp.where(lane < HD_A, zero, q)], axis=0)
        b = bias_ref[h]
        b2 = jnp.concatenate([b, b], axis=0)
        s_list = [lax.dot_general(qz, _head_rows(k_ref, h, H_A, size), _NT, preferred_element_type=F32)
                  + b2[:, lo:lo + size] for k_ref, _, lo, size in pieces]
        o = _softmax_pieces(s_list, [_head_rows(v_ref, h, H_A, size) for _, v_ref, _, size in pieces])
        o = o[:nq] - lam * o[nq:]
        y = o * lax.rsqrt(jnp.mean(o * o, axis=-1, keepdims=True) + EPS)
        o_ref[:, cols] = ((y * g_ref[...]) * out_scale).astype(o_ref.dtype)


def _short_a(q, k_past, v_past, k_new, v_new, bias, subln_g, lam, *, out_scale):
    b, nq, d = q.shape
    n_past = k_past.shape[1]
    assert n_past % LANES == 0
    rows = lambda n: pl.BlockSpec((None, n * H_A, DV_A), lambda bb: (bb, 0, 0))
    merge = lambda a: a.reshape(b, a.shape[1] * H_A, DV_A)
    return pl.pallas_call(
        functools.partial(_short_a_body, nq, n_past, out_scale),
        grid=(b,),
        in_specs=[
            pl.BlockSpec(memory_space=pltpu.SMEM),
            pl.BlockSpec((None, nq, d), lambda bb: (bb, 0, 0)),
            rows(n_past), rows(n_past), rows(nq), rows(nq),
            pl.BlockSpec(bias.shape, lambda bb: (0, 0, 0)),
            pl.BlockSpec((1, DV_A), lambda bb: (0, 0)),
        ],
        out_specs=pl.BlockSpec((None, nq, d), lambda bb: (bb, 0, 0)),
        out_shape=jax.ShapeDtypeStruct((b, nq, d), BF16),
        compiler_params=_cparams(("parallel",)),
        name="diff_attn_short",
    )(lam, q, merge(k_past), merge(v_past), merge(k_new), merge(v_new), bias, subln_g)


def _short_dsa_body(nq, n_past, topk, qb_ref, qi_ref, wi_ref, kbp_ref, vbp_ref, kip_ref, kbn_ref, vbn_ref,
                    kin_ref, bias_ref, o_ref):
    lane = lax.broadcasted_iota(I32, (nq, LANES), 1)
    spans = ((0, n_past), (n_past, nq))

    qz = []
    for h in range(H_IDX):
        qs = qi_ref[:, (h // 2) * LANES:(h // 2 + 1) * LANES]
        zero = jnp.zeros_like(qs)
        qz.append(jnp.where(lane < D_IDX, qs, zero) if h % 2 == 0 else jnp.where(lane < D_IDX, zero, qs))
    qz = jnp.concatenate(qz, axis=0)
    keys, kpos = [], []
    for ki_ref, (lo, size) in zip((kip_ref, kin_ref), spans):
        ki = ki_ref[...].astype(BF16)
        s_idx = lax.dot_general(qz, jnp.concatenate([ki, ki], axis=-1), _NT, preferred_element_type=F32)
        score = jnp.zeros((nq, size), F32)
        for h in range(H_IDX):
            score = score + jnp.maximum(s_idx[h * nq:(h + 1) * nq], 0.0) * wi_ref[:, h:h + 1]
        qp = n_past + lax.broadcasted_iota(I32, (nq, size), 0)
        kp = lo + lax.broadcasted_iota(I32, (nq, size), 1)
        keys.append(jnp.where((kp // CHUNK) <= (qp // CHUNK), _order_key(score), INT_MIN))
        kpos.append(kp)

    def count(hit_fn):
        return sum(jnp.sum(jnp.where(hit_fn(k, kp), 1.0, 0.0), axis=-1, keepdims=True)
                   for k, kp in zip(keys, kpos))

    def search_body(state):
        it, lo, hi, c_lo, c_hi, _ = state
        mid = (lo >> 1) + (hi >> 1) + (lo & hi & 1)
        c = count(lambda k, kp: k >= mid)
        live = lo + 1 < hi
        ge = c >= topk
        up = jnp.logical_and(live, ge)
        down = jnp.logical_and(live, jnp.logical_not(ge))
        lo = jnp.where(up, mid, lo)
        c_lo = jnp.where(up, c, c_lo)
        hi = jnp.where(jnp.logical_and(live, c == topk), mid + 1, jnp.where(down, mid, hi))
        c_hi = jnp.where(down, c, c_hi)
        return it + 1, lo, hi, c_lo, c_hi, jnp.max(jnp.where(lo + 1 < hi, 1, 0))

    n_keys = n_past + nq
    n_adm = count(lambda k, kp: k > INT_MIN)
    lo0 = jnp.full((nq, 1), INT_MIN + 1, I32)
    hi0 = jnp.where(n_adm <= topk, lo0 + 1, INT_MAX)
    _, thr, _, c_thr, c_above, _ = lax.while_loop(
        lambda st: jnp.logical_and(st[0] < 34, st[-1] > 0), search_body,
        (jnp.int32(0), lo0, hi0, n_adm, jnp.zeros((nq, 1), F32), jnp.int32(1)))

    need = topk - c_above

    def cut_body(_, st):
        p_lo, p_hi = st
        p_mid = (p_lo + p_hi) >> 1
        ok = count(lambda k, kp: jnp.logical_and(k == thr, kp <= p_mid)) >= need
        return jnp.where(ok, p_lo, p_mid), jnp.where(ok, p_mid, p_hi)

    _, cut = lax.fori_loop(0, n_keys.bit_length(), cut_body,
                           (jnp.full((nq, 1), -1, I32), jnp.full((nq, 1), n_keys - 1, I32)))
    off = [jnp.where(jnp.logical_and(k >= thr, jnp.logical_not(jnp.logical_and(k == thr, kp > cut))), 0.0, NEG)
           for k, kp in zip(keys, kpos)]

    for n in range(HKV_B):
        heads = range(n * G_B, (n + 1) * G_B)
        qs = jnp.concatenate([qb_ref[:, h * HD_B:(h + 1) * HD_B] for h in heads], axis=0)
        s_list = []
        for kb_ref, off_i, (lo, size) in zip((kbp_ref, kbn_ref), off, spans):
            s = lax.dot_general(qs, _head_rows(kb_ref, n, HKV_B, size), _NT, preferred_element_type=F32)
            s_list.append(s + jnp.concatenate([bias_ref[h][:, lo:lo + size] + off_i for h in heads], axis=0))
        o = _softmax_pieces(s_list, [_head_rows(vbp_ref, n, HKV_B, n_past), _head_rows(vbn_ref, n, HKV_B, nq)])
        for g, h in enumerate(heads):
            o_ref[:, h * HD_B:(h + 1) * HD_B] = o[g * nq:(g + 1) * nq].astype(o_ref.dtype)


def _short_dsa(qb, qi, wi, kb_past, vb_past, ki_past, kb_new, vb_new, ki_new, bias, *, topk):
    b, nq, d = qb.shape
    n_past = kb_past.shape[1]
    assert n_past % LANES == 0
    per_batch = lambda *shape: pl.BlockSpec((None,) + shape, lambda bb: (bb,) + (0,) * len(shape))
    merge = lambda a: a.reshape(b, a.shape[1] * HKV_B, HD_B)
    return pl.pallas_call(
        functools.partial(_short_dsa_body, nq, n_past, topk),
        grid=(b,),
        in_specs=[per_batch(nq, d), per_batch(nq, d), per_batch(nq, H_IDX),
                  per_batch(n_past * HKV_B, HD_B), per_batch(n_past * HKV_B, HD_B), per_batch(n_past, D_IDX),
                  per_batch(nq * HKV_B, HD_B), per_batch(nq * HKV_B, HD_B), per_batch(nq, D_IDX),
                  pl.BlockSpec(bias.shape, lambda bb: (0, 0, 0))],
        out_specs=per_batch(nq, d),
        out_shape=jax.ShapeDtypeStruct((b, nq, d), BF16),
        compiler_params=_cparams(("parallel",)),
        name="dsa_short",
    )(qb, qi, wi, merge(kb_past), merge(vb_past), ki_past, merge(kb_new), merge(vb_new), ki_new, bias)


def _sigmoid(x):
    return 1.0 / (1.0 + jnp.exp(-x))


def _merge_body(h_ref, oa_ref, ob_ref, wg_ref, woa_ref, wob_ref, o_ref):
    h = h_ref[...]
    ga = jnp.dot(h, wg_ref[:, :D_MODEL], preferred_element_type=F32)
    ya = jnp.dot(oa_ref[...], woa_ref[...], preferred_element_type=F32)
    merged = _sigmoid(ga) * ya
    gb = jnp.dot(h, wg_ref[:, D_MODEL:], preferred_element_type=F32)
    yb = jnp.dot(ob_ref[...], wob_ref[...], preferred_element_type=F32)
    o_ref[...] = (merged + _sigmoid(gb) * yb).astype(o_ref.dtype)


def _merge(h, oa, ob, wg, woa, wob, tm):
    n = h.shape[0]
    return pl.pallas_call(
        _merge_body,
        grid=(n // tm,),
        in_specs=[
            pl.BlockSpec((tm, D_MODEL), lambda i: (i, 0)),
            pl.BlockSpec((tm, H_A * DV_A), lambda i: (i, 0)),
            pl.BlockSpec((tm, H_B * HD_B), lambda i: (i, 0)),
            _resident((D_MODEL, 2 * D_MODEL), lambda i: (0, 0)),
            _resident((H_A * DV_A, D_MODEL), lambda i: (0, 0)),
            _resident((H_B * HD_B, D_MODEL), lambda i: (0, 0)),
        ],
        out_specs=pl.BlockSpec((tm, D_MODEL), lambda i: (i, 0)),
        out_shape=jax.ShapeDtypeStruct((n, D_MODEL), BF16),
        compiler_params=_cparams(("parallel",)),
        name="gated_merge",
    )(h, oa, ob, wg, woa, wob)


def _outproj_body(x_ref, mg_ref, w_ref, g_ref, x1_ref, h2_ref):
    x1 = x_ref[...] + jnp.dot(mg_ref[...], w_ref[...], preferred_element_type=F32)
    x1_ref[...] = x1
    y = x1 * lax.rsqrt(jnp.mean(x1 * x1, axis=-1, keepdims=True) + EPS)
    h2_ref[...] = (y * g_ref[...]).astype(h2_ref.dtype)


def _outproj(x, merged, w_out, g2, tm):
    n = x.shape[0]
    return pl.pallas_call(
        _outproj_body,
        grid=(n // tm,),
        in_specs=[
            pl.BlockSpec((tm, D_MODEL), lambda i: (i, 0)),
            pl.BlockSpec((tm, D_MODEL), lambda i: (i, 0)),
            _resident((D_MODEL, D_MODEL), lambda i: (0, 0)),
            pl.BlockSpec((1, D_MODEL), lambda i: (0, 0)),
        ],
        out_specs=[pl.BlockSpec((tm, D_MODEL), lambda i: (i, 0)), pl.BlockSpec((tm, D_MODEL), lambda i: (i, 0))],
        out_shape=[jax.ShapeDtypeStruct((n, D_MODEL), F32), jax.ShapeDtypeStruct((n, D_MODEL), BF16)],
        compiler_params=_cparams(("parallel",)),
        name="out_proj",
    )(x, merged, w_out, g2.reshape(1, D_MODEL))


def _ffn_body(x1_ref, h2_ref, w1_ref, w2_ref, o_ref):
    f = pl.program_id(1)

    @pl.when(f == 0)
    def _():
        o_ref[...] = x1_ref[...]

    u = jnp.maximum(jnp.dot(h2_ref[...], w1_ref[...], preferred_element_type=F32), 0.0)
    o_ref[...] += jnp.dot((u * u).astype(BF16), w2_ref[...], preferred_element_type=F32)


def _ffn(x1, h2, w1, w2, tm, tf):
    n = x1.shape[0]
    return pl.pallas_call(
        _ffn_body,
        grid=(n // tm, D_FF // tf),
        in_specs=[
            pl.BlockSpec((tm, D_MODEL), lambda i, f: (i, 0)),
            pl.BlockSpec((tm, D_MODEL), lambda i, f: (i, 0)),
            pl.BlockSpec((D_MODEL, tf), lambda i, f: (0, f)),
            pl.BlockSpec((tf, D_MODEL), lambda i, f: (f, 0)),
        ],
        out_specs=pl.BlockSpec((tm, D_MODEL), lambda i, f: (i, 0)),
        out_shape=jax.ShapeDtypeStruct((n, D_MODEL), F32),
        compiler_params=_cparams(("parallel", "arbitrary")),
        name="ffn",
    )(x1, h2, w1, w2)


def _prep_weights(w_in, w_o_a, w_o_b, w_out, w_ff1, w_ff2):
    sizes = (H_A * 2 * HD_A, H_A * 2 * HD_A, H_A * DV_A, H_B * HD_B, HKV_B * HD_B, HKV_B * HD_B,
             H_IDX * D_IDX, D_IDX, H_IDX, D_MODEL, D_MODEL)
    offs = [0]
    for s in sizes:
        offs.append(offs[-1] + s)
    col = lambda a, b: w_in[:, offs[a]:offs[b]]
    w_misc = jnp.concatenate(
        [col(4, 6), col(7, 8), col(7, 8), col(8, 9),
         jnp.zeros((D_MODEL, _MISC_COLS - _WI0 - H_IDX), w_in.dtype)], axis=1)
    return dict(
        qa=col(0, 1).astype(BF16), ka=col(1, 2).astype(BF16), va=col(2, 3).astype(BF16),
        qb=col(3, 4).astype(BF16), misc=w_misc.astype(BF16), qi=col(6, 7).astype(BF16),
        gate=col(9, 11).astype(BF16), oa=w_o_a.astype(BF16), ob=w_o_b.astype(BF16),
        out=w_out.astype(BF16), ff1=w_ff1.astype(BF16), ff2=w_ff2.astype(BF16))


def _layer(x, past, w, p, lam, lam_init, tab_t, *, tm, t_a=None, t_b=None):
    short = t_a is None
    b, t, _ = x.shape
    n = b * t
    xf = x.reshape(n, D_MODEL)
    h = _rmsnorm(xf, p["norm1_g"], tm)

    gain_a = lambda g: g.reshape(1, 2 * HD_A)
    v_layout = HEADS_T if (not short and past is None and b == 1) else FLAT
    (qa,) = _proj(h, w["qa"], [gain_a(p["qn_a_g"])], _epi_qa, [(1024, BF16, FLAT)], tm, "proj_qa")
    ka, ka_h = _proj(h, w["ka"], [gain_a(p["kn_a_g"])], _epi_ka, [(1024, F32, HEADS), (1024, BF16, FLAT)],
                     tm, "proj_ka")
    va, va_h = _proj(h, w["va"], [], _epi_copy2, [(1024, F32, HEADS), (1024, BF16, v_layout)], tm, "proj_va")
    (qb,) = _proj(h, w["qb"], [p["qn_b_g"].reshape(1, HD_B)], _epi_qb, [(1024, BF16, FLAT)], tm, "proj_qb")
    kb, kb_h, vb, vb_h, ki, ki2_h, wi = _proj(
        h, w["misc"], [p["kn_b_g"].reshape(1, HD_B)], _epi_misc,
        [(256, F32, HEADS), (256, BF16, FLAT), (256, F32, HEADS), (256, BF16, v_layout), (D_IDX, F32, FLAT),
         (2 * D_IDX, BF16, FLAT), (H_IDX, F32, FLAT)],
        tm, "proj_misc")
    (qi,) = _proj(h, w["qi"], [], _epi_copy1, [(1024, BF16, FLAT)], tm, "proj_qi")

    new_rows = (ka.reshape(b, t, H_A, 2 * HD_A), va.reshape(b, t, H_A, DV_A), kb.reshape(b, t, HKV_B, HD_B),
                vb.reshape(b, t, HKV_B, HD_B), ki.reshape(b, t, D_IDX))

    per_b = lambda a: a.reshape(b, t, -1)
    if short:
        n_keys = past[0].shape[1] + t
        topk = min(TOPK_MAX, n_keys // 4)
        oa = _short_a(per_b(qa), past[0], past[1], new_rows[0], new_rows[1],
                      _bias_dense(tab_t[:H_A], t, n_keys, n_keys - t, n_keys),
                      p["subln_a_g"].reshape(1, DV_A), lam, out_scale=1.0 - lam_init)
        ob = _short_dsa(per_b(qb), per_b(qi), per_b(wi), past[2], past[3], past[4], *new_rows[2:],
                        _bias_dense(tab_t[H_A:], t, n_keys, n_keys - t, n_keys), topk=topk)
    else:
        oa, ob = _tiled_mixers(qa, qb, qi, wi, ka_h, va_h, kb_h, vb_h, ki2_h, past, tab_t, p, lam, lam_init,
                               b=b, t=t, t_a=t_a, t_b=t_b)
    oa = oa.reshape(n, H_A * DV_A)
    ob = ob.reshape(n, H_B * HD_B)

    merged = _merge(h, oa, ob, w["gate"], w["oa"], w["ob"], min(tm, 256))
    x1, h2 = _outproj(xf, merged, w["out"], p["norm2_g"], min(tm, 256))
    y = _ffn(x1, h2, w["ff1"], w["ff2"], tm, 1024)
    return y.reshape(b, t, D_MODEL), new_rows


def _tiled_mixers(qa, qb, qi, wi, ka_h, va_h, kb_h, vb_h, ki2_h, past, tab_t, p, lam, lam_init,
                  *, b, t, t_a, t_b):
    def full_keys(new_h, past_arr):
        if new_h.ndim == 3:
            return new_h[None]
        new_h = new_h.reshape(b, t, -1)
        if past_arr is None:
            return new_h
        past_h = past_arr.reshape(b, past_arr.shape[1], -1).astype(BF16)
        return jnp.concatenate([past_h, new_h], axis=1)

    if past is None:
        pa_k = pa_v = pb_k = pb_v = pb_i = None
    else:
        pa_k, pa_v, pb_k, pb_v, pb_i = past
        pb_i = jnp.concatenate([pb_i, pb_i], axis=-1)
    ka_f, va_f = full_keys(ka_h, pa_k), full_keys(va_h, pa_v)
    kb_f, vb_f, ki_f = full_keys(kb_h, pb_k), full_keys(vb_h, pb_v), full_keys(ki2_h, pb_i)
    n_keys = ka_f.shape[1]
    q0 = n_keys - t
    topk = min(TOPK_MAX, n_keys // 4)
    t_pad = max(t_a, t_b)
    assert t_pad % t_a == 0 and t_pad % t_b == 0 and q0 % t_pad == 0
    nk_pad = -(-n_keys // t_pad) * t_pad
    pad_k = lambda a: jnp.pad(a, ((0, 0), (0, nk_pad - n_keys), (0, 0)))
    pad_q = lambda a, tt: jnp.pad(a.reshape(b, t, -1), ((0, 0), (0, -(-t // tt) * tt - t), (0, 0)))
    ka_f, kb_f, ki_f = (pad_k(a) for a in (ka_f, kb_f, ki_f))
    n_valid = lambda tt: n_keys - (n_keys - 1) // tt * tt
    assert all(t <= tt or n_valid(tt) == tt for tt in (t_a, t_b))

    def heads_t(v, nh):
        if v.ndim == 3:
            v = jnp.transpose(v.reshape(b, n_keys, nh, -1), (0, 2, 3, 1))
        return jnp.pad(v, ((0, 0), (0, 0), (0, 0), (0, nk_pad - n_keys)))

    bias_a = _bias_tiles(tab_t[:H_A], t_a, n_valid(t_a))
    oa = _attn_a(pad_q(qa, t_a), ka_f, heads_t(va_f, H_A), bias_a, p["subln_a_g"].reshape(DV_A, 1), lam,
                 t=t_a, q0=q0, out_scale=1.0 - lam_init)
    wi_t = jnp.swapaxes(pad_q(wi, t_b), 1, 2)
    bias_b = _bias_tiles(tab_t[H_A:], t_b, n_valid(t_b))
    ob = _dsa(pad_q(qb, t_b), pad_q(qi, t_b), wi_t, kb_f, heads_t(vb_f, HKV_B), ki_f, bias_b,
              t=t_b, q0=q0, topk=topk, n_keys=n_keys, n_valid_diag=n_valid(t_b))
    return oa[:, :t], ob[:, :t]


def kernel(x_prompt, x_sample, cache_a_k, cache_a_v, cache_b_k, cache_b_v, cache_b_kidx, rel_bias, norm1_g, w_in, qn_a_g, kn_a_g, lam_q1, lam_k1, lam_q2, lam_k2, subln_a_g, qn_b_g, kn_b_g, w_o_a, w_o_b, w_out, norm2_g, w_ff1, w_ff2):
    depth = w_in.shape[0]
    tab_t = rel_bias.T.astype(F32)
    y_prompt, y_sample = x_prompt, x_sample
    prompt_rows, sample_rows = [], []
    for l in range(depth):
        lam_init = 0.8 - 0.6 * math.exp(-0.3 * l)
        lam = (jnp.exp(jnp.sum(lam_q1[l].astype(F32) * lam_k1[l].astype(F32)))
               - jnp.exp(jnp.sum(lam_q2[l].astype(F32) * lam_k2[l].astype(F32))) + lam_init).reshape(1)
        w = _prep_weights(w_in[l], w_o_a[l], w_o_b[l], w_out[l], w_ff1[l], w_ff2[l])
        p = dict(norm1_g=norm1_g[l], qn_a_g=qn_a_g[l], kn_a_g=kn_a_g[l], subln_a_g=subln_a_g[l],
                 qn_b_g=qn_b_g[l], kn_b_g=kn_b_g[l], norm2_g=norm2_g[l])
        y_prompt, rp = _layer(y_prompt, None, w, p, lam, lam_init, tab_t, t_a=512, t_b=256, tm=512)
        past = (cache_a_k[l], cache_a_v[l], cache_b_k[l], cache_b_v[l], cache_b_kidx[l])
        y_sample, rs = _layer(y_sample, past, w, p, lam, lam_init, tab_t, tm=512)
        prompt_rows.append(rp)
        sample_rows.append(rs)
    p_rows = tuple(jnp.stack(r, axis=0) for r in zip(*prompt_rows))
    s_rows = tuple(jnp.stack(r, axis=0) for r in zip(*sample_rows))
    return (y_prompt, y_sample) + p_rows + s_rows
```

```python
import functools
import math

import jax
import jax.numpy as jnp
from jax import lax
from jax.experimental import pallas as pl
from jax.experimental.pallas import tpu as pltpu

F32 = jnp.float32
BF16 = jnp.bfloat16
I32 = jnp.int32

D_MODEL = 2048
CHUNK = 64
H_A = 8
DV_A = 128
HD_A = 64
H_B = 8
HD_B = 128
HKV_B = 2
G_B = H_B // HKV_B
H_IDX = 16
D_IDX = 64
TOPK_MAX = 256
N_BUCKETS = 32
D_FF = 4 * D_MODEL
EPS = 1e-6
LANES = 128
NEG = -1e30
LOG2E = math.log2(math.e)
INT_MIN = -(2 ** 31)
INT_MAX = 2 ** 31 - 1
VMEM_LIMIT = 56 * 1024 * 1024

_NT = (((1,), (1,)), ((), ()))


def _cparams(sem):
    return pltpu.CompilerParams(dimension_semantics=sem, vmem_limit_bytes=VMEM_LIMIT)


def _resident(block_shape, index_map):
    return pl.BlockSpec(block_shape, index_map, pipeline_mode=pl.Buffered(1))


def _rmsnorm_body(x_ref, g_ref, o_ref):
    x = x_ref[...]
    y = x * lax.rsqrt(jnp.mean(x * x, axis=-1, keepdims=True) + EPS)
    o_ref[...] = (y * g_ref[...]).astype(o_ref.dtype)


def _rmsnorm(x, g, tm):
    n, d = x.shape
    return pl.pallas_call(
        _rmsnorm_body,
        grid=(n // tm,),
        in_specs=[pl.BlockSpec((tm, d), lambda i: (i, 0)), pl.BlockSpec((1, d), lambda i: (0, 0))],
        out_specs=pl.BlockSpec((tm, d), lambda i: (i, 0)),
        out_shape=jax.ShapeDtypeStruct((n, d), BF16),
        compiler_params=_cparams(("parallel",)),
        name="rmsnorm",
    )(x, g.reshape(1, d))


def _head_norm(z, gain, split):
    lane = lax.broadcasted_iota(I32, (1, LANES), 1)
    lo = lane < (LANES // 2)
    outs = []
    for h in range(z.shape[1] // LANES):
        zh = z[:, h * LANES:(h + 1) * LANES]
        sq = zh * zh
        if split:
            s_lo = jnp.sum(jnp.where(lo, sq, 0.0), axis=-1, keepdims=True)
            s_hi = jnp.sum(jnp.where(lo, 0.0, sq), axis=-1, keepdims=True)
            r = jnp.where(lo, lax.rsqrt(s_lo * (2.0 / LANES) + EPS), lax.rsqrt(s_hi * (2.0 / LANES) + EPS))
        else:
            r = lax.rsqrt(jnp.mean(sq, axis=-1, keepdims=True) + EPS)
        outs.append(zh * r * gain)
    return jnp.concatenate(outs, axis=-1)


FLAT, HEADS, HEADS_T = "tokens x cols", "tokens x heads x 128", "heads x 128 x tokens"


def _proj_body(epilogue, n_aux, layouts, h_ref, w_ref, *rest):
    z = jnp.dot(h_ref[...], w_ref[...], preferred_element_type=F32)
    outs = epilogue(z, *[r[...] for r in rest[:n_aux]])
    for o_ref, o, layout in zip(rest[n_aux:], outs, layouts):
        if layout == FLAT:
            o_ref[...] = o.astype(o_ref.dtype)
            continue
        for hd in range(o.shape[1] // LANES):
            slab = o[:, hd * LANES:(hd + 1) * LANES]
            if layout == HEADS:
                o_ref[:, hd, :] = slab.astype(o_ref.dtype)
            else:
                o_ref[hd] = slab.T.astype(o_ref.dtype)


def _proj(h, w, aux, epilogue, out_defs, tm, name):
    n, k = h.shape
    c = w.shape[1]
    in_specs = [pl.BlockSpec((tm, k), lambda i: (i, 0)), _resident((k, c), lambda i: (0, 0))]
    in_specs += [pl.BlockSpec(a.shape, lambda i: (0, 0)) for a in aux]
    out_specs, out_shape = [], []
    for oc, dt, layout in out_defs:
        nh = oc // LANES
        if layout == FLAT:
            out_specs.append(pl.BlockSpec((tm, oc), lambda i: (i, 0)))
            out_shape.append(jax.ShapeDtypeStruct((n, oc), dt))
        elif layout == HEADS:
            out_specs.append(pl.BlockSpec((tm, nh, LANES), lambda i: (i, 0, 0)))
            out_shape.append(jax.ShapeDtypeStruct((n, nh, LANES), dt))
        else:
            out_specs.append(pl.BlockSpec((nh, LANES, tm), lambda i: (0, 0, i)))
            out_shape.append(jax.ShapeDtypeStruct((nh, LANES, n), dt))
    return pl.pallas_call(
        functools.partial(_proj_body, epilogue, len(aux), tuple(d[2] for d in out_defs)),
        grid=(n // tm,),
        in_specs=in_specs,
        out_specs=out_specs,
        out_shape=out_shape,
        compiler_params=_cparams(("parallel",)),
        name=name,
    )(h, w, *aux)


def _epi_qa(z, gain):
    return (_head_norm(z, gain, True) * (HD_A ** -0.5 * LOG2E),)


def _epi_ka(z, gain):
    y = _head_norm(z, gain, True)
    return y, y


def _epi_copy2(z):
    return z, z


def _epi_qb(z, gain):
    return (_head_norm(z, gain, False) * (HD_B ** -0.5 * LOG2E),)


def _epi_copy1(z):
    return (z,)


_KB0, _VB0, _KI0, _WI0, _MISC_COLS = 0, 256, 512, 640, 768


def _epi_misc(z, gain):
    kb = _head_norm(z[:, _KB0:_VB0], gain, False)
    vb = z[:, _VB0:_KI0]
    ki2 = z[:, _KI0:_WI0]
    return kb, kb, vb, vb, ki2[:, :D_IDX], ki2, z[:, _WI0:_WI0 + H_IDX]


def _rel_bias(tab_ref, h, rel):
    half = N_BUCKETS // 2
    exact = half // 2
    n = jnp.abs(rel)
    n2 = n * n
    v_neg = jnp.full(rel.shape, tab_ref[h, 0], F32)
    v_pos = jnp.full(rel.shape, tab_ref[h, half], F32)
    for b in range(1, half):
        cond = (n >= b) if b < exact else (n2 >= exact * exact * 2 ** (b - exact))
        v_neg = jnp.where(cond, tab_ref[h, b], v_neg)
        v_pos = jnp.where(cond, tab_ref[h, half + b], v_pos)
    return (jnp.where(rel > 0, v_pos, v_neg) - tab_ref[h, half - 1]) * LOG2E


def _bias_body(t, n_valid_diag, tab_ref, o_ref):
    h = pl.program_id(0)
    kl = lax.broadcasted_iota(I32, (t, t), 0)
    ql = lax.broadcasted_iota(I32, (t, t), 1)
    o_ref[0] = _rel_bias(tab_ref, h, kl - t - ql)
    visible = ((kl // CHUNK) <= (ql // CHUNK)) & (kl < n_valid_diag)
    o_ref[1] = jnp.where(visible, _rel_bias(tab_ref, h, kl - ql), NEG)


def _bias_dense_body(q0, n_keys, tab_ref, o_ref):
    h = pl.program_id(0)
    qpos = q0 + lax.broadcasted_iota(I32, o_ref.shape, 0)
    kpos = lax.broadcasted_iota(I32, o_ref.shape, 1)
    visible = ((kpos // CHUNK) <= (qpos // CHUNK)) & (kpos < n_keys)
    o_ref[...] = jnp.where(visible, _rel_bias(tab_ref, h, kpos - qpos), NEG)


def _bias_dense(tab_t, nq, nk_pad, q0, n_keys):
    nh = tab_t.shape[0]
    return pl.pallas_call(
        functools.partial(_bias_dense_body, q0, n_keys),
        grid=(nh,),
        in_specs=[pl.BlockSpec(memory_space=pltpu.SMEM)],
        out_specs=pl.BlockSpec((None, nq, nk_pad), lambda h: (h, 0, 0)),
        out_shape=jax.ShapeDtypeStruct((nh, nq, nk_pad), F32),
        compiler_params=_cparams(("arbitrary",)),
        name="bias_dense",
    )(tab_t)


def _bias_tiles(tab_t, t, n_valid_diag):
    nh = tab_t.shape[0]
    return pl.pallas_call(
        functools.partial(_bias_body, t, n_valid_diag),
        grid=(nh,),
        in_specs=[pl.BlockSpec(memory_space=pltpu.SMEM)],
        out_specs=pl.BlockSpec((None, 2, t, t), lambda h: (h, 0, 0, 0)),
        out_shape=jax.ShapeDtypeStruct((nh, 2, t, t), F32),
        compiler_params=_cparams(("arbitrary",)),
        name="bias_tiles",
    )(tab_t)


ONES_ROWS = 16


def _softmax_step(s_ref, vt, m_ref, acc_ref, width):
    vt_ones = jnp.concatenate([vt, jnp.ones((ONES_ROWS, vt.shape[1]), vt.dtype)], axis=0)
    for c0 in range(0, s_ref.shape[1], width):
        cols = slice(c0, c0 + width)
        s = s_ref[:, cols]
        m_old = m_ref[:, cols]
        m_new = jnp.maximum(m_old, jnp.max(s, axis=0, keepdims=True))
        alpha = jnp.exp2(m_old - m_new)
        p = jnp.exp2(s - m_new)
        acc_ref[:, cols] = acc_ref[:, cols] * alpha + jnp.dot(vt_ones, p.astype(BF16), preferred_element_type=F32)
        m_ref[:, cols] = m_new


def _normalised(acc, dv):
    return acc[:dv] / acc[dv:dv + 1]


def _pipelined_tiles(n_far, produce, consume, buf0, buf1):
    @pl.when(n_far == -1)
    def _():
        produce(0, buf0, 1)
        consume(0, buf0)

    @pl.when(n_far == 0)
    def _():
        produce(0, buf0, 0)
        produce(1, buf1, 1)
        consume(0, buf0)
        consume(1, buf1)

    @pl.when(n_far >= 1)
    def _():
        produce(0, buf0, None)

    pairs = jnp.maximum(n_far - 1, 0) // 2

    def pair(j):
        produce(j + 1, buf1, None)
        consume(j, buf0)
        produce(j + 2, buf0, None)
        consume(j + 1, buf1)

    def body(i, carry):
        for u in range(4):
            pair(8 * i + 2 * u)
        return carry

    def leftover(q, carry):
        pair(2 * q)
        return carry

    lax.fori_loop(0, pairs // 4, body, 0)
    lax.fori_loop(pairs // 4 * 4, pairs, leftover, 0)
    d = 2 * pairs
    rem = n_far - d

    @pl.when(rem == 1)
    def _():
        produce(d + 1, buf1, 0)
        consume(d, buf0)
        produce(d + 2, buf0, 1)
        consume(d + 1, buf1)
        consume(d + 2, buf0)

    @pl.when(rem == 2)
    def _():
        produce(d + 1, buf1, None)
        consume(d, buf0)
        produce(d + 2, buf0, 0)
        consume(d + 1, buf1)
        produce(d + 3, buf1, 1)
        consume(d + 2, buf0)
        consume(d + 3, buf1)


def _attn_a_body(t, q0, out_scale, lam_ref, q_ref, k_ref, vt_ref, bias_ref, g_ref, o_ref,
                 qz_ref, m_ref, acc_ref, s0_ref, s1_ref):
    i = pl.program_id(2)
    n_far = q0 // t + i - 1
    q = q_ref[...]
    lane = lax.broadcasted_iota(I32, (t, LANES), 1)
    zero = jnp.zeros_like(q)
    qz_ref[:, :t] = jnp.where(lane < HD_A, q, zero).T
    qz_ref[:, t:] = jnp.where(lane < HD_A, zero, q).T
    m_ref[...] = jnp.full(m_ref.shape, NEG, F32)
    acc_ref[...] = jnp.zeros(acc_ref.shape, F32)

    def produce(j, s_ref, kind):
        ks = pl.multiple_of(j * t, t)
        s = jnp.dot(k_ref[pl.ds(ks, t), :], qz_ref[...], preferred_element_type=F32)
        if kind is not None:
            b = bias_ref[kind]
            s = jnp.concatenate([s[:, :t] + b, s[:, t:] + b], axis=1)
        s_ref[...] = s

    def consume(j, s_ref):
        _softmax_step(s_ref, vt_ref[:, pl.ds(pl.multiple_of(j * t, t), t)], m_ref, acc_ref, t)

    _pipelined_tiles(n_far, produce, consume, s0_ref, s1_ref)

    lam = lam_ref[0]
    o = _normalised(acc_ref[...], DV_A)
    o = o[:, :t] - lam * o[:, t:]
    y = o * lax.rsqrt(jnp.mean(o * o, axis=0, keepdims=True) + EPS)
    y = (y * g_ref[...]) * out_scale
    o_ref[...] = y.T.astype(o_ref.dtype)


def _attn_a(q, k, vt, bias, subln_g, lam, *, t, q0, out_scale):
    b, nq, _ = q.shape
    nk = k.shape[1]
    return pl.pallas_call(
        functools.partial(_attn_a_body, t, q0, out_scale),
        grid=(b, H_A, nq // t),
        in_specs=[
            pl.BlockSpec(memory_space=pltpu.SMEM),
            pl.BlockSpec((None, t, LANES), lambda bb, h, i: (bb, i, h)),
            pl.BlockSpec((None, nk, LANES), lambda bb, h, i: (bb, 0, h)),
            pl.BlockSpec((None, None, DV_A, nk), lambda bb, h, i: (bb, h, 0, 0)),
            pl.BlockSpec((None, 2, t, t), lambda bb, h, i: (h, 0, 0, 0)),
            pl.BlockSpec((DV_A, 1), lambda bb, h, i: (0, 0)),
        ],
        out_specs=pl.BlockSpec((None, t, LANES), lambda bb, h, i: (bb, i, h)),
        out_shape=jax.ShapeDtypeStruct((b, nq, H_A * DV_A), BF16),
        scratch_shapes=[
            pltpu.VMEM((LANES, 2 * t), BF16),
            pltpu.VMEM((1, 2 * t), F32),
            pltpu.VMEM((DV_A + ONES_ROWS, 2 * t), F32),
            pltpu.VMEM((t, 2 * t), F32),
            pltpu.VMEM((t, 2 * t), F32),
        ],
        compiler_params=_cparams(("parallel", "parallel", "arbitrary")),
        name="diff_attn",
    )(lam, q, k, vt, bias, subln_g)


GROUP = 4


def _order_key(x):
    b = pltpu.bitcast(x, I32)
    return jnp.where(b < 0, b ^ INT_MAX, b)


def _order_unkey(k):
    return pltpu.bitcast(jnp.where(k < 0, k ^ INT_MAX, k), F32)


def _dsa_body(t, q0, topk, n_keys, n_valid_diag, qb_ref, qi_ref, wi_ref, kb_ref, vbt_ref, ki_ref, bias_ref, o_ref,
              keys_ref, qz_ref, qs_ref, mm_ref, m_ref, acc_ref, s0_ref, s1_ref):
    i = pl.program_id(1)
    n_tiles = q0 // t + i + 1
    n_far = n_tiles - 2

    lane = lax.broadcasted_iota(I32, (t, LANES), 1)
    for hp in range(H_IDX // 2):
        qs = qi_ref[:, hp * LANES:(hp + 1) * LANES]
        zero = jnp.zeros_like(qs)
        qz_ref[2 * hp] = jnp.where(lane < D_IDX, qs, zero).T
        qz_ref[2 * hp + 1] = jnp.where(lane < D_IDX, zero, qs).T

    def score_tile(j):
        kt = ki_ref[pl.ds(pl.multiple_of(j * t, t), t), :]
        acc = jnp.zeros((t, t), F32)
        for h in range(H_IDX):
            s = jnp.dot(kt, qz_ref[h], preferred_element_type=F32)
            acc = acc + jnp.maximum(s, 0.0) * wi_ref[h:h + 1, :]
        return acc

    def fold8(x, op):
        return op(x.reshape(x.shape[0] // 8, 8, t), axis=0)

    mm_ref[0] = jnp.full((8, t), jnp.inf, F32)
    mm_ref[1] = jnp.full((8, t), -jnp.inf, F32)

    def score_store(j):
        sc = score_tile(j)
        keys_ref[pl.ds(pl.multiple_of(j * t, t), t), :] = _order_key(sc)
        mm_ref[0] = jnp.minimum(mm_ref[0], fold8(sc, jnp.min))
        mm_ref[1] = jnp.maximum(mm_ref[1], fold8(sc, jnp.max))

    def score_group(jj, carry):
        for u in range(GROUP):
            score_store(GROUP * jj + u)
        return carry

    def score_single(j, carry):
        score_store(j)
        return carry

    jd = n_tiles - 1
    lax.fori_loop(0, jd // GROUP, score_group, 0)
    lax.fori_loop(jd // GROUP * GROUP, jd, score_single, 0)

    kl = lax.broadcasted_iota(I32, (t, t), 0)
    ql = lax.broadcasted_iota(I32, (t, t), 1)
    admissible = ((kl // CHUNK) <= (ql // CHUNK)) & (kl < n_valid_diag)
    sc = score_tile(jd)
    keys_ref[pl.ds(pl.multiple_of(jd * t, t), t), :] = jnp.where(admissible, _order_key(sc), INT_MIN)
    smin = jnp.minimum(mm_ref[0], fold8(jnp.where(admissible, sc, jnp.inf), jnp.min))
    smax = jnp.maximum(mm_ref[1], fold8(jnp.where(admissible, sc, -jnp.inf), jnp.max))
    for u in range(GROUP - 1):
        keys_ref[pl.ds(pl.multiple_of((n_tiles + u) * t, t), t), :] = jnp.full((t, t), INT_MIN, I32)
    n_groups = (n_tiles + GROUP - 1) // GROUP
    gt = GROUP * t

    def count_groups(hit_fn):
        def body(j, c):
            r0 = pl.multiple_of(j * gt, gt)
            hit = hit_fn(keys_ref[pl.ds(r0, gt), :], r0)
            return c + jnp.sum(hit.reshape(gt // 32, 32, t), axis=0)
        part = lax.fori_loop(0, n_groups, body, jnp.zeros((32, t), F32))
        return jnp.sum(part, axis=0, keepdims=True)

    def search_cond(state):
        return jnp.logical_and(state[0] < 4 * 34, state[-1] > 0)

    def search_body(state):
        it, lo, hi, c_lo, c_hi, _ = state
        f_lo, f_hi = _order_unkey(lo), _order_unkey(hi)
        a, b = jnp.log(c_lo + 0.5), jnp.log(c_hi + 0.5)
        frac = jnp.clip((a - math.log(topk)) / jnp.maximum(a - b, 1e-9), 0.0, 1.0)
        frac = jnp.where(it == 0, frac0, frac)
        guess = jnp.minimum(jnp.maximum(_order_key(f_lo + (f_hi - f_lo) * frac), lo + 1), hi - 1)
        mid = jnp.where(it % 4 == 3, (lo >> 1) + (hi >> 1) + (lo & hi & 1), guess)
        c = count_groups(lambda kk, r0: jnp.where(kk >= mid, 1.0, 0.0))
        live = lo + 1 < hi
        ge = c >= topk
        up = jnp.logical_and(live, ge)
        down = jnp.logical_and(live, jnp.logical_not(ge))
        lo = jnp.where(up, mid, lo)
        c_lo = jnp.where(up, c, c_lo)
        hi = jnp.where(jnp.logical_and(live, c == topk), mid + 1, jnp.where(down, mid, hi))
        c_hi = jnp.where(down, c, c_hi)
        open_ = jnp.logical_and(lo + 1 < hi, c_hi != topk - 1)
        return it + 1, lo, hi, c_lo, c_hi, jnp.max(jnp.where(open_, 1, 0))

    qpos = q0 + i * t + lax.broadcasted_iota(I32, (1, t), 1)
    n_adm = jnp.minimum((qpos // CHUNK + 1) * CHUNK, n_keys).astype(F32)
    lo0 = _order_key(jnp.min(smin, axis=0, keepdims=True))
    hi0 = jnp.where(n_adm <= topk, lo0 + 1, _order_key(jnp.max(smax, axis=0, keepdims=True)) + 1)

    def upper_quantile(p):
        u = jnp.sqrt(-2.0 * jnp.log(p))
        return u - ((0.010328 * u + 0.802853) * u + 2.515517) / (((0.001308 * u + 0.189269) * u + 1.432788) * u + 1.0)

    frac0 = 0.5 + upper_quantile(jnp.clip(topk / n_adm, 1e-6, 0.5)) / (2.0 * upper_quantile(1.0 / n_adm))
    frac0 = jnp.clip(frac0, 0.0, 1.0)
    _, lo, hi, c_lo, c_above, _ = lax.while_loop(
        search_cond, search_body, (jnp.int32(0), lo0, hi0, n_adm, jnp.zeros((1, t), F32), jnp.int32(1)))

    def below_hi_max(j, m):
        kk = keys_ref[pl.ds(pl.multiple_of(j * gt, gt), gt), :]
        return jnp.maximum(m, jnp.max(jnp.where(kk < hi, kk, INT_MIN).reshape(gt // 8, 8, t), axis=0))

    m8 = lax.fori_loop(0, n_groups, below_hi_max, jnp.full((8, t), INT_MIN, I32))
    for shift in (4, 2, 1):
        m8 = jnp.maximum(m8, pltpu.roll(m8, shift, axis=0))
    wide = lo + 1 < hi
    thr = jnp.where(wide, m8[:1], lo)
    c_eq = count_groups(lambda kk, r0: jnp.where(kk == thr, 1.0, 0.0))
    c_thr = jnp.where(wide, c_above + c_eq, c_lo)

    @pl.when(jnp.max(jnp.where(jnp.logical_and(c_thr > topk, qpos < n_keys), 1, 0)) > 0)
    def _():
        need = topk - c_above
        row = lax.broadcasted_iota(I32, (gt, t), 0)

        def cut_body(_, st):
            p_lo, p_hi = st
            p_mid = (p_lo + p_hi) >> 1
            c = count_groups(lambda kk, r0: jnp.where(
                jnp.logical_and(kk == thr, row + r0 <= p_mid), 1.0, 0.0))
            ok = c >= need
            return jnp.where(ok, p_lo, p_mid), jnp.where(ok, p_mid, p_hi)

        n_pos = keys_ref.shape[0]
        _, cut = lax.fori_loop(0, n_pos.bit_length(), cut_body,
                               (jnp.full((1, t), -1, I32), jnp.full((1, t), n_pos - 1, I32)))

        def demote_body(j, carry):
            r0 = pl.multiple_of(j * gt, gt)
            kk = keys_ref[pl.ds(r0, gt), :]
            keys_ref[pl.ds(r0, gt), :] = jnp.where(jnp.logical_and(kk == thr, row + r0 > cut), thr - 1, kk)
            return carry

        lax.fori_loop(0, n_groups, demote_body, 0)

    m_ref[...] = jnp.full(m_ref.shape, NEG, F32)
    acc_ref[...] = jnp.zeros(acc_ref.shape, F32)

    for h in range(H_B):
        qs_ref[h // G_B, :, (h % G_B) * t:(h % G_B + 1) * t] = qb_ref[:, h * HD_B:(h + 1) * HD_B].T

    def produce(j, s_ref, kind):
        ks = pl.multiple_of(j * t, t)
        off = jnp.where(keys_ref[pl.ds(ks, t), :] >= thr, 0.0, NEG)
        for n in range(HKV_B):
            s = jnp.dot(kb_ref[pl.ds(ks, t), n * HD_B:(n + 1) * HD_B], qs_ref[n], preferred_element_type=F32)
            parts = []
            for g in range(G_B):
                off_g = off if kind is None else off + bias_ref[n * G_B + g, kind]
                parts.append(s[:, g * t:(g + 1) * t] + off_g)
            s_ref[n] = jnp.concatenate(parts, axis=1)

    def consume(j, s_ref):
        ks = pl.multiple_of(j * t, t)
        for n in range(HKV_B):
            _softmax_step(s_ref.at[n], vbt_ref[n, :, pl.ds(ks, t)], m_ref.at[n], acc_ref.at[n], 2 * t)

    _pipelined_tiles(n_far, produce, consume, s0_ref, s1_ref)

    for n in range(HKV_B):
        o = _normalised(acc_ref[n], HD_B)
        for g in range(G_B):
            h = n * G_B + g
            o_ref[:, h * HD_B:(h + 1) * HD_B] = o[:, g * t:(g + 1) * t].T.astype(o_ref.dtype)


def _dsa(qb, qi, wi_t, kb, vbt, ki2, bias, *, t, q0, topk, n_keys, n_valid_diag):
    b, nq, _ = qb.shape
    nk = kb.shape[1]
    return pl.pallas_call(
        functools.partial(_dsa_body, t, q0, topk, n_keys, n_valid_diag),
        grid=(b, nq // t),
        in_specs=[
            pl.BlockSpec((None, t, H_B * HD_B), lambda bb, i: (bb, i, 0)),
            pl.BlockSpec((None, t, H_IDX * D_IDX), lambda bb, i: (bb, i, 0)),
            pl.BlockSpec((None, H_IDX, t), lambda bb, i: (bb, 0, i)),
            _resident((None, nk, HKV_B * HD_B), lambda bb, i: (bb, 0, 0)),
            _resident((None, HKV_B, HD_B, nk), lambda bb, i: (bb, 0, 0, 0)),
            _resident((None, nk, LANES), lambda bb, i: (bb, 0, 0)),
            _resident((H_B, 2, t, t), lambda bb, i: (0, 0, 0, 0)),
        ],
        out_specs=pl.BlockSpec((None, t, H_B * HD_B), lambda bb, i: (bb, i, 0)),
        out_shape=jax.ShapeDtypeStruct((b, nq, H_B * HD_B), BF16),
        scratch_shapes=[
            pltpu.VMEM((nk + (GROUP - 1) * t, t), I32),
            pltpu.VMEM((H_IDX, LANES, t), BF16),
            pltpu.VMEM((HKV_B, HD_B, G_B * t), BF16),
            pltpu.VMEM((2, 8, t), F32),
            pltpu.VMEM((HKV_B, 1, G_B * t), F32),
            pltpu.VMEM((HKV_B, HD_B + ONES_ROWS, G_B * t), F32),
            pltpu.VMEM((HKV_B, t, G_B * t), F32),
            pltpu.VMEM((HKV_B, t, G_B * t), F32),
        ],
        compiler_params=_cparams(("parallel", "arbitrary")),
        name="dsa",
    )(qb, qi, wi_t, kb, vbt, ki2, bias)


def _softmax_pieces(s_list, v_list):
    m = functools.reduce(jnp.maximum, [jnp.max(s, axis=-1, keepdims=True) for s in s_list])
    p_list = [jnp.exp2(s - m) for s in s_list]
    o = sum(jnp.dot(p.astype(BF16), v, preferred_element_type=F32) for p, v in zip(p_list, v_list))
    return o / sum(jnp.sum(p, axis=-1, keepdims=True) for p in p_list)


def _head_rows(ref, h, n_heads, n_pos):
    return ref[pl.ds(h, n_pos, stride=n_heads), :].astype(BF16)


def _short_a_body(nq, n_past, out_scale, lam_ref, q_ref, kp_ref, vp_ref, kn_ref, vn_ref, bias_ref, g_ref, o_ref):
    lam = lam_ref[0]
    lane = lax.broadcasted_iota(I32, (nq, LANES), 1)
    pieces = ((kp_ref, vp_ref, 0, n_past), (kn_ref, vn_ref, n_past, nq))
    for h in range(H_A):
        cols = slice(h * LANES, (h + 1) * LANES)
        q = q_ref[:, cols]
        zero = jnp.zeros_like(q)
        qz = jnp.concatenate([jnp.where(lane < HD_A, q, zero), jnp.where(lane < HD_A, zero, q)], axis=0)
        b = bias_ref[h]
        b2 = jnp.concatenate([b, b], axis=0)
        s_list = [lax.dot_general(qz, _head_rows(k_ref, h, H_A, size), _NT, preferred_element_type=F32)
                  + b2[:, lo:lo + size] for k_ref, _, lo, size in pieces]
        o = _softmax_pieces(s_list, [_head_rows(v_ref, h, H_A, size) for _, v_ref, _, size in pieces])
        o = o[:nq] - lam * o[nq:]
        y = o * lax.rsqrt(jnp.mean(o * o, axis=-1, keepdims=True) + EPS)
        o_ref[:, cols] = ((y * g_ref[...]) * out_scale).astype(o_ref.dtype)


def _short_a(q, k_past, v_past, k_new, v_new, bias, subln_g, lam, *, out_scale):
    b, nq, d = q.shape
    n_past = k_past.shape[1]
    assert n_past % LANES == 0
    rows = lambda n: pl.BlockSpec((None, n * H_A, DV_A), lambda bb: (bb, 0, 0))
    merge = lambda a: a.reshape(b, a.shape[1] * H_A, DV_A)
    return pl.pallas_call(
        functools.partial(_short_a_body, nq, n_past, out_scale),
        grid=(b,),
        in_specs=[
            pl.BlockSpec(memory_space=pltpu.SMEM),
            pl.BlockSpec((None, nq, d), lambda bb: (bb, 0, 0)),
            rows(n_past), rows(n_past), rows(nq), rows(nq),
            pl.BlockSpec(bias.shape, lambda bb: (0, 0, 0)),
            pl.BlockSpec((1, DV_A), lambda bb: (0, 0)),
        ],
        out_specs=pl.BlockSpec((None, nq, d), lambda bb: (bb, 0, 0)),
        out_shape=jax.ShapeDtypeStruct((b, nq, d), BF16),
        compiler_params=_cparams(("parallel",)),
        name="diff_attn_short",
    )(lam, q, merge(k_past), merge(v_past), merge(k_new), merge(v_new), bias, subln_g)


def _short_dsa_body(nq, n_past, topk, qb_ref, qi_ref, wi_ref, kbp_ref, vbp_ref, kip_ref, kbn_ref, vbn_ref,
                    kin_ref, bias_ref, o_ref):
    lane = lax.broadcasted_iota(I32, (nq, LANES), 1)
    spans = ((0, n_past), (n_past, nq))

    qz = []
    for h in range(H_IDX):
        qs = qi_ref[:, (h // 2) * LANES:(h // 2 + 1) * LANES]
        zero = jnp.zeros_like(qs)
        qz.append(jnp.where(lane < D_IDX, qs, zero) if h % 2 == 0 else jnp.where(lane < D_IDX, zero, qs))
    qz = jnp.concatenate(qz, axis=0)
    keys, kpos = [], []
    for ki_ref, (lo, size) in zip((kip_ref, kin_ref), spans):
        ki = ki_ref[...].astype(BF16)
        s_idx = lax.dot_general(qz, jnp.concatenate([ki, ki], axis=-1), _NT, preferred_element_type=F32)
        score = jnp.zeros((nq, size), F32)
        for h in range(H_IDX):
            score = score + jnp.maximum(s_idx[h * nq:(h + 1) * nq], 0.0) * wi_ref[:, h:h + 1]
        qp = n_past + lax.broadcasted_iota(I32, (nq, size), 0)
        kp = lo + lax.broadcasted_iota(I32, (nq, size), 1)
        keys.append(jnp.where((kp // CHUNK) <= (qp // CHUNK), _order_key(score), INT_MIN))
        kpos.append(kp)

    def count(hit_fn):
        return sum(jnp.sum(jnp.where(hit_fn(k, kp), 1.0, 0.0), axis=-1, keepdims=True)
                   for k, kp in zip(keys, kpos))

    def search_body(state):
        it, lo, hi, c_lo, c_hi, _ = state
        mid = (lo >> 1) + (hi >> 1) + (lo & hi & 1)
        c = count(lambda k, kp: k >= mid)
        live = lo + 1 < hi
        ge = c >= topk
        up = jnp.logical_and(live, ge)
        down = jnp.logical_and(live, jnp.logical_not(ge))
        lo = jnp.where(up, mid, lo)
        c_lo = jnp.where(up, c, c_lo)
        hi = jnp.where(jnp.logical_and(live, c == topk), mid + 1, jnp.where(down, mid, hi))
        c_hi = jnp.where(down, c, c_hi)
        return it + 1, lo, hi, c_lo, c_hi, jnp.max(jnp.where(lo + 1 < hi, 1, 0))

    n_keys = n_past + nq
    n_adm = count(lambda k, kp: k > INT_MIN)
    lo0 = jnp.full((nq, 1), INT_MIN + 1, I32)
    hi0 = jnp.where(n_adm <= topk, lo0 + 1, INT_MAX)
    _, thr, _, c_thr, c_above, _ = lax.while_loop(
        lambda st: jnp.logical_and(st[0] < 34, st[-1] > 0), search_body,
        (jnp.int32(0), lo0, hi0, n_adm, jnp.zeros((nq, 1), F32), jnp.int32(1)))

    need = topk - c_above

    def cut_body(_, st):
        p_lo, p_hi = st
        p_mid = (p_lo + p_hi) >> 1
        ok = count(lambda k, kp: jnp.logical_and(k == thr, kp <= p_mid)) >= need
        return jnp.where(ok, p_lo, p_mid), jnp.where(ok, p_mid, p_hi)

    _, cut = lax.fori_loop(0, n_keys.bit_length(), cut_body,
                           (jnp.full((nq, 1), -1, I32), jnp.full((nq, 1), n_keys - 1, I32)))
    off = [jnp.where(jnp.logical_and(k >= thr, jnp.logical_not(jnp.logical_and(k == thr, kp > cut))), 0.0, NEG)
           for k, kp in zip(keys, kpos)]

    for n in range(HKV_B):
        heads = range(n * G_B, (n + 1) * G_B)
        qs = jnp.concatenate([qb_ref[:, h * HD_B:(h + 1) * HD_B] for h in heads], axis=0)
        s_list = []
        for kb_ref, off_i, (lo, size) in zip((kbp_ref, kbn_ref), off, spans):
            s = lax.dot_general(qs, _head_rows(kb_ref, n, HKV_B, size), _NT, preferred_element_type=F32)
            s_list.append(s + jnp.concatenate([bias_ref[h][:, lo:lo + size] + off_i for h in heads], axis=0))
        o = _softmax_pieces(s_list, [_head_rows(vbp_ref, n, HKV_B, n_past), _head_rows(vbn_ref, n, HKV_B, nq)])
        for g, h in enumerate(heads):
            o_ref[:, h * HD_B:(h + 1) * HD_B] = o[g * nq:(g + 1) * nq].astype(o_ref.dtype)


def _short_dsa(qb, qi, wi, kb_past, vb_past, ki_past, kb_new, vb_new, ki_new, bias, *, topk):
    b, nq, d = qb.shape
    n_past = kb_past.shape[1]
    assert n_past % LANES == 0
    per_batch = lambda *shape: pl.BlockSpec((None,) + shape, lambda bb: (bb,) + (0,) * len(shape))
    merge = lambda a: a.reshape(b, a.shape[1] * HKV_B, HD_B)
    return pl.pallas_call(
        functools.partial(_short_dsa_body, nq, n_past, topk),
        grid=(b,),
        in_specs=[per_batch(nq, d), per_batch(nq, d), per_batch(nq, H_IDX),
                  per_batch(n_past * HKV_B, HD_B), per_batch(n_past * HKV_B, HD_B), per_batch(n_past, D_IDX),
                  per_batch(nq * HKV_B, HD_B), per_batch(nq * HKV_B, HD_B), per_batch(nq, D_IDX),
                  pl.BlockSpec(bias.shape, lambda bb: (0, 0, 0))],
        out_specs=per_batch(nq, d),
        out_shape=jax.ShapeDtypeStruct((b, nq, d), BF16),
        compiler_params=_cparams(("parallel",)),
        name="dsa_short",
    )(qb, qi, wi, merge(kb_past), merge(vb_past), ki_past, merge(kb_new), merge(vb_new), ki_new, bias)


def _sigmoid(x):
    return 1.0 / (1.0 + jnp.exp(-x))


def _merge_body(h_ref, oa_ref, ob_ref, wg_ref, woa_ref, wob_ref, o_ref):
    h = h_ref[...]
    ga = jnp.dot(h, wg_ref[:, :D_MODEL], preferred_element_type=F32)
    ya = jnp.dot(oa_ref[...], woa_ref[...], preferred_element_type=F32)
    merged = _sigmoid(ga) * ya
    gb = jnp.dot(h, wg_ref[:, D_MODEL:], preferred_element_type=F32)
    yb = jnp.dot(ob_ref[...], wob_ref[...], preferred_element_type=F32)
    o_ref[...] = (merged + _sigmoid(gb) * yb).astype(o_ref.dtype)


def _merge(h, oa, ob, wg, woa, wob, tm):
    n = h.shape[0]
    return pl.pallas_call(
        _merge_body,
        grid=(n // tm,),
        in_specs=[
            pl.BlockSpec((tm, D_MODEL), lambda i: (i, 0)),
            pl.BlockSpec((tm, H_A * DV_A), lambda i: (i, 0)),
            pl.BlockSpec((tm, H_B * HD_B), lambda i: (i, 0)),
            _resident((D_MODEL, 2 * D_MODEL), lambda i: (0, 0)),
            _resident((H_A * DV_A, D_MODEL), lambda i: (0, 0)),
            _resident((H_B * HD_B, D_MODEL), lambda i: (0, 0)),
        ],
        out_specs=pl.BlockSpec((tm, D_MODEL), lambda i: (i, 0)),
        out_shape=jax.ShapeDtypeStruct((n, D_MODEL), BF16),
        compiler_params=_cparams(("parallel",)),
        name="gated_merge",
    )(h, oa, ob, wg, woa, wob)


def _outproj_body(x_ref, mg_ref, w_ref, g_ref, x1_ref, h2_ref):
    x1 = x_ref[...] + jnp.dot(mg_ref[...], w_ref[...], preferred_element_type=F32)
    x1_ref[...] = x1
    y = x1 * lax.rsqrt(jnp.mean(x1 * x1, axis=-1, keepdims=True) + EPS)
    h2_ref[...] = (y * g_ref[...]).astype(h2_ref.dtype)


def _outproj(x, merged, w_out, g2, tm):
    n = x.shape[0]
    return pl.pallas_call(
        _outproj_body,
        grid=(n // tm,),
        in_specs=[
            pl.BlockSpec((tm, D_MODEL), lambda i: (i, 0)),
            pl.BlockSpec((tm, D_MODEL), lambda i: (i, 0)),
            _resident((D_MODEL, D_MODEL), lambda i: (0, 0)),
            pl.BlockSpec((1, D_MODEL), lambda i: (0, 0)),
        ],
        out_specs=[pl.BlockSpec((tm, D_MODEL), lambda i: (i, 0)), pl.BlockSpec((tm, D_MODEL), lambda i: (i, 0))],
        out_shape=[jax.ShapeDtypeStruct((n, D_MODEL), F32), jax.ShapeDtypeStruct((n, D_MODEL), BF16)],
        compiler_params=_cparams(("parallel",)),
        name="out_proj",
    )(x, merged, w_out, g2.reshape(1, D_MODEL))


def _ffn_body(x1_ref, h2_ref, w1_ref, w2_ref, o_ref):
    f = pl.program_id(1)

    @pl.when(f == 0)
    def _():
        o_ref[...] = x1_ref[...]

    u = jnp.maximum(jnp.dot(h2_ref[...], w1_ref[...], preferred_element_type=F32), 0.0)
    o_ref[...] += jnp.dot((u * u).astype(BF16), w2_ref[...], preferred_element_type=F32)


def _ffn(x1, h2, w1, w2, tm, tf):
    n = x1.shape[0]
    return pl.pallas_call(
        _ffn_body,
        grid=(n // tm, D_FF // tf),
        in_specs=[
            pl.BlockSpec((tm, D_MODEL), lambda i, f: (i, 0)),
            pl.BlockSpec((tm, D_MODEL), lambda i, f: (i, 0)),
            pl.BlockSpec((D_MODEL, tf), lambda i, f: (0, f)),
            pl.BlockSpec((tf, D_MODEL), lambda i, f: (f, 0)),
        ],
        out_specs=pl.BlockSpec((tm, D_MODEL), lambda i, f: (i, 0)),
        out_shape=jax.ShapeDtypeStruct((n, D_MODEL), F32),
        compiler_params=_cparams(("parallel", "arbitrary")),
        name="ffn",
    )(x1, h2, w1, w2)


def _prep_weights(w_in, w_o_a, w_o_b, w_out, w_ff1, w_ff2):
    sizes = (H_A * 2 * HD_A, H_A * 2 * HD_A, H_A * DV_A, H_B * HD_B, HKV_B * HD_B, HKV_B * HD_B,
             H_IDX * D_IDX, D_IDX, H_IDX, D_MODEL, D_MODEL)
    offs = [0]
    for s in sizes:
        offs.append(offs[-1] + s)
    col = lambda a, b: w_in[:, offs[a]:offs[b]]
    w_misc = jnp.concatenate(
        [col(4, 6), col(7, 8), col(7, 8), col(8, 9),
         jnp.zeros((D_MODEL, _MISC_COLS - _WI0 - H_IDX), w_in.dtype)], axis=1)
    return dict(
        qa=col(0, 1).astype(BF16), ka=col(1, 2).astype(BF16), va=col(2, 3).astype(BF16),
        qb=col(3, 4).astype(BF16), misc=w_misc.astype(BF16), qi=col(6, 7).astype(BF16),
        gate=col(9, 11).astype(BF16), oa=w_o_a.astype(BF16), ob=w_o_b.astype(BF16),
        out=w_out.astype(BF16), ff1=w_ff1.astype(BF16), ff2=w_ff2.astype(BF16))


def _layer(x, past, w, p, lam, lam_init, tab_t, *, tm, t_a=None, t_b=None):
    short = t_a is None
    b, t, _ = x.shape
    n = b * t
    xf = x.reshape(n, D_MODEL)
    h = _rmsnorm(xf, p["norm1_g"], tm)

    gain_a = lambda g: g.reshape(1, 2 * HD_A)
    v_layout = HEADS_T if (not short and past is None and b == 1) else FLAT
    (qa,) = _proj(h, w["qa"], [gain_a(p["qn_a_g"])], _epi_qa, [(1024, BF16, FLAT)], tm, "proj_qa")
    ka, ka_h = _proj(h, w["ka"], [gain_a(p["kn_a_g"])], _epi_ka, [(1024, F32, HEADS), (1024, BF16, FLAT)],
                     tm, "proj_ka")
    va, va_h = _proj(h, w["va"], [], _epi_copy2, [(1024, F32, HEADS), (1024, BF16, v_layout)], tm, "proj_va")
    (qb,) = _proj(h, w["qb"], [p["qn_b_g"].reshape(1, HD_B)], _epi_qb, [(1024, BF16, FLAT)], tm, "proj_qb")
    kb, kb_h, vb, vb_h, ki, ki2_h, wi = _proj(
        h, w["misc"], [p["kn_b_g"].reshape(1, HD_B)], _epi_misc,
        [(256, F32, HEADS), (256, BF16, FLAT), (256, F32, HEADS), (256, BF16, v_layout), (D_IDX, F32, FLAT),
         (2 * D_IDX, BF16, FLAT), (H_IDX, F32, FLAT)],
        tm, "proj_misc")
    (qi,) = _proj(h, w["qi"], [], _epi_copy1, [(1024, BF16, FLAT)], tm, "proj_qi")

    new_rows = (ka.reshape(b, t, H_A, 2 * HD_A), va.reshape(b, t, H_A, DV_A), kb.reshape(b, t, HKV_B, HD_B),
                vb.reshape(b, t, HKV_B, HD_B), ki.reshape(b, t, D_IDX))

    per_b = lambda a: a.reshape(b, t, -1)
    if short:
        n_keys = past[0].shape[1] + t
        topk = min(TOPK_MAX, n_keys // 4)
        oa = _short_a(per_b(qa), past[0], past[1], new_rows[0], new_rows[1],
                      _bias_dense(tab_t[:H_A], t, n_keys, n_keys - t, n_keys),
                      p["subln_a_g"].reshape(1, DV_A), lam, out_scale=1.0 - lam_init)
        ob = _short_dsa(per_b(qb), per_b(qi), per_b(wi), past[2], past[3], past[4], *new_rows[2:],
                        _bias_dense(tab_t[H_A:], t, n_keys, n_keys - t, n_keys), topk=topk)
    else:
        oa, ob = _tiled_mixers(qa, qb, qi, wi, ka_h, va_h, kb_h, vb_h, ki2_h, past, tab_t, p, lam, lam_init,
                               b=b, t=t, t_a=t_a, t_b=t_b)
    oa = oa.reshape(n, H_A * DV_A)
    ob = ob.reshape(n, H_B * HD_B)

    merged = _merge(h, oa, ob, w["gate"], w["oa"], w["ob"], min(tm, 256))
    x1, h2 = _outproj(xf, merged, w["out"], p["norm2_g"], min(tm, 256))
    y = _ffn(x1, h2, w["ff1"], w["ff2"], tm, 1024)
    return y.reshape(b, t, D_MODEL), new_rows


def _tiled_mixers(qa, qb, qi, wi, ka_h, va_h, kb_h, vb_h, ki2_h, past, tab_t, p, lam, lam_init,
                  *, b, t, t_a, t_b):
    def full_keys(new_h, past_arr):
        if new_h.ndim == 3:
            return new_h[None]
        new_h = new_h.reshape(b, t, -1)
        if past_arr is None:
            return new_h
        past_h = past_arr.reshape(b, past_arr.shape[1], -1).astype(BF16)
        return jnp.concatenate([past_h, new_h], axis=1)

    if past is None:
        pa_k = pa_v = pb_k = pb_v = pb_i = None
    else:
        pa_k, pa_v, pb_k, pb_v, pb_i = past
        pb_i = jnp.concatenate([pb_i, pb_i], axis=-1)
    ka_f, va_f = full_keys(ka_h, pa_k), full_keys(va_h, pa_v)
    kb_f, vb_f, ki_f = full_keys(kb_h, pb_k), full_keys(vb_h, pb_v), full_keys(ki2_h, pb_i)
    n_keys = ka_f.shape[1]
    q0 = n_keys - t
    topk = min(TOPK_MAX, n_keys // 4)
    t_pad = max(t_a, t_b)
    assert t_pad % t_a == 0 and t_pad % t_b == 0 and q0 % t_pad == 0
    nk_pad = -(-n_keys // t_pad) * t_pad
    pad_k = lambda a: jnp.pad(a, ((0, 0), (0, nk_pad - n_keys), (0, 0)))
    pad_q = lambda a, tt: jnp.pad(a.reshape(b, t, -1), ((0, 0), (0, -(-t // tt) * tt - t), (0, 0)))
    ka_f, kb_f, ki_f = (pad_k(a) for a in (ka_f, kb_f, ki_f))
    n_valid = lambda tt: n_keys - (n_keys - 1) // tt * tt
    assert all(t <= tt or n_valid(tt) == tt for tt in (t_a, t_b))

    def heads_t(v, nh):
        if v.ndim == 3:
            v = jnp.transpose(v.reshape(b, n_keys, nh, -1), (0, 2, 3, 1))
        return jnp.pad(v, ((0, 0), (0, 0), (0, 0), (0, nk_pad - n_keys)))

    bias_a = _bias_tiles(tab_t[:H_A], t_a, n_valid(t_a))
    oa = _attn_a(pad_q(qa, t_a), ka_f, heads_t(va_f, H_A), bias_a, p["subln_a_g"].reshape(DV_A, 1), lam,
                 t=t_a, q0=q0, out_scale=1.0 - lam_init)
    wi_t = jnp.swapaxes(pad_q(wi, t_b), 1, 2)
    bias_b = _bias_tiles(tab_t[H_A:], t_b, n_valid(t_b))
    ob = _dsa(pad_q(qb, t_b), pad_q(qi, t_b), wi_t, kb_f, heads_t(vb_f, HKV_B), ki_f, bias_b,
              t=t_b, q0=q0, topk=topk, n_keys=n_keys, n_valid_diag=n_valid(t_b))
    return oa[:, :t], ob[:, :t]


def kernel(x_prompt, x_sample, cache_a_k, cache_a_v, cache_b_k, cache_b_v, cache_b_kidx, rel_bias, norm1_g, w_in, qn_a_g, kn_a_g, lam_q1, lam_k1, lam_q2, lam_k2, subln_a_g, qn_b_g, kn_b_g, w_o_a, w_o_b, w_out, norm2_g, w_ff1, w_ff2):
    depth = w_in.shape[0]
    tab_t = rel_bias.T.astype(F32)
    y_prompt, y_sample = x_prompt, x_sample
    prompt_rows, sample_rows = [], []
    for l in range(depth):
        lam_init = 0.8 - 0.6 * math.exp(-0.3 * l)
        lam = (jnp.exp(jnp.sum(lam_q1[l].astype(F32) * lam_k1[l].astype(F32)))
               - jnp.exp(jnp.sum(lam_q2[l].astype(F32) * lam_k2[l].astype(F32))) + lam_init).reshape(1)
        w = _prep_weights(w_in[l], w_o_a[l], w_o_b[l], w_out[l], w_ff1[l], w_ff2[l])
        p = dict(norm1_g=norm1_g[l], qn_a_g=qn_a_g[l], kn_a_g=kn_a_g[l], subln_a_g=subln_a_g[l],
                 qn_b_g=qn_b_g[l], kn_b_g=kn_b_g[l], norm2_g=norm2_g[l])
        y_prompt, rp = _layer(y_prompt, None, w, p, lam, lam_init, tab_t, t_a=512, t_b=256, tm=512)
        past = (cache_a_k[l], cache_a_v[l], cache_b_k[l], cache_b_v[l], cache_b_kidx[l])
        y_sample, rs = _layer(y_sample, past, w, p, lam, lam_init, tab_t, tm=512)
        prompt_rows.append(rp)
        sample_rows.append(rs)
    p_rows = tuple(jnp.stack(r, axis=0) for r in zip(*prompt_rows))
    s_rows = tuple(jnp.stack(r, axis=0) for r in zip(*sample_rows))
    return (y_prompt, y_sample) + p_rows + s_rows
```

```python
import functools
import math

import jax
import jax.numpy as jnp
from jax import lax
from jax.experimental import pallas as pl
from jax.experimental.pallas import tpu as pltpu

F32 = jnp.float32
BF16 = jnp.bfloat16
I32 = jnp.int32

D_MODEL = 2048
CHUNK = 64
H_A = 8
DV_A = 128
HD_A = 64
H_B = 8
HD_B = 128
HKV_B = 2
G_B = H_B // HKV_B
H_IDX = 16
D_IDX = 64
TOPK_MAX = 256
N_BUCKETS = 32
D_FF = 4 * D_MODEL
EPS = 1e-6
LANES = 128
NEG = -1e30
LOG2E = math.log2(math.e)
INT_MIN = -(2 ** 31)
INT_MAX = 2 ** 31 - 1
VMEM_LIMIT = 56 * 1024 * 1024

_NT = (((1,), (1,)), ((), ()))


def _cparams(sem):
    return pltpu.CompilerParams(dimension_semantics=sem, vmem_limit_bytes=VMEM_LIMIT)


def _resident(block_shape, index_map):
    return pl.BlockSpec(block_shape, index_map, pipeline_mode=pl.Buffered(1))


def _rmsnorm_body(x_ref, g_ref, o_ref):
    x = x_ref[...]
    y = x * lax.rsqrt(jnp.mean(x * x, axis=-1, keepdims=True) + EPS)
    o_ref[...] = (y * g_ref[...]).astype(o_ref.dtype)


def _rmsnorm(x, g, tm):
    n, d = x.shape
    return pl.pallas_call(
        _rmsnorm_body,
        grid=(n // tm,),
        in_specs=[pl.BlockSpec((tm, d), lambda i: (i, 0)), pl.BlockSpec((1, d), lambda i: (0, 0))],
        out_specs=pl.BlockSpec((tm, d), lambda i: (i, 0)),
        out_shape=jax.ShapeDtypeStruct((n, d), BF16),
        compiler_params=_cparams(("parallel",)),
        name="rmsnorm",
    )(x, g.reshape(1, d))


def _head_norm(z, gain, split):
    lane = lax.broadcasted_iota(I32, (1, LANES), 1)
    lo = lane < (LANES // 2)
    outs = []
    for h in range(z.shape[1] // LANES):
        zh = z[:, h * LANES:(h + 1) * LANES]
        sq = zh * zh
        if split:
            s_lo = jnp.sum(jnp.where(lo, sq, 0.0), axis=-1, keepdims=True)
            s_hi = jnp.sum(jnp.where(lo, 0.0, sq), axis=-1, keepdims=True)
            r = jnp.where(lo, lax.rsqrt(s_lo * (2.0 / LANES) + EPS), lax.rsqrt(s_hi * (2.0 / LANES) + EPS))
        else:
            r = lax.rsqrt(jnp.mean(sq, axis=-1, keepdims=True) + EPS)
        outs.append(zh * r * gain)
    return jnp.concatenate(outs, axis=-1)


FLAT, HEADS, HEADS_T = "tokens x cols", "tokens x heads x 128", "heads x 128 x tokens"


def _proj_body(epilogue, n_aux, layouts, h_ref, w_ref, *rest):
    z = jnp.dot(h_ref[...], w_ref[...], preferred_element_type=F32)
    outs = epilogue(z, *[r[...] for r in rest[:n_aux]])
    for o_ref, o, layout in zip(rest[n_aux:], outs, layouts):
        if layout == FLAT:
            o_ref[...] = o.astype(o_ref.dtype)
            continue
        for hd in range(o.shape[1] // LANES):
            slab = o[:, hd * LANES:(hd + 1) * LANES]
            if layout == HEADS:
                o_ref[:, hd, :] = slab.astype(o_ref.dtype)
            else:
                o_ref[hd] = slab.T.astype(o_ref.dtype)


def _proj(h, w, aux, epilogue, out_defs, tm, name):
    n, k = h.shape
    c = w.shape[1]
    in_specs = [pl.BlockSpec((tm, k), lambda i: (i, 0)), _resident((k, c), lambda i: (0, 0))]
    in_specs += [pl.BlockSpec(a.shape, lambda i: (0, 0)) for a in aux]
    out_specs, out_shape = [], []
    for oc, dt, layout in out_defs:
        nh = oc // LANES
        if layout == FLAT:
            out_specs.append(pl.BlockSpec((tm, oc), lambda i: (i, 0)))
            out_shape.append(jax.ShapeDtypeStruct((n, oc), dt))
        elif layout == HEADS:
            out_specs.append(pl.BlockSpec((tm, nh, LANES), lambda i: (i, 0, 0)))
            out_shape.append(jax.ShapeDtypeStruct((n, nh, LANES), dt))
        else:
            out_specs.append(pl.BlockSpec((nh, LANES, tm), lambda i: (0, 0, i)))
            out_shape.append(jax.ShapeDtypeStruct((nh, LANES, n), dt))
    return pl.pallas_call(
        functools.partial(_proj_body, epilogue, len(aux), tuple(d[2] for d in out_defs)),
        grid=(n // tm,),
        in_specs=in_specs,
        out_specs=out_specs,
        out_shape=out_shape,
        compiler_params=_cparams(("parallel",)),
        name=name,
    )(h, w, *aux)


def _epi_qa(z, gain):
    return (_head_norm(z, gain, True) * (HD_A ** -0.5 * LOG2E),)


def _epi_ka(z, gain):
    y = _head_norm(z, gain, True)
    return y, y


def _epi_copy2(z):
    return z, z


def _epi_qb(z, gain):
    return (_head_norm(z, gain, False) * (HD_B ** -0.5 * LOG2E),)


def _epi_copy1(z):
    return (z,)


_KB0, _VB0, _KI0, _WI0, _MISC_COLS = 0, 256, 512, 640, 768


def _epi_misc(z, gain):
    kb = _head_norm(z[:, _KB0:_VB0], gain, False)
    vb = z[:, _VB0:_KI0]
    ki2 = z[:, _KI0:_WI0]
    return kb, kb, vb, vb, ki2[:, :D_IDX], ki2, z[:, _WI0:_WI0 + H_IDX]


def _rel_bias(tab_ref, h, rel):
    half = N_BUCKETS // 2
    exact = half // 2
    n = jnp.abs(rel)
    n2 = n * n
    v_neg = jnp.full(rel.shape, tab_ref[h, 0], F32)
    v_pos = jnp.full(rel.shape, tab_ref[h, half], F32)
    for b in range(1, half):
        cond = (n >= b) if b < exact else (n2 >= exact * exact * 2 ** (b - exact))
        v_neg = jnp.where(cond, tab_ref[h, b], v_neg)
        v_pos = jnp.where(cond, tab_ref[h, half + b], v_pos)
    return (jnp.where(rel > 0, v_pos, v_neg) - tab_ref[h, half - 1]) * LOG2E


def _bias_body(t, n_valid_diag, tab_ref, o_ref):
    h = pl.program_id(0)
    kl = lax.broadcasted_iota(I32, (t, t), 0)
    ql = lax.broadcasted_iota(I32, (t, t), 1)
    o_ref[0] = _rel_bias(tab_ref, h, kl - t - ql)
    visible = ((kl // CHUNK) <= (ql // CHUNK)) & (kl < n_valid_diag)
    o_ref[1] = jnp.where(visible, _rel_bias(tab_ref, h, kl - ql), NEG)


def _bias_dense_body(q0, n_keys, tab_ref, o_ref):
    h = pl.program_id(0)
    qpos = q0 + lax.broadcasted_iota(I32, o_ref.shape, 0)
    kpos = lax.broadcasted_iota(I32, o_ref.shape, 1)
    visible = ((kpos // CHUNK) <= (qpos // CHUNK)) & (kpos < n_keys)
    o_ref[...] = jnp.where(visible, _rel_bias(tab_ref, h, kpos - qpos), NEG)


def _bias_dense(tab_t, nq, nk_pad, q0, n_keys):
    nh = tab_t.shape[0]
    return pl.pallas_call(
        functools.partial(_bias_dense_body, q0, n_keys),
        grid=(nh,),
        in_specs=[pl.BlockSpec(memory_space=pltpu.SMEM)],
        out_specs=pl.BlockSpec((None, nq, nk_pad), lambda h: (h, 0, 0)),
        out_shape=jax.ShapeDtypeStruct((nh, nq, nk_pad), F32),
        compiler_params=_cparams(("arbitrary",)),
        name="bias_dense",
    )(tab_t)


def _bias_tiles(tab_t, t, n_valid_diag):
    assert (t + 1) ** 2 >= (N_BUCKETS // 4) ** 2 * 2 ** (N_BUCKETS // 2 - 1 - N_BUCKETS // 4)
    nh = tab_t.shape[0]
    return pl.pallas_call(
        functools.partial(_bias_body, t, n_valid_diag),
        grid=(nh,),
        in_specs=[pl.BlockSpec(memory_space=pltpu.SMEM)],
        out_specs=pl.BlockSpec((None, 2, t, t), lambda h: (h, 0, 0, 0)),
        out_shape=jax.ShapeDtypeStruct((nh, 2, t, t), F32),
        compiler_params=_cparams(("arbitrary",)),
        name="bias_tiles",
    )(tab_t)


ONES_ROWS = 16


def _softmax_step(s_ref, vt, m_ref, acc_ref, width):
    vt_ones = jnp.concatenate([vt, jnp.ones((ONES_ROWS, vt.shape[1]), vt.dtype)], axis=0)
    for c0 in range(0, s_ref.shape[1], width):
        cols = slice(c0, c0 + width)
        s = s_ref[:, cols]
        m_old = m_ref[:, cols]
        m_new = jnp.maximum(m_old, jnp.max(s, axis=0, keepdims=True))
        alpha = jnp.exp2(m_old - m_new)
        p = jnp.exp2(s - m_new)
        acc_ref[:, cols] = acc_ref[:, cols] * alpha + jnp.dot(vt_ones, p.astype(BF16), preferred_element_type=F32)
        m_ref[:, cols] = m_new


def _normalised(acc, dv):
    return acc[:dv] / acc[dv:dv + 1]


def _pipelined_tiles(n_far, produce, consume, buf0, buf1):
    @pl.when(n_far == -1)
    def _():
        produce(0, buf0, 1)
        consume(0, buf0)

    @pl.when(n_far == 0)
    def _():
        produce(0, buf0, 0)
        produce(1, buf1, 1)
        consume(0, buf0)
        consume(1, buf1)

    @pl.when(n_far >= 1)
    def _():
        produce(0, buf0, None)

    pairs = jnp.maximum(n_far - 1, 0) // 2

    def pair(j):
        produce(j + 1, buf1, None)
        consume(j, buf0)
        produce(j + 2, buf0, None)
        consume(j + 1, buf1)

    def body(i, carry):
        for u in range(4):
            pair(8 * i + 2 * u)
        return carry

    def leftover(q, carry):
        pair(2 * q)
        return carry

    lax.fori_loop(0, pairs // 4, body, 0)
    lax.fori_loop(pairs // 4 * 4, pairs, leftover, 0)
    d = 2 * pairs
    rem = n_far - d

    @pl.when(rem == 1)
    def _():
        produce(d + 1, buf1, 0)
        consume(d, buf0)
        produce(d + 2, buf0, 1)
        consume(d + 1, buf1)
        consume(d + 2, buf0)

    @pl.when(rem == 2)
    def _():
        produce(d + 1, buf1, None)
        consume(d, buf0)
        produce(d + 2, buf0, 0)
        consume(d + 1, buf1)
        produce(d + 3, buf1, 1)
        consume(d + 2, buf0)
        consume(d + 3, buf1)


def _attn_a_body(t, q0, out_scale, lam_ref, q_ref, k_ref, vt_ref, bias_ref, g_ref, o_ref,
                 qz_ref, m_ref, acc_ref, s0_ref, s1_ref):
    i = pl.program_id(2)
    n_far = q0 // t + i - 1
    q = q_ref[...]
    lane = lax.broadcasted_iota(I32, (t, LANES), 1)
    zero = jnp.zeros_like(q)
    qz_ref[:, :t] = jnp.where(lane < HD_A, q, zero).T
    qz_ref[:, t:] = jnp.where(lane < HD_A, zero, q).T
    m_ref[...] = jnp.full(m_ref.shape, NEG, F32)
    acc_ref[...] = jnp.zeros(acc_ref.shape, F32)

    def produce(j, s_ref, kind):
        ks = pl.multiple_of(j * t, t)
        s = jnp.dot(k_ref[pl.ds(ks, t), :], qz_ref[...], preferred_element_type=F32)
        if kind is not None:
            b = bias_ref[kind]
            s = jnp.concatenate([s[:, :t] + b, s[:, t:] + b], axis=1)
        s_ref[...] = s

    def consume(j, s_ref):
        _softmax_step(s_ref, vt_ref[:, pl.ds(pl.multiple_of(j * t, t), t)], m_ref, acc_ref, t)

    _pipelined_tiles(n_far, produce, consume, s0_ref, s1_ref)

    lam = lam_ref[0]
    o = _normalised(acc_ref[...], DV_A)
    o = o[:, :t] - lam * o[:, t:]
    y = o * lax.rsqrt(jnp.mean(o * o, axis=0, keepdims=True) + EPS)
    y = (y * g_ref[...]) * out_scale
    o_ref[...] = y.T.astype(o_ref.dtype)


def _attn_a(q, k, vt, bias, subln_g, lam, *, t, q0, out_scale):
    b, nq, _ = q.shape
    nk = k.shape[1]
    return pl.pallas_call(
        functools.partial(_attn_a_body, t, q0, out_scale),
        grid=(b, H_A, nq // t),
        in_specs=[
            pl.BlockSpec(memory_space=pltpu.SMEM),
            pl.BlockSpec((None, t, LANES), lambda bb, h, i: (bb, i, h)),
            pl.BlockSpec((None, nk, LANES), lambda bb, h, i: (bb, 0, h)),
            pl.BlockSpec((None, None, DV_A, nk), lambda bb, h, i: (bb, h, 0, 0)),
            pl.BlockSpec((None, 2, t, t), lambda bb, h, i: (h, 0, 0, 0)),
            pl.BlockSpec((DV_A, 1), lambda bb, h, i: (0, 0)),
        ],
        out_specs=pl.BlockSpec((None, t, LANES), lambda bb, h, i: (bb, i, h)),
        out_shape=jax.ShapeDtypeStruct((b, nq, H_A * DV_A), BF16),
        scratch_shapes=[
            pltpu.VMEM((LANES, 2 * t), BF16),
            pltpu.VMEM((1, 2 * t), F32),
            pltpu.VMEM((DV_A + ONES_ROWS, 2 * t), F32),
            pltpu.VMEM((t, 2 * t), F32),
            pltpu.VMEM((t, 2 * t), F32),
        ],
        compiler_params=_cparams(("parallel", "parallel", "arbitrary")),
        name="diff_attn",
    )(lam, q, k, vt, bias, subln_g)


GROUP = 4


def _order_key(x):
    b = pltpu.bitcast(x, I32)
    return jnp.where(b < 0, b ^ INT_MAX, b)


def _order_unkey(k):
    return pltpu.bitcast(jnp.where(k < 0, k ^ INT_MAX, k), F32)


def _dsa_body(t, q0, topk, n_keys, n_valid_diag, qb_ref, qi_ref, wi_ref, kb_ref, vbt_ref, ki_ref, bias_ref, o_ref,
              keys_ref, qz_ref, qs_ref, mm_ref, m_ref, acc_ref, s0_ref, s1_ref):
    i = pl.program_id(1)
    n_tiles = q0 // t + i + 1
    n_far = n_tiles - 2

    lane = lax.broadcasted_iota(I32, (t, LANES), 1)
    for hp in range(H_IDX // 2):
        qs = qi_ref[:, hp * LANES:(hp + 1) * LANES]
        zero = jnp.zeros_like(qs)
        qz_ref[2 * hp] = jnp.where(lane < D_IDX, qs, zero).T
        qz_ref[2 * hp + 1] = jnp.where(lane < D_IDX, zero, qs).T

    def score_tile(j):
        kt = ki_ref[pl.ds(pl.multiple_of(j * t, t), t), :]
        acc = jnp.zeros((t, t), F32)
        for h in range(H_IDX):
            s = jnp.dot(kt, qz_ref[h], preferred_element_type=F32)
            acc = acc + jnp.maximum(s, 0.0) * wi_ref[h:h + 1, :]
        return acc

    def fold8(x, op):
        return op(x.reshape(x.shape[0] // 8, 8, t), axis=0)

    mm_ref[0] = jnp.full((8, t), jnp.inf, F32)
    mm_ref[1] = jnp.full((8, t), -jnp.inf, F32)

    def score_store(j):
        sc = score_tile(j)
        keys_ref[pl.ds(pl.multiple_of(j * t, t), t), :] = _order_key(sc)
        mm_ref[0] = jnp.minimum(mm_ref[0], fold8(sc, jnp.min))
        mm_ref[1] = jnp.maximum(mm_ref[1], fold8(sc, jnp.max))

    def score_group(jj, carry):
        for u in range(GROUP):
            score_store(GROUP * jj + u)
        return carry

    def score_single(j, carry):
        score_store(j)
        return carry

    jd = n_tiles - 1
    lax.fori_loop(0, jd // GROUP, score_group, 0)
    lax.fori_loop(jd // GROUP * GROUP, jd, score_single, 0)

    kl = lax.broadcasted_iota(I32, (t, t), 0)
    ql = lax.broadcasted_iota(I32, (t, t), 1)
    admissible = ((kl // CHUNK) <= (ql // CHUNK)) & (kl < n_valid_diag)
    sc = score_tile(jd)
    keys_ref[pl.ds(pl.multiple_of(jd * t, t), t), :] = jnp.where(admissible, _order_key(sc), INT_MIN)
    smin = jnp.minimum(mm_ref[0], fold8(jnp.where(admissible, sc, jnp.inf), jnp.min))
    smax = jnp.maximum(mm_ref[1], fold8(jnp.where(admissible, sc, -jnp.inf), jnp.max))
    for u in range(GROUP - 1):
        keys_ref[pl.ds(pl.multiple_of((n_tiles + u) * t, t), t), :] = jnp.full((t, t), INT_MIN, I32)
    n_groups = (n_tiles + GROUP - 1) // GROUP
    gt = GROUP * t

    def count_groups(hit_fn):
        def body(j, c):
            r0 = pl.multiple_of(j * gt, gt)
            hit = hit_fn(keys_ref[pl.ds(r0, gt), :], r0)
            return c + jnp.sum(hit.reshape(gt // 32, 32, t), axis=0)
        part = lax.fori_loop(0, n_groups, body, jnp.zeros((32, t), F32))
        return jnp.sum(part, axis=0, keepdims=True)

    def search_cond(state):
        return jnp.logical_and(state[0] < 4 * 34, state[-1] > 0)

    def search_body(state):
        it, lo, hi, c_lo, c_hi, _ = state
        f_lo, f_hi = _order_unkey(lo), _order_unkey(hi)
        a, b = jnp.log(c_lo + 0.5), jnp.log(c_hi + 0.5)
        frac = jnp.clip((a - math.log(topk)) / jnp.maximum(a - b, 1e-9), 0.0, 1.0)
        frac = jnp.where(it == 0, frac0, frac)
        guess = jnp.minimum(jnp.maximum(_order_key(f_lo + (f_hi - f_lo) * frac), lo + 1), hi - 1)
        mid = jnp.where(it % 4 == 3, (lo >> 1) + (hi >> 1) + (lo & hi & 1), guess)
        c = count_groups(lambda kk, r0: jnp.where(kk >= mid, 1.0, 0.0))
        live = lo + 1 < hi
        ge = c >= topk
        up = jnp.logical_and(live, ge)
        down = jnp.logical_and(live, jnp.logical_not(ge))
        lo = jnp.where(up, mid, lo)
        c_lo = jnp.where(up, c, c_lo)
        hi = jnp.where(jnp.logical_and(live, c == topk), mid + 1, jnp.where(down, mid, hi))
        c_hi = jnp.where(down, c, c_hi)
        open_ = jnp.logical_and(lo + 1 < hi, c_hi != topk - 1)
        return it + 1, lo, hi, c_lo, c_hi, jnp.max(jnp.where(open_, 1, 0))

    qpos = q0 + i * t + lax.broadcasted_iota(I32, (1, t), 1)
    n_adm = jnp.minimum((qpos // CHUNK + 1) * CHUNK, n_keys).astype(F32)
    lo0 = _order_key(jnp.min(smin, axis=0, keepdims=True))
    hi0 = jnp.where(n_adm <= topk, lo0 + 1, _order_key(jnp.max(smax, axis=0, keepdims=True)) + 1)

    def upper_quantile(p):
        u = jnp.sqrt(-2.0 * jnp.log(p))
        return u - ((0.010328 * u + 0.802853) * u + 2.515517) / (((0.001308 * u + 0.189269) * u + 1.432788) * u + 1.0)

    frac0 = 0.5 + upper_quantile(jnp.clip(topk / n_adm, 1e-6, 0.5)) / (2.0 * upper_quantile(1.0 / n_adm))
    frac0 = jnp.clip(frac0, 0.0, 1.0)
    _, lo, hi, c_lo, c_above, _ = lax.while_loop(
        search_cond, search_body, (jnp.int32(0), lo0, hi0, n_adm, jnp.zeros((1, t), F32), jnp.int32(1)))

    def below_hi_max(j, m):
        kk = keys_ref[pl.ds(pl.multiple_of(j * gt, gt), gt), :]
        return jnp.maximum(m, jnp.max(jnp.where(kk < hi, kk, INT_MIN).reshape(gt // 8, 8, t), axis=0))

    m8 = lax.fori_loop(0, n_groups, below_hi_max, jnp.full((8, t), INT_MIN, I32))
    for shift in (4, 2, 1):
        m8 = jnp.maximum(m8, pltpu.roll(m8, shift, axis=0))
    wide = lo + 1 < hi
    thr = jnp.where(wide, m8[:1], lo)
    c_eq = count_groups(lambda kk, r0: jnp.where(kk == thr, 1.0, 0.0))
    c_thr = jnp.where(wide, c_above + c_eq, c_lo)

    @pl.when(jnp.max(jnp.where(jnp.logical_and(c_thr > topk, qpos < n_keys), 1, 0)) > 0)
    def _():
        need = topk - c_above
        row = lax.broadcasted_iota(I32, (gt, t), 0)

        def cut_body(_, st):
            p_lo, p_hi = st
            p_mid = (p_lo + p_hi) >> 1
            c = count_groups(lambda kk, r0: jnp.where(
                jnp.logical_and(kk == thr, row + r0 <= p_mid), 1.0, 0.0))
            ok = c >= need
            return jnp.where(ok, p_lo, p_mid), jnp.where(ok, p_mid, p_hi)

        n_pos = keys_ref.shape[0]
        _, cut = lax.fori_loop(0, n_pos.bit_length(), cut_body,
                               (jnp.full((1, t), -1, I32), jnp.full((1, t), n_pos - 1, I32)))

        def demote_body(j, carry):
            r0 = pl.multiple_of(j * gt, gt)
            kk = keys_ref[pl.ds(r0, gt), :]
            keys_ref[pl.ds(r0, gt), :] = jnp.where(jnp.logical_and(kk == thr, row + r0 > cut), thr - 1, kk)
            return carry

        lax.fori_loop(0, n_groups, demote_body, 0)

    m_ref[...] = jnp.full(m_ref.shape, NEG, F32)
    acc_ref[...] = jnp.zeros(acc_ref.shape, F32)

    for h in range(H_B):
        qs_ref[h // G_B, :, (h % G_B) * t:(h % G_B + 1) * t] = qb_ref[:, h * HD_B:(h + 1) * HD_B].T

    def produce(j, s_ref, kind):
        ks = pl.multiple_of(j * t, t)
        off = jnp.where(keys_ref[pl.ds(ks, t), :] >= thr, 0.0, NEG)
        for n in range(HKV_B):
            s = jnp.dot(kb_ref[pl.ds(ks, t), n * HD_B:(n + 1) * HD_B], qs_ref[n], preferred_element_type=F32)
            parts = []
            for g in range(G_B):
                off_g = off if kind is None else off + bias_ref[n * G_B + g, kind]
                parts.append(s[:, g * t:(g + 1) * t] + off_g)
            s_ref[n] = jnp.concatenate(parts, axis=1)

    def consume(j, s_ref):
        ks = pl.multiple_of(j * t, t)
        for n in range(HKV_B):
            _softmax_step(s_ref.at[n], vbt_ref[n, :, pl.ds(ks, t)], m_ref.at[n], acc_ref.at[n], 2 * t)

    _pipelined_tiles(n_far, produce, consume, s0_ref, s1_ref)

    for n in range(HKV_B):
        o = _normalised(acc_ref[n], HD_B)
        for g in range(G_B):
            h = n * G_B + g
            o_ref[:, h * HD_B:(h + 1) * HD_B] = o[:, g * t:(g + 1) * t].T.astype(o_ref.dtype)


def _dsa(qb, qi, wi_t, kb, vbt, ki2, bias, *, t, q0, topk, n_keys, n_valid_diag):
    b, nq, _ = qb.shape
    nk = kb.shape[1]
    return pl.pallas_call(
        functools.partial(_dsa_body, t, q0, topk, n_keys, n_valid_diag),
        grid=(b, nq // t),
        in_specs=[
            pl.BlockSpec((None, t, H_B * HD_B), lambda bb, i: (bb, i, 0)),
            pl.BlockSpec((None, t, H_IDX * D_IDX), lambda bb, i: (bb, i, 0)),
            pl.BlockSpec((None, H_IDX, t), lambda bb, i: (bb, 0, i)),
            _resident((None, nk, HKV_B * HD_B), lambda bb, i: (bb, 0, 0)),
            _resident((None, HKV_B, HD_B, nk), lambda bb, i: (bb, 0, 0, 0)),
            _resident((None, nk, LANES), lambda bb, i: (bb, 0, 0)),
            _resident((H_B, 2, t, t), lambda bb, i: (0, 0, 0, 0)),
        ],
        out_specs=pl.BlockSpec((None, t, H_B * HD_B), lambda bb, i: (bb, i, 0)),
        out_shape=jax.ShapeDtypeStruct((b, nq, H_B * HD_B), BF16),
        scratch_shapes=[
            pltpu.VMEM((nk + (GROUP - 1) * t, t), I32),
            pltpu.VMEM((H_IDX, LANES, t), BF16),
            pltpu.VMEM((HKV_B, HD_B, G_B * t), BF16),
            pltpu.VMEM((2, 8, t), F32),
            pltpu.VMEM((HKV_B, 1, G_B * t), F32),
            pltpu.VMEM((HKV_B, HD_B + ONES_ROWS, G_B * t), F32),
            pltpu.VMEM((HKV_B, t, G_B * t), F32),
            pltpu.VMEM((HKV_B, t, G_B * t), F32),
        ],
        compiler_params=_cparams(("parallel", "arbitrary")),
        name="dsa",
    )(qb, qi, wi_t, kb, vbt, ki2, bias)


def _softmax_pieces(s_list, v_list):
    m = functools.reduce(jnp.maximum, [jnp.max(s, axis=-1, keepdims=True) for s in s_list])
    p_list = [jnp.exp2(s - m) for s in s_list]
    o = sum(jnp.dot(p.astype(BF16), v, preferred_element_type=F32) for p, v in zip(p_list, v_list))
    return o / sum(jnp.sum(p, axis=-1, keepdims=True) for p in p_list)


def _head_rows(ref, h, n_heads, n_pos):
    return ref[pl.ds(h, n_pos, stride=n_heads), :].astype(BF16)


def _short_a_body(nq, n_past, out_scale, lam_ref, q_ref, kp_ref, vp_ref, kn_ref, vn_ref, bias_ref, g_ref, o_ref):
    lam = lam_ref[0]
    lane = lax.broadcasted_iota(I32, (nq, LANES), 1)
    pieces = ((kp_ref, vp_ref, 0, n_past), (kn_ref, vn_ref, n_past, nq))
    for h in range(H_A):
        cols = slice(h * LANES, (h + 1) * LANES)
        q = q_ref[:, cols]
        zero = jnp.zeros_like(q)
        qz = jnp.concatenate([jnp.where(lane < HD_A, q, zero), jnp.where(lane < HD_A, zero, q)], axis=0)
        b = bias_ref[h]
        b2 = jnp.concatenate([b, b], axis=0)
        s_list = [lax.dot_general(qz, _head_rows(k_ref, h, H_A, size), _NT, preferred_element_type=F32)
                  + b2[:, lo:lo + size] for k_ref, _, lo, size in pieces]
        o = _softmax_pieces(s_list, [_head_rows(v_ref, h, H_A, size) for _, v_ref, _, size in pieces])
        o = o[:nq] - lam * o[nq:]
        y = o * lax.rsqrt(jnp.mean(o * o, axis=-1, keepdims=True) + EPS)
        o_ref[:, cols] = ((y * g_ref[...]) * out_scale).astype(o_ref.dtype)


def _short_a(q, k_past, v_past, k_new, v_new, bias, subln_g, lam, *, out_scale):
    b, nq, d = q.shape
    n_past = k_past.shape[1]
    assert n_past % LANES == 0
    rows = lambda n: pl.BlockSpec((None, n * H_A, DV_A), lambda bb: (bb, 0, 0))
    merge = lambda a: a.reshape(b, a.shape[1] * H_A, DV_A)
    return pl.pallas_call(
        functools.partial(_short_a_body, nq, n_past, out_scale),
        grid=(b,),
        in_specs=[
            pl.BlockSpec(memory_space=pltpu.SMEM),
            pl.BlockSpec((None, nq, d), lambda bb: (bb, 0, 0)),
            rows(n_past), rows(n_past), rows(nq), rows(nq),
            pl.BlockSpec(bias.shape, lambda bb: (0, 0, 0)),
            pl.BlockSpec((1, DV_A), lambda bb: (0, 0)),
        ],
        out_specs=pl.BlockSpec((None, nq, d), lambda bb: (bb, 0, 0)),
        out_shape=jax.ShapeDtypeStruct((b, nq, d), BF16),
        compiler_params=_cparams(("parallel",)),
        name="diff_attn_short",
    )(lam, q, merge(k_past), merge(v_past), merge(k_new), merge(v_new), bias, subln_g)


def _short_dsa_body(nq, n_past, topk, qb_ref, qi_ref, wi_ref, kbp_ref, vbp_ref, kip_ref, kbn_ref, vbn_ref,
                    kin_ref, bias_ref, o_ref):
    lane = lax.broadcasted_iota(I32, (nq, LANES), 1)
    spans = ((0, n_past), (n_past, nq))

    qz = []
    for h in range(H_IDX):
        qs = qi_ref[:, (h // 2) * LANES:(h // 2 + 1) * LANES]
        zero = jnp.zeros_like(qs)
        qz.append(jnp.where(lane < D_IDX, qs, zero) if h % 2 == 0 else jnp.where(lane < D_IDX, zero, qs))
    qz = jnp.concatenate(qz, axis=0)
    keys, kpos = [], []
    for ki_ref, (lo, size) in zip((kip_ref, kin_ref), spans):
        ki = ki_ref[...].astype(BF16)
        s_idx = lax.dot_general(qz, jnp.concatenate([ki, ki], axis=-1), _NT, preferred_element_type=F32)
        score = jnp.zeros((nq, size), F32)
        for h in range(H_IDX):
            score = score + jnp.maximum(s_idx[h * nq:(h + 1) * nq], 0.0) * wi_ref[:, h:h + 1]
        qp = n_past + lax.broadcasted_iota(I32, (nq, size), 0)
        kp = lo + lax.broadcasted_iota(I32, (nq, size), 1)
        keys.append(jnp.where((kp // CHUNK) <= (qp // CHUNK), _order_key(score), INT_MIN))
        kpos.append(kp)

    def count(hit_fn):
        return sum(jnp.sum(jnp.where(hit_fn(k, kp), 1.0, 0.0), axis=-1, keepdims=True)
                   for k, kp in zip(keys, kpos))

    def search_body(state):
        it, lo, hi, c_lo, c_hi, _ = state
        mid = (lo >> 1) + (hi >> 1) + (lo & hi & 1)
        c = count(lambda k, kp: k >= mid)
        live = lo + 1 < hi
        ge = c >= topk
        up = jnp.logical_and(live, ge)
        down = jnp.logical_and(live, jnp.logical_not(ge))
        lo = jnp.where(up, mid, lo)
        c_lo = jnp.where(up, c, c_lo)
        hi = jnp.where(jnp.logical_and(live, c == topk), mid + 1, jnp.where(down, mid, hi))
        c_hi = jnp.where(down, c, c_hi)
        return it + 1, lo, hi, c_lo, c_hi, jnp.max(jnp.where(lo + 1 < hi, 1, 0))

    n_keys = n_past + nq
    n_adm = count(lambda k, kp: k > INT_MIN)
    lo0 = jnp.full((nq, 1), INT_MIN + 1, I32)
    hi0 = jnp.where(n_adm <= topk, lo0 + 1, INT_MAX)
    _, thr, _, c_thr, c_above, _ = lax.while_loop(
        lambda st: jnp.logical_and(st[0] < 34, st[-1] > 0), search_body,
        (jnp.int32(0), lo0, hi0, n_adm, jnp.zeros((nq, 1), F32), jnp.int32(1)))

    need = topk - c_above

    def cut_body(_, st):
        p_lo, p_hi = st
        p_mid = (p_lo + p_hi) >> 1
        ok = count(lambda k, kp: jnp.logical_and(k == thr, kp <= p_mid)) >= need
        return jnp.where(ok, p_lo, p_mid), jnp.where(ok, p_mid, p_hi)

    _, cut = lax.fori_loop(0, n_keys.bit_length(), cut_body,
                           (jnp.full((nq, 1), -1, I32), jnp.full((nq, 1), n_keys - 1, I32)))
    off = [jnp.where(jnp.logical_and(k >= thr, jnp.logical_not(jnp.logical_and(k == thr, kp > cut))), 0.0, NEG)
           for k, kp in zip(keys, kpos)]

    for n in range(HKV_B):
        heads = range(n * G_B, (n + 1) * G_B)
        qs = jnp.concatenate([qb_ref[:, h * HD_B:(h + 1) * HD_B] for h in heads], axis=0)
        s_list = []
        for kb_ref, off_i, (lo, size) in zip((kbp_ref, kbn_ref), off, spans):
            s = lax.dot_general(qs, _head_rows(kb_ref, n, HKV_B, size), _NT, preferred_element_type=F32)
            s_list.append(s + jnp.concatenate([bias_ref[h][:, lo:lo + size] + off_i for h in heads], axis=0))
        o = _softmax_pieces(s_list, [_head_rows(vbp_ref, n, HKV_B, n_past), _head_rows(vbn_ref, n, HKV_B, nq)])
        for g, h in enumerate(heads):
            o_ref[:, h * HD_B:(h + 1) * HD_B] = o[g * nq:(g + 1) * nq].astype(o_ref.dtype)


def _short_dsa(qb, qi, wi, kb_past, vb_past, ki_past, kb_new, vb_new, ki_new, bias, *, topk):
    b, nq, d = qb.shape
    n_past = kb_past.shape[1]
    assert n_past % LANES == 0
    per_batch = lambda *shape: pl.BlockSpec((None,) + shape, lambda bb: (bb,) + (0,) * len(shape))
    merge = lambda a: a.reshape(b, a.shape[1] * HKV_B, HD_B)
    return pl.pallas_call(
        functools.partial(_short_dsa_body, nq, n_past, topk),
        grid=(b,),
        in_specs=[per_batch(nq, d), per_batch(nq, d), per_batch(nq, H_IDX),
                  per_batch(n_past * HKV_B, HD_B), per_batch(n_past * HKV_B, HD_B), per_batch(n_past, D_IDX),
                  per_batch(nq * HKV_B, HD_B), per_batch(nq * HKV_B, HD_B), per_batch(nq, D_IDX),
                  pl.BlockSpec(bias.shape, lambda bb: (0, 0, 0))],
        out_specs=per_batch(nq, d),
        out_shape=jax.ShapeDtypeStruct((b, nq, d), BF16),
        compiler_params=_cparams(("parallel",)),
        name="dsa_short",
    )(qb, qi, wi, merge(kb_past), merge(vb_past), ki_past, merge(kb_new), merge(vb_new), ki_new, bias)


def _sigmoid(x):
    return 1.0 / (1.0 + jnp.exp(-x))


def _merge_body(h_ref, oa_ref, ob_ref, wg_ref, woa_ref, wob_ref, o_ref):
    h = h_ref[...]
    ga = jnp.dot(h, wg_ref[:, :D_MODEL], preferred_element_type=F32)
    ya = jnp.dot(oa_ref[...], woa_ref[...], preferred_element_type=F32)
    merged = _sigmoid(ga) * ya
    gb = jnp.dot(h, wg_ref[:, D_MODEL:], preferred_element_type=F32)
    yb = jnp.dot(ob_ref[...], wob_ref[...], preferred_element_type=F32)
    o_ref[...] = (merged + _sigmoid(gb) * yb).astype(o_ref.dtype)


def _merge(h, oa, ob, wg, woa, wob, tm):
    n = h.shape[0]
    return pl.pallas_call(
        _merge_body,
        grid=(n // tm,),
        in_specs=[
            pl.BlockSpec((tm, D_MODEL), lambda i: (i, 0)),
            pl.BlockSpec((tm, H_A * DV_A), lambda i: (i, 0)),
            pl.BlockSpec((tm, H_B * HD_B), lambda i: (i, 0)),
            _resident((D_MODEL, 2 * D_MODEL), lambda i: (0, 0)),
            _resident((H_A * DV_A, D_MODEL), lambda i: (0, 0)),
            _resident((H_B * HD_B, D_MODEL), lambda i: (0, 0)),
        ],
        out_specs=pl.BlockSpec((tm, D_MODEL), lambda i: (i, 0)),
        out_shape=jax.ShapeDtypeStruct((n, D_MODEL), BF16),
        compiler_params=_cparams(("parallel",)),
        name="gated_merge",
    )(h, oa, ob, wg, woa, wob)


def _outproj_body(x_ref, mg_ref, w_ref, g_ref, x1_ref, h2_ref):
    x1 = x_ref[...] + jnp.dot(mg_ref[...], w_ref[...], preferred_element_type=F32)
    x1_ref[...] = x1
    y = x1 * lax.rsqrt(jnp.mean(x1 * x1, axis=-1, keepdims=True) + EPS)
    h2_ref[...] = (y * g_ref[...]).astype(h2_ref.dtype)


def _outproj(x, merged, w_out, g2, tm):
    n = x.shape[0]
    return pl.pallas_call(
        _outproj_body,
        grid=(n // tm,),
        in_specs=[
            pl.BlockSpec((tm, D_MODEL), lambda i: (i, 0)),
            pl.BlockSpec((tm, D_MODEL), lambda i: (i, 0)),
            _resident((D_MODEL, D_MODEL), lambda i: (0, 0)),
            pl.BlockSpec((1, D_MODEL), lambda i: (0, 0)),
        ],
        out_specs=[pl.BlockSpec((tm, D_MODEL), lambda i: (i, 0)), pl.BlockSpec((tm, D_MODEL), lambda i: (i, 0))],
        out_shape=[jax.ShapeDtypeStruct((n, D_MODEL), F32), jax.ShapeDtypeStruct((n, D_MODEL), BF16)],
        compiler_params=_cparams(("parallel",)),
        name="out_proj",
    )(x, merged, w_out, g2.reshape(1, D_MODEL))


def _ffn_body(x1_ref, h2_ref, w1_ref, w2_ref, o_ref):
    f = pl.program_id(1)

    @pl.when(f == 0)
    def _():
        o_ref[...] = x1_ref[...]

    u = jnp.maximum(jnp.dot(h2_ref[...], w1_ref[...], preferred_element_type=F32), 0.0)
    o_ref[...] += jnp.dot((u * u).astype(BF16), w2_ref[...], preferred_element_type=F32)


def _ffn(x1, h2, w1, w2, tm, tf):
    n = x1.shape[0]
    return pl.pallas_call(
        _ffn_body,
        grid=(n // tm, D_FF // tf),
        in_specs=[
            pl.BlockSpec((tm, D_MODEL), lambda i, f: (i, 0)),
            pl.BlockSpec((tm, D_MODEL), lambda i, f: (i, 0)),
            pl.BlockSpec((D_MODEL, tf), lambda i, f: (0, f)),
            pl.BlockSpec((tf, D_MODEL), lambda i, f: (f, 0)),
        ],
        out_specs=pl.BlockSpec((tm, D_MODEL), lambda i, f: (i, 0)),
        out_shape=jax.ShapeDtypeStruct((n, D_MODEL), F32),
        compiler_params=_cparams(("parallel", "arbitrary")),
        name="ffn",
    )(x1, h2, w1, w2)


def _prep_weights(w_in, w_o_a, w_o_b, w_out, w_ff1, w_ff2):
    sizes = (H_A * 2 * HD_A, H_A * 2 * HD_A, H_A * DV_A, H_B * HD_B, HKV_B * HD_B, HKV_B * HD_B,
             H_IDX * D_IDX, D_IDX, H_IDX, D_MODEL, D_MODEL)
    offs = [0]
    for s in sizes:
        offs.append(offs[-1] + s)
    col = lambda a, b: w_in[:, offs[a]:offs[b]]
    w_misc = jnp.concatenate(
        [col(4, 6), col(7, 8), col(7, 8), col(8, 9),
         jnp.zeros((D_MODEL, _MISC_COLS - _WI0 - H_IDX), w_in.dtype)], axis=1)
    return dict(
        qa=col(0, 1).astype(BF16), ka=col(1, 2).astype(BF16), va=col(2, 3).astype(BF16),
        qb=col(3, 4).astype(BF16), misc=w_misc.astype(BF16), qi=col(6, 7).astype(BF16),
        gate=col(9, 11).astype(BF16), oa=w_o_a.astype(BF16), ob=w_o_b.astype(BF16),
        out=w_out.astype(BF16), ff1=w_ff1.astype(BF16), ff2=w_ff2.astype(BF16))


def _layer(x, past, w, p, lam, lam_init, tab_t, *, tm, t_a=None, t_b=None):
    short = t_a is None
    b, t, _ = x.shape
    n = b * t
    xf = x.reshape(n, D_MODEL)
    h = _rmsnorm(xf, p["norm1_g"], tm)
    tp = min(2 * tm, n)

    gain_a = lambda g: g.reshape(1, 2 * HD_A)
    v_layout = HEADS_T if (not short and past is None and b == 1) else FLAT
    (qa,) = _proj(h, w["qa"], [gain_a(p["qn_a_g"])], _epi_qa, [(1024, BF16, FLAT)], tp, "proj_qa")
    ka, ka_h = _proj(h, w["ka"], [gain_a(p["kn_a_g"])], _epi_ka, [(1024, F32, HEADS), (1024, BF16, FLAT)],
                     tp, "proj_ka")
    va, va_h = _proj(h, w["va"], [], _epi_copy2, [(1024, F32, HEADS), (1024, BF16, v_layout)], tp, "proj_va")
    (qb,) = _proj(h, w["qb"], [p["qn_b_g"].reshape(1, HD_B)], _epi_qb, [(1024, BF16, FLAT)], tp, "proj_qb")
    kb, kb_h, vb, vb_h, ki, ki2_h, wi = _proj(
        h, w["misc"], [p["kn_b_g"].reshape(1, HD_B)], _epi_misc,
        [(256, F32, HEADS), (256, BF16, FLAT), (256, F32, HEADS), (256, BF16, v_layout), (D_IDX, F32, FLAT),
         (2 * D_IDX, BF16, FLAT), (H_IDX, F32, FLAT)],
        tp, "proj_misc")
    (qi,) = _proj(h, w["qi"], [], _epi_copy1, [(1024, BF16, FLAT)], tp, "proj_qi")

    new_rows = (ka.reshape(b, t, H_A, 2 * HD_A), va.reshape(b, t, H_A, DV_A), kb.reshape(b, t, HKV_B, HD_B),
                vb.reshape(b, t, HKV_B, HD_B), ki.reshape(b, t, D_IDX))

    per_b = lambda a: a.reshape(b, t, -1)
    if short:
        n_keys = past[0].shape[1] + t
        topk = min(TOPK_MAX, n_keys // 4)
        oa = _short_a(per_b(qa), past[0], past[1], new_rows[0], new_rows[1],
                      _bias_dense(tab_t[:H_A], t, n_keys, n_keys - t, n_keys),
                      p["subln_a_g"].reshape(1, DV_A), lam, out_scale=1.0 - lam_init)
        ob = _short_dsa(per_b(qb), per_b(qi), per_b(wi), past[2], past[3], past[4], *new_rows[2:],
                        _bias_dense(tab_t[H_A:], t, n_keys, n_keys - t, n_keys), topk=topk)
    else:
        oa, ob = _tiled_mixers(qa, qb, qi, wi, ka_h, va_h, kb_h, vb_h, ki2_h, past, tab_t, p, lam, lam_init,
                               b=b, t=t, t_a=t_a, t_b=t_b)
    oa = oa.reshape(n, H_A * DV_A)
    ob = ob.reshape(n, H_B * HD_B)

    merged = _merge(h, oa, ob, w["gate"], w["oa"], w["ob"], min(tm, 256))
    x1, h2 = _outproj(xf, merged, w["out"], p["norm2_g"], min(tm, 256))
    y = _ffn(x1, h2, w["ff1"], w["ff2"], tm, 1024)
    return y.reshape(b, t, D_MODEL), new_rows


def _tiled_mixers(qa, qb, qi, wi, ka_h, va_h, kb_h, vb_h, ki2_h, past, tab_t, p, lam, lam_init,
                  *, b, t, t_a, t_b):
    def full_keys(new_h, past_arr):
        if new_h.ndim == 3:
            return new_h[None]
        new_h = new_h.reshape(b, t, -1)
        if past_arr is None:
            return new_h
        past_h = past_arr.reshape(b, past_arr.shape[1], -1).astype(BF16)
        return jnp.concatenate([past_h, new_h], axis=1)

    if past is None:
        pa_k = pa_v = pb_k = pb_v = pb_i = None
    else:
        pa_k, pa_v, pb_k, pb_v, pb_i = past
        pb_i = jnp.concatenate([pb_i, pb_i], axis=-1)
    ka_f, va_f = full_keys(ka_h, pa_k), full_keys(va_h, pa_v)
    kb_f, vb_f, ki_f = full_keys(kb_h, pb_k), full_keys(vb_h, pb_v), full_keys(ki2_h, pb_i)
    n_keys = ka_f.shape[1]
    q0 = n_keys - t
    topk = min(TOPK_MAX, n_keys // 4)
    t_pad = max(t_a, t_b)
    assert t_pad % t_a == 0 and t_pad % t_b == 0 and q0 % t_pad == 0
    nk_pad = -(-n_keys // t_pad) * t_pad
    pad_k = lambda a: jnp.pad(a, ((0, 0), (0, nk_pad - n_keys), (0, 0)))
    pad_q = lambda a, tt: jnp.pad(a.reshape(b, t, -1), ((0, 0), (0, -(-t // tt) * tt - t), (0, 0)))
    ka_f, kb_f, ki_f = (pad_k(a) for a in (ka_f, kb_f, ki_f))
    n_valid = lambda tt: n_keys - (n_keys - 1) // tt * tt
    assert all(t <= tt or n_valid(tt) == tt for tt in (t_a, t_b))

    def heads_t(v, nh):
        if v.ndim == 3:
            v = jnp.transpose(v.reshape(b, n_keys, nh, -1), (0, 2, 3, 1))
        return jnp.pad(v, ((0, 0), (0, 0), (0, 0), (0, nk_pad - n_keys)))

    bias_a = _bias_tiles(tab_t[:H_A], t_a, n_valid(t_a))
    oa = _attn_a(pad_q(qa, t_a), ka_f, heads_t(va_f, H_A), bias_a, p["subln_a_g"].reshape(DV_A, 1), lam,
                 t=t_a, q0=q0, out_scale=1.0 - lam_init)
    wi_t = jnp.swapaxes(pad_q(wi, t_b), 1, 2)
    bias_b = _bias_tiles(tab_t[H_A:], t_b, n_valid(t_b))
    ob = _dsa(pad_q(qb, t_b), pad_q(qi, t_b), wi_t, kb_f, heads_t(vb_f, HKV_B), ki_f, bias_b,
              t=t_b, q0=q0, topk=topk, n_keys=n_keys, n_valid_diag=n_valid(t_b))
    return oa[:, :t], ob[:, :t]


def kernel(x_prompt, x_sample, cache_a_k, cache_a_v, cache_b_k, cache_b_v, cache_b_kidx, rel_bias, norm1_g, w_in, qn_a_g, kn_a_g, lam_q1, lam_k1, lam_q2, lam_k2, subln_a_g, qn_b_g, kn_b_g, w_o_a, w_o_b, w_out, norm2_g, w_ff1, w_ff2):
    depth = w_in.shape[0]
    tab_t = rel_bias.T.astype(F32)
    y_prompt, y_sample = x_prompt, x_sample
    prompt_rows, sample_rows = [], []
    for l in range(depth):
        lam_init = 0.8 - 0.6 * math.exp(-0.3 * l)
        lam = (jnp.exp(jnp.sum(lam_q1[l].astype(F32) * lam_k1[l].astype(F32)))
               - jnp.exp(jnp.sum(lam_q2[l].astype(F32) * lam_k2[l].astype(F32))) + lam_init).reshape(1)
        w = _prep_weights(w_in[l], w_o_a[l], w_o_b[l], w_out[l], w_ff1[l], w_ff2[l])
        p = dict(norm1_g=norm1_g[l], qn_a_g=qn_a_g[l], kn_a_g=kn_a_g[l], subln_a_g=subln_a_g[l],
                 qn_b_g=qn_b_g[l], kn_b_g=kn_b_g[l], norm2_g=norm2_g[l])
        y_prompt, rp = _layer(y_prompt, None, w, p, lam, lam_init, tab_t, t_a=512, t_b=256, tm=512)
        past = (cache_a_k[l], cache_a_v[l], cache_b_k[l], cache_b_v[l], cache_b_kidx[l])
        y_sample, rs = _layer(y_sample, past, w, p, lam, lam_init, tab_t, tm=512)
        prompt_rows.append(rp)
        sample_rows.append(rs)
    p_rows = tuple(jnp.stack(r, axis=0) for r in zip(*prompt_rows))
    s_rows = tuple(jnp.stack(r, axis=0) for r in zip(*sample_rows))
    return (y_prompt, y_sample) + p_rows + s_rows
```

```python
import functools
import math

import jax
import jax.numpy as jnp
from jax import lax
from jax.experimental import pallas as pl
from jax.experimental.pallas import tpu as pltpu

F32 = jnp.float32
BF16 = jnp.bfloat16
I32 = jnp.int32

D_MODEL = 2048
CHUNK = 64
H_A = 8
DV_A = 128
HD_A = 64
H_B = 8
HD_B = 128
HKV_B = 2
G_B = H_B // HKV_B
H_IDX = 16
D_IDX = 64
TOPK_MAX = 256
N_BUCKETS = 32
D_FF = 4 * D_MODEL
EPS = 1e-6
LANES = 128
NEG = -1e30
LOG2E = math.log2(math.e)
INT_MIN = -(2 ** 31)
INT_MAX = 2 ** 31 - 1
VMEM_LIMIT = 56 * 1024 * 1024

_NT = (((1,), (1,)), ((), ()))


def _cparams(sem):
    return pltpu.CompilerParams(dimension_semantics=sem, vmem_limit_bytes=VMEM_LIMIT)


def _resident(block_shape, index_map):
    return pl.BlockSpec(block_shape, index_map, pipeline_mode=pl.Buffered(1))


def _rmsnorm_body(x_ref, g_ref, o_ref):
    x = x_ref[...]
    y = x * lax.rsqrt(jnp.mean(x * x, axis=-1, keepdims=True) + EPS)
    o_ref[...] = (y * g_ref[...]).astype(o_ref.dtype)


def _rmsnorm(x, g, tm):
    n, d = x.shape
    return pl.pallas_call(
        _rmsnorm_body,
        grid=(n // tm,),
        in_specs=[pl.BlockSpec((tm, d), lambda i: (i, 0)), pl.BlockSpec((1, d), lambda i: (0, 0))],
        out_specs=pl.BlockSpec((tm, d), lambda i: (i, 0)),
        out_shape=jax.ShapeDtypeStruct((n, d), BF16),
        compiler_params=_cparams(("parallel",)),
        name="rmsnorm",
    )(x, g.reshape(1, d))


def _head_norm(z, gain, split):
    lane = lax.broadcasted_iota(I32, (1, LANES), 1)
    lo = lane < (LANES // 2)
    outs = []
    for h in range(z.shape[1] // LANES):
        zh = z[:, h * LANES:(h + 1) * LANES]
        sq = zh * zh
        if split:
            s_lo = jnp.sum(jnp.where(lo, sq, 0.0), axis=-1, keepdims=True)
            s_hi = jnp.sum(jnp.where(lo, 0.0, sq), axis=-1, keepdims=True)
            r = jnp.where(lo, lax.rsqrt(s_lo * (2.0 / LANES) + EPS), lax.rsqrt(s_hi * (2.0 / LANES) + EPS))
        else:
            r = lax.rsqrt(jnp.mean(sq, axis=-1, keepdims=True) + EPS)
        outs.append(zh * r * gain)
    return jnp.concatenate(outs, axis=-1)


FLAT, HEADS, HEADS_T = "tokens x cols", "tokens x heads x 128", "heads x 128 x tokens"


def _proj_body(epilogue, n_aux, layouts, h_ref, w_ref, *rest):
    z = jnp.dot(h_ref[...], w_ref[...], preferred_element_type=F32)
    outs = epilogue(z, *[r[...] for r in rest[:n_aux]])
    for o_ref, o, layout in zip(rest[n_aux:], outs, layouts):
        if layout == FLAT:
            o_ref[...] = o.astype(o_ref.dtype)
            continue
        for hd in range(o.shape[1] // LANES):
            slab = o[:, hd * LANES:(hd + 1) * LANES]
            if layout == HEADS:
                o_ref[:, hd, :] = slab.astype(o_ref.dtype)
            else:
                o_ref[hd] = slab.T.astype(o_ref.dtype)


def _proj(h, w, aux, epilogue, out_defs, tm, name):
    n, k = h.shape
    c = w.shape[1]
    in_specs = [pl.BlockSpec((tm, k), lambda i: (i, 0)), _resident((k, c), lambda i: (0, 0))]
    in_specs += [pl.BlockSpec(a.shape, lambda i: (0, 0)) for a in aux]
    out_specs, out_shape = [], []
    for oc, dt, layout in out_defs:
        nh = oc // LANES
        if layout == FLAT:
            out_specs.append(pl.BlockSpec((tm, oc), lambda i: (i, 0)))
            out_shape.append(jax.ShapeDtypeStruct((n, oc), dt))
        elif layout == HEADS:
            out_specs.append(pl.BlockSpec((tm, nh, LANES), lambda i: (i, 0, 0)))
            out_shape.append(jax.ShapeDtypeStruct((n, nh, LANES), dt))
        else:
            out_specs.append(pl.BlockSpec((nh, LANES, tm), lambda i: (0, 0, i)))
            out_shape.append(jax.ShapeDtypeStruct((nh, LANES, n), dt))
    return pl.pallas_call(
        functools.partial(_proj_body, epilogue, len(aux), tuple(d[2] for d in out_defs)),
        grid=(n // tm,),
        in_specs=in_specs,
        out_specs=out_specs,
        out_shape=out_shape,
        compiler_params=_cparams(("parallel",)),
        name=name,
    )(h, w, *aux)


def _epi_qa(z, gain):
    return (_head_norm(z, gain, True) * (HD_A ** -0.5 * LOG2E),)


def _epi_ka(z, gain):
    y = _head_norm(z, gain, True)
    return y, y


def _epi_copy2(z):
    return z, z


def _epi_qb(z, gain):
    return (_head_norm(z, gain, False) * (HD_B ** -0.5 * LOG2E),)


def _epi_copy1(z):
    return (z,)


_KB0, _VB0, _KI0, _WI0, _MISC_COLS = 0, 256, 512, 640, 768


def _epi_misc(z, gain):
    kb = _head_norm(z[:, _KB0:_VB0], gain, False)
    vb = z[:, _VB0:_KI0]
    ki2 = z[:, _KI0:_WI0]
    return kb, kb, vb, vb, ki2[:, :D_IDX], ki2, z[:, _WI0:_WI0 + H_IDX]


def _rel_bias(tab_ref, h, rel):
    half = N_BUCKETS // 2
    exact = half // 2
    n = jnp.abs(rel)
    n2 = n * n
    v_neg = jnp.full(rel.shape, tab_ref[h, 0], F32)
    v_pos = jnp.full(rel.shape, tab_ref[h, half], F32)
    for b in range(1, half):
        cond = (n >= b) if b < exact else (n2 >= exact * exact * 2 ** (b - exact))
        v_neg = jnp.where(cond, tab_ref[h, b], v_neg)
        v_pos = jnp.where(cond, tab_ref[h, half + b], v_pos)
    return (jnp.where(rel > 0, v_pos, v_neg) - tab_ref[h, half - 1]) * LOG2E


def _bias_body(t, n_valid_diag, tab_ref, o_ref):
    h = pl.program_id(0)
    kl = lax.broadcasted_iota(I32, (t, t), 0)
    ql = lax.broadcasted_iota(I32, (t, t), 1)
    o_ref[0] = _rel_bias(tab_ref, h, kl - t - ql)
    visible = ((kl // CHUNK) <= (ql // CHUNK)) & (kl < n_valid_diag)
    o_ref[1] = jnp.where(visible, _rel_bias(tab_ref, h, kl - ql), NEG)


def _bias_dense_body(q0, n_keys, tab_ref, o_ref):
    h = pl.program_id(0)
    qpos = q0 + lax.broadcasted_iota(I32, o_ref.shape, 0)
    kpos = lax.broadcasted_iota(I32, o_ref.shape, 1)
    visible = ((kpos // CHUNK) <= (qpos // CHUNK)) & (kpos < n_keys)
    o_ref[...] = jnp.where(visible, _rel_bias(tab_ref, h, kpos - qpos), NEG)


def _bias_dense(tab_t, nq, nk_pad, q0, n_keys):
    nh = tab_t.shape[0]
    return pl.pallas_call(
        functools.partial(_bias_dense_body, q0, n_keys),
        grid=(nh,),
        in_specs=[pl.BlockSpec(memory_space=pltpu.SMEM)],
        out_specs=pl.BlockSpec((None, nq, nk_pad), lambda h: (h, 0, 0)),
        out_shape=jax.ShapeDtypeStruct((nh, nq, nk_pad), F32),
        compiler_params=_cparams(("arbitrary",)),
        name="bias_dense",
    )(tab_t)


def _bias_tiles(tab_t, t, n_valid_diag):
    assert (t + 1) ** 2 >= (N_BUCKETS // 4) ** 2 * 2 ** (N_BUCKETS // 2 - 1 - N_BUCKETS // 4)
    nh = tab_t.shape[0]
    return pl.pallas_call(
        functools.partial(_bias_body, t, n_valid_diag),
        grid=(nh,),
        in_specs=[pl.BlockSpec(memory_space=pltpu.SMEM)],
        out_specs=pl.BlockSpec((None, 2, t, t), lambda h: (h, 0, 0, 0)),
        out_shape=jax.ShapeDtypeStruct((nh, 2, t, t), F32),
        compiler_params=_cparams(("arbitrary",)),
        name="bias_tiles",
    )(tab_t)


ONES_ROWS = 16


def _softmax_step(s_ref, vt, m_ref, acc_ref, width):
    vt_ones = jnp.concatenate([vt, jnp.ones((ONES_ROWS, vt.shape[1]), vt.dtype)], axis=0)
    for c0 in range(0, s_ref.shape[1], width):
        cols = slice(c0, c0 + width)
        s = s_ref[:, cols]
        m_old = m_ref[:, cols]
        m_new = jnp.maximum(m_old, jnp.max(s, axis=0, keepdims=True))
        alpha = jnp.exp2(m_old - m_new)
        p = jnp.exp2(s - m_new)
        acc_ref[:, cols] = acc_ref[:, cols] * alpha + jnp.dot(vt_ones, p.astype(BF16), preferred_element_type=F32)
        m_ref[:, cols] = m_new


def _normalised(acc, dv):
    return acc[:dv] / acc[dv:dv + 1]


def _pipelined_tiles(n_far, produce, consume, buf0, buf1):
    @pl.when(n_far == -1)
    def _():
        produce(0, buf0, 1)
        consume(0, buf0)

    @pl.when(n_far == 0)
    def _():
        produce(0, buf0, 0)
        produce(1, buf1, 1)
        consume(0, buf0)
        consume(1, buf1)

    @pl.when(n_far >= 1)
    def _():
        produce(0, buf0, None)

    pairs = jnp.maximum(n_far - 1, 0) // 2

    def pair(j):
        produce(j + 1, buf1, None)
        consume(j, buf0)
        produce(j + 2, buf0, None)
        consume(j + 1, buf1)

    def body(i, carry):
        for u in range(4):
            pair(8 * i + 2 * u)
        return carry

    def leftover(q, carry):
        pair(2 * q)
        return carry

    lax.fori_loop(0, pairs // 4, body, 0)
    lax.fori_loop(pairs // 4 * 4, pairs, leftover, 0)
    d = 2 * pairs
    rem = n_far - d

    @pl.when(rem == 1)
    def _():
        produce(d + 1, buf1, 0)
        consume(d, buf0)
        produce(d + 2, buf0, 1)
        consume(d + 1, buf1)
        consume(d + 2, buf0)

    @pl.when(rem == 2)
    def _():
        produce(d + 1, buf1, None)
        consume(d, buf0)
        produce(d + 2, buf0, 0)
        consume(d + 1, buf1)
        produce(d + 3, buf1, 1)
        consume(d + 2, buf0)
        consume(d + 3, buf1)


def _attn_a_body(t, q0, out_scale, lam_ref, q_ref, k_ref, vt_ref, bias_ref, g_ref, o_ref,
                 qz_ref, m_ref, acc_ref, s0_ref, s1_ref):
    i = pl.program_id(2)
    n_far = q0 // t + i - 1
    zero = jnp.zeros((HD_A, t), BF16)
    qz_ref[:HD_A, :t] = q_ref[:HD_A, :]
    qz_ref[HD_A:, :t] = zero
    qz_ref[:HD_A, t:] = zero
    qz_ref[HD_A:, t:] = q_ref[HD_A:, :]
    m_ref[...] = jnp.full(m_ref.shape, NEG, F32)
    acc_ref[...] = jnp.zeros(acc_ref.shape, F32)

    def produce(j, s_ref, kind):
        ks = pl.multiple_of(j * t, t)
        s = jnp.dot(k_ref[pl.ds(ks, t), :], qz_ref[...], preferred_element_type=F32)
        if kind is not None:
            b = bias_ref[kind]
            s = jnp.concatenate([s[:, :t] + b, s[:, t:] + b], axis=1)
        s_ref[...] = s

    def consume(j, s_ref):
        _softmax_step(s_ref, vt_ref[:, pl.ds(pl.multiple_of(j * t, t), t)], m_ref, acc_ref, t)

    _pipelined_tiles(n_far, produce, consume, s0_ref, s1_ref)

    lam = lam_ref[0]
    o = _normalised(acc_ref[...], DV_A)
    o = o[:, :t] - lam * o[:, t:]
    y = o * lax.rsqrt(jnp.mean(o * o, axis=0, keepdims=True) + EPS)
    y = (y * g_ref[...]) * out_scale
    o_ref[...] = y.T.astype(o_ref.dtype)


def _attn_a(q, k, vt, bias, subln_g, lam, *, t, q0, out_scale):
    b, _, _, nq = q.shape
    nk = k.shape[1]
    return pl.pallas_call(
        functools.partial(_attn_a_body, t, q0, out_scale),
        grid=(b, H_A, nq // t),
        in_specs=[
            pl.BlockSpec(memory_space=pltpu.SMEM),
            pl.BlockSpec((None, None, 2 * HD_A, t), lambda bb, h, i: (bb, h, 0, i)),
            pl.BlockSpec((None, nk, LANES), lambda bb, h, i: (bb, 0, h)),
            pl.BlockSpec((None, None, DV_A, nk), lambda bb, h, i: (bb, h, 0, 0)),
            pl.BlockSpec((None, 2, t, t), lambda bb, h, i: (h, 0, 0, 0)),
            pl.BlockSpec((DV_A, 1), lambda bb, h, i: (0, 0)),
        ],
        out_specs=pl.BlockSpec((None, t, LANES), lambda bb, h, i: (bb, i, h)),
        out_shape=jax.ShapeDtypeStruct((b, nq, H_A * DV_A), BF16),
        scratch_shapes=[
            pltpu.VMEM((LANES, 2 * t), BF16),
            pltpu.VMEM((1, 2 * t), F32),
            pltpu.VMEM((DV_A + ONES_ROWS, 2 * t), F32),
            pltpu.VMEM((t, 2 * t), F32),
            pltpu.VMEM((t, 2 * t), F32),
        ],
        compiler_params=_cparams(("parallel", "parallel", "arbitrary")),
        name="diff_attn",
    )(lam, q, k, vt, bias, subln_g)


GROUP = 4


def _order_key(x):
    b = pltpu.bitcast(x, I32)
    return jnp.where(b < 0, b ^ INT_MAX, b)


def _order_unkey(k):
    return pltpu.bitcast(jnp.where(k < 0, k ^ INT_MAX, k), F32)


def _dsa_body(t, q0, topk, n_keys, n_valid_diag, qb_ref, qi_ref, wi_ref, kb_ref, vbt_ref, ki_ref, bias_ref, o_ref,
              keys_ref, qz_ref, qs_ref, mm_ref, m_ref, acc_ref, s0_ref, s1_ref):
    i = pl.program_id(1)
    n_tiles = q0 // t + i + 1
    n_far = n_tiles - 2

    zero = jnp.zeros((D_IDX, t), BF16)
    for hp in range(H_IDX // 2):
        qz_ref[2 * hp, :D_IDX, :] = qi_ref[hp, :D_IDX, :]
        qz_ref[2 * hp, D_IDX:, :] = zero
        qz_ref[2 * hp + 1, :D_IDX, :] = zero
        qz_ref[2 * hp + 1, D_IDX:, :] = qi_ref[hp, D_IDX:, :]

    def score_tile(j):
        kt = ki_ref[pl.ds(pl.multiple_of(j * t, t), t), :]
        acc = jnp.zeros((t, t), F32)
        for h in range(H_IDX):
            s = jnp.dot(kt, qz_ref[h], preferred_element_type=F32)
            acc = acc + jnp.maximum(s, 0.0) * wi_ref[h:h + 1, :]
        return acc

    def fold8(x, op):
        return op(x.reshape(x.shape[0] // 8, 8, t), axis=0)

    mm_ref[0] = jnp.full((8, t), jnp.inf, F32)
    mm_ref[1] = jnp.full((8, t), -jnp.inf, F32)

    def score_store(j):
        sc = score_tile(j)
        keys_ref[pl.ds(pl.multiple_of(j * t, t), t), :] = _order_key(sc)
        mm_ref[0] = jnp.minimum(mm_ref[0], fold8(sc, jnp.min))
        mm_ref[1] = jnp.maximum(mm_ref[1], fold8(sc, jnp.max))

    def score_group(jj, carry):
        for u in range(GROUP):
            score_store(GROUP * jj + u)
        return carry

    def score_single(j, carry):
        score_store(j)
        return carry

    jd = n_tiles - 1
    lax.fori_loop(0, jd // GROUP, score_group, 0)
    lax.fori_loop(jd // GROUP * GROUP, jd, score_single, 0)

    kl = lax.broadcasted_iota(I32, (t, t), 0)
    ql = lax.broadcasted_iota(I32, (t, t), 1)
    admissible = ((kl // CHUNK) <= (ql // CHUNK)) & (kl < n_valid_diag)
    sc = score_tile(jd)
    keys_ref[pl.ds(pl.multiple_of(jd * t, t), t), :] = jnp.where(admissible, _order_key(sc), INT_MIN)
    smin = jnp.minimum(mm_ref[0], fold8(jnp.where(admissible, sc, jnp.inf), jnp.min))
    smax = jnp.maximum(mm_ref[1], fold8(jnp.where(admissible, sc, -jnp.inf), jnp.max))
    for u in range(GROUP - 1):
        keys_ref[pl.ds(pl.multiple_of((n_tiles + u) * t, t), t), :] = jnp.full((t, t), INT_MIN, I32)
    n_groups = (n_tiles + GROUP - 1) // GROUP
    gt = GROUP * t

    def count_groups(hit_fn):
        def body(j, c):
            r0 = pl.multiple_of(j * gt, gt)
            hit = hit_fn(keys_ref[pl.ds(r0, gt), :], r0)
            return c + jnp.sum(hit.reshape(gt // 32, 32, t), axis=0)
        part = lax.fori_loop(0, n_groups, body, jnp.zeros((32, t), F32))
        return jnp.sum(part, axis=0, keepdims=True)

    def search_cond(state):
        return jnp.logical_and(state[0] < 4 * 34, state[-1] > 0)

    def search_body(state):
        it, lo, hi, c_lo, c_hi, _ = state
        f_lo, f_hi = _order_unkey(lo), _order_unkey(hi)
        a, b = jnp.log(c_lo + 0.5), jnp.log(c_hi + 0.5)
        frac = jnp.clip((a - math.log(topk)) / jnp.maximum(a - b, 1e-9), 0.0, 1.0)
        frac = jnp.where(it == 0, frac0, frac)
        guess = jnp.minimum(jnp.maximum(_order_key(f_lo + (f_hi - f_lo) * frac), lo + 1), hi - 1)
        mid = jnp.where(it % 4 == 3, (lo >> 1) + (hi >> 1) + (lo & hi & 1), guess)
        c = count_groups(lambda kk, r0: jnp.where(kk >= mid, 1.0, 0.0))
        live = lo + 1 < hi
        ge = c >= topk
        up = jnp.logical_and(live, ge)
        down = jnp.logical_and(live, jnp.logical_not(ge))
        lo = jnp.where(up, mid, lo)
        c_lo = jnp.where(up, c, c_lo)
        hi = jnp.where(jnp.logical_and(live, c == topk), mid + 1, jnp.where(down, mid, hi))
        c_hi = jnp.where(down, c, c_hi)
        open_ = jnp.logical_and(lo + 1 < hi, c_hi != topk - 1)
        return it + 1, lo, hi, c_lo, c_hi, jnp.max(jnp.where(open_, 1, 0))

    qpos = q0 + i * t + lax.broadcasted_iota(I32, (1, t), 1)
    n_adm = jnp.minimum((qpos // CHUNK + 1) * CHUNK, n_keys).astype(F32)
    lo0 = _order_key(jnp.min(smin, axis=0, keepdims=True))
    hi0 = jnp.where(n_adm <= topk, lo0 + 1, _order_key(jnp.max(smax, axis=0, keepdims=True)) + 1)

    def upper_quantile(p):
        u = jnp.sqrt(-2.0 * jnp.log(p))
        return u - ((0.010328 * u + 0.802853) * u + 2.515517) / (((0.001308 * u + 0.189269) * u + 1.432788) * u + 1.0)

    frac0 = 0.5 + upper_quantile(jnp.clip(topk / n_adm, 1e-6, 0.5)) / (2.0 * upper_quantile(1.0 / n_adm))
    frac0 = jnp.clip(frac0, 0.0, 1.0)
    _, lo, hi, c_lo, c_above, _ = lax.while_loop(
        search_cond, search_body, (jnp.int32(0), lo0, hi0, n_adm, jnp.zeros((1, t), F32), jnp.int32(1)))

    def below_hi_max(j, m):
        kk = keys_ref[pl.ds(pl.multiple_of(j * gt, gt), gt), :]
        return jnp.maximum(m, jnp.max(jnp.where(kk < hi, kk, INT_MIN).reshape(gt // 8, 8, t), axis=0))

    m8 = lax.fori_loop(0, n_groups, below_hi_max, jnp.full((8, t), INT_MIN, I32))
    for shift in (4, 2, 1):
        m8 = jnp.maximum(m8, pltpu.roll(m8, shift, axis=0))
    wide = lo + 1 < hi
    thr = jnp.where(wide, m8[:1], lo)
    c_eq = count_groups(lambda kk, r0: jnp.where(kk == thr, 1.0, 0.0))
    c_thr = jnp.where(wide, c_above + c_eq, c_lo)

    @pl.when(jnp.max(jnp.where(jnp.logical_and(c_thr > topk, qpos < n_keys), 1, 0)) > 0)
    def _():
        need = topk - c_above
        row = lax.broadcasted_iota(I32, (gt, t), 0)

        def cut_body(_, st):
            p_lo, p_hi = st
            p_mid = (p_lo + p_hi) >> 1
            c = count_groups(lambda kk, r0: jnp.where(
                jnp.logical_and(kk == thr, row + r0 <= p_mid), 1.0, 0.0))
            ok = c >= need
            return jnp.where(ok, p_lo, p_mid), jnp.where(ok, p_mid, p_hi)

        n_pos = keys_ref.shape[0]
        _, cut = lax.fori_loop(0, n_pos.bit_length(), cut_body,
                               (jnp.full((1, t), -1, I32), jnp.full((1, t), n_pos - 1, I32)))

        def demote_body(j, carry):
            r0 = pl.multiple_of(j * gt, gt)
            kk = keys_ref[pl.ds(r0, gt), :]
            keys_ref[pl.ds(r0, gt), :] = jnp.where(jnp.logical_and(kk == thr, row + r0 > cut), thr - 1, kk)
            return carry

        lax.fori_loop(0, n_groups, demote_body, 0)

    m_ref[...] = jnp.full(m_ref.shape, NEG, F32)
    acc_ref[...] = jnp.zeros(acc_ref.shape, F32)

    for h in range(H_B):
        qs_ref[h // G_B, :, (h % G_B) * t:(h % G_B + 1) * t] = qb_ref[h]

    def produce(j, s_ref, kind):
        ks = pl.multiple_of(j * t, t)
        off = jnp.where(keys_ref[pl.ds(ks, t), :] >= thr, 0.0, NEG)
        for n in range(HKV_B):
            s = jnp.dot(kb_ref[pl.ds(ks, t), n * HD_B:(n + 1) * HD_B], qs_ref[n], preferred_element_type=F32)
            parts = []
            for g in range(G_B):
                off_g = off if kind is None else off + bias_ref[n * G_B + g, kind]
                parts.append(s[:, g * t:(g + 1) * t] + off_g)
            s_ref[n] = jnp.concatenate(parts, axis=1)

    def consume(j, s_ref):
        ks = pl.multiple_of(j * t, t)
        for n in range(HKV_B):
            _softmax_step(s_ref.at[n], vbt_ref[n, :, pl.ds(ks, t)], m_ref.at[n], acc_ref.at[n], 2 * t)

    _pipelined_tiles(n_far, produce, consume, s0_ref, s1_ref)

    for n in range(HKV_B):
        o = _normalised(acc_ref[n], HD_B)
        for g in range(G_B):
            h = n * G_B + g
            o_ref[:, h * HD_B:(h + 1) * HD_B] = o[:, g * t:(g + 1) * t].T.astype(o_ref.dtype)


def _dsa(qb, qi, wi_t, kb, vbt, ki2, bias, *, t, q0, topk, n_keys, n_valid_diag):
    b, _, _, nq = qb.shape
    nk = kb.shape[1]
    return pl.pallas_call(
        functools.partial(_dsa_body, t, q0, topk, n_keys, n_valid_diag),
        grid=(b, nq // t),
        in_specs=[
            pl.BlockSpec((None, H_B, HD_B, t), lambda bb, i: (bb, 0, 0, i)),
            pl.BlockSpec((None, H_IDX // 2, LANES, t), lambda bb, i: (bb, 0, 0, i)),
            pl.BlockSpec((None, H_IDX, t), lambda bb, i: (bb, 0, i)),
            _resident((None, nk, HKV_B * HD_B), lambda bb, i: (bb, 0, 0)),
            _resident((None, HKV_B, HD_B, nk), lambda bb, i: (bb, 0, 0, 0)),
            _resident((None, nk, LANES), lambda bb, i: (bb, 0, 0)),
            _resident((H_B, 2, t, t), lambda bb, i: (0, 0, 0, 0)),
        ],
        out_specs=pl.BlockSpec((None, t, H_B * HD_B), lambda bb, i: (bb, i, 0)),
        out_shape=jax.ShapeDtypeStruct((b, nq, H_B * HD_B), BF16),
        scratch_shapes=[
            pltpu.VMEM((nk + (GROUP - 1) * t, t), I32),
            pltpu.VMEM((H_IDX, LANES, t), BF16),
            pltpu.VMEM((HKV_B, HD_B, G_B * t), BF16),
            pltpu.VMEM((2, 8, t), F32),
            pltpu.VMEM((HKV_B, 1, G_B * t), F32),
            pltpu.VMEM((HKV_B, HD_B + ONES_ROWS, G_B * t), F32),
            pltpu.VMEM((HKV_B, t, G_B * t), F32),
            pltpu.VMEM((HKV_B, t, G_B * t), F32),
        ],
        compiler_params=_cparams(("parallel", "arbitrary")),
        name="dsa",
    )(qb, qi, wi_t, kb, vbt, ki2, bias)


def _softmax_pieces(s_list, v_list):
    m = functools.reduce(jnp.maximum, [jnp.max(s, axis=-1, keepdims=True) for s in s_list])
    p_list = [jnp.exp2(s - m) for s in s_list]
    o = sum(jnp.dot(p.astype(BF16), v, preferred_element_type=F32) for p, v in zip(p_list, v_list))
    return o / sum(jnp.sum(p, axis=-1, keepdims=True) for p in p_list)


def _head_rows(ref, h, n_heads, n_pos):
    return ref[pl.ds(h, n_pos, stride=n_heads), :].astype(BF16)


def _short_a_body(nq, n_past, out_scale, lam_ref, q_ref, kp_ref, vp_ref, kn_ref, vn_ref, bias_ref, g_ref, o_ref):
    lam = lam_ref[0]
    lane = lax.broadcasted_iota(I32, (nq, LANES), 1)
    pieces = ((kp_ref, vp_ref, 0, n_past), (kn_ref, vn_ref, n_past, nq))
    for h in range(H_A):
        cols = slice(h * LANES, (h + 1) * LANES)
        q = q_ref[:, cols]
        zero = jnp.zeros_like(q)
        qz = jnp.concatenate([jnp.where(lane < HD_A, q, zero), jnp.where(lane < HD_A, zero, q)], axis=0)
        b = bias_ref[h]
        b2 = jnp.concatenate([b, b], axis=0)
        s_list = [lax.dot_general(qz, _head_rows(k_ref, h, H_A, size), _NT, preferred_element_type=F32)
                  + b2[:, lo:lo + size] for k_ref, _, lo, size in pieces]
        o = _softmax_pieces(s_list, [_head_rows(v_ref, h, H_A, size) for _, v_ref, _, size in pieces])
        o = o[:nq] - lam * o[nq:]
        y = o * lax.rsqrt(jnp.mean(o * o, axis=-1, keepdims=True) + EPS)
        o_ref[:, cols] = ((y * g_ref[...]) * out_scale).astype(o_ref.dtype)


def _short_a(q, k_past, v_past, k_new, v_new, bias, subln_g, lam, *, out_scale):
    b, nq, d = q.shape
    n_past = k_past.shape[1]
    assert n_past % LANES == 0
    rows = lambda n: pl.BlockSpec((None, n * H_A, DV_A), lambda bb: (bb, 0, 0))
    merge = lambda a: a.reshape(b, a.shape[1] * H_A, DV_A)
    return pl.pallas_call(
        functools.partial(_short_a_body, nq, n_past, out_scale),
        grid=(b,),
        in_specs=[
            pl.BlockSpec(memory_space=pltpu.SMEM),
            pl.BlockSpec((None, nq, d), lambda bb: (bb, 0, 0)),
            rows(n_past), rows(n_past), rows(nq), rows(nq),
            pl.BlockSpec(bias.shape, lambda bb: (0, 0, 0)),
            pl.BlockSpec((1, DV_A), lambda bb: (0, 0)),
        ],
        out_specs=pl.BlockSpec((None, nq, d), lambda bb: (bb, 0, 0)),
        out_shape=jax.ShapeDtypeStruct((b, nq, d), BF16),
        compiler_params=_cparams(("parallel",)),
        name="diff_attn_short",
    )(lam, q, merge(k_past), merge(v_past), merge(k_new), merge(v_new), bias, subln_g)


def _short_dsa_body(nq, n_past, topk, qb_ref, qi_ref, wi_ref, kbp_ref, vbp_ref, kip_ref, kbn_ref, vbn_ref,
                    kin_ref, bias_ref, o_ref):
    lane = lax.broadcasted_iota(I32, (nq, LANES), 1)
    spans = ((0, n_past), (n_past, nq))

    qz = []
    for h in range(H_IDX):
        qs = qi_ref[:, (h // 2) * LANES:(h // 2 + 1) * LANES]
        zero = jnp.zeros_like(qs)
        qz.append(jnp.where(lane < D_IDX, qs, zero) if h % 2 == 0 else jnp.where(lane < D_IDX, zero, qs))
    qz = jnp.concatenate(qz, axis=0)
    keys, kpos = [], []
    for ki_ref, (lo, size) in zip((kip_ref, kin_ref), spans):
        ki = ki_ref[...].astype(BF16)
        s_idx = lax.dot_general(qz, jnp.concatenate([ki, ki], axis=-1), _NT, preferred_element_type=F32)
        score = jnp.zeros((nq, size), F32)
        for h in range(H_IDX):
            score = score + jnp.maximum(s_idx[h * nq:(h + 1) * nq], 0.0) * wi_ref[:, h:h + 1]
        qp = n_past + lax.broadcasted_iota(I32, (nq, size), 0)
        kp = lo + lax.broadcasted_iota(I32, (nq, size), 1)
        keys.append(jnp.where((kp // CHUNK) <= (qp // CHUNK), _order_key(score), INT_MIN))
        kpos.append(kp)

    def count(hit_fn):
        return sum(jnp.sum(jnp.where(hit_fn(k, kp), 1.0, 0.0), axis=-1, keepdims=True)
                   for k, kp in zip(keys, kpos))

    def search_body(state):
        it, lo, hi, c_lo, c_hi, _ = state
        mid = (lo >> 1) + (hi >> 1) + (lo & hi & 1)
        c = count(lambda k, kp: k >= mid)
        live = lo + 1 < hi
        ge = c >= topk
        up = jnp.logical_and(live, ge)
        down = jnp.logical_and(live, jnp.logical_not(ge))
        lo = jnp.where(up, mid, lo)
        c_lo = jnp.where(up, c, c_lo)
        hi = jnp.where(jnp.logical_and(live, c == topk), mid + 1, jnp.where(down, mid, hi))
        c_hi = jnp.where(down, c, c_hi)
        return it + 1, lo, hi, c_lo, c_hi, jnp.max(jnp.where(lo + 1 < hi, 1, 0))

    n_keys = n_past + nq
    n_adm = count(lambda k, kp: k > INT_MIN)
    lo0 = jnp.full((nq, 1), INT_MIN + 1, I32)
    hi0 = jnp.where(n_adm <= topk, lo0 + 1, INT_MAX)
    _, thr, _, c_thr, c_above, _ = lax.while_loop(
        lambda st: jnp.logical_and(st[0] < 34, st[-1] > 0), search_body,
        (jnp.int32(0), lo0, hi0, n_adm, jnp.zeros((nq, 1), F32), jnp.int32(1)))

    need = topk - c_above

    def cut_body(_, st):
        p_lo, p_hi = st
        p_mid = (p_lo + p_hi) >> 1
        ok = count(lambda k, kp: jnp.logical_and(k == thr, kp <= p_mid)) >= need
        return jnp.where(ok, p_lo, p_mid), jnp.where(ok, p_mid, p_hi)

    _, cut = lax.fori_loop(0, n_keys.bit_length(), cut_body,
                           (jnp.full((nq, 1), -1, I32), jnp.full((nq, 1), n_keys - 1, I32)))
    off = [jnp.where(jnp.logical_and(k >= thr, jnp.logical_not(jnp.logical_and(k == thr, kp > cut))), 0.0, NEG)
           for k, kp in zip(keys, kpos)]

    for n in range(HKV_B):
        heads = range(n * G_B, (n + 1) * G_B)
        qs = jnp.concatenate([qb_ref[:, h * HD_B:(h + 1) * HD_B] for h in heads], axis=0)
        s_list = []
        for kb_ref, off_i, (lo, size) in zip((kbp_ref, kbn_ref), off, spans):
            s = lax.dot_general(qs, _head_rows(kb_ref, n, HKV_B, size), _NT, preferred_element_type=F32)
            s_list.append(s + jnp.concatenate([bias_ref[h][:, lo:lo + size] + off_i for h in heads], axis=0))
        o = _softmax_pieces(s_list, [_head_rows(vbp_ref, n, HKV_B, n_past), _head_rows(vbn_ref, n, HKV_B, nq)])
        for g, h in enumerate(heads):
            o_ref[:, h * HD_B:(h + 1) * HD_B] = o[g * nq:(g + 1) * nq].astype(o_ref.dtype)


def _short_dsa(qb, qi, wi, kb_past, vb_past, ki_past, kb_new, vb_new, ki_new, bias, *, topk):
    b, nq, d = qb.shape
    n_past = kb_past.shape[1]
    assert n_past % LANES == 0
    per_batch = lambda *shape: pl.BlockSpec((None,) + shape, lambda bb: (bb,) + (0,) * len(shape))
    merge = lambda a: a.reshape(b, a.shape[1] * HKV_B, HD_B)
    return pl.pallas_call(
        functools.partial(_short_dsa_body, nq, n_past, topk),
        grid=(b,),
        in_specs=[per_batch(nq, d), per_batch(nq, d), per_batch(nq, H_IDX),
                  per_batch(n_past * HKV_B, HD_B), per_batch(n_past * HKV_B, HD_B), per_batch(n_past, D_IDX),
                  per_batch(nq * HKV_B, HD_B), per_batch(nq * HKV_B, HD_B), per_batch(nq, D_IDX),
                  pl.BlockSpec(bias.shape, lambda bb: (0, 0, 0))],
        out_specs=per_batch(nq, d),
        out_shape=jax.ShapeDtypeStruct((b, nq, d), BF16),
        compiler_params=_cparams(("parallel",)),
        name="dsa_short",
    )(qb, qi, wi, merge(kb_past), merge(vb_past), ki_past, merge(kb_new), merge(vb_new), ki_new, bias)


def _sigmoid(x):
    return 1.0 / (1.0 + jnp.exp(-x))


def _merge_body(h_ref, oa_ref, ob_ref, wg_ref, woa_ref, wob_ref, o_ref):
    h = h_ref[...]
    ga = jnp.dot(h, wg_ref[:, :D_MODEL], preferred_element_type=F32)
    ya = jnp.dot(oa_ref[...], woa_ref[...], preferred_element_type=F32)
    merged = _sigmoid(ga) * ya
    gb = jnp.dot(h, wg_ref[:, D_MODEL:], preferred_element_type=F32)
    yb = jnp.dot(ob_ref[...], wob_ref[...], preferred_element_type=F32)
    o_ref[...] = (merged + _sigmoid(gb) * yb).astype(o_ref.dtype)


def _merge(h, oa, ob, wg, woa, wob, tm):
    n = h.shape[0]
    return pl.pallas_call(
        _merge_body,
        grid=(n // tm,),
        in_specs=[
            pl.BlockSpec((tm, D_MODEL), lambda i: (i, 0)),
            pl.BlockSpec((tm, H_A * DV_A), lambda i: (i, 0)),
            pl.BlockSpec((tm, H_B * HD_B), lambda i: (i, 0)),
            _resident((D_MODEL, 2 * D_MODEL), lambda i: (0, 0)),
            _resident((H_A * DV_A, D_MODEL), lambda i: (0, 0)),
            _resident((H_B * HD_B, D_MODEL), lambda i: (0, 0)),
        ],
        out_specs=pl.BlockSpec((tm, D_MODEL), lambda i: (i, 0)),
        out_shape=jax.ShapeDtypeStruct((n, D_MODEL), BF16),
        compiler_params=_cparams(("parallel",)),
        name="gated_merge",
    )(h, oa, ob, wg, woa, wob)


def _outproj_body(x_ref, mg_ref, w_ref, g_ref, x1_ref, h2_ref):
    x1 = x_ref[...] + jnp.dot(mg_ref[...], w_ref[...], preferred_element_type=F32)
    x1_ref[...] = x1
    y = x1 * lax.rsqrt(jnp.mean(x1 * x1, axis=-1, keepdims=True) + EPS)
    h2_ref[...] = (y * g_ref[...]).astype(h2_ref.dtype)


def _outproj(x, merged, w_out, g2, tm):
    n = x.shape[0]
    return pl.pallas_call(
        _outproj_body,
        grid=(n // tm,),
        in_specs=[
            pl.BlockSpec((tm, D_MODEL), lambda i: (i, 0)),
            pl.BlockSpec((tm, D_MODEL), lambda i: (i, 0)),
            _resident((D_MODEL, D_MODEL), lambda i: (0, 0)),
            pl.BlockSpec((1, D_MODEL), lambda i: (0, 0)),
        ],
        out_specs=[pl.BlockSpec((tm, D_MODEL), lambda i: (i, 0)), pl.BlockSpec((tm, D_MODEL), lambda i: (i, 0))],
        out_shape=[jax.ShapeDtypeStruct((n, D_MODEL), F32), jax.ShapeDtypeStruct((n, D_MODEL), BF16)],
        compiler_params=_cparams(("parallel",)),
        name="out_proj",
    )(x, merged, w_out, g2.reshape(1, D_MODEL))


def _ffn_body(x1_ref, h2_ref, w1_ref, w2_ref, o_ref):
    f = pl.program_id(1)

    @pl.when(f == 0)
    def _():
        o_ref[...] = x1_ref[...]

    u = jnp.maximum(jnp.dot(h2_ref[...], w1_ref[...], preferred_element_type=F32), 0.0)
    o_ref[...] += jnp.dot((u * u).astype(BF16), w2_ref[...], preferred_element_type=F32)


def _ffn(x1, h2, w1, w2, tm, tf):
    n = x1.shape[0]
    return pl.pallas_call(
        _ffn_body,
        grid=(n // tm, D_FF // tf),
        in_specs=[
            pl.BlockSpec((tm, D_MODEL), lambda i, f: (i, 0)),
            pl.BlockSpec((tm, D_MODEL), lambda i, f: (i, 0)),
            pl.BlockSpec((D_MODEL, tf), lambda i, f: (0, f)),
            pl.BlockSpec((tf, D_MODEL), lambda i, f: (f, 0)),
        ],
        out_specs=pl.BlockSpec((tm, D_MODEL), lambda i, f: (i, 0)),
        out_shape=jax.ShapeDtypeStruct((n, D_MODEL), F32),
        compiler_params=_cparams(("parallel", "arbitrary")),
        name="ffn",
    )(x1, h2, w1, w2)


def _prep_weights(w_in, w_o_a, w_o_b, w_out, w_ff1, w_ff2):
    sizes = (H_A * 2 * HD_A, H_A * 2 * HD_A, H_A * DV_A, H_B * HD_B, HKV_B * HD_B, HKV_B * HD_B,
             H_IDX * D_IDX, D_IDX, H_IDX, D_MODEL, D_MODEL)
    offs = [0]
    for s in sizes:
        offs.append(offs[-1] + s)
    col = lambda a, b: w_in[:, offs[a]:offs[b]]
    w_misc = jnp.concatenate(
        [col(4, 6), col(7, 8), col(7, 8), col(8, 9),
         jnp.zeros((D_MODEL, _MISC_COLS - _WI0 - H_IDX), w_in.dtype)], axis=1)
    return dict(
        qa=col(0, 1).astype(BF16), ka=col(1, 2).astype(BF16), va=col(2, 3).astype(BF16),
        qb=col(3, 4).astype(BF16), misc=w_misc.astype(BF16), qi=col(6, 7).astype(BF16),
        gate=col(9, 11).astype(BF16), oa=w_o_a.astype(BF16), ob=w_o_b.astype(BF16),
        out=w_out.astype(BF16), ff1=w_ff1.astype(BF16), ff2=w_ff2.astype(BF16))


def _layer(x, past, w, p, lam, lam_init, tab_t, *, tm, t_a=None, t_b=None):
    short = t_a is None
    b, t, _ = x.shape
    n = b * t
    xf = x.reshape(n, D_MODEL)
    h = _rmsnorm(xf, p["norm1_g"], tm)
    tp = min(2 * tm, n)

    gain_a = lambda g: g.reshape(1, 2 * HD_A)
    v_layout = HEADS_T if (not short and past is None and b == 1) else FLAT
    q_layout = FLAT if short else HEADS_T
    (qa,) = _proj(h, w["qa"], [gain_a(p["qn_a_g"])], _epi_qa, [(1024, BF16, q_layout)], tp, "proj_qa")
    ka, ka_h = _proj(h, w["ka"], [gain_a(p["kn_a_g"])], _epi_ka, [(1024, F32, HEADS), (1024, BF16, FLAT)],
                     tp, "proj_ka")
    va, va_h = _proj(h, w["va"], [], _epi_copy2, [(1024, F32, HEADS), (1024, BF16, v_layout)], tp, "proj_va")
    (qb,) = _proj(h, w["qb"], [p["qn_b_g"].reshape(1, HD_B)], _epi_qb, [(1024, BF16, q_layout)], tp, "proj_qb")
    kb, kb_h, vb, vb_h, ki, ki2_h, wi = _proj(
        h, w["misc"], [p["kn_b_g"].reshape(1, HD_B)], _epi_misc,
        [(256, F32, HEADS), (256, BF16, FLAT), (256, F32, HEADS), (256, BF16, v_layout), (D_IDX, F32, FLAT),
         (2 * D_IDX, BF16, FLAT), (H_IDX, F32, FLAT)],
        tp, "proj_misc")
    (qi,) = _proj(h, w["qi"], [], _epi_copy1, [(1024, BF16, q_layout)], tp, "proj_qi")

    new_rows = (ka.reshape(b, t, H_A, 2 * HD_A), va.reshape(b, t, H_A, DV_A), kb.reshape(b, t, HKV_B, HD_B),
                vb.reshape(b, t, HKV_B, HD_B), ki.reshape(b, t, D_IDX))

    per_b = lambda a: a.reshape(b, t, -1)
    if short:
        n_keys = past[0].shape[1] + t
        topk = min(TOPK_MAX, n_keys // 4)
        oa = _short_a(per_b(qa), past[0], past[1], new_rows[0], new_rows[1],
                      _bias_dense(tab_t[:H_A], t, n_keys, n_keys - t, n_keys),
                      p["subln_a_g"].reshape(1, DV_A), lam, out_scale=1.0 - lam_init)
        ob = _short_dsa(per_b(qb), per_b(qi), per_b(wi), past[2], past[3], past[4], *new_rows[2:],
                        _bias_dense(tab_t[H_A:], t, n_keys, n_keys - t, n_keys), topk=topk)
    else:
        oa, ob = _tiled_mixers(qa, qb, qi, wi, ka_h, va_h, kb_h, vb_h, ki2_h, past, tab_t, p, lam, lam_init,
                               b=b, t=t, t_a=t_a, t_b=t_b)
    oa = oa.reshape(n, H_A * DV_A)
    ob = ob.reshape(n, H_B * HD_B)

    merged = _merge(h, oa, ob, w["gate"], w["oa"], w["ob"], min(tm, 256))
    x1, h2 = _outproj(xf, merged, w["out"], p["norm2_g"], min(tm, 256))
    y = _ffn(x1, h2, w["ff1"], w["ff2"], tm, 1024)
    return y.reshape(b, t, D_MODEL), new_rows


def _tiled_mixers(qa, qb, qi, wi, ka_h, va_h, kb_h, vb_h, ki2_h, past, tab_t, p, lam, lam_init,
                  *, b, t, t_a, t_b):
    def full_keys(new_h, past_arr):
        if new_h.ndim == 3:
            return new_h[None]
        new_h = new_h.reshape(b, t, -1)
        if past_arr is None:
            return new_h
        past_h = past_arr.reshape(b, past_arr.shape[1], -1).astype(BF16)
        return jnp.concatenate([past_h, new_h], axis=1)

    if past is None:
        pa_k = pa_v = pb_k = pb_v = pb_i = None
    else:
        pa_k, pa_v, pb_k, pb_v, pb_i = past
        pb_i = jnp.concatenate([pb_i, pb_i], axis=-1)
    ka_f, va_f = full_keys(ka_h, pa_k), full_keys(va_h, pa_v)
    kb_f, vb_f, ki_f = full_keys(kb_h, pb_k), full_keys(vb_h, pb_v), full_keys(ki2_h, pb_i)
    n_keys = ka_f.shape[1]
    q0 = n_keys - t
    topk = min(TOPK_MAX, n_keys // 4)
    t_pad = max(t_a, t_b)
    assert t_pad % t_a == 0 and t_pad % t_b == 0 and q0 % t_pad == 0
    nk_pad = -(-n_keys // t_pad) * t_pad
    pad_k = lambda a: jnp.pad(a, ((0, 0), (0, nk_pad - n_keys), (0, 0)))
    pad_q = lambda a, tt: jnp.pad(a.reshape(b, t, -1), ((0, 0), (0, -(-t // tt) * tt - t), (0, 0)))
    ka_f, kb_f, ki_f = (pad_k(a) for a in (ka_f, kb_f, ki_f))
    n_valid = lambda tt: n_keys - (n_keys - 1) // tt * tt
    assert all(t <= tt or n_valid(tt) == tt for tt in (t_a, t_b))

    def heads_t(v, nh):
        if v.ndim == 3:
            v = jnp.transpose(v.reshape(b, n_keys, nh, -1), (0, 2, 3, 1))
        return jnp.pad(v, ((0, 0), (0, 0), (0, 0), (0, nk_pad - n_keys)))

    def q_slabs(q, tt):
        q = jnp.transpose(q.reshape(q.shape[0], LANES, b, t), (2, 0, 1, 3))
        return jnp.pad(q, ((0, 0), (0, 0), (0, 0), (0, -(-t // tt) * tt - t)))

    bias_a = _bias_tiles(tab_t[:H_A], t_a, n_valid(t_a))
    oa = _attn_a(q_slabs(qa, t_a), ka_f, heads_t(va_f, H_A), bias_a, p["subln_a_g"].reshape(DV_A, 1), lam,
                 t=t_a, q0=q0, out_scale=1.0 - lam_init)
    wi_t = jnp.swapaxes(pad_q(wi, t_b), 1, 2)
    bias_b = _bias_tiles(tab_t[H_A:], t_b, n_valid(t_b))
    ob = _dsa(q_slabs(qb, t_b), q_slabs(qi, t_b), wi_t, kb_f, heads_t(vb_f, HKV_B), ki_f, bias_b,
              t=t_b, q0=q0, topk=topk, n_keys=n_keys, n_valid_diag=n_valid(t_b))
    return oa[:, :t], ob[:, :t]


def kernel(x_prompt, x_sample, cache_a_k, cache_a_v, cache_b_k, cache_b_v, cache_b_kidx, rel_bias, norm1_g, w_in, qn_a_g, kn_a_g, lam_q1, lam_k1, lam_q2, lam_k2, subln_a_g, qn_b_g, kn_b_g, w_o_a, w_o_b, w_out, norm2_g, w_ff1, w_ff2):
    depth = w_in.shape[0]
    tab_t = rel_bias.T.astype(F32)
    y_prompt, y_sample = x_prompt, x_sample
    prompt_rows, sample_rows = [], []
    for l in range(depth):
        lam_init = 0.8 - 0.6 * math.exp(-0.3 * l)
        lam = (jnp.exp(jnp.sum(lam_q1[l].astype(F32) * lam_k1[l].astype(F32)))
               - jnp.exp(jnp.sum(lam_q2[l].astype(F32) * lam_k2[l].astype(F32))) + lam_init).reshape(1)
        w = _prep_weights(w_in[l], w_o_a[l], w_o_b[l], w_out[l], w_ff1[l], w_ff2[l])
        p = dict(norm1_g=norm1_g[l], qn_a_g=qn_a_g[l], kn_a_g=kn_a_g[l], subln_a_g=subln_a_g[l],
                 qn_b_g=qn_b_g[l], kn_b_g=kn_b_g[l], norm2_g=norm2_g[l])
        y_prompt, rp = _layer(y_prompt, None, w, p, lam, lam_init, tab_t, t_a=512, t_b=256, tm=512)
        past = (cache_a_k[l], cache_a_v[l], cache_b_k[l], cache_b_v[l], cache_b_kidx[l])
        y_sample, rs = _layer(y_sample, past, w, p, lam, lam_init, tab_t, tm=512)
        prompt_rows.append(rp)
        sample_rows.append(rs)
    p_rows = tuple(jnp.stack(r, axis=0) for r in zip(*prompt_rows))
    s_rows = tuple(jnp.stack(r, axis=0) for r in zip(*sample_rows))
    return (y_prompt, y_sample) + p_rows + s_rows
```

```python
import functools
import math

import jax
import jax.numpy as jnp
from jax import lax
from jax.experimental import pallas as pl
from jax.experimental.pallas import tpu as pltpu

F32 = jnp.float32
BF16 = jnp.bfloat16
I32 = jnp.int32

D_MODEL = 2048
CHUNK = 64
H_A = 8
DV_A = 128
HD_A = 64
H_B = 8
HD_B = 128
HKV_B = 2
G_B = H_B // HKV_B
H_IDX = 16
D_IDX = 64
TOPK_MAX = 256
N_BUCKETS = 32
D_FF = 4 * D_MODEL
EPS = 1e-6
LANES = 128
NEG = -1e30
LOG2E = math.log2(math.e)
INT_MIN = -(2 ** 31)
INT_MAX = 2 ** 31 - 1
VMEM_LIMIT = 56 * 1024 * 1024

_NT = (((1,), (1,)), ((), ()))


def _cparams(sem):
    return pltpu.CompilerParams(dimension_semantics=sem, vmem_limit_bytes=VMEM_LIMIT)


def _resident(block_shape, index_map):
    return pl.BlockSpec(block_shape, index_map, pipeline_mode=pl.Buffered(1))


def _rmsnorm_body(x_ref, g_ref, o_ref):
    x = x_ref[...]
    y = x * lax.rsqrt(jnp.mean(x * x, axis=-1, keepdims=True) + EPS)
    o_ref[...] = (y * g_ref[...]).astype(o_ref.dtype)


def _rmsnorm(x, g, tm):
    n, d = x.shape
    return pl.pallas_call(
        _rmsnorm_body,
        grid=(n // tm,),
        in_specs=[pl.BlockSpec((tm, d), lambda i: (i, 0)), pl.BlockSpec((1, d), lambda i: (0, 0))],
        out_specs=pl.BlockSpec((tm, d), lambda i: (i, 0)),
        out_shape=jax.ShapeDtypeStruct((n, d), BF16),
        compiler_params=_cparams(("parallel",)),
        name="rmsnorm",
    )(x, g.reshape(1, d))


def _head_norm(z, gain, split):
    lane = lax.broadcasted_iota(I32, (1, LANES), 1)
    lo = lane < (LANES // 2)
    outs = []
    for h in range(z.shape[1] // LANES):
        zh = z[:, h * LANES:(h + 1) * LANES]
        sq = zh * zh
        if split:
            s_lo = jnp.sum(jnp.where(lo, sq, 0.0), axis=-1, keepdims=True)
            s_hi = jnp.sum(jnp.where(lo, 0.0, sq), axis=-1, keepdims=True)
            r = jnp.where(lo, lax.rsqrt(s_lo * (2.0 / LANES) + EPS), lax.rsqrt(s_hi * (2.0 / LANES) + EPS))
        else:
            r = lax.rsqrt(jnp.mean(sq, axis=-1, keepdims=True) + EPS)
        outs.append(zh * r * gain)
    return jnp.concatenate(outs, axis=-1)


FLAT, HEADS, HEADS_T = "tokens x cols", "tokens x heads x 128", "heads x 128 x tokens"


def _proj_body(epilogue, n_aux, layouts, h_ref, w_ref, *rest):
    z = jnp.dot(h_ref[...], w_ref[...], preferred_element_type=F32)
    outs = epilogue(z, *[r[...] for r in rest[:n_aux]])
    for o_ref, o, layout in zip(rest[n_aux:], outs, layouts):
        if layout == FLAT:
            o_ref[...] = o.astype(o_ref.dtype)
            continue
        for hd in range(o.shape[1] // LANES):
            slab = o[:, hd * LANES:(hd + 1) * LANES]
            if layout == HEADS:
                o_ref[:, hd, :] = slab.astype(o_ref.dtype)
            else:
                o_ref[hd] = slab.T.astype(o_ref.dtype)


def _proj(h, w, aux, epilogue, out_defs, tm, name):
    n, k = h.shape
    c = w.shape[1]
    in_specs = [pl.BlockSpec((tm, k), lambda i: (i, 0)), _resident((k, c), lambda i: (0, 0))]
    in_specs += [pl.BlockSpec(a.shape, lambda i: (0, 0)) for a in aux]
    out_specs, out_shape = [], []
    for oc, dt, layout in out_defs:
        nh = oc // LANES
        if layout == FLAT:
            out_specs.append(pl.BlockSpec((tm, oc), lambda i: (i, 0)))
            out_shape.append(jax.ShapeDtypeStruct((n, oc), dt))
        elif layout == HEADS:
            out_specs.append(pl.BlockSpec((tm, nh, LANES), lambda i: (i, 0, 0)))
            out_shape.append(jax.ShapeDtypeStruct((n, nh, LANES), dt))
        else:
            out_specs.append(pl.BlockSpec((nh, LANES, tm), lambda i: (0, 0, i)))
            out_shape.append(jax.ShapeDtypeStruct((nh, LANES, n), dt))
    return pl.pallas_call(
        functools.partial(_proj_body, epilogue, len(aux), tuple(d[2] for d in out_defs)),
        grid=(n // tm,),
        in_specs=in_specs,
        out_specs=out_specs,
        out_shape=out_shape,
        compiler_params=_cparams(("parallel",)),
        name=name,
    )(h, w, *aux)


def _epi_qa(z, gain):
    return (_head_norm(z, gain, True) * (HD_A ** -0.5 * LOG2E),)


def _epi_ka(z, gain):
    y = _head_norm(z, gain, True)
    return y, y


def _epi_copy2(z):
    return z, z


def _epi_qb(z, gain):
    return (_head_norm(z, gain, False) * (HD_B ** -0.5 * LOG2E),)


def _epi_copy1(z):
    return (z,)


_KB0, _VB0, _KI0, _WI0, _MISC_COLS = 0, 256, 512, 640, 768


def _epi_misc(z, gain):
    kb = _head_norm(z[:, _KB0:_VB0], gain, False)
    vb = z[:, _VB0:_KI0]
    ki2 = z[:, _KI0:_WI0]
    return kb, kb, vb, vb, ki2[:, :D_IDX], ki2, z[:, _WI0:_WI0 + H_IDX]


def _rel_bias(tab_ref, h, rel):
    half = N_BUCKETS // 2
    exact = half // 2
    n = jnp.abs(rel)
    n2 = n * n
    v_neg = jnp.full(rel.shape, tab_ref[h, 0], F32)
    v_pos = jnp.full(rel.shape, tab_ref[h, half], F32)
    for b in range(1, half):
        cond = (n >= b) if b < exact else (n2 >= exact * exact * 2 ** (b - exact))
        v_neg = jnp.where(cond, tab_ref[h, b], v_neg)
        v_pos = jnp.where(cond, tab_ref[h, half + b], v_pos)
    return (jnp.where(rel > 0, v_pos, v_neg) - tab_ref[h, half - 1]) * LOG2E


def _bias_body(t, n_valid_diag, tab_ref, o_ref):
    h = pl.program_id(0)
    kl = lax.broadcasted_iota(I32, (t, t), 0)
    ql = lax.broadcasted_iota(I32, (t, t), 1)
    o_ref[0] = _rel_bias(tab_ref, h, kl - t - ql)
    visible = ((kl // CHUNK) <= (ql // CHUNK)) & (kl < n_valid_diag)
    o_ref[1] = jnp.where(visible, _rel_bias(tab_ref, h, kl - ql), NEG)


def _bias_dense_body(q0, n_keys, tab_ref, o_ref):
    h = pl.program_id(0)
    qpos = q0 + lax.broadcasted_iota(I32, o_ref.shape, 0)
    kpos = lax.broadcasted_iota(I32, o_ref.shape, 1)
    visible = ((kpos // CHUNK) <= (qpos // CHUNK)) & (kpos < n_keys)
    o_ref[...] = jnp.where(visible, _rel_bias(tab_ref, h, kpos - qpos), NEG)


def _bias_dense(tab_t, nq, nk_pad, q0, n_keys):
    nh = tab_t.shape[0]
    return pl.pallas_call(
        functools.partial(_bias_dense_body, q0, n_keys),
        grid=(nh,),
        in_specs=[pl.BlockSpec(memory_space=pltpu.SMEM)],
        out_specs=pl.BlockSpec((None, nq, nk_pad), lambda h: (h, 0, 0)),
        out_shape=jax.ShapeDtypeStruct((nh, nq, nk_pad), F32),
        compiler_params=_cparams(("arbitrary",)),
        name="bias_dense",
    )(tab_t)


def _bias_tiles(tab_t, t, n_valid_diag):
    assert (t + 1) ** 2 >= (N_BUCKETS // 4) ** 2 * 2 ** (N_BUCKETS // 2 - 1 - N_BUCKETS // 4)
    nh = tab_t.shape[0]
    return pl.pallas_call(
        functools.partial(_bias_body, t, n_valid_diag),
        grid=(nh,),
        in_specs=[pl.BlockSpec(memory_space=pltpu.SMEM)],
        out_specs=pl.BlockSpec((None, 2, t, t), lambda h: (h, 0, 0, 0)),
        out_shape=jax.ShapeDtypeStruct((nh, 2, t, t), F32),
        compiler_params=_cparams(("arbitrary",)),
        name="bias_tiles",
    )(tab_t)


ONES_ROWS = 16


def _softmax_step(s_ref, vt, m_ref, acc_ref, width):
    vt_ones = jnp.concatenate([vt, jnp.ones((ONES_ROWS, vt.shape[1]), vt.dtype)], axis=0)
    for c0 in range(0, s_ref.shape[1], width):
        cols = slice(c0, c0 + width)
        s = s_ref[:, cols]
        m_old = m_ref[:, cols]
        m_new = jnp.maximum(m_old, jnp.max(s, axis=0, keepdims=True))
        alpha = jnp.exp2(m_old - m_new)
        p = jnp.exp2(s - m_new)
        acc_ref[:, cols] = acc_ref[:, cols] * alpha + jnp.dot(vt_ones, p.astype(BF16), preferred_element_type=F32)
        m_ref[:, cols] = m_new


def _normalised(acc, dv):
    return acc[:dv] / acc[dv:dv + 1]


def _pipelined_tiles(n_far, produce, consume, buf0, buf1):
    @pl.when(n_far == -1)
    def _():
        produce(0, buf0, 1)
        consume(0, buf0)

    @pl.when(n_far == 0)
    def _():
        produce(0, buf0, 0)
        produce(1, buf1, 1)
        consume(0, buf0)
        consume(1, buf1)

    @pl.when(n_far >= 1)
    def _():
        produce(0, buf0, None)

    pairs = jnp.maximum(n_far - 1, 0) // 2

    def pair(j):
        produce(j + 1, buf1, None)
        consume(j, buf0)
        produce(j + 2, buf0, None)
        consume(j + 1, buf1)

    def body(i, carry):
        for u in range(4):
            pair(8 * i + 2 * u)
        return carry

    def leftover(q, carry):
        pair(2 * q)
        return carry

    lax.fori_loop(0, pairs // 4, body, 0)
    lax.fori_loop(pairs // 4 * 4, pairs, leftover, 0)
    d = 2 * pairs
    rem = n_far - d

    @pl.when(rem == 1)
    def _():
        produce(d + 1, buf1, 0)
        consume(d, buf0)
        produce(d + 2, buf0, 1)
        consume(d + 1, buf1)
        consume(d + 2, buf0)

    @pl.when(rem == 2)
    def _():
        produce(d + 1, buf1, None)
        consume(d, buf0)
        produce(d + 2, buf0, 0)
        consume(d + 1, buf1)
        produce(d + 3, buf1, 1)
        consume(d + 2, buf0)
        consume(d + 3, buf1)


def _attn_a_body(t, q0, out_scale, lam_ref, q_ref, k_ref, vt_ref, bias_ref, g_ref, o_ref,
                 qz_ref, m_ref, acc_ref, s0_ref, s1_ref):
    i = pl.program_id(2)
    n_far = q0 // t + i - 1
    zero = jnp.zeros((HD_A, t), BF16)
    qz_ref[:HD_A, :t] = q_ref[:HD_A, :]
    qz_ref[HD_A:, :t] = zero
    qz_ref[:HD_A, t:] = zero
    qz_ref[HD_A:, t:] = q_ref[HD_A:, :]
    m_ref[...] = jnp.full(m_ref.shape, NEG, F32)
    acc_ref[...] = jnp.zeros(acc_ref.shape, F32)

    def produce(j, s_ref, kind):
        ks = pl.multiple_of(j * t, t)
        s = jnp.dot(k_ref[pl.ds(ks, t), :], qz_ref[...], preferred_element_type=F32)
        if kind is not None:
            b = bias_ref[kind]
            s = jnp.concatenate([s[:, :t] + b, s[:, t:] + b], axis=1)
        s_ref[...] = s

    def consume(j, s_ref):
        _softmax_step(s_ref, vt_ref[:, pl.ds(pl.multiple_of(j * t, t), t)], m_ref, acc_ref, t)

    _pipelined_tiles(n_far, produce, consume, s0_ref, s1_ref)

    lam = lam_ref[0]
    o = _normalised(acc_ref[...], DV_A)
    o = o[:, :t] - lam * o[:, t:]
    y = o * lax.rsqrt(jnp.mean(o * o, axis=0, keepdims=True) + EPS)
    y = (y * g_ref[...]) * out_scale
    o_ref[...] = y.astype(o_ref.dtype)


def _attn_a(q, k, vt, bias, subln_g, lam, *, t, q0, out_scale):
    b, _, _, nq = q.shape
    nk = k.shape[1]
    return pl.pallas_call(
        functools.partial(_attn_a_body, t, q0, out_scale),
        grid=(b, H_A, nq // t),
        in_specs=[
            pl.BlockSpec(memory_space=pltpu.SMEM),
            pl.BlockSpec((None, None, 2 * HD_A, t), lambda bb, h, i: (bb, h, 0, i)),
            pl.BlockSpec((None, nk, LANES), lambda bb, h, i: (bb, 0, h)),
            pl.BlockSpec((None, None, DV_A, nk), lambda bb, h, i: (bb, h, 0, 0)),
            pl.BlockSpec((None, 2, t, t), lambda bb, h, i: (h, 0, 0, 0)),
            pl.BlockSpec((DV_A, 1), lambda bb, h, i: (0, 0)),
        ],
        out_specs=pl.BlockSpec((None, None, DV_A, t), lambda bb, h, i: (bb, h, 0, i)),
        out_shape=jax.ShapeDtypeStruct((b, H_A, DV_A, nq), BF16),
        scratch_shapes=[
            pltpu.VMEM((LANES, 2 * t), BF16),
            pltpu.VMEM((1, 2 * t), F32),
            pltpu.VMEM((DV_A + ONES_ROWS, 2 * t), F32),
            pltpu.VMEM((t, 2 * t), F32),
            pltpu.VMEM((t, 2 * t), F32),
        ],
        compiler_params=_cparams(("parallel", "parallel", "arbitrary")),
        name="diff_attn",
    )(lam, q, k, vt, bias, subln_g)


GROUP = 4


def _order_key(x):
    b = pltpu.bitcast(x, I32)
    return jnp.where(b < 0, b ^ INT_MAX, b)


def _order_unkey(k):
    return pltpu.bitcast(jnp.where(k < 0, k ^ INT_MAX, k), F32)


def _dsa_body(t, q0, topk, n_keys, n_valid_diag, qb_ref, qi_ref, wi_ref, kb_ref, vbt_ref, ki_ref, bias_ref, o_ref,
              keys_ref, qz_ref, qs_ref, mm_ref, m_ref, acc_ref, s0_ref, s1_ref):
    i = pl.program_id(1)
    n_tiles = q0 // t + i + 1
    n_far = n_tiles - 2

    zero = jnp.zeros((D_IDX, t), BF16)
    for hp in range(H_IDX // 2):
        qz_ref[2 * hp, :D_IDX, :] = qi_ref[hp, :D_IDX, :]
        qz_ref[2 * hp, D_IDX:, :] = zero
        qz_ref[2 * hp + 1, :D_IDX, :] = zero
        qz_ref[2 * hp + 1, D_IDX:, :] = qi_ref[hp, D_IDX:, :]

    def score_tile(j):
        kt = ki_ref[pl.ds(pl.multiple_of(j * t, t), t), :]
        acc = jnp.zeros((t, t), F32)
        for h in range(H_IDX):
            s = jnp.dot(kt, qz_ref[h], preferred_element_type=F32)
            acc = acc + jnp.maximum(s, 0.0) * wi_ref[h:h + 1, :]
        return acc

    def fold8(x, op):
        return op(x.reshape(x.shape[0] // 8, 8, t), axis=0)

    mm_ref[0] = jnp.full((8, t), jnp.inf, F32)
    mm_ref[1] = jnp.full((8, t), -jnp.inf, F32)

    def score_store(j):
        sc = score_tile(j)
        keys_ref[pl.ds(pl.multiple_of(j * t, t), t), :] = _order_key(sc)
        mm_ref[0] = jnp.minimum(mm_ref[0], fold8(sc, jnp.min))
        mm_ref[1] = jnp.maximum(mm_ref[1], fold8(sc, jnp.max))

    def score_group(jj, carry):
        for u in range(GROUP):
            score_store(GROUP * jj + u)
        return carry

    def score_single(j, carry):
        score_store(j)
        return carry

    jd = n_tiles - 1
    lax.fori_loop(0, jd // GROUP, score_group, 0)
    lax.fori_loop(jd // GROUP * GROUP, jd, score_single, 0)

    kl = lax.broadcasted_iota(I32, (t, t), 0)
    ql = lax.broadcasted_iota(I32, (t, t), 1)
    admissible = ((kl // CHUNK) <= (ql // CHUNK)) & (kl < n_valid_diag)
    sc = score_tile(jd)
    keys_ref[pl.ds(pl.multiple_of(jd * t, t), t), :] = jnp.where(admissible, _order_key(sc), INT_MIN)
    smin = jnp.minimum(mm_ref[0], fold8(jnp.where(admissible, sc, jnp.inf), jnp.min))
    smax = jnp.maximum(mm_ref[1], fold8(jnp.where(admissible, sc, -jnp.inf), jnp.max))
    for u in range(GROUP - 1):
        keys_ref[pl.ds(pl.multiple_of((n_tiles + u) * t, t), t), :] = jnp.full((t, t), INT_MIN, I32)
    n_groups = (n_tiles + GROUP - 1) // GROUP
    gt = GROUP * t

    def count_groups(hit_fn):
        def body(j, c):
            r0 = pl.multiple_of(j * gt, gt)
            hit = hit_fn(keys_ref[pl.ds(r0, gt), :], r0)
            return c + jnp.sum(hit.reshape(gt // 32, 32, t), axis=0)
        part = lax.fori_loop(0, n_groups, body, jnp.zeros((32, t), F32))
        return jnp.sum(part, axis=0, keepdims=True)

    def search_cond(state):
        return jnp.logical_and(state[0] < 4 * 34, state[-1] > 0)

    def search_body(state):
        it, lo, hi, c_lo, c_hi, _ = state
        f_lo, f_hi = _order_unkey(lo), _order_unkey(hi)
        a, b = jnp.log(c_lo + 0.5), jnp.log(c_hi + 0.5)
        frac = jnp.clip((a - math.log(topk)) / jnp.maximum(a - b, 1e-9), 0.0, 1.0)
        frac = jnp.where(it == 0, frac0, frac)
        guess = jnp.minimum(jnp.maximum(_order_key(f_lo + (f_hi - f_lo) * frac), lo + 1), hi - 1)
        mid = jnp.where(it % 4 == 3, (lo >> 1) + (hi >> 1) + (lo & hi & 1), guess)
        c = count_groups(lambda kk, r0: jnp.where(kk >= mid, 1.0, 0.0))
        live = lo + 1 < hi
        ge = c >= topk
        up = jnp.logical_and(live, ge)
        down = jnp.logical_and(live, jnp.logical_not(ge))
        lo = jnp.where(up, mid, lo)
        c_lo = jnp.where(up, c, c_lo)
        hi = jnp.where(jnp.logical_and(live, c == topk), mid + 1, jnp.where(down, mid, hi))
        c_hi = jnp.where(down, c, c_hi)
        open_ = jnp.logical_and(lo + 1 < hi, c_hi != topk - 1)
        return it + 1, lo, hi, c_lo, c_hi, jnp.max(jnp.where(open_, 1, 0))

    qpos = q0 + i * t + lax.broadcasted_iota(I32, (1, t), 1)
    n_adm = jnp.minimum((qpos // CHUNK + 1) * CHUNK, n_keys).astype(F32)
    lo0 = _order_key(jnp.min(smin, axis=0, keepdims=True))
    hi0 = jnp.where(n_adm <= topk, lo0 + 1, _order_key(jnp.max(smax, axis=0, keepdims=True)) + 1)

    def upper_quantile(p):
        u = jnp.sqrt(-2.0 * jnp.log(p))
        return u - ((0.010328 * u + 0.802853) * u + 2.515517) / (((0.001308 * u + 0.189269) * u + 1.432788) * u + 1.0)

    frac0 = 0.5 + upper_quantile(jnp.clip(topk / n_adm, 1e-6, 0.5)) / (2.0 * upper_quantile(1.0 / n_adm))
    frac0 = jnp.clip(frac0, 0.0, 1.0)
    _, lo, hi, c_lo, c_above, _ = lax.while_loop(
        search_cond, search_body, (jnp.int32(0), lo0, hi0, n_adm, jnp.zeros((1, t), F32), jnp.int32(1)))

    def below_hi_max(j, m):
        kk = keys_ref[pl.ds(pl.multiple_of(j * gt, gt), gt), :]
        return jnp.maximum(m, jnp.max(jnp.where(kk < hi, kk, INT_MIN).reshape(gt // 8, 8, t), axis=0))

    m8 = lax.fori_loop(0, n_groups, below_hi_max, jnp.full((8, t), INT_MIN, I32))
    for shift in (4, 2, 1):
        m8 = jnp.maximum(m8, pltpu.roll(m8, shift, axis=0))
    wide = lo + 1 < hi
    thr = jnp.where(wide, m8[:1], lo)
    c_eq = count_groups(lambda kk, r0: jnp.where(kk == thr, 1.0, 0.0))
    c_thr = jnp.where(wide, c_above + c_eq, c_lo)

    @pl.when(jnp.max(jnp.where(jnp.logical_and(c_thr > topk, qpos < n_keys), 1, 0)) > 0)
    def _():
        need = topk - c_above
        row = lax.broadcasted_iota(I32, (gt, t), 0)

        def cut_body(_, st):
            p_lo, p_hi = st
            p_mid = (p_lo + p_hi) >> 1
            c = count_groups(lambda kk, r0: jnp.where(
                jnp.logical_and(kk == thr, row + r0 <= p_mid), 1.0, 0.0))
            ok = c >= need
            return jnp.where(ok, p_lo, p_mid), jnp.where(ok, p_mid, p_hi)

        n_pos = keys_ref.shape[0]
        _, cut = lax.fori_loop(0, n_pos.bit_length(), cut_body,
                               (jnp.full((1, t), -1, I32), jnp.full((1, t), n_pos - 1, I32)))

        def demote_body(j, carry):
            r0 = pl.multiple_of(j * gt, gt)
            kk = keys_ref[pl.ds(r0, gt), :]
            keys_ref[pl.ds(r0, gt), :] = jnp.where(jnp.logical_and(kk == thr, row + r0 > cut), thr - 1, kk)
            return carry

        lax.fori_loop(0, n_groups, demote_body, 0)

    m_ref[...] = jnp.full(m_ref.shape, NEG, F32)
    acc_ref[...] = jnp.zeros(acc_ref.shape, F32)

    for h in range(H_B):
        qs_ref[h // G_B, :, (h % G_B) * t:(h % G_B + 1) * t] = qb_ref[h]

    def produce(j, s_ref, kind):
        ks = pl.multiple_of(j * t, t)
        off = jnp.where(keys_ref[pl.ds(ks, t), :] >= thr, 0.0, NEG)
        for n in range(HKV_B):
            s = jnp.dot(kb_ref[pl.ds(ks, t), n * HD_B:(n + 1) * HD_B], qs_ref[n], preferred_element_type=F32)
            parts = []
            for g in range(G_B):
                off_g = off if kind is None else off + bias_ref[n * G_B + g, kind]
                parts.append(s[:, g * t:(g + 1) * t] + off_g)
            s_ref[n] = jnp.concatenate(parts, axis=1)

    def consume(j, s_ref):
        ks = pl.multiple_of(j * t, t)
        for n in range(HKV_B):
            _softmax_step(s_ref.at[n], vbt_ref[n, :, pl.ds(ks, t)], m_ref.at[n], acc_ref.at[n], 2 * t)

    _pipelined_tiles(n_far, produce, consume, s0_ref, s1_ref)

    for n in range(HKV_B):
        o = _normalised(acc_ref[n], HD_B)
        for g in range(G_B):
            h = n * G_B + g
            o_ref[h] = o[:, g * t:(g + 1) * t].astype(o_ref.dtype)


def _dsa(qb, qi, wi_t, kb, vbt, ki2, bias, *, t, q0, topk, n_keys, n_valid_diag):
    b, _, _, nq = qb.shape
    nk = kb.shape[1]
    return pl.pallas_call(
        functools.partial(_dsa_body, t, q0, topk, n_keys, n_valid_diag),
        grid=(b, nq // t),
        in_specs=[
            pl.BlockSpec((None, H_B, HD_B, t), lambda bb, i: (bb, 0, 0, i)),
            pl.BlockSpec((None, H_IDX // 2, LANES, t), lambda bb, i: (bb, 0, 0, i)),
            pl.BlockSpec((None, H_IDX, t), lambda bb, i: (bb, 0, i)),
            _resident((None, nk, HKV_B * HD_B), lambda bb, i: (bb, 0, 0)),
            _resident((None, HKV_B, HD_B, nk), lambda bb, i: (bb, 0, 0, 0)),
            _resident((None, nk, LANES), lambda bb, i: (bb, 0, 0)),
            _resident((H_B, 2, t, t), lambda bb, i: (0, 0, 0, 0)),
        ],
        out_specs=pl.BlockSpec((None, H_B, HD_B, t), lambda bb, i: (bb, 0, 0, i)),
        out_shape=jax.ShapeDtypeStruct((b, H_B, HD_B, nq), BF16),
        scratch_shapes=[
            pltpu.VMEM((nk + (GROUP - 1) * t, t), I32),
            pltpu.VMEM((H_IDX, LANES, t), BF16),
            pltpu.VMEM((HKV_B, HD_B, G_B * t), BF16),
            pltpu.VMEM((2, 8, t), F32),
            pltpu.VMEM((HKV_B, 1, G_B * t), F32),
            pltpu.VMEM((HKV_B, HD_B + ONES_ROWS, G_B * t), F32),
            pltpu.VMEM((HKV_B, t, G_B * t), F32),
            pltpu.VMEM((HKV_B, t, G_B * t), F32),
        ],
        compiler_params=_cparams(("parallel", "arbitrary")),
        name="dsa",
    )(qb, qi, wi_t, kb, vbt, ki2, bias)


def _softmax_pieces(s_list, v_list):
    m = functools.reduce(jnp.maximum, [jnp.max(s, axis=-1, keepdims=True) for s in s_list])
    p_list = [jnp.exp2(s - m) for s in s_list]
    o = sum(jnp.dot(p.astype(BF16), v, preferred_element_type=F32) for p, v in zip(p_list, v_list))
    return o / sum(jnp.sum(p, axis=-1, keepdims=True) for p in p_list)


def _head_rows(ref, h, n_heads, n_pos):
    return ref[pl.ds(h, n_pos, stride=n_heads), :].astype(BF16)


def _short_a_body(nq, n_past, out_scale, lam_ref, q_ref, kp_ref, vp_ref, kn_ref, vn_ref, bias_ref, g_ref, o_ref):
    lam = lam_ref[0]
    lane = lax.broadcasted_iota(I32, (nq, LANES), 1)
    pieces = ((kp_ref, vp_ref, 0, n_past), (kn_ref, vn_ref, n_past, nq))
    for h in range(H_A):
        cols = slice(h * LANES, (h + 1) * LANES)
        q = q_ref[:, cols]
        zero = jnp.zeros_like(q)
        qz = jnp.concatenate([jnp.where(lane < HD_A, q, zero), jnp.where(lane < HD_A, zero, q)], axis=0)
        b = bias_ref[h]
        b2 = jnp.concatenate([b, b], axis=0)
        s_list = [lax.dot_general(qz, _head_rows(k_ref, h, H_A, size), _NT, preferred_element_type=F32)
                  + b2[:, lo:lo + size] for k_ref, _, lo, size in pieces]
        o = _softmax_pieces(s_list, [_head_rows(v_ref, h, H_A, size) for _, v_ref, _, size in pieces])
        o = o[:nq] - lam * o[nq:]
        y = o * lax.rsqrt(jnp.mean(o * o, axis=-1, keepdims=True) + EPS)
        o_ref[:, cols] = ((y * g_ref[...]) * out_scale).astype(o_ref.dtype)


def _short_a(q, k_past, v_past, k_new, v_new, bias, subln_g, lam, *, out_scale):
    b, nq, d = q.shape
    n_past = k_past.shape[1]
    assert n_past % LANES == 0
    rows = lambda n: pl.BlockSpec((None, n * H_A, DV_A), lambda bb: (bb, 0, 0))
    merge = lambda a: a.reshape(b, a.shape[1] * H_A, DV_A)
    return pl.pallas_call(
        functools.partial(_short_a_body, nq, n_past, out_scale),
        grid=(b,),
        in_specs=[
            pl.BlockSpec(memory_space=pltpu.SMEM),
            pl.BlockSpec((None, nq, d), lambda bb: (bb, 0, 0)),
            rows(n_past), rows(n_past), rows(nq), rows(nq),
            pl.BlockSpec(bias.shape, lambda bb: (0, 0, 0)),
            pl.BlockSpec((1, DV_A), lambda bb: (0, 0)),
        ],
        out_specs=pl.BlockSpec((None, nq, d), lambda bb: (bb, 0, 0)),
        out_shape=jax.ShapeDtypeStruct((b, nq, d), BF16),
        compiler_params=_cparams(("parallel",)),
        name="diff_attn_short",
    )(lam, q, merge(k_past), merge(v_past), merge(k_new), merge(v_new), bias, subln_g)


def _short_dsa_body(nq, n_past, topk, qb_ref, qi_ref, wi_ref, kbp_ref, vbp_ref, kip_ref, kbn_ref, vbn_ref,
                    kin_ref, bias_ref, o_ref):
    lane = lax.broadcasted_iota(I32, (nq, LANES), 1)
    spans = ((0, n_past), (n_past, nq))

    qz = []
    for h in range(H_IDX):
        qs = qi_ref[:, (h // 2) * LANES:(h // 2 + 1) * LANES]
        zero = jnp.zeros_like(qs)
        qz.append(jnp.where(lane < D_IDX, qs, zero) if h % 2 == 0 else jnp.where(lane < D_IDX, zero, qs))
    qz = jnp.concatenate(qz, axis=0)
    keys, kpos = [], []
    for ki_ref, (lo, size) in zip((kip_ref, kin_ref), spans):
        ki = ki_ref[...].astype(BF16)
        s_idx = lax.dot_general(qz, jnp.concatenate([ki, ki], axis=-1), _NT, preferred_element_type=F32)
        score = jnp.zeros((nq, size), F32)
        for h in range(H_IDX):
            score = score + jnp.maximum(s_idx[h * nq:(h + 1) * nq], 0.0) * wi_ref[:, h:h + 1]
        qp = n_past + lax.broadcasted_iota(I32, (nq, size), 0)
        kp = lo + lax.broadcasted_iota(I32, (nq, size), 1)
        keys.append(jnp.where((kp // CHUNK) <= (qp // CHUNK), _order_key(score), INT_MIN))
        kpos.append(kp)

    def count(hit_fn):
        return sum(jnp.sum(jnp.where(hit_fn(k, kp), 1.0, 0.0), axis=-1, keepdims=True)
                   for k, kp in zip(keys, kpos))

    def search_body(state):
        it, lo, hi, c_lo, c_hi, _ = state
        mid = (lo >> 1) + (hi >> 1) + (lo & hi & 1)
        c = count(lambda k, kp: k >= mid)
        live = lo + 1 < hi
        ge = c >= topk
        up = jnp.logical_and(live, ge)
        down = jnp.logical_and(live, jnp.logical_not(ge))
        lo = jnp.where(up, mid, lo)
        c_lo = jnp.where(up, c, c_lo)
        hi = jnp.where(jnp.logical_and(live, c == topk), mid + 1, jnp.where(down, mid, hi))
        c_hi = jnp.where(down, c, c_hi)
        return it + 1, lo, hi, c_lo, c_hi, jnp.max(jnp.where(lo + 1 < hi, 1, 0))

    n_keys = n_past + nq
    n_adm = count(lambda k, kp: k > INT_MIN)
    lo0 = jnp.full((nq, 1), INT_MIN + 1, I32)
    hi0 = jnp.where(n_adm <= topk, lo0 + 1, INT_MAX)
    _, thr, _, c_thr, c_above, _ = lax.while_loop(
        lambda st: jnp.logical_and(st[0] < 34, st[-1] > 0), search_body,
        (jnp.int32(0), lo0, hi0, n_adm, jnp.zeros((nq, 1), F32), jnp.int32(1)))

    need = topk - c_above

    def cut_body(_, st):
        p_lo, p_hi = st
        p_mid = (p_lo + p_hi) >> 1
        ok = count(lambda k, kp: jnp.logical_and(k == thr, kp <= p_mid)) >= need
        return jnp.where(ok, p_lo, p_mid), jnp.where(ok, p_mid, p_hi)

    _, cut = lax.fori_loop(0, n_keys.bit_length(), cut_body,
                           (jnp.full((nq, 1), -1, I32), jnp.full((nq, 1), n_keys - 1, I32)))
    off = [jnp.where(jnp.logical_and(k >= thr, jnp.logical_not(jnp.logical_and(k == thr, kp > cut))), 0.0, NEG)
           for k, kp in zip(keys, kpos)]

    for n in range(HKV_B):
        heads = range(n * G_B, (n + 1) * G_B)
        qs = jnp.concatenate([qb_ref[:, h * HD_B:(h + 1) * HD_B] for h in heads], axis=0)
        s_list = []
        for kb_ref, off_i, (lo, size) in zip((kbp_ref, kbn_ref), off, spans):
            s = lax.dot_general(qs, _head_rows(kb_ref, n, HKV_B, size), _NT, preferred_element_type=F32)
            s_list.append(s + jnp.concatenate([bias_ref[h][:, lo:lo + size] + off_i for h in heads], axis=0))
        o = _softmax_pieces(s_list, [_head_rows(vbp_ref, n, HKV_B, n_past), _head_rows(vbn_ref, n, HKV_B, nq)])
        for g, h in enumerate(heads):
            o_ref[:, h * HD_B:(h + 1) * HD_B] = o[g * nq:(g + 1) * nq].astype(o_ref.dtype)


def _short_dsa(qb, qi, wi, kb_past, vb_past, ki_past, kb_new, vb_new, ki_new, bias, *, topk):
    b, nq, d = qb.shape
    n_past = kb_past.shape[1]
    assert n_past % LANES == 0
    per_batch = lambda *shape: pl.BlockSpec((None,) + shape, lambda bb: (bb,) + (0,) * len(shape))
    merge = lambda a: a.reshape(b, a.shape[1] * HKV_B, HD_B)
    return pl.pallas_call(
        functools.partial(_short_dsa_body, nq, n_past, topk),
        grid=(b,),
        in_specs=[per_batch(nq, d), per_batch(nq, d), per_batch(nq, H_IDX),
                  per_batch(n_past * HKV_B, HD_B), per_batch(n_past * HKV_B, HD_B), per_batch(n_past, D_IDX),
                  per_batch(nq * HKV_B, HD_B), per_batch(nq * HKV_B, HD_B), per_batch(nq, D_IDX),
                  pl.BlockSpec(bias.shape, lambda bb: (0, 0, 0))],
        out_specs=per_batch(nq, d),
        out_shape=jax.ShapeDtypeStruct((b, nq, d), BF16),
        compiler_params=_cparams(("parallel",)),
        name="dsa_short",
    )(qb, qi, wi, merge(kb_past), merge(vb_past), ki_past, merge(kb_new), merge(vb_new), ki_new, bias)


def _sigmoid(x):
    return 1.0 / (1.0 + jnp.exp(-x))


def _merge_body(mixers_t, h_ref, oa_ref, ob_ref, wg_ref, woa_ref, wob_ref, o_ref):
    h = h_ref[...]
    oa, ob = (oa_ref[...].T, ob_ref[...].T) if mixers_t else (oa_ref[...], ob_ref[...])
    ga = jnp.dot(h, wg_ref[:, :D_MODEL], preferred_element_type=F32)
    ya = jnp.dot(oa, woa_ref[...], preferred_element_type=F32)
    merged = _sigmoid(ga) * ya
    gb = jnp.dot(h, wg_ref[:, D_MODEL:], preferred_element_type=F32)
    yb = jnp.dot(ob, wob_ref[...], preferred_element_type=F32)
    o_ref[...] = (merged + _sigmoid(gb) * yb).astype(o_ref.dtype)


def _merge(h, oa, ob, wg, woa, wob, tm, mixers_t):
    n = h.shape[0]
    mixer_spec = (lambda c: pl.BlockSpec((c, tm), lambda i: (0, i))) if mixers_t else \
        (lambda c: pl.BlockSpec((tm, c), lambda i: (i, 0)))
    return pl.pallas_call(
        functools.partial(_merge_body, mixers_t),
        grid=(n // tm,),
        in_specs=[
            pl.BlockSpec((tm, D_MODEL), lambda i: (i, 0)),
            mixer_spec(H_A * DV_A),
            mixer_spec(H_B * HD_B),
            _resident((D_MODEL, 2 * D_MODEL), lambda i: (0, 0)),
            _resident((H_A * DV_A, D_MODEL), lambda i: (0, 0)),
            _resident((H_B * HD_B, D_MODEL), lambda i: (0, 0)),
        ],
        out_specs=pl.BlockSpec((tm, D_MODEL), lambda i: (i, 0)),
        out_shape=jax.ShapeDtypeStruct((n, D_MODEL), BF16),
        compiler_params=_cparams(("parallel",)),
        name="gated_merge",
    )(h, oa, ob, wg, woa, wob)


def _outproj_body(x_ref, mg_ref, w_ref, g_ref, x1_ref, h2_ref):
    x1 = x_ref[...] + jnp.dot(mg_ref[...], w_ref[...], preferred_element_type=F32)
    x1_ref[...] = x1
    y = x1 * lax.rsqrt(jnp.mean(x1 * x1, axis=-1, keepdims=True) + EPS)
    h2_ref[...] = (y * g_ref[...]).astype(h2_ref.dtype)


def _outproj(x, merged, w_out, g2, tm):
    n = x.shape[0]
    return pl.pallas_call(
        _outproj_body,
        grid=(n // tm,),
        in_specs=[
            pl.BlockSpec((tm, D_MODEL), lambda i: (i, 0)),
            pl.BlockSpec((tm, D_MODEL), lambda i: (i, 0)),
            _resident((D_MODEL, D_MODEL), lambda i: (0, 0)),
            pl.BlockSpec((1, D_MODEL), lambda i: (0, 0)),
        ],
        out_specs=[pl.BlockSpec((tm, D_MODEL), lambda i: (i, 0)), pl.BlockSpec((tm, D_MODEL), lambda i: (i, 0))],
        out_shape=[jax.ShapeDtypeStruct((n, D_MODEL), F32), jax.ShapeDtypeStruct((n, D_MODEL), BF16)],
        compiler_params=_cparams(("parallel",)),
        name="out_proj",
    )(x, merged, w_out, g2.reshape(1, D_MODEL))


def _ffn_body(x1_ref, h2_ref, w1_ref, w2_ref, o_ref):
    f = pl.program_id(1)

    @pl.when(f == 0)
    def _():
        o_ref[...] = x1_ref[...]

    u = jnp.maximum(jnp.dot(h2_ref[...], w1_ref[...], preferred_element_type=F32), 0.0)
    o_ref[...] += jnp.dot((u * u).astype(BF16), w2_ref[...], preferred_element_type=F32)


def _ffn(x1, h2, w1, w2, tm, tf):
    n = x1.shape[0]
    return pl.pallas_call(
        _ffn_body,
        grid=(n // tm, D_FF // tf),
        in_specs=[
            pl.BlockSpec((tm, D_MODEL), lambda i, f: (i, 0)),
            pl.BlockSpec((tm, D_MODEL), lambda i, f: (i, 0)),
            pl.BlockSpec((D_MODEL, tf), lambda i, f: (0, f)),
            pl.BlockSpec((tf, D_MODEL), lambda i, f: (f, 0)),
        ],
        out_specs=pl.BlockSpec((tm, D_MODEL), lambda i, f: (i, 0)),
        out_shape=jax.ShapeDtypeStruct((n, D_MODEL), F32),
        compiler_params=_cparams(("parallel", "arbitrary")),
        name="ffn",
    )(x1, h2, w1, w2)


def _prep_weights(w_in, w_o_a, w_o_b, w_out, w_ff1, w_ff2):
    sizes = (H_A * 2 * HD_A, H_A * 2 * HD_A, H_A * DV_A, H_B * HD_B, HKV_B * HD_B, HKV_B * HD_B,
             H_IDX * D_IDX, D_IDX, H_IDX, D_MODEL, D_MODEL)
    offs = [0]
    for s in sizes:
        offs.append(offs[-1] + s)
    col = lambda a, b: w_in[:, offs[a]:offs[b]]
    w_misc = jnp.concatenate(
        [col(4, 6), col(7, 8), col(7, 8), col(8, 9),
         jnp.zeros((D_MODEL, _MISC_COLS - _WI0 - H_IDX), w_in.dtype)], axis=1)
    return dict(
        qa=col(0, 1).astype(BF16), ka=col(1, 2).astype(BF16), va=col(2, 3).astype(BF16),
        qb=col(3, 4).astype(BF16), misc=w_misc.astype(BF16), qi=col(6, 7).astype(BF16),
        gate=col(9, 11).astype(BF16), oa=w_o_a.astype(BF16), ob=w_o_b.astype(BF16),
        out=w_out.astype(BF16), ff1=w_ff1.astype(BF16), ff2=w_ff2.astype(BF16))


def _layer(x, past, w, p, lam, lam_init, tab_t, *, tm, t_a=None, t_b=None):
    short = t_a is None
    b, t, _ = x.shape
    n = b * t
    xf = x.reshape(n, D_MODEL)
    h = _rmsnorm(xf, p["norm1_g"], tm)
    tp = min(2 * tm, n)

    gain_a = lambda g: g.reshape(1, 2 * HD_A)
    v_layout = HEADS_T if (not short and past is None and b == 1) else FLAT
    q_layout = FLAT if short else HEADS_T
    (qa,) = _proj(h, w["qa"], [gain_a(p["qn_a_g"])], _epi_qa, [(1024, BF16, q_layout)], tp, "proj_qa")
    ka, ka_h = _proj(h, w["ka"], [gain_a(p["kn_a_g"])], _epi_ka, [(1024, F32, HEADS), (1024, BF16, FLAT)],
                     tp, "proj_ka")
    va, va_h = _proj(h, w["va"], [], _epi_copy2, [(1024, F32, HEADS), (1024, BF16, v_layout)], tp, "proj_va")
    (qb,) = _proj(h, w["qb"], [p["qn_b_g"].reshape(1, HD_B)], _epi_qb, [(1024, BF16, q_layout)], tp, "proj_qb")
    kb, kb_h, vb, vb_h, ki, ki2_h, wi = _proj(
        h, w["misc"], [p["kn_b_g"].reshape(1, HD_B)], _epi_misc,
        [(256, F32, HEADS), (256, BF16, FLAT), (256, F32, HEADS), (256, BF16, v_layout), (D_IDX, F32, FLAT),
         (2 * D_IDX, BF16, FLAT), (H_IDX, F32, FLAT)],
        tp, "proj_misc")
    (qi,) = _proj(h, w["qi"], [], _epi_copy1, [(1024, BF16, q_layout)], tp, "proj_qi")

    new_rows = (ka.reshape(b, t, H_A, 2 * HD_A), va.reshape(b, t, H_A, DV_A), kb.reshape(b, t, HKV_B, HD_B),
                vb.reshape(b, t, HKV_B, HD_B), ki.reshape(b, t, D_IDX))

    per_b = lambda a: a.reshape(b, t, -1)
    if short:
        n_keys = past[0].shape[1] + t
        topk = min(TOPK_MAX, n_keys // 4)
        oa = _short_a(per_b(qa), past[0], past[1], new_rows[0], new_rows[1],
                      _bias_dense(tab_t[:H_A], t, n_keys, n_keys - t, n_keys),
                      p["subln_a_g"].reshape(1, DV_A), lam, out_scale=1.0 - lam_init)
        ob = _short_dsa(per_b(qb), per_b(qi), per_b(wi), past[2], past[3], past[4], *new_rows[2:],
                        _bias_dense(tab_t[H_A:], t, n_keys, n_keys - t, n_keys), topk=topk)
    else:
        oa, ob = _tiled_mixers(qa, qb, qi, wi, ka_h, va_h, kb_h, vb_h, ki2_h, past, tab_t, p, lam, lam_init,
                               b=b, t=t, t_a=t_a, t_b=t_b)
    if short:
        oa, ob = oa.reshape(n, H_A * DV_A), ob.reshape(n, H_B * HD_B)
    else:
        oa, ob = (jnp.transpose(o, (1, 2, 0, 3)).reshape(-1, n) for o in (oa, ob))

    merged = _merge(h, oa, ob, w["gate"], w["oa"], w["ob"], min(tm, 256), mixers_t=not short)
    x1, h2 = _outproj(xf, merged, w["out"], p["norm2_g"], min(tm, 256))
    y = _ffn(x1, h2, w["ff1"], w["ff2"], tm, 1024)
    return y.reshape(b, t, D_MODEL), new_rows


def _tiled_mixers(qa, qb, qi, wi, ka_h, va_h, kb_h, vb_h, ki2_h, past, tab_t, p, lam, lam_init,
                  *, b, t, t_a, t_b):
    def full_keys(new_h, past_arr):
        if new_h.ndim == 3:
            return new_h[None]
        new_h = new_h.reshape(b, t, -1)
        if past_arr is None:
            return new_h
        past_h = past_arr.reshape(b, past_arr.shape[1], -1).astype(BF16)
        return jnp.concatenate([past_h, new_h], axis=1)

    if past is None:
        pa_k = pa_v = pb_k = pb_v = pb_i = None
    else:
        pa_k, pa_v, pb_k, pb_v, pb_i = past
        pb_i = jnp.concatenate([pb_i, pb_i], axis=-1)
    ka_f, va_f = full_keys(ka_h, pa_k), full_keys(va_h, pa_v)
    kb_f, vb_f, ki_f = full_keys(kb_h, pb_k), full_keys(vb_h, pb_v), full_keys(ki2_h, pb_i)
    n_keys = ka_f.shape[1]
    q0 = n_keys - t
    topk = min(TOPK_MAX, n_keys // 4)
    t_pad = max(t_a, t_b)
    assert t_pad % t_a == 0 and t_pad % t_b == 0 and q0 % t_pad == 0
    nk_pad = -(-n_keys // t_pad) * t_pad
    pad_k = lambda a: jnp.pad(a, ((0, 0), (0, nk_pad - n_keys), (0, 0)))
    pad_q = lambda a, tt: jnp.pad(a.reshape(b, t, -1), ((0, 0), (0, -(-t // tt) * tt - t), (0, 0)))
    ka_f, kb_f, ki_f = (pad_k(a) for a in (ka_f, kb_f, ki_f))
    n_valid = lambda tt: n_keys - (n_keys - 1) // tt * tt
    assert all(t <= tt or n_valid(tt) == tt for tt in (t_a, t_b))

    def heads_t(v, nh):
        if v.ndim == 3:
            v = jnp.transpose(v.reshape(b, n_keys, nh, -1), (0, 2, 3, 1))
        return jnp.pad(v, ((0, 0), (0, 0), (0, 0), (0, nk_pad - n_keys)))

    def q_slabs(q, tt):
        q = jnp.transpose(q.reshape(q.shape[0], LANES, b, t), (2, 0, 1, 3))
        return jnp.pad(q, ((0, 0), (0, 0), (0, 0), (0, -(-t // tt) * tt - t)))

    bias_a = _bias_tiles(tab_t[:H_A], t_a, n_valid(t_a))
    oa = _attn_a(q_slabs(qa, t_a), ka_f, heads_t(va_f, H_A), bias_a, p["subln_a_g"].reshape(DV_A, 1), lam,
                 t=t_a, q0=q0, out_scale=1.0 - lam_init)
    wi_t = jnp.swapaxes(pad_q(wi, t_b), 1, 2)
    bias_b = _bias_tiles(tab_t[H_A:], t_b, n_valid(t_b))
    ob = _dsa(q_slabs(qb, t_b), q_slabs(qi, t_b), wi_t, kb_f, heads_t(vb_f, HKV_B), ki_f, bias_b,
              t=t_b, q0=q0, topk=topk, n_keys=n_keys, n_valid_diag=n_valid(t_b))
    return oa[..., :t], ob[..., :t]


def kernel(x_prompt, x_sample, cache_a_k, cache_a_v, cache_b_k, cache_b_v, cache_b_kidx, rel_bias, norm1_g, w_in, qn_a_g, kn_a_g, lam_q1, lam_k1, lam_q2, lam_k2, subln_a_g, qn_b_g, kn_b_g, w_o_a, w_o_b, w_out, norm2_g, w_ff1, w_ff2):
    depth = w_in.shape[0]
    tab_t = rel_bias.T.astype(F32)
    y_prompt, y_sample = x_prompt, x_sample
    prompt_rows, sample_rows = [], []
    for l in range(depth):
        lam_init = 0.8 - 0.6 * math.exp(-0.3 * l)
        lam = (jnp.exp(jnp.sum(lam_q1[l].astype(F32) * lam_k1[l].astype(F32)))
               - jnp.exp(jnp.sum(lam_q2[l].astype(F32) * lam_k2[l].astype(F32))) + lam_init).reshape(1)
        w = _prep_weights(w_in[l], w_o_a[l], w_o_b[l], w_out[l], w_ff1[l], w_ff2[l])
        p = dict(norm1_g=norm1_g[l], qn_a_g=qn_a_g[l], kn_a_g=kn_a_g[l], subln_a_g=subln_a_g[l],
                 qn_b_g=qn_b_g[l], kn_b_g=kn_b_g[l], norm2_g=norm2_g[l])
        y_prompt, rp = _layer(y_prompt, None, w, p, lam, lam_init, tab_t, t_a=512, t_b=256, tm=512)
        past = (cache_a_k[l], cache_a_v[l], cache_b_k[l], cache_b_v[l], cache_b_kidx[l])
        y_sample, rs = _layer(y_sample, past, w, p, lam, lam_init, tab_t, tm=512)
        prompt_rows.append(rp)
        sample_rows.append(rs)
    p_rows = tuple(jnp.stack(r, axis=0) for r in zip(*prompt_rows))
    s_rows = tuple(jnp.stack(r, axis=0) for r in zip(*sample_rows))
    return (y_prompt, y_sample) + p_rows + s_rows
```

```python
import functools
import math

import jax
import jax.numpy as jnp
from jax import lax
from jax.experimental import pallas as pl
from jax.experimental.pallas import tpu as pltpu

F32 = jnp.float32
BF16 = jnp.bfloat16
I32 = jnp.int32

D_MODEL = 2048
CHUNK = 64
H_A = 8
DV_A = 128
HD_A = 64
H_B = 8
HD_B = 128
HKV_B = 2
G_B = H_B // HKV_B
H_IDX = 16
D_IDX = 64
TOPK_MAX = 256
N_BUCKETS = 32
D_FF = 4 * D_MODEL
EPS = 1e-6
LANES = 128
NEG = -1e30
LOG2E = math.log2(math.e)
INT_MIN = -(2 ** 31)
INT_MAX = 2 ** 31 - 1
VMEM_LIMIT = 56 * 1024 * 1024

_NT = (((1,), (1,)), ((), ()))


def _cparams(sem):
    return pltpu.CompilerParams(dimension_semantics=sem, vmem_limit_bytes=VMEM_LIMIT)


def _resident(block_shape, index_map):
    return pl.BlockSpec(block_shape, index_map, pipeline_mode=pl.Buffered(1))


def _rmsnorm_body(x_ref, g_ref, o_ref):
    x = x_ref[...]
    y = x * lax.rsqrt(jnp.mean(x * x, axis=-1, keepdims=True) + EPS)
    o_ref[...] = (y * g_ref[...]).astype(o_ref.dtype)


def _rmsnorm(x, g, tm):
    n, d = x.shape
    return pl.pallas_call(
        _rmsnorm_body,
        grid=(n // tm,),
        in_specs=[pl.BlockSpec((tm, d), lambda i: (i, 0)), pl.BlockSpec((1, d), lambda i: (0, 0))],
        out_specs=pl.BlockSpec((tm, d), lambda i: (i, 0)),
        out_shape=jax.ShapeDtypeStruct((n, d), BF16),
        compiler_params=_cparams(("parallel",)),
        name="rmsnorm",
    )(x, g.reshape(1, d))


def _head_norm(z, gain, split):
    lane = lax.broadcasted_iota(I32, (1, LANES), 1)
    lo = lane < (LANES // 2)
    outs = []
    for h in range(z.shape[1] // LANES):
        zh = z[:, h * LANES:(h + 1) * LANES]
        sq = zh * zh
        if split:
            s_lo = jnp.sum(jnp.where(lo, sq, 0.0), axis=-1, keepdims=True)
            s_hi = jnp.sum(jnp.where(lo, 0.0, sq), axis=-1, keepdims=True)
            r = jnp.where(lo, lax.rsqrt(s_lo * (2.0 / LANES) + EPS), lax.rsqrt(s_hi * (2.0 / LANES) + EPS))
        else:
            r = lax.rsqrt(jnp.mean(sq, axis=-1, keepdims=True) + EPS)
        outs.append(zh * r * gain)
    return jnp.concatenate(outs, axis=-1)


FLAT, HEADS, HEADS_T = "tokens x cols", "tokens x heads x 128", "heads x 128 x tokens"


def _proj_body(epilogue, n_aux, layouts, h_ref, w_ref, *rest):
    z = jnp.dot(h_ref[...], w_ref[...], preferred_element_type=F32)
    outs = epilogue(z, *[r[...] for r in rest[:n_aux]])
    for o_ref, o, layout in zip(rest[n_aux:], outs, layouts):
        if layout == FLAT:
            o_ref[...] = o.astype(o_ref.dtype)
            continue
        for hd in range(o.shape[1] // LANES):
            slab = o[:, hd * LANES:(hd + 1) * LANES]
            if layout == HEADS:
                o_ref[:, hd, :] = slab.astype(o_ref.dtype)
            else:
                o_ref[hd] = slab.T.astype(o_ref.dtype)


def _proj(h, w, aux, epilogue, out_defs, tm, name):
    n, k = h.shape
    c = w.shape[1]
    in_specs = [pl.BlockSpec((tm, k), lambda i: (i, 0)), _resident((k, c), lambda i: (0, 0))]
    in_specs += [pl.BlockSpec(a.shape, lambda i: (0, 0)) for a in aux]
    out_specs, out_shape = [], []
    for oc, dt, layout in out_defs:
        nh = oc // LANES
        if layout == FLAT:
            out_specs.append(pl.BlockSpec((tm, oc), lambda i: (i, 0)))
            out_shape.append(jax.ShapeDtypeStruct((n, oc), dt))
        elif layout == HEADS:
            out_specs.append(pl.BlockSpec((tm, nh, LANES), lambda i: (i, 0, 0)))
            out_shape.append(jax.ShapeDtypeStruct((n, nh, LANES), dt))
        else:
            out_specs.append(pl.BlockSpec((nh, LANES, tm), lambda i: (0, 0, i)))
            out_shape.append(jax.ShapeDtypeStruct((nh, LANES, n), dt))
    return pl.pallas_call(
        functools.partial(_proj_body, epilogue, len(aux), tuple(d[2] for d in out_defs)),
        grid=(n // tm,),
        in_specs=in_specs,
        out_specs=out_specs,
        out_shape=out_shape,
        compiler_params=_cparams(("parallel",)),
        name=name,
    )(h, w, *aux)


def _epi_qa(z, gain):
    return (_head_norm(z, gain, True) * (HD_A ** -0.5 * LOG2E),)


def _epi_ka(z, gain):
    y = _head_norm(z, gain, True)
    return y, y


def _epi_copy2(z):
    return z, z


def _epi_qb(z, gain):
    return (_head_norm(z, gain, False) * (HD_B ** -0.5 * LOG2E),)


def _epi_copy1(z):
    return (z,)


_KB0, _VB0, _KI0, _WI0, _MISC_COLS = 0, 256, 512, 640, 768


def _epi_misc(z, gain):
    kb = _head_norm(z[:, _KB0:_VB0], gain, False)
    vb = z[:, _VB0:_KI0]
    ki2 = z[:, _KI0:_WI0]
    return kb, kb, vb, vb, ki2[:, :D_IDX], ki2, z[:, _WI0:_WI0 + H_IDX]


def _rel_bias(tab_ref, h, rel):
    half = N_BUCKETS // 2
    exact = half // 2
    n = jnp.abs(rel)
    n2 = n * n
    v_neg = jnp.full(rel.shape, tab_ref[h, 0], F32)
    v_pos = jnp.full(rel.shape, tab_ref[h, half], F32)
    for b in range(1, half):
        cond = (n >= b) if b < exact else (n2 >= exact * exact * 2 ** (b - exact))
        v_neg = jnp.where(cond, tab_ref[h, b], v_neg)
        v_pos = jnp.where(cond, tab_ref[h, half + b], v_pos)
    return (jnp.where(rel > 0, v_pos, v_neg) - tab_ref[h, half - 1]) * LOG2E


def _bias_body(t, n_valid_diag, tab_ref, o_ref):
    h = pl.program_id(0)
    kl = lax.broadcasted_iota(I32, (t, t), 0)
    ql = lax.broadcasted_iota(I32, (t, t), 1)
    o_ref[0] = _rel_bias(tab_ref, h, kl - t - ql)
    visible = ((kl // CHUNK) <= (ql // CHUNK)) & (kl < n_valid_diag)
    o_ref[1] = jnp.where(visible, _rel_bias(tab_ref, h, kl - ql), NEG)


def _bias_dense_body(q0, n_keys, tab_ref, o_ref):
    h = pl.program_id(0)
    qpos = q0 + lax.broadcasted_iota(I32, o_ref.shape, 0)
    kpos = lax.broadcasted_iota(I32, o_ref.shape, 1)
    visible = ((kpos // CHUNK) <= (qpos // CHUNK)) & (kpos < n_keys)
    o_ref[...] = jnp.where(visible, _rel_bias(tab_ref, h, kpos - qpos), NEG)


def _bias_dense(tab_t, nq, nk_pad, q0, n_keys):
    nh = tab_t.shape[0]
    return pl.pallas_call(
        functools.partial(_bias_dense_body, q0, n_keys),
        grid=(nh,),
        in_specs=[pl.BlockSpec(memory_space=pltpu.SMEM)],
        out_specs=pl.BlockSpec((None, nq, nk_pad), lambda h: (h, 0, 0)),
        out_shape=jax.ShapeDtypeStruct((nh, nq, nk_pad), F32),
        compiler_params=_cparams(("arbitrary",)),
        name="bias_dense",
    )(tab_t)


def _bias_tiles(tab_t, t, n_valid_diag):
    assert (t + 1) ** 2 >= (N_BUCKETS // 4) ** 2 * 2 ** (N_BUCKETS // 2 - 1 - N_BUCKETS // 4)
    nh = tab_t.shape[0]
    return pl.pallas_call(
        functools.partial(_bias_body, t, n_valid_diag),
        grid=(nh,),
        in_specs=[pl.BlockSpec(memory_space=pltpu.SMEM)],
        out_specs=pl.BlockSpec((None, 2, t, t), lambda h: (h, 0, 0, 0)),
        out_shape=jax.ShapeDtypeStruct((nh, 2, t, t), F32),
        compiler_params=_cparams(("arbitrary",)),
        name="bias_tiles",
    )(tab_t)


ONES_ROWS = 16


def _softmax_step(s_ref, vt, m_ref, acc_ref, width):
    vt_ones = jnp.concatenate([vt, jnp.ones((ONES_ROWS, vt.shape[1]), vt.dtype)], axis=0)
    for c0 in range(0, s_ref.shape[1], width):
        cols = slice(c0, c0 + width)
        s = s_ref[:, cols]
        m_old = m_ref[:, cols]
        m_new = jnp.maximum(m_old, jnp.max(s, axis=0, keepdims=True))
        alpha = jnp.exp2(m_old - m_new)
        p = jnp.exp2(s - m_new)
        acc_ref[:, cols] = acc_ref[:, cols] * alpha + jnp.dot(vt_ones, p.astype(BF16), preferred_element_type=F32)
        m_ref[:, cols] = m_new


def _normalised(acc, dv):
    return acc[:dv] / acc[dv:dv + 1]


def _pipelined_tiles(n_far, produce, consume, buf0, buf1):
    @pl.when(n_far == -1)
    def _():
        produce(0, buf0, 1)
        consume(0, buf0)

    @pl.when(n_far == 0)
    def _():
        produce(0, buf0, 0)
        produce(1, buf1, 1)
        consume(0, buf0)
        consume(1, buf1)

    @pl.when(n_far >= 1)
    def _():
        produce(0, buf0, None)

    pairs = jnp.maximum(n_far - 1, 0) // 2

    def pair(j):
        produce(j + 1, buf1, None)
        consume(j, buf0)
        produce(j + 2, buf0, None)
        consume(j + 1, buf1)

    def body(i, carry):
        for u in range(4):
            pair(8 * i + 2 * u)
        return carry

    def leftover(q, carry):
        pair(2 * q)
        return carry

    lax.fori_loop(0, pairs // 4, body, 0)
    lax.fori_loop(pairs // 4 * 4, pairs, leftover, 0)
    d = 2 * pairs
    rem = n_far - d

    @pl.when(rem == 1)
    def _():
        produce(d + 1, buf1, 0)
        consume(d, buf0)
        produce(d + 2, buf0, 1)
        consume(d + 1, buf1)
        consume(d + 2, buf0)

    @pl.when(rem == 2)
    def _():
        produce(d + 1, buf1, None)
        consume(d, buf0)
        produce(d + 2, buf0, 0)
        consume(d + 1, buf1)
        produce(d + 3, buf1, 1)
        consume(d + 2, buf0)
        consume(d + 3, buf1)


def _attn_a_body(t, q0, out_scale, lam_ref, q_ref, k_ref, vt_ref, bias_ref, g_ref, o_ref,
                 qz_ref, m_ref, acc_ref, s0_ref, s1_ref):
    i = pl.program_id(2)
    n_far = q0 // t + i - 1
    zero = jnp.zeros((HD_A, t), BF16)
    qz_ref[:HD_A, :t] = q_ref[:HD_A, :]
    qz_ref[HD_A:, :t] = zero
    qz_ref[:HD_A, t:] = zero
    qz_ref[HD_A:, t:] = q_ref[HD_A:, :]
    m_ref[...] = jnp.full(m_ref.shape, NEG, F32)
    acc_ref[...] = jnp.zeros(acc_ref.shape, F32)

    def produce(j, s_ref, kind):
        ks = pl.multiple_of(j * t, t)
        s = jnp.dot(k_ref[pl.ds(ks, t), :], qz_ref[...], preferred_element_type=F32)
        if kind is not None:
            b = bias_ref[kind]
            s = jnp.concatenate([s[:, :t] + b, s[:, t:] + b], axis=1)
        s_ref[...] = s

    def consume(j, s_ref):
        _softmax_step(s_ref, vt_ref[:, pl.ds(pl.multiple_of(j * t, t), t)], m_ref, acc_ref, t)

    _pipelined_tiles(n_far, produce, consume, s0_ref, s1_ref)

    lam = lam_ref[0]
    o = _normalised(acc_ref[...], DV_A)
    o = o[:, :t] - lam * o[:, t:]
    y = o * lax.rsqrt(jnp.mean(o * o, axis=0, keepdims=True) + EPS)
    y = (y * g_ref[...]) * out_scale
    o_ref[...] = y.astype(o_ref.dtype)


def _attn_a(q, k, vt, bias, subln_g, lam, *, t, q0, out_scale):
    b, _, _, nq = q.shape
    nk = k.shape[1]
    return pl.pallas_call(
        functools.partial(_attn_a_body, t, q0, out_scale),
        grid=(b, H_A, nq // t),
        in_specs=[
            pl.BlockSpec(memory_space=pltpu.SMEM),
            pl.BlockSpec((None, None, 2 * HD_A, t), lambda bb, h, i: (bb, h, 0, i)),
            pl.BlockSpec((None, nk, LANES), lambda bb, h, i: (bb, 0, h)),
            pl.BlockSpec((None, None, DV_A, nk), lambda bb, h, i: (bb, h, 0, 0)),
            pl.BlockSpec((None, 2, t, t), lambda bb, h, i: (h, 0, 0, 0)),
            pl.BlockSpec((DV_A, 1), lambda bb, h, i: (0, 0)),
        ],
        out_specs=pl.BlockSpec((None, None, DV_A, t), lambda bb, h, i: (bb, h, 0, i)),
        out_shape=jax.ShapeDtypeStruct((b, H_A, DV_A, nq), BF16),
        scratch_shapes=[
            pltpu.VMEM((LANES, 2 * t), BF16),
            pltpu.VMEM((1, 2 * t), F32),
            pltpu.VMEM((DV_A + ONES_ROWS, 2 * t), F32),
            pltpu.VMEM((t, 2 * t), F32),
            pltpu.VMEM((t, 2 * t), F32),
        ],
        compiler_params=_cparams(("parallel", "parallel", "arbitrary")),
        name="diff_attn",
    )(lam, q, k, vt, bias, subln_g)


GROUP = 4


def _order_key(x):
    b = pltpu.bitcast(x, I32)
    return jnp.where(b < 0, b ^ INT_MAX, b)


def _order_unkey(k):
    return pltpu.bitcast(jnp.where(k < 0, k ^ INT_MAX, k), F32)


def _dsa_body(t, q0, topk, n_keys, n_valid_diag, qb_ref, qi_ref, wi_ref, kb_ref, vbt_ref, ki_ref, bias_ref, o_ref,
              keys_ref, qz_ref, qs_ref, mm_ref, m_ref, acc_ref, s0_ref, s1_ref):
    i = pl.program_id(1)
    n_tiles = q0 // t + i + 1
    n_far = n_tiles - 2

    zero = jnp.zeros((D_IDX, t), BF16)
    for hp in range(H_IDX // 2):
        qz_ref[2 * hp, :D_IDX, :] = qi_ref[hp, :D_IDX, :]
        qz_ref[2 * hp, D_IDX:, :] = zero
        qz_ref[2 * hp + 1, :D_IDX, :] = zero
        qz_ref[2 * hp + 1, D_IDX:, :] = qi_ref[hp, D_IDX:, :]

    def score_tile(j):
        kt = ki_ref[pl.ds(pl.multiple_of(j * t, t), t), :]
        acc = jnp.zeros((t, t), F32)
        for h in range(H_IDX):
            s = jnp.dot(kt, qz_ref[h], preferred_element_type=F32)
            acc = acc + jnp.maximum(s, 0.0) * wi_ref[h:h + 1, :]
        return acc

    def fold8(x, op):
        return op(x.reshape(x.shape[0] // 8, 8, t), axis=0)

    mm_ref[0] = jnp.full((8, t), jnp.inf, F32)
    mm_ref[1] = jnp.full((8, t), -jnp.inf, F32)

    def score_store(j):
        sc = score_tile(j)
        keys_ref[pl.ds(pl.multiple_of(j * t, t), t), :] = _order_key(sc)
        mm_ref[0] = jnp.minimum(mm_ref[0], fold8(sc, jnp.min))
        mm_ref[1] = jnp.maximum(mm_ref[1], fold8(sc, jnp.max))

    def score_group(jj, carry):
        for u in range(GROUP):
            score_store(GROUP * jj + u)
        return carry

    def score_single(j, carry):
        score_store(j)
        return carry

    jd = n_tiles - 1
    lax.fori_loop(0, jd // GROUP, score_group, 0)
    lax.fori_loop(jd // GROUP * GROUP, jd, score_single, 0)

    kl = lax.broadcasted_iota(I32, (t, t), 0)
    ql = lax.broadcasted_iota(I32, (t, t), 1)
    admissible = ((kl // CHUNK) <= (ql // CHUNK)) & (kl < n_valid_diag)
    sc = score_tile(jd)
    keys_ref[pl.ds(pl.multiple_of(jd * t, t), t), :] = jnp.where(admissible, _order_key(sc), INT_MIN)
    smin = jnp.minimum(mm_ref[0], fold8(jnp.where(admissible, sc, jnp.inf), jnp.min))
    smax = jnp.maximum(mm_ref[1], fold8(jnp.where(admissible, sc, -jnp.inf), jnp.max))
    for u in range(GROUP - 1):
        keys_ref[pl.ds(pl.multiple_of((n_tiles + u) * t, t), t), :] = jnp.full((t, t), INT_MIN, I32)
    n_groups = (n_tiles + GROUP - 1) // GROUP
    gt = GROUP * t

    def count_groups(hit_fn):
        def body(j, c):
            r0 = pl.multiple_of(j * gt, gt)
            hit = hit_fn(keys_ref[pl.ds(r0, gt), :], r0)
            return c + jnp.sum(hit.reshape(gt // 32, 32, t), axis=0)
        part = lax.fori_loop(0, n_groups, body, jnp.zeros((32, t), F32))
        return jnp.sum(part, axis=0, keepdims=True)

    def search_cond(state):
        return jnp.logical_and(state[0] < 4 * 34, state[-1] > 0)

    def search_body(state):
        it, lo, hi, c_lo, c_hi, _ = state
        f_lo, f_hi = _order_unkey(lo), _order_unkey(hi)
        a, b = jnp.log(c_lo + 0.5), jnp.log(c_hi + 0.5)
        frac = jnp.clip((a - math.log(topk)) / jnp.maximum(a - b, 1e-9), 0.0, 1.0)
        frac = jnp.where(it == 0, frac0, frac)
        guess = jnp.minimum(jnp.maximum(_order_key(f_lo + (f_hi - f_lo) * frac), lo + 1), hi - 1)
        mid = jnp.where(it % 4 == 3, (lo >> 1) + (hi >> 1) + (lo & hi & 1), guess)
        c = count_groups(lambda kk, r0: jnp.where(kk >= mid, 1.0, 0.0))
        live = lo + 1 < hi
        ge = c >= topk
        up = jnp.logical_and(live, ge)
        down = jnp.logical_and(live, jnp.logical_not(ge))
        lo = jnp.where(up, mid, lo)
        c_lo = jnp.where(up, c, c_lo)
        hi = jnp.where(jnp.logical_and(live, c == topk), mid + 1, jnp.where(down, mid, hi))
        c_hi = jnp.where(down, c, c_hi)
        open_ = jnp.logical_and(lo + 1 < hi, c_hi != topk - 1)
        return it + 1, lo, hi, c_lo, c_hi, jnp.max(jnp.where(open_, 1, 0))

    qpos = q0 + i * t + lax.broadcasted_iota(I32, (1, t), 1)
    n_adm = jnp.minimum((qpos // CHUNK + 1) * CHUNK, n_keys).astype(F32)
    lo0 = _order_key(jnp.min(smin, axis=0, keepdims=True))
    hi0 = jnp.where(n_adm <= topk, lo0 + 1, _order_key(jnp.max(smax, axis=0, keepdims=True)) + 1)

    def upper_quantile(p):
        u = jnp.sqrt(-2.0 * jnp.log(p))
        return u - ((0.010328 * u + 0.802853) * u + 2.515517) / (((0.001308 * u + 0.189269) * u + 1.432788) * u + 1.0)

    frac0 = 0.5 + upper_quantile(jnp.clip(topk / n_adm, 1e-6, 0.5)) / (2.0 * upper_quantile(1.0 / n_adm))
    frac0 = jnp.clip(frac0, 0.0, 1.0)
    _, lo, hi, c_lo, c_above, _ = lax.while_loop(
        search_cond, search_body, (jnp.int32(0), lo0, hi0, n_adm, jnp.zeros((1, t), F32), jnp.int32(1)))

    def below_hi_max(j, m):
        kk = keys_ref[pl.ds(pl.multiple_of(j * gt, gt), gt), :]
        return jnp.maximum(m, jnp.max(jnp.where(kk < hi, kk, INT_MIN).reshape(gt // 8, 8, t), axis=0))

    m8 = lax.fori_loop(0, n_groups, below_hi_max, jnp.full((8, t), INT_MIN, I32))
    for shift in (4, 2, 1):
        m8 = jnp.maximum(m8, pltpu.roll(m8, shift, axis=0))
    wide = lo + 1 < hi
    thr = jnp.where(wide, m8[:1], lo)
    c_eq = count_groups(lambda kk, r0: jnp.where(kk == thr, 1.0, 0.0))
    c_thr = jnp.where(wide, c_above + c_eq, c_lo)

    @pl.when(jnp.max(jnp.where(jnp.logical_and(c_thr > topk, qpos < n_keys), 1, 0)) > 0)
    def _():
        need = topk - c_above
        row = lax.broadcasted_iota(I32, (gt, t), 0)

        def cut_body(_, st):
            p_lo, p_hi = st
            p_mid = (p_lo + p_hi) >> 1
            c = count_groups(lambda kk, r0: jnp.where(
                jnp.logical_and(kk == thr, row + r0 <= p_mid), 1.0, 0.0))
            ok = c >= need
            return jnp.where(ok, p_lo, p_mid), jnp.where(ok, p_mid, p_hi)

        n_pos = keys_ref.shape[0]
        _, cut = lax.fori_loop(0, n_pos.bit_length(), cut_body,
                               (jnp.full((1, t), -1, I32), jnp.full((1, t), n_pos - 1, I32)))

        def demote_body(j, carry):
            r0 = pl.multiple_of(j * gt, gt)
            kk = keys_ref[pl.ds(r0, gt), :]
            keys_ref[pl.ds(r0, gt), :] = jnp.where(jnp.logical_and(kk == thr, row + r0 > cut), thr - 1, kk)
            return carry

        lax.fori_loop(0, n_groups, demote_body, 0)

    m_ref[...] = jnp.full(m_ref.shape, NEG, F32)
    acc_ref[...] = jnp.zeros(acc_ref.shape, F32)

    for h in range(H_B):
        qs_ref[h // G_B, :, (h % G_B) * t:(h % G_B + 1) * t] = qb_ref[h]

    def produce(j, s_ref, kind):
        ks = pl.multiple_of(j * t, t)
        off = jnp.where(keys_ref[pl.ds(ks, t), :] >= thr, 0.0, NEG)
        for n in range(HKV_B):
            s = jnp.dot(kb_ref[pl.ds(ks, t), n * HD_B:(n + 1) * HD_B], qs_ref[n], preferred_element_type=F32)
            parts = []
            for g in range(G_B):
                off_g = off if kind is None else off + bias_ref[n * G_B + g, kind]
                parts.append(s[:, g * t:(g + 1) * t] + off_g)
            s_ref[n] = jnp.concatenate(parts, axis=1)

    def consume(j, s_ref):
        ks = pl.multiple_of(j * t, t)
        for n in range(HKV_B):
            _softmax_step(s_ref.at[n], vbt_ref[n, :, pl.ds(ks, t)], m_ref.at[n], acc_ref.at[n], 2 * t)

    _pipelined_tiles(n_far, produce, consume, s0_ref, s1_ref)

    for n in range(HKV_B):
        o = _normalised(acc_ref[n], HD_B)
        for g in range(G_B):
            h = n * G_B + g
            o_ref[h] = o[:, g * t:(g + 1) * t].astype(o_ref.dtype)


def _dsa(qb, qi, wi_t, kb, vbt, ki2, bias, *, t, q0, topk, n_keys, n_valid_diag):
    b, _, _, nq = qb.shape
    nk = kb.shape[1]
    return pl.pallas_call(
        functools.partial(_dsa_body, t, q0, topk, n_keys, n_valid_diag),
        grid=(b, nq // t),
        in_specs=[
            pl.BlockSpec((None, H_B, HD_B, t), lambda bb, i: (bb, 0, 0, i)),
            pl.BlockSpec((None, H_IDX // 2, LANES, t), lambda bb, i: (bb, 0, 0, i)),
            pl.BlockSpec((None, H_IDX, t), lambda bb, i: (bb, 0, i)),
            _resident((None, nk, HKV_B * HD_B), lambda bb, i: (bb, 0, 0)),
            _resident((None, HKV_B, HD_B, nk), lambda bb, i: (bb, 0, 0, 0)),
            _resident((None, nk, LANES), lambda bb, i: (bb, 0, 0)),
            _resident((H_B, 2, t, t), lambda bb, i: (0, 0, 0, 0)),
        ],
        out_specs=pl.BlockSpec((None, H_B, HD_B, t), lambda bb, i: (bb, 0, 0, i)),
        out_shape=jax.ShapeDtypeStruct((b, H_B, HD_B, nq), BF16),
        scratch_shapes=[
            pltpu.VMEM((nk + (GROUP - 1) * t, t), I32),
            pltpu.VMEM((H_IDX, LANES, t), BF16),
            pltpu.VMEM((HKV_B, HD_B, G_B * t), BF16),
            pltpu.VMEM((2, 8, t), F32),
            pltpu.VMEM((HKV_B, 1, G_B * t), F32),
            pltpu.VMEM((HKV_B, HD_B + ONES_ROWS, G_B * t), F32),
            pltpu.VMEM((HKV_B, t, G_B * t), F32),
            pltpu.VMEM((HKV_B, t, G_B * t), F32),
        ],
        compiler_params=_cparams(("parallel", "arbitrary")),
        name="dsa",
    )(qb, qi, wi_t, kb, vbt, ki2, bias)


def _softmax_pieces(s_list, v_list):
    m = functools.reduce(jnp.maximum, [jnp.max(s, axis=-1, keepdims=True) for s in s_list])
    p_list = [jnp.exp2(s - m) for s in s_list]
    o = sum(jnp.dot(p.astype(BF16), v, preferred_element_type=F32) for p, v in zip(p_list, v_list))
    return o / sum(jnp.sum(p, axis=-1, keepdims=True) for p in p_list)


def _head_rows(ref, h, n_heads, n_pos):
    return ref[pl.ds(h, n_pos, stride=n_heads), :].astype(BF16)


def _short_a_body(nq, n_past, out_scale, lam_ref, q_ref, kp_ref, vp_ref, kn_ref, vn_ref, bias_ref, g_ref, o_ref):
    lam = lam_ref[0]
    lane = lax.broadcasted_iota(I32, (nq, LANES), 1)
    pieces = ((kp_ref, vp_ref, 0, n_past), (kn_ref, vn_ref, n_past, nq))
    for h in range(H_A):
        cols = slice(h * LANES, (h + 1) * LANES)
        q = q_ref[:, cols]
        zero = jnp.zeros_like(q)
        qz = jnp.concatenate([jnp.where(lane < HD_A, q, zero), jnp.where(lane < HD_A, zero, q)], axis=0)
        b = bias_ref[h]
        b2 = jnp.concatenate([b, b], axis=0)
        s_list = [lax.dot_general(qz, _head_rows(k_ref, h, H_A, size), _NT, preferred_element_type=F32)
                  + b2[:, lo:lo + size] for k_ref, _, lo, size in pieces]
        o = _softmax_pieces(s_list, [_head_rows(v_ref, h, H_A, size) for _, v_ref, _, size in pieces])
        o = o[:nq] - lam * o[nq:]
        y = o * lax.rsqrt(jnp.mean(o * o, axis=-1, keepdims=True) + EPS)
        o_ref[:, cols] = ((y * g_ref[...]) * out_scale).astype(o_ref.dtype)


def _short_a(q, k_past, v_past, k_new, v_new, bias, subln_g, lam, *, out_scale):
    b, nq, d = q.shape
    n_past = k_past.shape[1]
    assert n_past % LANES == 0
    rows = lambda n: pl.BlockSpec((None, n * H_A, DV_A), lambda bb: (bb, 0, 0))
    merge = lambda a: a.reshape(b, a.shape[1] * H_A, DV_A)
    return pl.pallas_call(
        functools.partial(_short_a_body, nq, n_past, out_scale),
        grid=(b,),
        in_specs=[
            pl.BlockSpec(memory_space=pltpu.SMEM),
            pl.BlockSpec((None, nq, d), lambda bb: (bb, 0, 0)),
            rows(n_past), rows(n_past), rows(nq), rows(nq),
            pl.BlockSpec(bias.shape, lambda bb: (0, 0, 0)),
            pl.BlockSpec((1, DV_A), lambda bb: (0, 0)),
        ],
        out_specs=pl.BlockSpec((None, nq, d), lambda bb: (bb, 0, 0)),
        out_shape=jax.ShapeDtypeStruct((b, nq, d), BF16),
        compiler_params=_cparams(("parallel",)),
        name="diff_attn_short",
    )(lam, q, merge(k_past), merge(v_past), merge(k_new), merge(v_new), bias, subln_g)


def _short_dsa_body(nq, n_past, topk, qb_ref, qi_ref, wi_ref, kbp_ref, vbp_ref, kip_ref, kbn_ref, vbn_ref,
                    kin_ref, bias_ref, o_ref):
    lane = lax.broadcasted_iota(I32, (nq, LANES), 1)
    spans = ((0, n_past), (n_past, nq))

    qz = []
    for h in range(H_IDX):
        qs = qi_ref[:, (h // 2) * LANES:(h // 2 + 1) * LANES]
        zero = jnp.zeros_like(qs)
        qz.append(jnp.where(lane < D_IDX, qs, zero) if h % 2 == 0 else jnp.where(lane < D_IDX, zero, qs))
    qz = jnp.concatenate(qz, axis=0)
    keys, kpos = [], []
    for ki_ref, (lo, size) in zip((kip_ref, kin_ref), spans):
        ki = ki_ref[...].astype(BF16)
        s_idx = lax.dot_general(qz, jnp.concatenate([ki, ki], axis=-1), _NT, preferred_element_type=F32)
        score = jnp.zeros((nq, size), F32)
        for h in range(H_IDX):
            score = score + jnp.maximum(s_idx[h * nq:(h + 1) * nq], 0.0) * wi_ref[:, h:h + 1]
        qp = n_past + lax.broadcasted_iota(I32, (nq, size), 0)
        kp = lo + lax.broadcasted_iota(I32, (nq, size), 1)
        keys.append(jnp.where((kp // CHUNK) <= (qp // CHUNK), _order_key(score), INT_MIN))
        kpos.append(kp)

    def count(hit_fn):
        return sum(jnp.sum(jnp.where(hit_fn(k, kp), 1.0, 0.0), axis=-1, keepdims=True)
                   for k, kp in zip(keys, kpos))

    def search_body(state):
        it, lo, hi, c_lo, c_hi, _ = state
        mid = (lo >> 1) + (hi >> 1) + (lo & hi & 1)
        c = count(lambda k, kp: k >= mid)
        live = lo + 1 < hi
        ge = c >= topk
        up = jnp.logical_and(live, ge)
        down = jnp.logical_and(live, jnp.logical_not(ge))
        lo = jnp.where(up, mid, lo)
        c_lo = jnp.where(up, c, c_lo)
        hi = jnp.where(jnp.logical_and(live, c == topk), mid + 1, jnp.where(down, mid, hi))
        c_hi = jnp.where(down, c, c_hi)
        return it + 1, lo, hi, c_lo, c_hi, jnp.max(jnp.where(lo + 1 < hi, 1, 0))

    n_keys = n_past + nq
    n_adm = count(lambda k, kp: k > INT_MIN)
    lo0 = jnp.full((nq, 1), INT_MIN + 1, I32)
    hi0 = jnp.where(n_adm <= topk, lo0 + 1, INT_MAX)
    _, thr, _, c_thr, c_above, _ = lax.while_loop(
        lambda st: jnp.logical_and(st[0] < 34, st[-1] > 0), search_body,
        (jnp.int32(0), lo0, hi0, n_adm, jnp.zeros((nq, 1), F32), jnp.int32(1)))

    need = topk - c_above

    def cut_body(_, st):
        p_lo, p_hi = st
        p_mid = (p_lo + p_hi) >> 1
        ok = count(lambda k, kp: jnp.logical_and(k == thr, kp <= p_mid)) >= need
        return jnp.where(ok, p_lo, p_mid), jnp.where(ok, p_mid, p_hi)

    _, cut = lax.fori_loop(0, n_keys.bit_length(), cut_body,
                           (jnp.full((nq, 1), -1, I32), jnp.full((nq, 1), n_keys - 1, I32)))
    off = [jnp.where(jnp.logical_and(k >= thr, jnp.logical_not(jnp.logical_and(k == thr, kp > cut))), 0.0, NEG)
           for k, kp in zip(keys, kpos)]

    for n in range(HKV_B):
        heads = range(n * G_B, (n + 1) * G_B)
        qs = jnp.concatenate([qb_ref[:, h * HD_B:(h + 1) * HD_B] for h in heads], axis=0)
        s_list = []
        for kb_ref, off_i, (lo, size) in zip((kbp_ref, kbn_ref), off, spans):
            s = lax.dot_general(qs, _head_rows(kb_ref, n, HKV_B, size), _NT, preferred_element_type=F32)
            s_list.append(s + jnp.concatenate([bias_ref[h][:, lo:lo + size] + off_i for h in heads], axis=0))
        o = _softmax_pieces(s_list, [_head_rows(vbp_ref, n, HKV_B, n_past), _head_rows(vbn_ref, n, HKV_B, nq)])
        for g, h in enumerate(heads):
            o_ref[:, h * HD_B:(h + 1) * HD_B] = o[g * nq:(g + 1) * nq].astype(o_ref.dtype)


def _short_dsa(qb, qi, wi, kb_past, vb_past, ki_past, kb_new, vb_new, ki_new, bias, *, topk):
    b, nq, d = qb.shape
    n_past = kb_past.shape[1]
    assert n_past % LANES == 0
    per_batch = lambda *shape: pl.BlockSpec((None,) + shape, lambda bb: (bb,) + (0,) * len(shape))
    merge = lambda a: a.reshape(b, a.shape[1] * HKV_B, HD_B)
    return pl.pallas_call(
        functools.partial(_short_dsa_body, nq, n_past, topk),
        grid=(b,),
        in_specs=[per_batch(nq, d), per_batch(nq, d), per_batch(nq, H_IDX),
                  per_batch(n_past * HKV_B, HD_B), per_batch(n_past * HKV_B, HD_B), per_batch(n_past, D_IDX),
                  per_batch(nq * HKV_B, HD_B), per_batch(nq * HKV_B, HD_B), per_batch(nq, D_IDX),
                  pl.BlockSpec(bias.shape, lambda bb: (0, 0, 0))],
        out_specs=per_batch(nq, d),
        out_shape=jax.ShapeDtypeStruct((b, nq, d), BF16),
        compiler_params=_cparams(("parallel",)),
        name="dsa_short",
    )(qb, qi, wi, merge(kb_past), merge(vb_past), ki_past, merge(kb_new), merge(vb_new), ki_new, bias)


def _sigmoid(x):
    return 1.0 / (1.0 + jnp.exp(-x))


def _merge_body(mixers_t, h_ref, oa_ref, ob_ref, wg_ref, woa_ref, wob_ref, o_ref):
    h = h_ref[...]
    oa, ob = (oa_ref[...].T, ob_ref[...].T) if mixers_t else (oa_ref[...], ob_ref[...])
    ga = jnp.dot(h, wg_ref[:, :D_MODEL], preferred_element_type=F32)
    ya = jnp.dot(oa, woa_ref[...], preferred_element_type=F32)
    merged = _sigmoid(ga) * ya
    gb = jnp.dot(h, wg_ref[:, D_MODEL:], preferred_element_type=F32)
    yb = jnp.dot(ob, wob_ref[...], preferred_element_type=F32)
    o_ref[...] = (merged + _sigmoid(gb) * yb).astype(o_ref.dtype)


def _merge(h, oa, ob, wg, woa, wob, tm, mixers_t):
    n = h.shape[0]
    mixer_spec = (lambda c: pl.BlockSpec((c, tm), lambda i: (0, i))) if mixers_t else \
        (lambda c: pl.BlockSpec((tm, c), lambda i: (i, 0)))
    return pl.pallas_call(
        functools.partial(_merge_body, mixers_t),
        grid=(n // tm,),
        in_specs=[
            pl.BlockSpec((tm, D_MODEL), lambda i: (i, 0)),
            mixer_spec(H_A * DV_A),
            mixer_spec(H_B * HD_B),
            _resident((D_MODEL, 2 * D_MODEL), lambda i: (0, 0)),
            _resident((H_A * DV_A, D_MODEL), lambda i: (0, 0)),
            _resident((H_B * HD_B, D_MODEL), lambda i: (0, 0)),
        ],
        out_specs=pl.BlockSpec((tm, D_MODEL), lambda i: (i, 0)),
        out_shape=jax.ShapeDtypeStruct((n, D_MODEL), BF16),
        compiler_params=_cparams(("parallel",)),
        name="gated_merge",
    )(h, oa, ob, wg, woa, wob)


def _tail_body(x_ref, mg_ref, wo_ref, g_ref, w1_ref, w2_ref, o_ref, h2_ref):
    f = pl.program_id(1)

    @pl.when(f == 0)
    def _():
        x1 = x_ref[...] + jnp.dot(mg_ref[...], wo_ref[...], preferred_element_type=F32)
        o_ref[...] = x1
        y = x1 * lax.rsqrt(jnp.mean(x1 * x1, axis=-1, keepdims=True) + EPS)
        h2_ref[...] = (y * g_ref[...]).astype(h2_ref.dtype)

    u = jnp.maximum(jnp.dot(h2_ref[...], w1_ref[...], preferred_element_type=F32), 0.0)
    o_ref[...] += jnp.dot((u * u).astype(BF16), w2_ref[...], preferred_element_type=F32)


def _tail(x, merged, w_out, g2, w1, w2, tm, tf):
    n = x.shape[0]
    return pl.pallas_call(
        _tail_body,
        grid=(n // tm, D_FF // tf),
        in_specs=[
            pl.BlockSpec((tm, D_MODEL), lambda i, f: (i, 0)),
            pl.BlockSpec((tm, D_MODEL), lambda i, f: (i, 0)),
            _resident((D_MODEL, D_MODEL), lambda i, f: (0, 0)),
            pl.BlockSpec((1, D_MODEL), lambda i, f: (0, 0)),
            pl.BlockSpec((D_MODEL, tf), lambda i, f: (0, f)),
            pl.BlockSpec((tf, D_MODEL), lambda i, f: (f, 0)),
        ],
        out_specs=pl.BlockSpec((tm, D_MODEL), lambda i, f: (i, 0)),
        out_shape=jax.ShapeDtypeStruct((n, D_MODEL), F32),
        scratch_shapes=[pltpu.VMEM((tm, D_MODEL), BF16)],
        compiler_params=_cparams(("parallel", "arbitrary")),
        name="out_proj_ffn",
    )(x, merged, w_out, g2.reshape(1, D_MODEL), w1, w2)


def _prep_weights(w_in, w_o_a, w_o_b, w_out, w_ff1, w_ff2):
    sizes = (H_A * 2 * HD_A, H_A * 2 * HD_A, H_A * DV_A, H_B * HD_B, HKV_B * HD_B, HKV_B * HD_B,
             H_IDX * D_IDX, D_IDX, H_IDX, D_MODEL, D_MODEL)
    offs = [0]
    for s in sizes:
        offs.append(offs[-1] + s)
    col = lambda a, b: w_in[:, offs[a]:offs[b]]
    w_misc = jnp.concatenate(
        [col(4, 6), col(7, 8), col(7, 8), col(8, 9),
         jnp.zeros((D_MODEL, _MISC_COLS - _WI0 - H_IDX), w_in.dtype)], axis=1)
    return dict(
        qa=col(0, 1).astype(BF16), ka=col(1, 2).astype(BF16), va=col(2, 3).astype(BF16),
        qb=col(3, 4).astype(BF16), misc=w_misc.astype(BF16), qi=col(6, 7).astype(BF16),
        gate=col(9, 11).astype(BF16), oa=w_o_a.astype(BF16), ob=w_o_b.astype(BF16),
        out=w_out.astype(BF16), ff1=w_ff1.astype(BF16), ff2=w_ff2.astype(BF16))


def _layer(x, past, w, p, lam, lam_init, tab_t, *, tm, t_a=None, t_b=None):
    short = t_a is None
    b, t, _ = x.shape
    n = b * t
    xf = x.reshape(n, D_MODEL)
    h = _rmsnorm(xf, p["norm1_g"], tm)
    tp = min(2 * tm, n)

    gain_a = lambda g: g.reshape(1, 2 * HD_A)
    v_layout = HEADS_T if (not short and past is None and b == 1) else FLAT
    q_layout = FLAT if short else HEADS_T
    (qa,) = _proj(h, w["qa"], [gain_a(p["qn_a_g"])], _epi_qa, [(1024, BF16, q_layout)], tp, "proj_qa")
    ka, ka_h = _proj(h, w["ka"], [gain_a(p["kn_a_g"])], _epi_ka, [(1024, F32, HEADS), (1024, BF16, FLAT)],
                     tp, "proj_ka")
    va, va_h = _proj(h, w["va"], [], _epi_copy2, [(1024, F32, HEADS), (1024, BF16, v_layout)], tp, "proj_va")
    (qb,) = _proj(h, w["qb"], [p["qn_b_g"].reshape(1, HD_B)], _epi_qb, [(1024, BF16, q_layout)], tp, "proj_qb")
    kb, kb_h, vb, vb_h, ki, ki2_h, wi = _proj(
        h, w["misc"], [p["kn_b_g"].reshape(1, HD_B)], _epi_misc,
        [(256, F32, HEADS), (256, BF16, FLAT), (256, F32, HEADS), (256, BF16, v_layout), (D_IDX, F32, FLAT),
         (2 * D_IDX, BF16, FLAT), (H_IDX, F32, FLAT)],
        tp, "proj_misc")
    (qi,) = _proj(h, w["qi"], [], _epi_copy1, [(1024, BF16, q_layout)], tp, "proj_qi")

    new_rows = (ka.reshape(b, t, H_A, 2 * HD_A), va.reshape(b, t, H_A, DV_A), kb.reshape(b, t, HKV_B, HD_B),
                vb.reshape(b, t, HKV_B, HD_B), ki.reshape(b, t, D_IDX))

    per_b = lambda a: a.reshape(b, t, -1)
    if short:
        n_keys = past[0].shape[1] + t
        topk = min(TOPK_MAX, n_keys // 4)
        oa = _short_a(per_b(qa), past[0], past[1], new_rows[0], new_rows[1],
                      _bias_dense(tab_t[:H_A], t, n_keys, n_keys - t, n_keys),
                      p["subln_a_g"].reshape(1, DV_A), lam, out_scale=1.0 - lam_init)
        ob = _short_dsa(per_b(qb), per_b(qi), per_b(wi), past[2], past[3], past[4], *new_rows[2:],
                        _bias_dense(tab_t[H_A:], t, n_keys, n_keys - t, n_keys), topk=topk)
    else:
        oa, ob = _tiled_mixers(qa, qb, qi, wi, ka_h, va_h, kb_h, vb_h, ki2_h, past, tab_t, p, lam, lam_init,
                               b=b, t=t, t_a=t_a, t_b=t_b)
    if short:
        oa, ob = oa.reshape(n, H_A * DV_A), ob.reshape(n, H_B * HD_B)
    else:
        oa, ob = (jnp.transpose(o, (1, 2, 0, 3)).reshape(-1, n) for o in (oa, ob))

    merged = _merge(h, oa, ob, w["gate"], w["oa"], w["ob"], tm, mixers_t=not short)
    y = _tail(xf, merged, w["out"], p["norm2_g"], w["ff1"], w["ff2"], tm, 1024)
    return y.reshape(b, t, D_MODEL), new_rows


def _tiled_mixers(qa, qb, qi, wi, ka_h, va_h, kb_h, vb_h, ki2_h, past, tab_t, p, lam, lam_init,
                  *, b, t, t_a, t_b):
    def full_keys(new_h, past_arr):
        if new_h.ndim == 3:
            return new_h[None]
        new_h = new_h.reshape(b, t, -1)
        if past_arr is None:
            return new_h
        past_h = past_arr.reshape(b, past_arr.shape[1], -1).astype(BF16)
        return jnp.concatenate([past_h, new_h], axis=1)

    if past is None:
        pa_k = pa_v = pb_k = pb_v = pb_i = None
    else:
        pa_k, pa_v, pb_k, pb_v, pb_i = past
        pb_i = jnp.concatenate([pb_i, pb_i], axis=-1)
    ka_f, va_f = full_keys(ka_h, pa_k), full_keys(va_h, pa_v)
    kb_f, vb_f, ki_f = full_keys(kb_h, pb_k), full_keys(vb_h, pb_v), full_keys(ki2_h, pb_i)
    n_keys = ka_f.shape[1]
    q0 = n_keys - t
    topk = min(TOPK_MAX, n_keys // 4)
    t_pad = max(t_a, t_b)
    assert t_pad % t_a == 0 and t_pad % t_b == 0 and q0 % t_pad == 0
    nk_pad = -(-n_keys // t_pad) * t_pad
    pad_k = lambda a: jnp.pad(a, ((0, 0), (0, nk_pad - n_keys), (0, 0)))
    pad_q = lambda a, tt: jnp.pad(a.reshape(b, t, -1), ((0, 0), (0, -(-t // tt) * tt - t), (0, 0)))
    ka_f, kb_f, ki_f = (pad_k(a) for a in (ka_f, kb_f, ki_f))
    n_valid = lambda tt: n_keys - (n_keys - 1) // tt * tt
    assert all(t <= tt or n_valid(tt) == tt for tt in (t_a, t_b))

    def heads_t(v, nh):
        if v.ndim == 3:
            v = jnp.transpose(v.reshape(b, n_keys, nh, -1), (0, 2, 3, 1))
        return jnp.pad(v, ((0, 0), (0, 0), (0, 0), (0, nk_pad - n_keys)))

    def q_slabs(q, tt):
        q = jnp.transpose(q.reshape(q.shape[0], LANES, b, t), (2, 0, 1, 3))
        return jnp.pad(q, ((0, 0), (0, 0), (0, 0), (0, -(-t // tt) * tt - t)))

    bias_a = _bias_tiles(tab_t[:H_A], t_a, n_valid(t_a))
    oa = _attn_a(q_slabs(qa, t_a), ka_f, heads_t(va_f, H_A), bias_a, p["subln_a_g"].reshape(DV_A, 1), lam,
                 t=t_a, q0=q0, out_scale=1.0 - lam_init)
    wi_t = jnp.swapaxes(pad_q(wi, t_b), 1, 2)
    bias_b = _bias_tiles(tab_t[H_A:], t_b, n_valid(t_b))
    ob = _dsa(q_slabs(qb, t_b), q_slabs(qi, t_b), wi_t, kb_f, heads_t(vb_f, HKV_B), ki_f, bias_b,
              t=t_b, q0=q0, topk=topk, n_keys=n_keys, n_valid_diag=n_valid(t_b))
    return oa[..., :t], ob[..., :t]


def kernel(x_prompt, x_sample, cache_a_k, cache_a_v, cache_b_k, cache_b_v, cache_b_kidx, rel_bias, norm1_g, w_in, qn_a_g, kn_a_g, lam_q1, lam_k1, lam_q2, lam_k2, subln_a_g, qn_b_g, kn_b_g, w_o_a, w_o_b, w_out, norm2_g, w_ff1, w_ff2):
    depth = w_in.shape[0]
    tab_t = rel_bias.T.astype(F32)
    y_prompt, y_sample = x_prompt, x_sample
    prompt_rows, sample_rows = [], []
    for l in range(depth):
        lam_init = 0.8 - 0.6 * math.exp(-0.3 * l)
        lam = (jnp.exp(jnp.sum(lam_q1[l].astype(F32) * lam_k1[l].astype(F32)))
               - jnp.exp(jnp.sum(lam_q2[l].astype(F32) * lam_k2[l].astype(F32))) + lam_init).reshape(1)
        w = _prep_weights(w_in[l], w_o_a[l], w_o_b[l], w_out[l], w_ff1[l], w_ff2[l])
        p = dict(norm1_g=norm1_g[l], qn_a_g=qn_a_g[l], kn_a_g=kn_a_g[l], subln_a_g=subln_a_g[l],
                 qn_b_g=qn_b_g[l], kn_b_g=kn_b_g[l], norm2_g=norm2_g[l])
        y_prompt, rp = _layer(y_prompt, None, w, p, lam, lam_init, tab_t, t_a=512, t_b=256, tm=512)
        past = (cache_a_k[l], cache_a_v[l], cache_b_k[l], cache_b_v[l], cache_b_kidx[l])
        y_sample, rs = _layer(y_sample, past, w, p, lam, lam_init, tab_t, tm=512)
        prompt_rows.append(rp)
        sample_rows.append(rs)
    p_rows = tuple(jnp.stack(r, axis=0) for r in zip(*prompt_rows))
    s_rows = tuple(jnp.stack(r, axis=0) for r in zip(*sample_rows))
    return (y_prompt, y_sample) + p_rows + s_rows
```
